```python
import math
import jax
import jax.numpy as jnp
from jax import lax
import numpy as np

D_MODEL = 1024
BATCH = 1
SEQ = 16384
DEPTH = 4

GRID_W = 64
CTX_LEN = 256
N_MIXERS = 4
HEAD_DIM = 64
ROPE_BASE = 10000.0
RMS_EPS = 1e-6
D_FF = ((8 * D_MODEL // 3 + 255) // 256) * 256
N_MOD = 6

DA_HEADS = D_MODEL // (2 * HEAD_DIM)
DA_VDIM = 2 * HEAD_DIM
Q_BLOCK = 128

SC_WIDTH = 3

WA_HEADS = D_MODEL // HEAD_DIM
WA_KV_HEADS = 4
WINDOW = 128
W_BLOCK = WINDOW

HG_EXPAND = 128
HG_HEADS = D_MODEL // HG_EXPAND
HG_DK = HG_EXPAND
HG_DV = D_MODEL // HG_HEADS
HG_FDIM = HG_HEADS * HG_DK
CHUNK = 64

kernel_name = 'hybrid_interleaved_diffusion_block'


def n_layers_of(mixer):
    return len(range(mixer, DEPTH, N_MIXERS))


def rmsnorm(x, g):
    x32 = x.astype(jnp.float32)
    y = x32 * lax.rsqrt(jnp.mean(x32 * x32, axis=-1, keepdims=True) + RMS_EPS)
    return y.astype(x.dtype) * g


def modulate(x, shift, scale):
    return x * (1.0 + scale) + shift


def swiglu(t, w_in, w_out):
    a, b = jnp.split(t @ w_in, 2, axis=-1)
    return (jax.nn.silu(a) * b) @ w_out


def rope_tables(n_tokens):
    rows = n_tokens // GRID_W
    row = jnp.repeat(jnp.arange(rows), GRID_W).astype(jnp.float32)
    col = jnp.tile(jnp.arange(GRID_W), rows).astype(jnp.float32)
    half = HEAD_DIM // 2
    inv_freq = 1.0 / (ROPE_BASE ** (jnp.arange(0, half, 2, dtype=jnp.float32) / half))
    ang = jnp.concatenate([row[:, None] * inv_freq, col[:, None] * inv_freq], axis=-1)
    return jnp.cos(ang), jnp.sin(ang)


def apply_rope_2d(x, cos, sin):
    n_mid = x.ndim - 3
    qd = HEAD_DIM // 4
    bshape = (cos.shape[0],) + (1,) * n_mid + (2, qd)
    cs, sn = cos.reshape(bshape), sin.reshape(bshape)
    xs = x.reshape(x.shape[:-1] + (2, 2, qd))
    x1, x2 = xs[..., 0, :], xs[..., 1, :]
    out = jnp.stack([x1 * cs - x2 * sn, x2 * cs + x1 * sn], axis=-2)
    return out.reshape(x.shape)


def sink_softmax(scores, sink):
    m = sink
    for s in scores:
        m = jnp.maximum(m, jnp.max(s, axis=-1, keepdims=True))
    es = [jnp.exp(s - m) for s in scores]
    den = jnp.exp(sink - m)
    for e in es:
        den = den + jnp.sum(e, axis=-1, keepdims=True)
    return [e / den for e in es]


def diff_attention(h, hc, wqkv, lam, subln_g, wo, cos, sin, layer_idx, need_ctx):
    B, L, _ = h.shape
    lam_init = 0.8 - 0.6 * math.exp(-0.3 * layer_idx)
    lam32 = lam.astype(jnp.float32)
    lam_full = (jnp.exp(jnp.sum(lam32[0] * lam32[1]))
                - jnp.exp(jnp.sum(lam32[2] * lam32[3])) + lam_init)
    scale = HEAD_DIM ** -0.5

    def project(t):
        n = t.shape[1]
        q, k, v = jnp.split(t @ wqkv, 3, axis=-1)
        return (q.reshape(B, n, DA_HEADS, 2, HEAD_DIM),
                k.reshape(B, n, DA_HEADS, 2, HEAD_DIM),
                v.reshape(B, n, DA_HEADS, DA_VDIM))

    def attend(qb, kk, vv):
        s = jnp.einsum('bqhmd,bkhmd->bhmqk', qb, kk).astype(jnp.float32) * scale
        p = jax.nn.softmax(s, axis=-1)
        p = p[:, :, 0] - lam_full * p[:, :, 1]
        return jnp.einsum('bhqk,bkhe->bqhe', p.astype(vv.dtype), vv)

    def readout(o):
        n = o.shape[1]
        o = rmsnorm(o, subln_g) * (1.0 - lam_init)
        return o.reshape(B, n, D_MODEL) @ wo

    q, k, v = project(h)
    q, k = apply_rope_2d(q, cos, sin), apply_rope_2d(k, cos, sin)
    qc, kc, vc = project(hc)
    k_all = jnp.concatenate([kc, k], axis=1)
    v_all = jnp.concatenate([vc, v], axis=1)
    nb = L // Q_BLOCK
    q_blocks = jnp.moveaxis(q.reshape(B, nb, Q_BLOCK, DA_HEADS, 2, HEAD_DIM), 1, 0)
    o = lax.map(lambda qb: attend(qb, k_all, v_all), q_blocks)
    o = jnp.moveaxis(o, 0, 1).reshape(B, L, DA_HEADS, DA_VDIM)
    y = readout(o)
    yc = readout(attend(qc, kc, vc)) if need_ctx else None
    return yc, y


def short_conv(h, hc, w_in, conv_w, w_out, need_ctx):
    pad = SC_WIDTH // 2

    def run(t):
        n = t.shape[1]
        b_gate, c_gate, u = jnp.split(t @ w_in, 3, axis=-1)
        z = c_gate * u
        zp = jnp.pad(z, ((0, 0), (pad, pad), (0, 0)))
        conv = conv_w[0] * zp[:, 0:n]
        for j in range(1, SC_WIDTH):
            conv = conv + conv_w[j] * zp[:, j:j + n]
        return (b_gate * conv) @ w_out

    y = run(h)
    yc = run(hc) if need_ctx else None
    return yc, y


def window_attention(h, hc, wqkv, sinks, wo, cos, sin, need_ctx):
    B, L, _ = h.shape
    G, R = WA_KV_HEADS, WA_HEADS // WA_KV_HEADS
    scale = HEAD_DIM ** -0.5
    qw, kw = WA_HEADS * HEAD_DIM, WA_KV_HEADS * HEAD_DIM

    def project(t):
        n = t.shape[1]
        qkv = t @ wqkv
        q = qkv[..., :qw].reshape(B, n, G, R, HEAD_DIM)
        k = qkv[..., qw:qw + kw].reshape(B, n, G, HEAD_DIM)
        v = qkv[..., qw + kw:].reshape(B, n, G, HEAD_DIM)
        return q, k, v

    sink = sinks.reshape(G, R).astype(jnp.float32)
    q, k, v = project(h)
    q, k = apply_rope_2d(q, cos, sin), apply_rope_2d(k, cos, sin)
    qc, kc, vc = project(hc)

    nb = L // W_BLOCK
    qb = q.reshape(B, nb, W_BLOCK, G, R, HEAD_DIM)

    def band(a):
        ab = jnp.pad(a.reshape(B, nb, W_BLOCK, G, HEAD_DIM), ((0, 0), (1, 1), (0, 0), (0, 0), (0, 0)))
        return jnp.concatenate([ab[:, :-2], ab[:, 1:-1], ab[:, 2:]], axis=2)

    kb, vb = band(k), band(v)
    q_pos = jnp.arange(nb)[:, None] * W_BLOCK + jnp.arange(W_BLOCK)[None, :]
    k_pos = (jnp.arange(nb)[:, None] - 1) * W_BLOCK + jnp.arange(3 * W_BLOCK)[None, :]
    rel = k_pos[:, None, :] - q_pos[:, :, None]
    valid = (jnp.abs(rel) <= WINDOW) & (k_pos[:, None, :] >= 0) & (k_pos[:, None, :] < L)

    s_win = jnp.einsum('bnqgrd,bnkgd->bngrqk', qb, kb).astype(jnp.float32) * scale
    s_win = jnp.where(valid[None, :, None, None], s_win, -jnp.inf)
    s_ctx = jnp.einsum('bnqgrd,bcgd->bngrqc', qb, kc).astype(jnp.float32) * scale
    p_ctx, p_win = sink_softmax([s_ctx, s_win], sink[None, None, :, :, None, None])
    o = (jnp.einsum('bngrqc,bcgd->bnqgrd', p_ctx.astype(vc.dtype), vc)
         + jnp.einsum('bngrqk,bnkgd->bnqgrd', p_win.astype(vb.dtype), vb))
    y = o.reshape(B, L, D_MODEL) @ wo

    yc = None
    if need_ctx:
        n = hc.shape[1]
        s_c = jnp.einsum('bqgrd,bcgd->bgrqc', qc, kc).astype(jnp.float32) * scale
        (p_c,) = sink_softmax([s_c], sink[None, :, :, None, None])
        oc = jnp.einsum('bgrqc,bcgd->bqgrd', p_c.astype(vc.dtype), vc)
        yc = oc.reshape(B, n, D_MODEL) @ wo
    return yc, y


def gla_scan(q, k, v, logf, s0, with_output):
    B, NH, T, _ = q.shape
    nc = T // CHUNK

    def chunks(a):
        return jnp.moveaxis(a.astype(jnp.float32).reshape(B, NH, nc, CHUNK, a.shape[-1]), 2, 0)

    tri = jnp.tril(jnp.ones((CHUNK, CHUNK), dtype=bool))[:, :, None]

    def step(S, xs):
        qc, kc, vc, gc = xs
        b = jnp.cumsum(gc, axis=2)
        b_end = b[:, :, -1:, :]
        S_next = (jnp.exp(b_end)[:, :, 0, :, None] * S
                  + jnp.einsum('bhsk,bhsv->bhkv', kc * jnp.exp(b_end - b), vc))
        if not with_output:
            return S_next, None
        rel = jnp.where(tri, b[:, :, :, None, :] - b[:, :, None, :, :], -jnp.inf)
        att = jnp.sum(qc[:, :, :, None, :] * kc[:, :, None, :, :] * jnp.exp(rel), axis=-1)
        o = att @ vc + jnp.einsum('bhtk,bhkv->bhtv', qc * jnp.exp(b), S)
        return S_next, o

    S_fin, o = lax.scan(step, s0, (chunks(q), chunks(k), chunks(v), chunks(logf)))
    if not with_output:
        return S_fin, None
    return S_fin, jnp.moveaxis(o, 0, 2).reshape(B, NH, T, v.shape[-1])


def hgrn2(h, hc, w_in, lb_param, gnorm_g, wo, layer_idx, need_ctx):
    B = h.shape[0]
    H = HG_HEADS
    p = jax.nn.softmax(lb_param.astype(jnp.float32), axis=1)
    lb = (jnp.cumsum(p, axis=1) - p[:, :1])[:, layer_idx][:, None, None, :]

    def gates(t):
        q, f_fw, f_bw, i_in, og = jnp.split(t @ w_in, 5, axis=-1)
        q = jax.nn.silu(q)
        f_pre = jnp.stack([f_fw, f_bw], axis=0).astype(jnp.float32)
        logf = jnp.logaddexp(jnp.log(lb), jnp.log1p(-lb) + jax.nn.log_sigmoid(f_pre))
        k = -jnp.expm1(logf)
        return q, k, i_in, logf, og

    def to_heads(a):
        return a.reshape(a.shape[0], a.shape[1], H, -1).transpose(0, 2, 1, 3)

    def both(a_fw, a_bw):
        return jnp.concatenate([to_heads(a_fw), jnp.flip(to_heads(a_bw), axis=2)], axis=1)

    def merge(o):
        return (o[:, :H] + jnp.flip(o[:, H:], axis=2)).transpose(0, 2, 1, 3)

    def readout(o, og):
        n = o.shape[1]
        o = rmsnorm(o.astype(h.dtype), gnorm_g) * jax.nn.silu(og).reshape(B, n, H, HG_DV)
        return o.reshape(B, n, D_MODEL) @ wo

    qc, kc, vc, gc, ogc = gates(hc)
    s0 = jnp.zeros((B, 2 * H, HG_DK, HG_DV), jnp.float32)
    S_ctx, o_ctx = gla_scan(both(qc, qc), both(kc[0], kc[1]), both(vc, vc), both(gc[0], gc[1]),
                            s0, need_ctx)
    q, k, v, g, og = gates(h)
    _, o = gla_scan(both(q, q), both(k[0], k[1]), both(v, v), both(g[0], g[1]), S_ctx, True)
    y = readout(merge(o), og)
    yc = readout(merge(o_ctx), ogc) if need_ctx else None
    return yc, y


def setup_inputs(seed: int = 0) -> dict:
    key = jax.random.key(seed)
    keys = jax.random.split(key, 32)
    counter = [0]

    def nrm(shape, scale):
        k = keys[counter[0]]
        counter[0] += 1
        return jax.random.normal(k, shape, jnp.float32) * scale

    D, F = D_MODEL, D_FF
    nA, nB, nC, nD = n_layers_of(0), n_layers_of(1), n_layers_of(2), n_layers_of(3)
    wa_cols = (WA_HEADS + 2 * WA_KV_HEADS) * HEAD_DIM
    return {
        'x': nrm((BATCH, SEQ, D), 1.0),
        'c': nrm((BATCH, D), 1.0),
        'ctx': nrm((BATCH, CTX_LEN, D), 1.0),
        'c_ctx': nrm((D,), 1.0),
        'ada_w': nrm((DEPTH, D, N_MOD * D), 0.5 * D ** -0.5),
        'ada_b': nrm((DEPTH, N_MOD * D), 0.02),
        'norm1_g': 1.0 + nrm((DEPTH, D), 0.02),
        'norm2_g': 1.0 + nrm((DEPTH, D), 0.02),
        'ffn_w_in': nrm((DEPTH, D, 2 * F), D ** -0.5),
        'ffn_w_out': nrm((DEPTH, F, D), F ** -0.5),
        'final_g': 1.0 + nrm((D,), 0.02),
        'da_wqkv': nrm((nA, D, 3 * D), D ** -0.5),
        'da_lambda': nrm((nA, 4, HEAD_DIM), 0.1),
        'da_subln_g': 1.0 + nrm((nA, DA_VDIM), 0.02),
        'da_wo': nrm((nA, D, D), D ** -0.5),
        'sc_w_in': nrm((nB, D, 3 * D), D ** -0.5),
        'sc_conv_w': nrm((nB, SC_WIDTH, D), SC_WIDTH ** -0.5),
        'sc_w_out': nrm((nB, D, D), D ** -0.5),
        'wa_wqkv': nrm((nC, D, wa_cols), D ** -0.5),
        'wa_sinks': nrm((nC, WA_HEADS), 0.5),
        'wa_wo': nrm((nC, WA_HEADS * HEAD_DIM, D), D ** -0.5),
        'hg_w_in': nrm((nD, D, 5 * D), D ** -0.5),
        'hg_lb': nrm((2, DEPTH, HG_FDIM), 0.1),
        'hg_gnorm_g': 1.0 + nrm((nD, HG_DV), 0.02),
        'hg_wo': nrm((nD, D, D), D ** -0.5),
    }


def reference(x, c, ctx, c_ctx, ada_w, ada_b, norm1_g, norm2_g, ffn_w_in, ffn_w_out, final_g,
              da_wqkv, da_lambda, da_subln_g, da_wo, sc_w_in, sc_conv_w, sc_w_out,
              wa_wqkv, wa_sinks, wa_wo, hg_w_in, hg_lb, hg_gnorm_g, hg_wo):
    B, L, _ = x.shape
    cos, sin = rope_tables(L)
    cos, sin = cos.astype(x.dtype), sin.astype(x.dtype)
    silu_c = jax.nn.silu(c)
    silu_cc = jax.nn.silu(c_ctx)[None, :]
    for i in range(DEPTH):
        mixer, slot = i % N_MIXERS, i // N_MIXERS
        need_ctx = i < DEPTH - 1
        mod = (silu_c @ ada_w[i] + ada_b[i])[:, None, :]
        mod_c = (silu_cc @ ada_w[i] + ada_b[i])[:, None, :]
        sh1, sc1, g1, sh2, sc2, g2 = jnp.split(mod, N_MOD, axis=-1)
        csh1, csc1, cg1, csh2, csc2, cg2 = jnp.split(mod_c, N_MOD, axis=-1)
        h = modulate(rmsnorm(x, norm1_g[i]), sh1, sc1)
        hc = modulate(rmsnorm(ctx, norm1_g[i]), csh1, csc1)
        if mixer == 0:
            yc, y = diff_attention(h, hc, da_wqkv[slot], da_lambda[slot], da_subln_g[slot],
                                   da_wo[slot], cos, sin, i, need_ctx)
        elif mixer == 1:
            yc, y = short_conv(h, hc, sc_w_in[slot], sc_conv_w[slot], sc_w_out[slot], need_ctx)
        elif mixer == 2:
            yc, y = window_attention(h, hc, wa_wqkv[slot], wa_sinks[slot], wa_wo[slot],
                                     cos, sin, need_ctx)
        else:
            yc, y = hgrn2(h, hc, hg_w_in[slot], hg_lb, hg_gnorm_g[slot], hg_wo[slot], i, need_ctx)
        x = x + g1 * y
        x = x + g2 * swiglu(modulate(rmsnorm(x, norm2_g[i]), sh2, sc2), ffn_w_in[i], ffn_w_out[i])
        if need_ctx:
            ctx = ctx + cg1 * yc
            ctx = ctx + cg2 * swiglu(modulate(rmsnorm(ctx, norm2_g[i]), csh2, csc2),
                                     ffn_w_in[i], ffn_w_out[i])
    return rmsnorm(x, final_g)
```

```python
import functools
import math

import numpy as np
import jax
import jax.numpy as jnp
from jax import lax
from jax.experimental import pallas as pl
from jax.experimental.pallas import tpu as pltpu

F32 = jnp.float32
BF16 = jnp.bfloat16

HEAD_DIM = 64
GRID_W = 64
ROPE_BASE = 10000.0
RMS_EPS = 1e-6
N_MOD = 6
WINDOW = 128
WA_KV_HEADS = 4
SC_WIDTH = 3
LANES = 128
NEG_BIG = -1e30

TM = 256
HG_CHUNK = 128
DA_TQ = 256
DA_TK = 1024
WA_TQ = 256
FFN_CHUNK = 256
VMEM_LIMIT = 56 * 1024 * 1024

NT_DIMS = (((1,), (1,)), ((), ()))


def _cparams(n_axes):
    return pltpu.CompilerParams(
        dimension_semantics=("arbitrary",) * n_axes, vmem_limit_bytes=VMEM_LIMIT)


def _const_spec(shape):
    nd = len(shape)
    return pl.BlockSpec(shape, lambda *_: (0,) * nd, pipeline_mode=pl.Buffered(1))


def _is_ctx_rows(tile_idx, tm, lc):
    row = tile_idx * tm + lax.broadcasted_iota(jnp.int32, (tm, 1), 0)
    return row < lc


def _pick(mod_ref, k, is_ctx):
    return jnp.where(is_ctx, mod_ref[k:k + 1, :], mod_ref[N_MOD + k:N_MOD + k + 1, :])


def _rms(x):
    return x * lax.rsqrt(jnp.mean(x * x, axis=-1, keepdims=True) + RMS_EPS)


def _norm_mod(x, g, shift, scale):
    return (_rms(x) * g) * (1.0 + scale) + shift


def _silu(x):
    return x * (1.0 / (1.0 + jnp.exp(-x)))


def _rope(chunk, cs, sn, first_half):
    partner = jnp.where(first_half, pltpu.roll(chunk, LANES - 16, 1), pltpu.roll(chunk, 16, 1))
    return chunk * cs + partner * sn


def _first_half_lanes():
    lane = lax.broadcasted_iota(jnp.int32, (1, LANES), 1)
    return (lane & 31) < 16


def _mod_kernel(cc_ref, w_ref, b_ref, o_ref):
    a = _silu(cc_ref[...])
    o_ref[0] = jnp.dot(a, w_ref[0], precision=lax.Precision.HIGHEST,
                       preferred_element_type=F32) + b_ref[0]


def _mod_call(cc, ada_w, ada_b):
    depth, d, nd = ada_w.shape
    tn = 1536
    assert nd % tn == 0
    return pl.pallas_call(
        _mod_kernel,
        grid=(depth, nd // tn),
        in_specs=[pl.BlockSpec((8, d), lambda i, j: (0, 0)),
                  pl.BlockSpec((1, d, tn), lambda i, j: (i, 0, j)),
                  pl.BlockSpec((1, 1, tn), lambda i, j: (i, 0, j))],
        out_specs=pl.BlockSpec((1, 8, tn), lambda i, j: (i, 0, j)),
        out_shape=jax.ShapeDtypeStruct((depth, 8, nd), F32),
        compiler_params=_cparams(2),
        name="adaln_mod",
    )(cc, ada_w, ada_b.reshape(depth, 1, nd))


def _proj_da_kernel(x_ref, g_ref, mod_ref, cs_ref, sn_ref, wqk_ref, wvt_ref,
                    q_ref, k_ref, vt_ref, *, tm, lc, d):
    is_ctx = _is_ctx_rows(pl.program_id(0), tm, lc)
    h = _norm_mod(x_ref[...], g_ref[...], _pick(mod_ref, 0, is_ctx), _pick(mod_ref, 1, is_ctx))
    hb = h.astype(BF16)
    qk = jnp.dot(hb, wqk_ref[...], preferred_element_type=F32)
    cs, sn, first = cs_ref[...], sn_ref[...], _first_half_lanes()
    nq = d // LANES
    for j in range(2 * nq):
        r = _rope(qk[:, j * LANES:(j + 1) * LANES], cs, sn, first)
        if j < nq:
            q_ref[:, j * LANES:(j + 1) * LANES] = (r * (HEAD_DIM ** -0.5)).astype(BF16)
        else:
            k_ref[:, (j - nq) * LANES:(j - nq + 1) * LANES] = r.astype(BF16)
    vt = lax.dot_general(wvt_ref[...], hb, NT_DIMS, preferred_element_type=F32)
    vt_ref[0] = vt.astype(BF16)


def _proj_da_call(xs, g, mod12, cs, sn, wqk, wvt, lc):
    n, d = xs.shape
    nt = n // TM
    row = lambda i: (i, 0)
    return pl.pallas_call(
        functools.partial(_proj_da_kernel, tm=TM, lc=lc, d=d),
        grid=(nt,),
        in_specs=[pl.BlockSpec((TM, d), row), _const_spec((1, d)), _const_spec((2 * N_MOD, d)),
                  pl.BlockSpec((TM, LANES), row), pl.BlockSpec((TM, LANES), row),
                  _const_spec(wqk.shape), _const_spec(wvt.shape)],
        out_specs=[pl.BlockSpec((TM, d), row), pl.BlockSpec((TM, d), row),
                   pl.BlockSpec((1, d, TM), lambda i: (i, 0, 0))],
        out_shape=[jax.ShapeDtypeStruct((n, d), BF16), jax.ShapeDtypeStruct((n, d), BF16),
                   jax.ShapeDtypeStruct((nt, d, TM), BF16)],
        compiler_params=_cparams(1),
        name="proj_diff_attn",
    )(xs, g, mod12, cs, sn, wqk, wvt)


def _proj_sc_kernel(x_ref, g_ref, mod_ref, w_ref, b_ref, z_ref, *, tm, lc, d):
    is_ctx = _is_ctx_rows(pl.program_id(0), tm, lc)
    h = _norm_mod(x_ref[...], g_ref[...], _pick(mod_ref, 0, is_ctx), _pick(mod_ref, 1, is_ctx))
    y = jnp.dot(h.astype(BF16), w_ref[...], preferred_element_type=F32)
    b_ref[...] = y[:, :d].astype(BF16)
    z_ref[...] = y[:, d:2 * d] * y[:, 2 * d:]


def _proj_sc_call(xs, g, mod12, w, lc):
    n, d = xs.shape
    row = lambda i: (i, 0)
    return pl.pallas_call(
        functools.partial(_proj_sc_kernel, tm=TM, lc=lc, d=d),
        grid=(n // TM,),
        in_specs=[pl.BlockSpec((TM, d), row), _const_spec((1, d)), _const_spec((2 * N_MOD, d)),
                  _const_spec(w.shape)],
        out_specs=[pl.BlockSpec((TM, d), row), pl.BlockSpec((TM, d), row)],
        out_shape=[jax.ShapeDtypeStruct((n, d), BF16), jax.ShapeDtypeStruct((n, d), F32)],
        compiler_params=_cparams(1),
        name="proj_short_conv",
    )(xs, g, mod12, w)


def _proj_wa_kernel(x_ref, g_ref, mod_ref, cs_ref, sn_ref, w_ref, q_ref, k_ref, v_ref,
                    *, tm, lc, d, kw):
    is_ctx = _is_ctx_rows(pl.program_id(0), tm, lc)
    h = _norm_mod(x_ref[...], g_ref[...], _pick(mod_ref, 0, is_ctx), _pick(mod_ref, 1, is_ctx))
    y = jnp.dot(h.astype(BF16), w_ref[...], preferred_element_type=F32)
    cs, sn, first = cs_ref[...], sn_ref[...], _first_half_lanes()
    for j in range(d // LANES):
        r = _rope(y[:, j * LANES:(j + 1) * LANES], cs, sn, first)
        q_ref[:, j * LANES:(j + 1) * LANES] = (r * (HEAD_DIM ** -0.5)).astype(BF16)
    for j in range(kw // LANES):
        r = _rope(y[:, d + j * LANES:d + (j + 1) * LANES], cs, sn, first)
        k_ref[:, j * LANES:(j + 1) * LANES] = r.astype(BF16)
    v_ref[...] = y[:, d + kw:].astype(BF16)


def _proj_wa_call(xs, g, mod12, cs, sn, w, lc, kw):
    n, d = xs.shape
    row = lambda i: (i, 0)
    return pl.pallas_call(
        functools.partial(_proj_wa_kernel, tm=TM, lc=lc, d=d, kw=kw),
        grid=(n // TM,),
        in_specs=[pl.BlockSpec((TM, d), row), _const_spec((1, d)), _const_spec((2 * N_MOD, d)),
                  pl.BlockSpec((TM, LANES), row), pl.BlockSpec((TM, LANES), row),
                  _const_spec(w.shape)],
        out_specs=[pl.BlockSpec((TM, d), row), pl.BlockSpec((TM, kw), row),
                   pl.BlockSpec((TM, kw), row)],
        out_shape=[jax.ShapeDtypeStruct((n, d), BF16), jax.ShapeDtypeStruct((n, kw), BF16),
                   jax.ShapeDtypeStruct((n, kw), BF16)],
        compiler_params=_cparams(1),
        name="proj_window_attn",
    )(xs, g, mod12, cs, sn, w)


def _log1p(x):
    return jnp.log1p(x)


def _proj_hg_kernel(x_ref, g_ref, mod_ref, lla_ref, l1m_ref, w_ref, wvt_ref,
                    q_ref, gate_ref, v_ref, vt_ref, og_ref, *, tm, lc, d):
    is_ctx = _is_ctx_rows(pl.program_id(0), tm, lc)
    h = _norm_mod(x_ref[...], g_ref[...], _pick(mod_ref, 0, is_ctx), _pick(mod_ref, 1, is_ctx))
    hb = h.astype(BF16)
    y = jnp.dot(hb, w_ref[...], preferred_element_type=F32)
    q_ref[...] = _silu(y[:, :d]).astype(BF16)
    for dr in range(2):
        f = y[:, (1 + dr) * d:(2 + dr) * d]
        log_sig = jnp.minimum(f, 0.0) - _log1p(jnp.exp(-jnp.abs(f)))
        a = lla_ref[dr:dr + 1, :]
        c = l1m_ref[dr:dr + 1, :] + log_sig
        gate_ref[dr] = jnp.maximum(a, c) + _log1p(jnp.exp(-jnp.abs(a - c)))
    v_ref[...] = y[:, 3 * d:4 * d].astype(BF16)
    og_ref[...] = _silu(y[:, 4 * d:]).astype(BF16)
    vt = lax.dot_general(wvt_ref[...], hb, NT_DIMS, preferred_element_type=F32)
    vt_ref[0] = vt.astype(BF16)


def _proj_hg_call(xs, g, mod12, lla, l1m, w, wvt, lc):
    n, d = xs.shape
    nt = n // TM
    row = lambda i: (i, 0)
    return pl.pallas_call(
        functools.partial(_proj_hg_kernel, tm=TM, lc=lc, d=d),
        grid=(nt,),
        in_specs=[pl.BlockSpec((TM, d), row), _const_spec((1, d)), _const_spec((2 * N_MOD, d)),
                  _const_spec((2, d)), _const_spec((2, d)),
                  _const_spec(w.shape), _const_spec(wvt.shape)],
        out_specs=[pl.BlockSpec((TM, d), row), pl.BlockSpec((2, TM, d), lambda i: (0, i, 0)),
                   pl.BlockSpec((TM, d), row), pl.BlockSpec((1, d, TM), lambda i: (i, 0, 0)),
                   pl.BlockSpec((TM, d), row)],
        out_shape=[jax.ShapeDtypeStruct((n, d), BF16), jax.ShapeDtypeStruct((2, n, d), F32),
                   jax.ShapeDtypeStruct((n, d), BF16), jax.ShapeDtypeStruct((nt, d, TM), BF16),
                   jax.ShapeDtypeStruct((n, d), BF16)],
        compiler_params=_cparams(1),
        name="proj_hgrn",
    )(xs, g, mod12, lla, l1m, w, wvt)


def _diff_attn_kernel(lam_ref, q_ref, k_ref, vt_ref, g_ref, o_ref, m_ref, l_ref, acc_ref,
                      *, tq, tk, lc, n, post_scale):
    qi = pl.program_id(1)
    q = q_ref[...]
    lane = lax.broadcasted_iota(jnp.int32, (1, LANES), 1)
    zero = jnp.zeros_like(q)
    qm = (jnp.where(lane < HEAD_DIM, q, zero), jnp.where(lane >= HEAD_DIM, q, zero))
    sub = vt_ref.shape[2]

    def scores(kc, m):
        return lax.dot_general(kc, qm[m], NT_DIMS, preferred_element_type=F32)

    def pv(slab0, n_slabs, p):
        out = None
        for j in range(n_slabs):
            t = jnp.dot(vt_ref[slab0 + j], p[j * sub:(j + 1) * sub, :],
                        preferred_element_type=F32)
            out = t if out is None else out + t
        return out

    kc = k_ref[0:lc, :]
    for m in range(2):
        s = scores(kc, m)
        mx = jnp.max(s, axis=0, keepdims=True)
        p = jnp.exp(s - mx)
        m_ref[m] = mx
        l_ref[m] = jnp.sum(p, axis=0, keepdims=True)
        acc_ref[m] = pv(0, lc // sub, p.astype(BF16))

    n_chunks = jnp.where(qi * tq < lc, 0, (n - lc) // tk)

    def body(c, carry):
        off = pl.multiple_of(lc + c * tk, LANES)
        kc = k_ref[pl.ds(off, tk), :]
        slab0 = (lc + c * tk) // sub
        for m in range(2):
            s = scores(kc, m)
            m_old = m_ref[m]
            m_new = jnp.maximum(m_old, jnp.max(s, axis=0, keepdims=True))
            alpha = jnp.exp(m_old - m_new)
            p = jnp.exp(s - m_new)
            m_ref[m] = m_new
            l_ref[m] = alpha * l_ref[m] + jnp.sum(p, axis=0, keepdims=True)
            acc_ref[m] = alpha * acc_ref[m] + pv(slab0, tk // sub, p.astype(BF16))
        return carry

    lax.fori_loop(0, n_chunks, body, 0)

    o = acc_ref[0] * (1.0 / l_ref[0]) - lam_ref[0] * (acc_ref[1] * (1.0 / l_ref[1]))
    o = o * lax.rsqrt(jnp.mean(o * o, axis=0, keepdims=True) + RMS_EPS)
    o = o * (g_ref[...] * post_scale)
    o_ref[...] = o.T.astype(BF16)


def _diff_attn_call(lam, q, k, vt3, subln_g, lc, post_scale):
    n, d = q.shape
    nh = d // LANES
    nslab, _, sub = vt3.shape
    tk = min(DA_TK, n - lc)
    assert lc % DA_TQ == 0 and n % DA_TQ == 0 and (n - lc) % tk == 0 and tk % sub == 0 and lc % sub == 0
    return pl.pallas_call(
        functools.partial(_diff_attn_kernel, tq=DA_TQ, tk=tk, lc=lc, n=n, post_scale=post_scale),
        grid=(nh, n // DA_TQ),
        in_specs=[pl.BlockSpec(memory_space=pltpu.SMEM),
                  pl.BlockSpec((DA_TQ, LANES), lambda h, i: (i, h)),
                  pl.BlockSpec((n, LANES), lambda h, i: (0, h)),
                  pl.BlockSpec((nslab, LANES, sub), lambda h, i: (0, h, 0)),
                  pl.BlockSpec((LANES, 1), lambda h, i: (0, 0))],
        out_specs=pl.BlockSpec((DA_TQ, LANES), lambda h, i: (i, h)),
        out_shape=jax.ShapeDtypeStruct((n, d), BF16),
        scratch_shapes=[pltpu.VMEM((2, 1, DA_TQ), F32), pltpu.VMEM((2, 1, DA_TQ), F32),
                        pltpu.VMEM((2, LANES, DA_TQ), F32)],
        compiler_params=_cparams(2),
        name="diff_attention",
    )(lam, q, k, vt3, subln_g)


def _win_attn_kernel(sink_ref, q_ref, kc_ref, kp_ref, ko_ref, kn_ref,
                     vc_ref, vp_ref, vo_ref, vn_ref, o_ref, *, tq, lc, n, rep):
    j = pl.program_id(0)
    g = pl.program_id(1)
    kall = jnp.concatenate([kc_ref[...], kp_ref[...], ko_ref[...], kn_ref[...]], axis=0)
    vall = jnp.concatenate([vc_ref[...], vp_ref[...], vo_ref[...], vn_ref[...]], axis=0)
    nk = lc + tq + 2 * WINDOW
    qpos = j * tq + lax.broadcasted_iota(jnp.int32, (tq, 1), 0)
    col = lax.broadcasted_iota(jnp.int32, (1, nk), 1)
    kpos = j * tq - WINDOW + (col - lc)
    rel = kpos - qpos
    win_ok = (rel <= WINDOW) & (rel >= -WINDOW) & (kpos >= lc) & (kpos < n) & (qpos >= lc)
    valid = (col < lc) | win_ok
    lane = lax.broadcasted_iota(jnp.int32, (1, LANES), 1)
    low = lane < HEAD_DIM
    for c in range(rep // 2):
        qc = q_ref[:, c * LANES:(c + 1) * LANES]
        zero = jnp.zeros_like(qc)
        outs = []
        for half in range(2):
            qm = jnp.where(low if half == 0 else ~low, qc, zero)
            s = lax.dot_general(qm, kall, NT_DIMS, preferred_element_type=F32)
            s = jnp.where(valid, s, NEG_BIG)
            sink = sink_ref[g * rep + 2 * c + half]
            mx = jnp.maximum(jnp.max(s, axis=-1, keepdims=True), sink)
            e = jnp.exp(s - mx)
            den = jnp.exp(sink - mx) + jnp.sum(e, axis=-1, keepdims=True)
            p = (e * (1.0 / den)).astype(BF16)
            outs.append(jnp.dot(p, vall, preferred_element_type=F32))
        o_ref[:, c * LANES:(c + 1) * LANES] = jnp.where(low, outs[0], outs[1]).astype(BF16)


def _win_attn_call(sinks, q, k2, v2, lc):
    n, d = q.shape
    ng = WA_KV_HEADS
    rep = d // HEAD_DIM // ng
    gw = rep * HEAD_DIM
    tq = WA_TQ
    assert n % tq == 0 and tq % WINDOW == 0 and lc % WINDOW == 0 and rep % 2 == 0
    r = tq // WINDOW
    last = n // WINDOW - 1
    kv_specs = [pl.BlockSpec((lc, LANES), lambda j, g: (0, g)),
                pl.BlockSpec((WINDOW, LANES), lambda j, g: (jnp.maximum(j * r - 1, 0), g)),
                pl.BlockSpec((tq, LANES), lambda j, g: (j, g)),
                pl.BlockSpec((WINDOW, LANES), lambda j, g: (jnp.minimum((j + 1) * r, last), g))]
    return pl.pallas_call(
        functools.partial(_win_attn_kernel, tq=tq, lc=lc, n=n, rep=rep),
        grid=(n // tq, ng),
        in_specs=[pl.BlockSpec(memory_space=pltpu.SMEM),
                  pl.BlockSpec((tq, gw), lambda j, g: (j, g))] + kv_specs + kv_specs,
        out_specs=pl.BlockSpec((tq, gw), lambda j, g: (j, g)),
        out_shape=jax.ShapeDtypeStruct((n, d), BF16),
        compiler_params=_cparams(2),
        name="window_attention",
    )(sinks, q, k2, k2, k2, k2, v2, v2, v2, v2)


def _hgrn_consts(chunk):
    n_lv = int(math.log2(chunk))
    walls, masks_all = [], []
    for direction in range(2):
        tau = np.arange(chunk) if direction == 0 else chunk - 1 - np.arange(chunk)
        tr, tc = tau[:, None], tau[None, :]
        blocks = [tc <= tr, tc > tr]
        masks = []
        for lv in range(n_lv):
            h = chunk >> (lv + 1)
            mid = (tr // (2 * h)) * 2 * h + h - 1
            second_r = (tr % (2 * h)) >= h
            second_c = (tc % (2 * h)) >= h
            blocks.append(np.where(second_r, (tc > mid) & (tc <= tr), (tc > tr) & (tc <= mid)))
            masks.append(second_r & ~second_c & ((tr // (2 * h)) == (tc // (2 * h))))
        masks.append(tr == tc)
        w = np.concatenate(blocks, axis=0).astype(np.float32)
        walls.append(np.concatenate([w, w], axis=1))
        masks_all.append(np.stack(masks).astype(np.float32))
    return np.stack(walls), np.stack(masks_all)


def _hgrn_kernel(wall_ref, mask_ref, q_ref, g_ref, v_ref, vt_ref, o_ref, st_ref, *, chunk, n_lv):
    @pl.when(pl.program_id(2) == 0)
    def _():
        st_ref[...] = jnp.zeros_like(st_ref)

    g = g_ref[0]
    g_hi = g.astype(BF16)
    g_lo = (g - g_hi.astype(F32)).astype(BF16)
    sums = jnp.dot(wall_ref[0], jnp.concatenate([g_hi, g_lo], axis=0),
                   preferred_element_type=F32)
    q = q_ref[...].astype(F32)
    k = 1.0 - jnp.exp(g)
    v = v_ref[...]
    st = st_ref[...]

    qb = (q * jnp.exp(sums[0:chunk])).astype(BF16)
    o = lax.dot_general(qb, st.astype(BF16), NT_DIMS, preferred_element_type=F32)

    att = lax.dot_general(q_ref[...], k.astype(BF16), NT_DIMS,
                          preferred_element_type=F32) * mask_ref[0, n_lv]
    for lv in range(n_lv):
        e = jnp.exp(sums[(2 + lv) * chunk:(3 + lv) * chunk])
        a = lax.dot_general((q * e).astype(BF16), (k * e).astype(BF16), NT_DIMS,
                            preferred_element_type=F32)
        att = att + a * mask_ref[0, lv]
    o_ref[0] = o + jnp.dot(att.astype(BF16), v, preferred_element_type=F32)

    ks = (k * jnp.exp(sums[chunk:2 * chunk])).astype(BF16)
    decay = jnp.exp(jnp.sum(g, axis=0, keepdims=True))
    st_ref[...] = st * decay + jnp.dot(vt_ref[0], ks, preferred_element_type=F32)


def _hgrn_call(qh, gates, v, vt3, lc):
    n, d = qh.shape
    c = HG_CHUNK
    nh = d // LANES
    nc, ncc = n // c, lc // c
    per_slab = vt3.shape[2] // c
    assert n % c == 0 and lc % c == 0 and vt3.shape[2] % c == 0
    n_lv = int(math.log2(c))
    wall_np, mask_np = _hgrn_consts(c)
    wall = jnp.asarray(wall_np, dtype=BF16)
    masks = jnp.asarray(mask_np, dtype=F32)

    def blk(dr, s):
        back = jnp.where(s < ncc, ncc - 1 - s, nc - 1 - (s - ncc))
        return jnp.where(dr == 0, s, back)

    return pl.pallas_call(
        functools.partial(_hgrn_kernel, chunk=c, n_lv=n_lv),
        grid=(2, nh, nc),
        in_specs=[pl.BlockSpec((1,) + wall_np.shape[1:], lambda dr, h, s: (dr, 0, 0)),
                  pl.BlockSpec((1,) + mask_np.shape[1:], lambda dr, h, s: (dr, 0, 0, 0)),
                  pl.BlockSpec((c, LANES), lambda dr, h, s: (blk(dr, s), h)),
                  pl.BlockSpec((1, c, LANES), lambda dr, h, s: (dr, blk(dr, s), h)),
                  pl.BlockSpec((c, LANES), lambda dr, h, s: (blk(dr, s), h)),
                  pl.BlockSpec((1, LANES, c),
                               lambda dr, h, s: (blk(dr, s) // per_slab, h, blk(dr, s) % per_slab))],
        out_specs=pl.BlockSpec((1, c, LANES), lambda dr, h, s: (dr, blk(dr, s), h)),
        out_shape=jax.ShapeDtypeStruct((2, n, d), F32),
        scratch_shapes=[pltpu.VMEM((LANES, LANES), F32)],
        compiler_params=_cparams(3),
        name="hgrn_scan",
    )(wall, masks, qh, gates, v, vt3)


def _post_kernel(*refs, mode, final, tm, lc, n, d, f, tile0):
    it = iter(refs)
    x_ref, mod_ref, g2_ref, wo_ref, win_ref, wout_ref = (next(it) for _ in range(6))
    if mode == "plain":
        a_ref = next(it)
    elif mode == "conv":
        b_ref, z_ref, zp_ref, zn_ref, cw_ref = (next(it) for _ in range(5))
    else:
        ofw_ref, obw_ref, og_ref, gn_ref = (next(it) for _ in range(4))
    fin_ref = next(it) if final else None
    o_ref = next(it)

    ti = pl.program_id(0) + tile0
    is_ctx = _is_ctx_rows(ti, tm, lc)
    if mode == "plain":
        a = a_ref[...]
    elif mode == "conv":
        z = z_ref[...]
        rl = lax.broadcasted_iota(jnp.int32, (tm, 1), 0)
        row = ti * tm + rl
        z_prev = jnp.where(rl == 0, zp_ref[7:8, :], pltpu.roll(z, 1, 0))
        z_next = jnp.where(rl == tm - 1, zn_ref[0:1, :], pltpu.roll(z, tm - 1, 0))
        z_prev = jnp.where((row == 0) | (row == lc), 0.0, z_prev)
        z_next = jnp.where((row == lc - 1) | (row == n - 1), 0.0, z_next)
        conv = cw_ref[0:1, :] * z_prev + cw_ref[1:2, :] * z + cw_ref[2:3, :] * z_next
        a = (b_ref[...].astype(F32) * conv).astype(BF16)
    else:
        gn = gn_ref[...]
        heads = []
        for hh in range(d // LANES):
            sl = slice(hh * LANES, (hh + 1) * LANES)
            oh = _rms(ofw_ref[0, :, sl] + obw_ref[0, :, sl])
            heads.append((oh * gn * og_ref[:, sl].astype(F32)).astype(BF16))
        a = jnp.concatenate(heads, axis=1)

    y = jnp.dot(a, wo_ref[...], preferred_element_type=F32)
    x = x_ref[...] + _pick(mod_ref, 2, is_ctx) * y
    h = _norm_mod(x, g2_ref[...], _pick(mod_ref, 3, is_ctx), _pick(mod_ref, 4, is_ctx)).astype(BF16)
    acc = jnp.zeros((tm, d), F32)
    for c0 in range(0, f, FFN_CHUNK):
        u = jnp.dot(h, win_ref[:, c0:c0 + FFN_CHUNK], preferred_element_type=F32)
        w = jnp.dot(h, win_ref[:, f + c0:f + c0 + FFN_CHUNK], preferred_element_type=F32)
        act = (_silu(u) * w).astype(BF16)
        acc = acc + jnp.dot(act, wout_ref[c0:c0 + FFN_CHUNK, :], preferred_element_type=F32)
    x = x + _pick(mod_ref, 5, is_ctx) * acc
    if final:
        x = _rms(x) * fin_ref[...]
    o_ref[...] = x


def _post_call(xs, mod12, g2, wo, win, wout, mixer_inputs, mode, lc, final_g=None):
    n, d = xs.shape
    f = wout.shape[0]
    assert f % FFN_CHUNK == 0 and lc % TM == 0
    final = final_g is not None
    tile0 = lc // TM if final else 0
    nt = n // TM - tile0
    row = lambda i: (i + tile0, 0)
    in_specs = [pl.BlockSpec((TM, d), row), _const_spec((2 * N_MOD, d)), _const_spec((1, d)),
                _const_spec(wo.shape), _const_spec(win.shape), _const_spec(wout.shape)]
    args = [xs, mod12, g2, wo, win, wout]
    if mode == "plain":
        in_specs += [pl.BlockSpec((TM, d), row)]
    elif mode == "conv":
        r8 = TM // 8
        last8 = n // 8 - 1
        in_specs += [pl.BlockSpec((TM, d), row), pl.BlockSpec((TM, d), row),
                     pl.BlockSpec((8, d), lambda i: (jnp.maximum((i + tile0) * r8 - 1, 0), 0)),
                     pl.BlockSpec((8, d), lambda i: (jnp.minimum((i + tile0 + 1) * r8, last8), 0)),
                     _const_spec((SC_WIDTH, d))]
    else:
        in_specs += [pl.BlockSpec((1, TM, d), lambda i: (0, i + tile0, 0)),
                     pl.BlockSpec((1, TM, d), lambda i: (1, i + tile0, 0)),
                     pl.BlockSpec((TM, d), row), _const_spec((1, LANES))]
    args += list(mixer_inputs)
    if final:
        in_specs += [_const_spec((1, d))]
        args += [final_g]
    return pl.pallas_call(
        functools.partial(_post_kernel, mode=mode, final=final, tm=TM, lc=lc, n=n, d=d, f=f,
                          tile0=tile0),
        grid=(nt,),
        in_specs=in_specs,
        out_specs=pl.BlockSpec((TM, d), lambda i: (i, 0)),
        out_shape=jax.ShapeDtypeStruct((nt * TM, d), F32),
        compiler_params=_cparams(1),
        name="post_" + mode + ("_final" if final else ""),
    )(*args)


def _rope_tables(l, lc):
    rows = l // GRID_W
    r = jnp.repeat(jnp.arange(rows), GRID_W).astype(F32)
    c = jnp.tile(jnp.arange(GRID_W), rows).astype(F32)
    half = HEAD_DIM // 2
    inv_freq = 1.0 / (ROPE_BASE ** (jnp.arange(0, half, 2, dtype=F32) / half))
    ar, ac = r[:, None] * inv_freq, c[:, None] * inv_freq
    cs = jnp.concatenate([jnp.cos(ar), jnp.cos(ar), jnp.cos(ac), jnp.cos(ac)], axis=-1)
    sn = jnp.concatenate([-jnp.sin(ar), jnp.sin(ar), -jnp.sin(ac), jnp.sin(ac)], axis=-1)
    cs = jnp.concatenate([jnp.ones((lc, HEAD_DIM), F32), cs], axis=0)
    sn = jnp.concatenate([jnp.zeros((lc, HEAD_DIM), F32), sn], axis=0)
    return jnp.tile(cs, (1, 2)), jnp.tile(sn, (1, 2))


def kernel(x, c, ctx, c_ctx, ada_w, ada_b, norm1_g, norm2_g, ffn_w_in, ffn_w_out, final_g,
           da_wqkv, da_lambda, da_subln_g, da_wo, sc_w_in, sc_conv_w, sc_w_out,
           wa_wqkv, wa_sinks, wa_wo, hg_w_in, hg_lb, hg_gnorm_g, hg_wo):
    b, l, d = x.shape
    lc = ctx.shape[1]
    depth = ada_w.shape[0]
    assert b == 1 and d % LANES == 0 and lc % TM == 0 and l % TM == 0
    n = lc + l
    xs = jnp.concatenate([ctx[0], x[0]], axis=0)

    cc = jnp.zeros((8, d), F32).at[0].set(c_ctx).at[1].set(c[0])
    mods = _mod_call(cc, ada_w, ada_b)
    cs, sn = _rope_tables(l, lc)

    n_mix = 4
    for i in range(depth):
        mixer, slot = i % n_mix, i // n_mix
        last = i == depth - 1
        mod12 = jnp.concatenate([mods[i, 0].reshape(N_MOD, d), mods[i, 1].reshape(N_MOD, d)], axis=0)
        g1 = norm1_g[i].reshape(1, d)
        g2 = norm2_g[i].reshape(1, d)
        win = ffn_w_in[i].astype(BF16)
        wout = ffn_w_out[i].astype(BF16)
        fin = final_g.reshape(1, d) if last else None

        if mixer == 0:
            w = da_wqkv[slot]
            lam = da_lambda[slot].astype(F32)
            lam_init = 0.8 - 0.6 * math.exp(-0.3 * i)
            lam_full = (jnp.exp(jnp.sum(lam[0] * lam[1])) - jnp.exp(jnp.sum(lam[2] * lam[3]))
                        + lam_init).reshape(1)
            q, k, vt3 = _proj_da_call(xs, g1, mod12, cs, sn, w[:, :2 * d].astype(BF16),
                                      w[:, 2 * d:].T.astype(BF16), lc)
            a = _diff_attn_call(lam_full, q, k, vt3, da_subln_g[slot].reshape(LANES, 1), lc,
                                1.0 - lam_init)
            xs = _post_call(xs, mod12, g2, da_wo[slot].astype(BF16), win, wout, [a], "plain", lc, fin)
        elif mixer == 1:
            bg, z = _proj_sc_call(xs, g1, mod12, sc_w_in[slot].astype(BF16), lc)
            xs = _post_call(xs, mod12, g2, sc_w_out[slot].astype(BF16), win, wout,
                            [bg, z, z, z, sc_conv_w[slot]], "conv", lc, fin)
        elif mixer == 2:
            w = wa_wqkv[slot]
            kw = WA_KV_HEADS * HEAD_DIM
            dup = lambda m: jnp.tile(m.reshape(d, WA_KV_HEADS, 1, HEAD_DIM), (1, 1, 2, 1)).reshape(d, 2 * kw)
            w2 = jnp.concatenate([w[:, :d], dup(w[:, d:d + kw]), dup(w[:, d + kw:])], axis=1)
            q, k2, v2 = _proj_wa_call(xs, g1, mod12, cs, sn, w2.astype(BF16), lc, 2 * kw)
            a = _win_attn_call(wa_sinks[slot].astype(F32), q, k2, v2, lc)
            xs = _post_call(xs, mod12, g2, wa_wo[slot].astype(BF16), win, wout, [a], "plain", lc, fin)
        else:
            w = hg_w_in[slot]
            p = jax.nn.softmax(hg_lb.astype(F32), axis=1)
            lb = (jnp.cumsum(p, axis=1) - p[:, :1])[:, i]
            qh, gates, v, vt3, og = _proj_hg_call(xs, g1, mod12, jnp.log(lb), jnp.log1p(-lb),
                                                  w.astype(BF16),
                                                  w[:, 3 * d:4 * d].T.astype(BF16), lc)
            o2 = _hgrn_call(qh, gates, v, vt3, lc)
            xs = _post_call(xs, mod12, g2, hg_wo[slot].astype(BF16), win, wout,
                            [o2, o2, og, hg_gnorm_g[slot].reshape(1, LANES)], "hgrn", lc, fin)
    return xs.reshape(1, l, d)
```

```python
import functools
import math

import numpy as np
import jax
import jax.numpy as jnp
from jax import lax
from jax.experimental import pallas as pl
from jax.experimental.pallas import tpu as pltpu

F32 = jnp.float32
BF16 = jnp.bfloat16

HEAD_DIM = 64
GRID_W = 64
ROPE_BASE = 10000.0
RMS_EPS = 1e-6
N_MOD = 6
WINDOW = 128
WA_KV_HEADS = 4
SC_WIDTH = 3
LANES = 128
NEG_BIG = -1e30
LOG2E = math.log2(math.e)

TM = 256
HG_CHUNK = 128
DA_TQ = 256
DA_TK = 1024
WA_TQ = 256
FFN_CHUNK = 256
VMEM_LIMIT = 56 * 1024 * 1024

NT_DIMS = (((1,), (1,)), ((), ()))


def _cparams(n_axes):
    return pltpu.CompilerParams(
        dimension_semantics=("arbitrary",) * n_axes, vmem_limit_bytes=VMEM_LIMIT)


def _const_spec(shape):
    nd = len(shape)
    return pl.BlockSpec(shape, lambda *_: (0,) * nd, pipeline_mode=pl.Buffered(1))


def _is_ctx_rows(tile_idx, tm, lc):
    row = tile_idx * tm + lax.broadcasted_iota(jnp.int32, (tm, 1), 0)
    return row < lc


def _pick(mod_ref, k, is_ctx):
    return jnp.where(is_ctx, mod_ref[k:k + 1, :], mod_ref[N_MOD + k:N_MOD + k + 1, :])


def _rms(x):
    return x * lax.rsqrt(jnp.mean(x * x, axis=-1, keepdims=True) + RMS_EPS)


def _norm_mod(x, g, shift, scale):
    return (_rms(x) * g) * (1.0 + scale) + shift


def _silu(x):
    return x * (1.0 / (1.0 + jnp.exp(-x)))


def _rope(chunk, cs, sn, first_half):
    partner = jnp.where(first_half, pltpu.roll(chunk, LANES - 16, 1), pltpu.roll(chunk, 16, 1))
    return chunk * cs + partner * sn


def _first_half_lanes():
    lane = lax.broadcasted_iota(jnp.int32, (1, LANES), 1)
    return (lane & 31) < 16


def _mod_kernel(cc_ref, w_ref, b_ref, o_ref):
    a = _silu(cc_ref[...])
    o_ref[0] = jnp.dot(a, w_ref[0], precision=lax.Precision.HIGHEST,
                       preferred_element_type=F32) + b_ref[0]


def _mod_call(cc, ada_w, ada_b):
    depth, d, nd = ada_w.shape
    tn = 1536
    assert nd % tn == 0
    return pl.pallas_call(
        _mod_kernel,
        grid=(depth, nd // tn),
        in_specs=[pl.BlockSpec((8, d), lambda i, j: (0, 0)),
                  pl.BlockSpec((1, d, tn), lambda i, j: (i, 0, j)),
                  pl.BlockSpec((1, 1, tn), lambda i, j: (i, 0, j))],
        out_specs=pl.BlockSpec((1, 8, tn), lambda i, j: (i, 0, j)),
        out_shape=jax.ShapeDtypeStruct((depth, 8, nd), F32),
        compiler_params=_cparams(2),
        name="adaln_mod",
    )(cc, ada_w, ada_b.reshape(depth, 1, nd))


def _proj_da_kernel(x_ref, g_ref, mod_ref, cs_ref, sn_ref, wqk_ref, wvt_ref,
                    q_ref, k_ref, vt_ref, *, tm, lc, d):
    is_ctx = _is_ctx_rows(pl.program_id(0), tm, lc)
    h = _norm_mod(x_ref[...], g_ref[...], _pick(mod_ref, 0, is_ctx), _pick(mod_ref, 1, is_ctx))
    hb = h.astype(BF16)
    qk = jnp.dot(hb, wqk_ref[...], preferred_element_type=F32)
    cs, sn, first = cs_ref[...], sn_ref[...], _first_half_lanes()
    nq = d // LANES
    for j in range(2 * nq):
        r = _rope(qk[:, j * LANES:(j + 1) * LANES], cs, sn, first)
        if j < nq:
            q_ref[:, j * LANES:(j + 1) * LANES] = (r * (LOG2E * HEAD_DIM ** -0.5)).astype(BF16)
        else:
            k_ref[:, (j - nq) * LANES:(j - nq + 1) * LANES] = r.astype(BF16)
    vt = lax.dot_general(wvt_ref[...], hb, NT_DIMS, preferred_element_type=F32)
    vt_ref[0] = vt.astype(BF16)


def _proj_da_call(xs, g, mod12, cs, sn, wqk, wvt, lc):
    n, d = xs.shape
    nt = n // TM
    row = lambda i: (i, 0)
    return pl.pallas_call(
        functools.partial(_proj_da_kernel, tm=TM, lc=lc, d=d),
        grid=(nt,),
        in_specs=[pl.BlockSpec((TM, d), row), _const_spec((1, d)), _const_spec((2 * N_MOD, d)),
                  pl.BlockSpec((TM, LANES), row), pl.BlockSpec((TM, LANES), row),
                  _const_spec(wqk.shape), _const_spec(wvt.shape)],
        out_specs=[pl.BlockSpec((TM, d), row), pl.BlockSpec((TM, d), row),
                   pl.BlockSpec((1, d, TM), lambda i: (i, 0, 0))],
        out_shape=[jax.ShapeDtypeStruct((n, d), BF16), jax.ShapeDtypeStruct((n, d), BF16),
                   jax.ShapeDtypeStruct((nt, d, TM), BF16)],
        compiler_params=_cparams(1),
        name="proj_diff_attn",
    )(xs, g, mod12, cs, sn, wqk, wvt)


def _proj_sc_kernel(x_ref, g_ref, mod_ref, w_ref, b_ref, z_ref, *, tm, lc, d):
    is_ctx = _is_ctx_rows(pl.program_id(0), tm, lc)
    h = _norm_mod(x_ref[...], g_ref[...], _pick(mod_ref, 0, is_ctx), _pick(mod_ref, 1, is_ctx))
    y = jnp.dot(h.astype(BF16), w_ref[...], preferred_element_type=F32)
    b_ref[...] = y[:, :d].astype(BF16)
    z_ref[...] = y[:, d:2 * d] * y[:, 2 * d:]


def _proj_sc_call(xs, g, mod12, w, lc):
    n, d = xs.shape
    row = lambda i: (i, 0)
    return pl.pallas_call(
        functools.partial(_proj_sc_kernel, tm=TM, lc=lc, d=d),
        grid=(n // TM,),
        in_specs=[pl.BlockSpec((TM, d), row), _const_spec((1, d)), _const_spec((2 * N_MOD, d)),
                  _const_spec(w.shape)],
        out_specs=[pl.BlockSpec((TM, d), row), pl.BlockSpec((TM, d), row)],
        out_shape=[jax.ShapeDtypeStruct((n, d), BF16), jax.ShapeDtypeStruct((n, d), F32)],
        compiler_params=_cparams(1),
        name="proj_short_conv",
    )(xs, g, mod12, w)


def _proj_wa_kernel(x_ref, g_ref, mod_ref, cs_ref, sn_ref, w_ref, q_ref, k_ref, v_ref,
                    *, tm, lc, d, kw):
    is_ctx = _is_ctx_rows(pl.program_id(0), tm, lc)
    h = _norm_mod(x_ref[...], g_ref[...], _pick(mod_ref, 0, is_ctx), _pick(mod_ref, 1, is_ctx))
    y = jnp.dot(h.astype(BF16), w_ref[...], preferred_element_type=F32)
    cs, sn, first = cs_ref[...], sn_ref[...], _first_half_lanes()
    for j in range(d // LANES):
        r = _rope(y[:, j * LANES:(j + 1) * LANES], cs, sn, first)
        q_ref[:, j * LANES:(j + 1) * LANES] = (r * (HEAD_DIM ** -0.5)).astype(BF16)
    for j in range(kw // LANES):
        r = _rope(y[:, d + j * LANES:d + (j + 1) * LANES], cs, sn, first)
        k_ref[:, j * LANES:(j + 1) * LANES] = r.astype(BF16)
    v_ref[...] = y[:, d + kw:].astype(BF16)


def _proj_wa_call(xs, g, mod12, cs, sn, w, lc, kw):
    n, d = xs.shape
    row = lambda i: (i, 0)
    return pl.pallas_call(
        functools.partial(_proj_wa_kernel, tm=TM, lc=lc, d=d, kw=kw),
        grid=(n // TM,),
        in_specs=[pl.BlockSpec((TM, d), row), _const_spec((1, d)), _const_spec((2 * N_MOD, d)),
                  pl.BlockSpec((TM, LANES), row), pl.BlockSpec((TM, LANES), row),
                  _const_spec(w.shape)],
        out_specs=[pl.BlockSpec((TM, d), row), pl.BlockSpec((TM, kw), row),
                   pl.BlockSpec((TM, kw), row)],
        out_shape=[jax.ShapeDtypeStruct((n, d), BF16), jax.ShapeDtypeStruct((n, kw), BF16),
                   jax.ShapeDtypeStruct((n, kw), BF16)],
        compiler_params=_cparams(1),
        name="proj_window_attn",
    )(xs, g, mod12, cs, sn, w)


def _log1p(x):
    return jnp.log1p(x)


def _proj_hg_kernel(x_ref, g_ref, mod_ref, lla_ref, l1m_ref, w_ref, wvt_ref,
                    q_ref, gate_ref, v_ref, vt_ref, og_ref, *, tm, lc, d):
    is_ctx = _is_ctx_rows(pl.program_id(0), tm, lc)
    h = _norm_mod(x_ref[...], g_ref[...], _pick(mod_ref, 0, is_ctx), _pick(mod_ref, 1, is_ctx))
    hb = h.astype(BF16)
    y = jnp.dot(hb, w_ref[...], preferred_element_type=F32)
    q_ref[...] = _silu(y[:, :d]).astype(BF16)
    for dr in range(2):
        f = y[:, (1 + dr) * d:(2 + dr) * d]
        log_sig = jnp.minimum(f, 0.0) - _log1p(jnp.exp(-jnp.abs(f)))
        a = lla_ref[dr:dr + 1, :]
        c = l1m_ref[dr:dr + 1, :] + log_sig
        gate_ref[dr] = jnp.maximum(a, c) + _log1p(jnp.exp(-jnp.abs(a - c)))
    v_ref[...] = y[:, 3 * d:4 * d].astype(BF16)
    og_ref[...] = _silu(y[:, 4 * d:]).astype(BF16)
    vt = lax.dot_general(wvt_ref[...], hb, NT_DIMS, preferred_element_type=F32)
    vt_ref[0] = vt.astype(BF16)


def _proj_hg_call(xs, g, mod12, lla, l1m, w, wvt, lc):
    n, d = xs.shape
    nt = n // TM
    row = lambda i: (i, 0)
    return pl.pallas_call(
        functools.partial(_proj_hg_kernel, tm=TM, lc=lc, d=d),
        grid=(nt,),
        in_specs=[pl.BlockSpec((TM, d), row), _const_spec((1, d)), _const_spec((2 * N_MOD, d)),
                  _const_spec((2, d)), _const_spec((2, d)),
                  _const_spec(w.shape), _const_spec(wvt.shape)],
        out_specs=[pl.BlockSpec((TM, d), row), pl.BlockSpec((2, TM, d), lambda i: (0, i, 0)),
                   pl.BlockSpec((TM, d), row), pl.BlockSpec((1, d, TM), lambda i: (i, 0, 0)),
                   pl.BlockSpec((TM, d), row)],
        out_shape=[jax.ShapeDtypeStruct((n, d), BF16), jax.ShapeDtypeStruct((2, n, d), F32),
                   jax.ShapeDtypeStruct((n, d), BF16), jax.ShapeDtypeStruct((nt, d, TM), BF16),
                   jax.ShapeDtypeStruct((n, d), BF16)],
        compiler_params=_cparams(1),
        name="proj_hgrn",
    )(xs, g, mod12, lla, l1m, w, wvt)


def _diff_attn_kernel(lam_ref, q_ref, k_ref, vt_ref, g_ref, o_ref, m_ref, l_ref, acc_ref,
                      sa_ref, sb_ref, ma_ref, mb_ref, *, tq, tk, lc, n, unroll, post_scale):
    qi = pl.program_id(1)
    q = q_ref[...]
    lane = lax.broadcasted_iota(jnp.int32, (1, LANES), 1)
    zero = jnp.zeros_like(q)
    qcat = jnp.concatenate([jnp.where(lane < HEAD_DIM, q, zero),
                            jnp.where(lane >= HEAD_DIM, q, zero)], axis=0)
    sub = vt_ref.shape[2]
    n_chunks = (n - lc) // tk

    def scores(kc):
        return lax.dot_general(kc, qcat, NT_DIMS, preferred_element_type=F32)

    def pv(slab0, n_slabs, p):
        out = None
        for j in range(n_slabs):
            t = jnp.dot(vt_ref[slab0 + j], p[j * sub:(j + 1) * sub, :],
                        preferred_element_type=F32)
            out = t if out is None else out + t
        return out

    def absorb_context():
        s = scores(k_ref[0:lc, :])
        mx = jnp.max(s, axis=0, keepdims=True)
        p = jnp.exp2(s - mx)
        m_ref[...] = mx
        l_ref[...] = jnp.sum(p, axis=0, keepdims=True)
        acc_ref[...] = pv(0, lc // sub, p.astype(BF16))

    def qk(c, s_ref, mx_ref):
        off = pl.multiple_of(lc + c * tk, LANES)
        s = scores(k_ref[pl.ds(off, tk), :])
        s_ref[...] = s
        mx_ref[...] = jnp.max(s, axis=0, keepdims=True)

    def absorb(c, s_ref, mx_ref):
        m_old = m_ref[...]
        m_new = jnp.maximum(m_old, mx_ref[...])
        alpha = jnp.exp2(m_old - m_new)
        p = jnp.exp2(s_ref[...] - m_new)
        m_ref[...] = m_new
        l_ref[...] = alpha * l_ref[...] + jnp.sum(p, axis=0, keepdims=True)
        acc_ref[...] = alpha * acc_ref[...] + pv((lc + c * tk) // sub, tk // sub, p.astype(BF16))

    @pl.when(qi * tq < lc)
    def _():
        absorb_context()

    @pl.when(qi * tq >= lc)
    def _():
        qk(0, sa_ref, ma_ref)
        absorb_context()
        bufs = ((sa_ref, ma_ref), (sb_ref, mb_ref))

        def group(c0, count, last):
            for j in range(count):
                if not (last and j == count - 1):
                    qk(c0 + j + 1, *bufs[(j + 1) % 2])
                absorb(c0 + j, *bufs[j % 2])

        def body(i, carry):
            group(unroll * i, unroll, False)
            return carry

        lax.fori_loop(0, n_chunks // unroll - 1, body, 0)
        group(n_chunks - unroll, unroll, True)

    acc = acc_ref[...] * (1.0 / l_ref[...])
    o = acc[:, :tq] - lam_ref[0] * acc[:, tq:]
    o = o * lax.rsqrt(jnp.mean(o * o, axis=0, keepdims=True) + RMS_EPS)
    o = o * (g_ref[...] * post_scale)
    o_ref[...] = o.T.astype(BF16)


def _diff_attn_call(lam, q, k, vt3, subln_g, lc, post_scale):
    n, d = q.shape
    nh = d // LANES
    nslab, _, sub = vt3.shape
    tk = min(DA_TK, (n - lc) // 2)
    assert lc % DA_TQ == 0 and n % DA_TQ == 0 and tk % sub == 0 and lc % sub == 0
    assert (n - lc) % (2 * tk) == 0
    return pl.pallas_call(
        functools.partial(_diff_attn_kernel, tq=DA_TQ, tk=tk, lc=lc, n=n,
                          unroll=4 if (n - lc) % (4 * tk) == 0 else 2, post_scale=post_scale),
        grid=(nh, n // DA_TQ),
        in_specs=[pl.BlockSpec(memory_space=pltpu.SMEM),
                  pl.BlockSpec((DA_TQ, LANES), lambda h, i: (i, h)),
                  pl.BlockSpec((n, LANES), lambda h, i: (0, h)),
                  pl.BlockSpec((nslab, LANES, sub), lambda h, i: (0, h, 0)),
                  pl.BlockSpec((LANES, 1), lambda h, i: (0, 0))],
        out_specs=pl.BlockSpec((DA_TQ, LANES), lambda h, i: (i, h)),
        out_shape=jax.ShapeDtypeStruct((n, d), BF16),
        scratch_shapes=[pltpu.VMEM((1, 2 * DA_TQ), F32), pltpu.VMEM((1, 2 * DA_TQ), F32),
                        pltpu.VMEM((LANES, 2 * DA_TQ), F32),
                        pltpu.VMEM((tk, 2 * DA_TQ), F32), pltpu.VMEM((tk, 2 * DA_TQ), F32),
                        pltpu.VMEM((1, 2 * DA_TQ), F32), pltpu.VMEM((1, 2 * DA_TQ), F32)],
        compiler_params=_cparams(2),
        name="diff_attention",
    )(lam, q, k, vt3, subln_g)


def _win_attn_kernel(sink_ref, q_ref, kc_ref, kp_ref, ko_ref, kn_ref,
                     vc_ref, vp_ref, vo_ref, vn_ref, o_ref, *, tq, lc, n, rep):
    j = pl.program_id(0)
    g = pl.program_id(1)
    kall = jnp.concatenate([kc_ref[...], kp_ref[...], ko_ref[...], kn_ref[...]], axis=0)
    vall = jnp.concatenate([vc_ref[...], vp_ref[...], vo_ref[...], vn_ref[...]], axis=0)
    nk = lc + tq + 2 * WINDOW
    qpos = j * tq + lax.broadcasted_iota(jnp.int32, (tq, 1), 0)
    col = lax.broadcasted_iota(jnp.int32, (1, nk), 1)
    kpos = j * tq - WINDOW + (col - lc)
    rel = kpos - qpos
    win_ok = (rel <= WINDOW) & (rel >= -WINDOW) & (kpos >= lc) & (kpos < n) & (qpos >= lc)
    valid = (col < lc) | win_ok
    lane = lax.broadcasted_iota(jnp.int32, (1, LANES), 1)
    low = lane < HEAD_DIM
    for c in range(rep // 2):
        qc = q_ref[:, c * LANES:(c + 1) * LANES]
        zero = jnp.zeros_like(qc)
        outs = []
        for half in range(2):
            qm = jnp.where(low if half == 0 else ~low, qc, zero)
            s = lax.dot_general(qm, kall, NT_DIMS, preferred_element_type=F32)
            s = jnp.where(valid, s, NEG_BIG)
            sink = sink_ref[g * rep + 2 * c + half]
            mx = jnp.maximum(jnp.max(s, axis=-1, keepdims=True), sink)
            e = jnp.exp(s - mx)
            den = jnp.exp(sink - mx) + jnp.sum(e, axis=-1, keepdims=True)
            p = (e * (1.0 / den)).astype(BF16)
            outs.append(jnp.dot(p, vall, preferred_element_type=F32))
        o_ref[:, c * LANES:(c + 1) * LANES] = jnp.where(low, outs[0], outs[1]).astype(BF16)


def _win_attn_call(sinks, q, k2, v2, lc):
    n, d = q.shape
    ng = WA_KV_HEADS
    rep = d // HEAD_DIM // ng
    gw = rep * HEAD_DIM
    tq = WA_TQ
    assert n % tq == 0 and tq % WINDOW == 0 and lc % WINDOW == 0 and rep % 2 == 0
    r = tq // WINDOW
    last = n // WINDOW - 1
    kv_specs = [pl.BlockSpec((lc, LANES), lambda j, g: (0, g)),
                pl.BlockSpec((WINDOW, LANES), lambda j, g: (jnp.maximum(j * r - 1, 0), g)),
                pl.BlockSpec((tq, LANES), lambda j, g: (j, g)),
                pl.BlockSpec((WINDOW, LANES), lambda j, g: (jnp.minimum((j + 1) * r, last), g))]
    return pl.pallas_call(
        functools.partial(_win_attn_kernel, tq=tq, lc=lc, n=n, rep=rep),
        grid=(n // tq, ng),
        in_specs=[pl.BlockSpec(memory_space=pltpu.SMEM),
                  pl.BlockSpec((tq, gw), lambda j, g: (j, g))] + kv_specs + kv_specs,
        out_specs=pl.BlockSpec((tq, gw), lambda j, g: (j, g)),
        out_shape=jax.ShapeDtypeStruct((n, d), BF16),
        compiler_params=_cparams(2),
        name="window_attention",
    )(sinks, q, k2, k2, k2, k2, v2, v2, v2, v2)


def _hgrn_consts(chunk):
    n_lv = int(math.log2(chunk))
    walls, masks_all = [], []
    for direction in range(2):
        tau = np.arange(chunk) if direction == 0 else chunk - 1 - np.arange(chunk)
        tr, tc = tau[:, None], tau[None, :]
        blocks = [tc <= tr, tc > tr]
        masks = []
        for lv in range(n_lv):
            h = chunk >> (lv + 1)
            mid = (tr // (2 * h)) * 2 * h + h - 1
            second_r = (tr % (2 * h)) >= h
            second_c = (tc % (2 * h)) >= h
            blocks.append(np.where(second_r, (tc > mid) & (tc <= tr), (tc > tr) & (tc <= mid)))
            masks.append(second_r & ~second_c & ((tr // (2 * h)) == (tc // (2 * h))))
        masks.append(tr == tc)
        w = np.concatenate(blocks, axis=0).astype(np.float32)
        walls.append(np.concatenate([w, w], axis=1))
        masks_all.append(np.stack(masks).astype(np.float32))
    return np.stack(walls), np.stack(masks_all)


def _hgrn_kernel(wall_ref, mask_ref, q_ref, g_ref, v_ref, vt_ref, o_ref, st_ref,
                 *, chunk, n_lv, n_heads):
    @pl.when(pl.program_id(1) == 0)
    def _():
        st_ref[...] = jnp.zeros_like(st_ref)

    g_all = g_ref[0]
    g_hi = g_all.astype(BF16)
    g_lo = (g_all - g_hi.astype(F32)).astype(BF16)
    sums_all = jnp.dot(wall_ref[0], jnp.concatenate([g_hi, g_lo], axis=0),
                       preferred_element_type=F32)
    for hh in range(n_heads):
        sl = slice(hh * LANES, (hh + 1) * LANES)
        sums = sums_all[:, sl]
        g = g_all[:, sl]
        qb16 = q_ref[:, sl]
        q = qb16.astype(F32)
        k = 1.0 - jnp.exp(g)
        st = st_ref[hh]

        qb = (q * jnp.exp(sums[0:chunk])).astype(BF16)
        o = lax.dot_general(qb, st.astype(BF16), NT_DIMS, preferred_element_type=F32)

        att = lax.dot_general(qb16, k.astype(BF16), NT_DIMS,
                              preferred_element_type=F32) * mask_ref[0, n_lv]
        for lv in range(n_lv):
            e = jnp.exp(sums[(2 + lv) * chunk:(3 + lv) * chunk])
            a = lax.dot_general((q * e).astype(BF16), (k * e).astype(BF16), NT_DIMS,
                                preferred_element_type=F32)
            att = att + a * mask_ref[0, lv]
        o_ref[0, :, sl] = o + jnp.dot(att.astype(BF16), v_ref[:, sl], preferred_element_type=F32)

        ks = (k * jnp.exp(sums[chunk:2 * chunk])).astype(BF16)
        decay = jnp.exp(jnp.sum(g, axis=0, keepdims=True))
        st_ref[hh] = st * decay + jnp.dot(vt_ref[0, sl, :], ks, preferred_element_type=F32)


def _hgrn_call(qh, gates, v, vt3, lc):
    n, d = qh.shape
    c = HG_CHUNK
    nh = d // LANES
    nc, ncc = n // c, lc // c
    per_slab = vt3.shape[2] // c
    assert n % c == 0 and lc % c == 0 and vt3.shape[2] % c == 0
    n_lv = int(math.log2(c))
    wall_np, mask_np = _hgrn_consts(c)
    wall = jnp.asarray(wall_np, dtype=BF16)
    masks = jnp.asarray(mask_np, dtype=F32)

    def blk(dr, s):
        back = jnp.where(s < ncc, ncc - 1 - s, nc - 1 - (s - ncc))
        return jnp.where(dr == 0, s, back)

    return pl.pallas_call(
        functools.partial(_hgrn_kernel, chunk=c, n_lv=n_lv, n_heads=nh),
        grid=(2, nc),
        in_specs=[pl.BlockSpec((1,) + wall_np.shape[1:], lambda dr, s: (dr, 0, 0)),
                  pl.BlockSpec((1,) + mask_np.shape[1:], lambda dr, s: (dr, 0, 0, 0)),
                  pl.BlockSpec((c, d), lambda dr, s: (blk(dr, s), 0)),
                  pl.BlockSpec((1, c, d), lambda dr, s: (dr, blk(dr, s), 0)),
                  pl.BlockSpec((c, d), lambda dr, s: (blk(dr, s), 0)),
                  pl.BlockSpec((1, d, c),
                               lambda dr, s: (blk(dr, s) // per_slab, 0, blk(dr, s) % per_slab))],
        out_specs=pl.BlockSpec((1, c, d), lambda dr, s: (dr, blk(dr, s), 0)),
        out_shape=jax.ShapeDtypeStruct((2, n, d), F32),
        scratch_shapes=[pltpu.VMEM((nh, LANES, LANES), F32)],
        compiler_params=_cparams(2),
        name="hgrn_scan",
    )(wall, masks, qh, gates, v, vt3)


def _post_kernel(*refs, mode, final, tm, lc, n, d, f, tile0):
    it = iter(refs)
    x_ref, mod_ref, g2_ref, wo_ref, win_ref, wout_ref = (next(it) for _ in range(6))
    if mode == "plain":
        a_ref = next(it)
    elif mode == "conv":
        b_ref, z_ref, zp_ref, zn_ref, cw_ref = (next(it) for _ in range(5))
    else:
        ofw_ref, obw_ref, og_ref, gn_ref = (next(it) for _ in range(4))
    fin_ref = next(it) if final else None
    o_ref = next(it)

    ti = pl.program_id(0) + tile0
    is_ctx = _is_ctx_rows(ti, tm, lc)
    if mode == "plain":
        a = a_ref[...]
    elif mode == "conv":
        z = z_ref[...]
        rl = lax.broadcasted_iota(jnp.int32, (tm, 1), 0)
        row = ti * tm + rl
        z_prev = jnp.where(rl == 0, zp_ref[7:8, :], pltpu.roll(z, 1, 0))
        z_next = jnp.where(rl == tm - 1, zn_ref[0:1, :], pltpu.roll(z, tm - 1, 0))
        z_prev = jnp.where((row == 0) | (row == lc), 0.0, z_prev)
        z_next = jnp.where((row == lc - 1) | (row == n - 1), 0.0, z_next)
        conv = cw_ref[0:1, :] * z_prev + cw_ref[1:2, :] * z + cw_ref[2:3, :] * z_next
        a = (b_ref[...].astype(F32) * conv).astype(BF16)
    else:
        gn = gn_ref[...]
        heads = []
        for hh in range(d // LANES):
            sl = slice(hh * LANES, (hh + 1) * LANES)
            oh = _rms(ofw_ref[0, :, sl] + obw_ref[0, :, sl])
            heads.append((oh * gn * og_ref[:, sl].astype(F32)).astype(BF16))
        a = jnp.concatenate(heads, axis=1)

    y = jnp.dot(a, wo_ref[...], preferred_element_type=F32)
    x = x_ref[...] + _pick(mod_ref, 2, is_ctx) * y
    h = _norm_mod(x, g2_ref[...], _pick(mod_ref, 3, is_ctx), _pick(mod_ref, 4, is_ctx)).astype(BF16)
    acc = jnp.zeros((tm, d), F32)
    for c0 in range(0, f, FFN_CHUNK):
        u = jnp.dot(h, win_ref[:, c0:c0 + FFN_CHUNK], preferred_element_type=F32)
        w = jnp.dot(h, win_ref[:, f + c0:f + c0 + FFN_CHUNK], preferred_element_type=F32)
        act = (_silu(u) * w).astype(BF16)
        acc = acc + jnp.dot(act, wout_ref[c0:c0 + FFN_CHUNK, :], preferred_element_type=F32)
    x = x + _pick(mod_ref, 5, is_ctx) * acc
    if final:
        x = _rms(x) * fin_ref[...]
    o_ref[...] = x


def _post_call(xs, mod12, g2, wo, win, wout, mixer_inputs, mode, lc, final_g=None):
    n, d = xs.shape
    f = wout.shape[0]
    assert f % FFN_CHUNK == 0 and lc % TM == 0
    final = final_g is not None
    tile0 = lc // TM if final else 0
    nt = n // TM - tile0
    row = lambda i: (i + tile0, 0)
    in_specs = [pl.BlockSpec((TM, d), row), _const_spec((2 * N_MOD, d)), _const_spec((1, d)),
                _const_spec(wo.shape), _const_spec(win.shape), _const_spec(wout.shape)]
    args = [xs, mod12, g2, wo, win, wout]
    if mode == "plain":
        in_specs += [pl.BlockSpec((TM, d), row)]
    elif mode == "conv":
        r8 = TM // 8
        last8 = n // 8 - 1
        in_specs += [pl.BlockSpec((TM, d), row), pl.BlockSpec((TM, d), row),
                     pl.BlockSpec((8, d), lambda i: (jnp.maximum((i + tile0) * r8 - 1, 0), 0)),
                     pl.BlockSpec((8, d), lambda i: (jnp.minimum((i + tile0 + 1) * r8, last8), 0)),
                     _const_spec((SC_WIDTH, d))]
    else:
        in_specs += [pl.BlockSpec((1, TM, d), lambda i: (0, i + tile0, 0)),
                     pl.BlockSpec((1, TM, d), lambda i: (1, i + tile0, 0)),
                     pl.BlockSpec((TM, d), row), _const_spec((1, LANES))]
    args += list(mixer_inputs)
    if final:
        in_specs += [_const_spec((1, d))]
        args += [final_g]
    return pl.pallas_call(
        functools.partial(_post_kernel, mode=mode, final=final, tm=TM, lc=lc, n=n, d=d, f=f,
                          tile0=tile0),
        grid=(nt,),
        in_specs=in_specs,
        out_specs=pl.BlockSpec((TM, d), lambda i: (i, 0)),
        out_shape=jax.ShapeDtypeStruct((nt * TM, d), F32),
        compiler_params=_cparams(1),
        name="post_" + mode + ("_final" if final else ""),
    )(*args)


def _rope_tables(l, lc):
    rows = l // GRID_W
    r = jnp.repeat(jnp.arange(rows), GRID_W).astype(F32)
    c = jnp.tile(jnp.arange(GRID_W), rows).astype(F32)
    half = HEAD_DIM // 2
    inv_freq = 1.0 / (ROPE_BASE ** (jnp.arange(0, half, 2, dtype=F32) / half))
    ar, ac = r[:, None] * inv_freq, c[:, None] * inv_freq
    cs = jnp.concatenate([jnp.cos(ar), jnp.cos(ar), jnp.cos(ac), jnp.cos(ac)], axis=-1)
    sn = jnp.concatenate([-jnp.sin(ar), jnp.sin(ar), -jnp.sin(ac), jnp.sin(ac)], axis=-1)
    cs = jnp.concatenate([jnp.ones((lc, HEAD_DIM), F32), cs], axis=0)
    sn = jnp.concatenate([jnp.zeros((lc, HEAD_DIM), F32), sn], axis=0)
    return jnp.tile(cs, (1, 2)), jnp.tile(sn, (1, 2))


def kernel(x, c, ctx, c_ctx, ada_w, ada_b, norm1_g, norm2_g, ffn_w_in, ffn_w_out, final_g,
           da_wqkv, da_lambda, da_subln_g, da_wo, sc_w_in, sc_conv_w, sc_w_out,
           wa_wqkv, wa_sinks, wa_wo, hg_w_in, hg_lb, hg_gnorm_g, hg_wo):
    b, l, d = x.shape
    lc = ctx.shape[1]
    depth = ada_w.shape[0]
    assert b == 1 and d % LANES == 0 and lc % TM == 0 and l % TM == 0
    n = lc + l
    xs = jnp.concatenate([ctx[0], x[0]], axis=0)

    cc = jnp.zeros((8, d), F32).at[0].set(c_ctx).at[1].set(c[0])
    mods = _mod_call(cc, ada_w, ada_b)
    cs, sn = _rope_tables(l, lc)

    n_mix = 4
    for i in range(depth):
        mixer, slot = i % n_mix, i // n_mix
        last = i == depth - 1
        mod12 = jnp.concatenate([mods[i, 0].reshape(N_MOD, d), mods[i, 1].reshape(N_MOD, d)], axis=0)
        g1 = norm1_g[i].reshape(1, d)
        g2 = norm2_g[i].reshape(1, d)
        win = ffn_w_in[i].astype(BF16)
        wout = ffn_w_out[i].astype(BF16)
        fin = final_g.reshape(1, d) if last else None

        if mixer == 0:
            w = da_wqkv[slot]
            lam = da_lambda[slot].astype(F32)
            lam_init = 0.8 - 0.6 * math.exp(-0.3 * i)
            lam_full = (jnp.exp(jnp.sum(lam[0] * lam[1])) - jnp.exp(jnp.sum(lam[2] * lam[3]))
                        + lam_init).reshape(1)
            q, k, vt3 = _proj_da_call(xs, g1, mod12, cs, sn, w[:, :2 * d].astype(BF16),
                                      w[:, 2 * d:].T.astype(BF16), lc)
            a = _diff_attn_call(lam_full, q, k, vt3, da_subln_g[slot].reshape(LANES, 1), lc,
                                1.0 - lam_init)
            xs = _post_call(xs, mod12, g2, da_wo[slot].astype(BF16), win, wout, [a], "plain", lc, fin)
        elif mixer == 1:
            bg, z = _proj_sc_call(xs, g1, mod12, sc_w_in[slot].astype(BF16), lc)
            xs = _post_call(xs, mod12, g2, sc_w_out[slot].astype(BF16), win, wout,
                            [bg, z, z, z, sc_conv_w[slot]], "conv", lc, fin)
        elif mixer == 2:
            w = wa_wqkv[slot]
            kw = WA_KV_HEADS * HEAD_DIM
            dup = lambda m: jnp.tile(m.reshape(d, WA_KV_HEADS, 1, HEAD_DIM), (1, 1, 2, 1)).reshape(d, 2 * kw)
            w2 = jnp.concatenate([w[:, :d], dup(w[:, d:d + kw]), dup(w[:, d + kw:])], axis=1)
            q, k2, v2 = _proj_wa_call(xs, g1, mod12, cs, sn, w2.astype(BF16), lc, 2 * kw)
            a = _win_attn_call(wa_sinks[slot].astype(F32), q, k2, v2, lc)
            xs = _post_call(xs, mod12, g2, wa_wo[slot].astype(BF16), win, wout, [a], "plain", lc, fin)
        else:
            w = hg_w_in[slot]
            p = jax.nn.softmax(hg_lb.astype(F32), axis=1)
            lb = (jnp.cumsum(p, axis=1) - p[:, :1])[:, i]
            qh, gates, v, vt3, og = _proj_hg_call(xs, g1, mod12, jnp.log(lb), jnp.log1p(-lb),
                                                  w.astype(BF16),
                                                  w[:, 3 * d:4 * d].T.astype(BF16), lc)
            o2 = _hgrn_call(qh, gates, v, vt3, lc)
            xs = _post_call(xs, mod12, g2, hg_wo[slot].astype(BF16), win, wout,
                            [o2, o2, og, hg_gnorm_g[slot].reshape(1, LANES)], "hgrn", lc, fin)
    return xs.reshape(1, l, d)
```

```python
import functools
import math

import numpy as np
import jax
import jax.numpy as jnp
from jax import lax
from jax.experimental import pallas as pl
from jax.experimental.pallas import tpu as pltpu

F32 = jnp.float32
BF16 = jnp.bfloat16

HEAD_DIM = 64
GRID_W = 64
ROPE_BASE = 10000.0
RMS_EPS = 1e-6
N_MOD = 6
WINDOW = 128
WA_KV_HEADS = 4
SC_WIDTH = 3
LANES = 128
NEG_BIG = -1e30
LOG2E = math.log2(math.e)

TM = 256
POST_TM_CHOICES = (640, 512, 256)
HG_CHUNK = 128
DA_TQ = 256
DA_TK = 1024
DA_ONES_ROWS = 16
WA_TQ = 256
FFN_CHUNK = 256
VMEM_LIMIT = 56 * 1024 * 1024

NT_DIMS = (((1,), (1,)), ((), ()))


def _cparams(n_axes):
    return pltpu.CompilerParams(
        dimension_semantics=("arbitrary",) * n_axes, vmem_limit_bytes=VMEM_LIMIT)


def _const_spec(shape):
    nd = len(shape)
    return pl.BlockSpec(shape, lambda *_: (0,) * nd, pipeline_mode=pl.Buffered(1))


def _is_ctx_rows(tile_idx, tm, lc):
    row = tile_idx * tm + lax.broadcasted_iota(jnp.int32, (tm, 1), 0)
    return row < lc


def _pick(mod_ref, k, is_ctx):
    return jnp.where(is_ctx, mod_ref[k:k + 1, :], mod_ref[N_MOD + k:N_MOD + k + 1, :])


def _rms(x):
    return x * lax.rsqrt(jnp.mean(x * x, axis=-1, keepdims=True) + RMS_EPS)


def _norm_mod(x, g, shift, scale):
    return (_rms(x) * g) * (1.0 + scale) + shift


def _silu(x):
    return x * (1.0 / (1.0 + jnp.exp(-x)))


def _rope(chunk, cs, sn, first_half):
    partner = jnp.where(first_half, pltpu.roll(chunk, LANES - 16, 1), pltpu.roll(chunk, 16, 1))
    return chunk * cs + partner * sn


def _first_half_lanes():
    lane = lax.broadcasted_iota(jnp.int32, (1, LANES), 1)
    return (lane & 31) < 16


def _mod_kernel(cc_ref, w_ref, b_ref, o_ref):
    a = _silu(cc_ref[...])
    o_ref[0] = jnp.dot(a, w_ref[0], precision=lax.Precision.HIGHEST,
                       preferred_element_type=F32) + b_ref[0]


def _mod_call(cc, ada_w, ada_b):
    depth, d, nd = ada_w.shape
    tn = 1536
    assert nd % tn == 0
    return pl.pallas_call(
        _mod_kernel,
        grid=(depth, nd // tn),
        in_specs=[pl.BlockSpec((8, d), lambda i, j: (0, 0)),
                  pl.BlockSpec((1, d, tn), lambda i, j: (i, 0, j)),
                  pl.BlockSpec((1, 1, tn), lambda i, j: (i, 0, j))],
        out_specs=pl.BlockSpec((1, 8, tn), lambda i, j: (i, 0, j)),
        out_shape=jax.ShapeDtypeStruct((depth, 8, nd), F32),
        compiler_params=_cparams(2),
        name="adaln_mod",
    )(cc, ada_w, ada_b.reshape(depth, 1, nd))


def _proj_da_kernel(x_ref, g_ref, mod_ref, cs_ref, sn_ref, wqk_ref, wvt_ref,
                    q_ref, k_ref, vt_ref, *, tm, lc, d):
    is_ctx = _is_ctx_rows(pl.program_id(0), tm, lc)
    h = _norm_mod(x_ref[...], g_ref[...], _pick(mod_ref, 0, is_ctx), _pick(mod_ref, 1, is_ctx))
    hb = h.astype(BF16)
    qk = jnp.dot(hb, wqk_ref[...], preferred_element_type=F32)
    cs, sn, first = cs_ref[...], sn_ref[...], _first_half_lanes()
    nq = d // LANES
    for j in range(2 * nq):
        r = _rope(qk[:, j * LANES:(j + 1) * LANES], cs, sn, first)
        if j < nq:
            q_ref[:, j * LANES:(j + 1) * LANES] = (r * (LOG2E * HEAD_DIM ** -0.5)).astype(BF16)
        else:
            k_ref[:, (j - nq) * LANES:(j - nq + 1) * LANES] = r.astype(BF16)
    vt = lax.dot_general(wvt_ref[...], hb, NT_DIMS, preferred_element_type=F32)
    ones = jnp.ones((DA_ONES_ROWS, tm), BF16)
    vr = LANES + DA_ONES_ROWS
    for hh in range(d // LANES):
        vt_ref[0, hh * vr:hh * vr + LANES, :] = vt[hh * LANES:(hh + 1) * LANES, :].astype(BF16)
        vt_ref[0, hh * vr + LANES:(hh + 1) * vr, :] = ones


def _proj_da_call(xs, g, mod12, cs, sn, wqk, wvt, lc):
    n, d = xs.shape
    nt = n // TM
    row = lambda i: (i, 0)
    dv = d // LANES * (LANES + DA_ONES_ROWS)
    return pl.pallas_call(
        functools.partial(_proj_da_kernel, tm=TM, lc=lc, d=d),
        grid=(nt,),
        in_specs=[pl.BlockSpec((TM, d), row), _const_spec((1, d)), _const_spec((2 * N_MOD, d)),
                  pl.BlockSpec((TM, LANES), row), pl.BlockSpec((TM, LANES), row),
                  _const_spec(wqk.shape), _const_spec(wvt.shape)],
        out_specs=[pl.BlockSpec((TM, d), row), pl.BlockSpec((TM, d), row),
                   pl.BlockSpec((1, dv, TM), lambda i: (i, 0, 0))],
        out_shape=[jax.ShapeDtypeStruct((n, d), BF16), jax.ShapeDtypeStruct((n, d), BF16),
                   jax.ShapeDtypeStruct((nt, dv, TM), BF16)],
        compiler_params=_cparams(1),
        name="proj_diff_attn",
    )(xs, g, mod12, cs, sn, wqk, wvt)


def _proj_sc_kernel(x_ref, g_ref, mod_ref, w_ref, b_ref, z_ref, *, tm, lc, d):
    is_ctx = _is_ctx_rows(pl.program_id(0), tm, lc)
    h = _norm_mod(x_ref[...], g_ref[...], _pick(mod_ref, 0, is_ctx), _pick(mod_ref, 1, is_ctx))
    y = jnp.dot(h.astype(BF16), w_ref[...], preferred_element_type=F32)
    b_ref[...] = y[:, :d].astype(BF16)
    z_ref[...] = y[:, d:2 * d] * y[:, 2 * d:]


def _proj_sc_call(xs, g, mod12, w, lc):
    n, d = xs.shape
    row = lambda i: (i, 0)
    return pl.pallas_call(
        functools.partial(_proj_sc_kernel, tm=TM, lc=lc, d=d),
        grid=(n // TM,),
        in_specs=[pl.BlockSpec((TM, d), row), _const_spec((1, d)), _const_spec((2 * N_MOD, d)),
                  _const_spec(w.shape)],
        out_specs=[pl.BlockSpec((TM, d), row), pl.BlockSpec((TM, d), row)],
        out_shape=[jax.ShapeDtypeStruct((n, d), BF16), jax.ShapeDtypeStruct((n, d), F32)],
        compiler_params=_cparams(1),
        name="proj_short_conv",
    )(xs, g, mod12, w)


def _proj_wa_kernel(x_ref, g_ref, mod_ref, cs_ref, sn_ref, w_ref, q_ref, k_ref, v_ref,
                    *, tm, lc, d, kw):
    is_ctx = _is_ctx_rows(pl.program_id(0), tm, lc)
    h = _norm_mod(x_ref[...], g_ref[...], _pick(mod_ref, 0, is_ctx), _pick(mod_ref, 1, is_ctx))
    y = jnp.dot(h.astype(BF16), w_ref[...], preferred_element_type=F32)
    cs, sn, first = cs_ref[...], sn_ref[...], _first_half_lanes()
    for j in range(d // LANES):
        r = _rope(y[:, j * LANES:(j + 1) * LANES], cs, sn, first)
        q_ref[:, j * LANES:(j + 1) * LANES] = (r * (LOG2E * HEAD_DIM ** -0.5)).astype(BF16)
    for j in range(kw // LANES):
        r = _rope(y[:, d + j * LANES:d + (j + 1) * LANES], cs, sn, first)
        k_ref[:, j * LANES:(j + 1) * LANES] = r.astype(BF16)
    vt = y[:, d + kw:].T
    for s in range(tm // WINDOW):
        v_ref[s] = vt[:, s * WINDOW:(s + 1) * WINDOW].astype(BF16)


def _proj_wa_call(xs, g, mod12, cs, sn, w, lc, kw):
    n, d = xs.shape
    row = lambda i: (i, 0)
    per = TM // WINDOW
    return pl.pallas_call(
        functools.partial(_proj_wa_kernel, tm=TM, lc=lc, d=d, kw=kw),
        grid=(n // TM,),
        in_specs=[pl.BlockSpec((TM, d), row), _const_spec((1, d)), _const_spec((2 * N_MOD, d)),
                  pl.BlockSpec((TM, LANES), row), pl.BlockSpec((TM, LANES), row),
                  _const_spec(w.shape)],
        out_specs=[pl.BlockSpec((TM, d), row), pl.BlockSpec((TM, kw), row),
                   pl.BlockSpec((per, kw, WINDOW), lambda i: (i, 0, 0))],
        out_shape=[jax.ShapeDtypeStruct((n, d), BF16), jax.ShapeDtypeStruct((n, kw), BF16),
                   jax.ShapeDtypeStruct((n // WINDOW, kw, WINDOW), BF16)],
        compiler_params=_cparams(1),
        name="proj_window_attn",
    )(xs, g, mod12, cs, sn, w)


def _log1p_unit(x):
    return jnp.log(1.0 + x)


def _proj_hg_kernel(x_ref, g_ref, mod_ref, lla_ref, l1m_ref, w_ref,
                    q_ref, gate_ref, v_ref, vt_ref, og_ref, *, tm, lc, d):
    is_ctx = _is_ctx_rows(pl.program_id(0), tm, lc)
    h = _norm_mod(x_ref[...], g_ref[...], _pick(mod_ref, 0, is_ctx), _pick(mod_ref, 1, is_ctx))
    hb = h.astype(BF16)
    y = jnp.dot(hb, w_ref[...], preferred_element_type=F32)
    q_ref[...] = _silu(y[:, :d]).astype(BF16)
    for dr in range(2):
        f = y[:, (1 + dr) * d:(2 + dr) * d]
        log_sig = jnp.minimum(f, 0.0) - _log1p_unit(jnp.exp(-jnp.abs(f)))
        a = lla_ref[dr:dr + 1, :]
        c = l1m_ref[dr:dr + 1, :] + log_sig
        gate_ref[dr] = jnp.maximum(a, c) + _log1p_unit(jnp.exp(-jnp.abs(a - c)))
    v = y[:, 3 * d:4 * d]
    v_ref[...] = v.astype(BF16)
    vt_ref[0] = v.T.astype(BF16)
    og_ref[...] = _silu(y[:, 4 * d:]).astype(BF16)


def _proj_hg_call(xs, g, mod12, lla, l1m, w, lc):
    n, d = xs.shape
    nt = n // TM
    row = lambda i: (i, 0)
    return pl.pallas_call(
        functools.partial(_proj_hg_kernel, tm=TM, lc=lc, d=d),
        grid=(nt,),
        in_specs=[pl.BlockSpec((TM, d), row), _const_spec((1, d)), _const_spec((2 * N_MOD, d)),
                  _const_spec((2, d)), _const_spec((2, d)), _const_spec(w.shape)],
        out_specs=[pl.BlockSpec((TM, d), row), pl.BlockSpec((2, TM, d), lambda i: (0, i, 0)),
                   pl.BlockSpec((TM, d), row), pl.BlockSpec((1, d, TM), lambda i: (i, 0, 0)),
                   pl.BlockSpec((TM, d), row)],
        out_shape=[jax.ShapeDtypeStruct((n, d), BF16), jax.ShapeDtypeStruct((2, n, d), F32),
                   jax.ShapeDtypeStruct((n, d), BF16), jax.ShapeDtypeStruct((nt, d, TM), BF16),
                   jax.ShapeDtypeStruct((n, d), BF16)],
        compiler_params=_cparams(1),
        name="proj_hgrn",
    )(xs, g, mod12, lla, l1m, w)


def _diff_attn_kernel(lam_ref, q_ref, k_ref, vt_ref, g_ref, o_ref, m_ref, acc_ref,
                      sa_ref, sb_ref, ma_ref, mb_ref, *, tq, tk, lc, n, unroll, post_scale):
    qi = pl.program_id(1)
    q = q_ref[...]
    lane = lax.broadcasted_iota(jnp.int32, (1, LANES), 1)
    zero = jnp.zeros_like(q)
    qcat = jnp.concatenate([jnp.where(lane < HEAD_DIM, q, zero),
                            jnp.where(lane >= HEAD_DIM, q, zero)], axis=0)
    sub = vt_ref.shape[2]
    n_chunks = (n - lc) // tk

    def scores(kc):
        return lax.dot_general(kc, qcat, NT_DIMS, preferred_element_type=F32)

    def pv(slab0, n_slabs, p):
        out = None
        for j in range(n_slabs):
            t = jnp.dot(vt_ref[slab0 + j], p[j * sub:(j + 1) * sub, :],
                        preferred_element_type=F32)
            out = t if out is None else out + t
        return out

    def absorb_context():
        s = scores(k_ref[0:lc, :])
        mx = jnp.max(s, axis=0, keepdims=True)
        m_ref[...] = mx
        acc_ref[...] = pv(0, lc // sub, jnp.exp2((s - mx).astype(BF16)))

    def qk(c, s_ref, mx_ref):
        off = pl.multiple_of(lc + c * tk, LANES)
        s = scores(k_ref[pl.ds(off, tk), :])
        s_ref[...] = s
        mx_ref[...] = jnp.max(s, axis=0, keepdims=True)

    def absorb(c, s_ref, mx_ref):
        m_old = m_ref[...]
        m_new = jnp.maximum(m_old, mx_ref[...])
        alpha = jnp.exp2(m_old - m_new)
        p = jnp.exp2(s_ref[...] - m_new).astype(BF16)
        m_ref[...] = m_new
        acc_ref[...] = alpha * acc_ref[...] + pv((lc + c * tk) // sub, tk // sub, p)

    @pl.when(qi * tq < lc)
    def _():
        absorb_context()

    @pl.when(qi * tq >= lc)
    def _():
        qk(0, sa_ref, ma_ref)
        absorb_context()
        bufs = ((sa_ref, ma_ref), (sb_ref, mb_ref))

        def group(c0, count, last):
            for j in range(count):
                if not (last and j == count - 1):
                    qk(c0 + j + 1, *bufs[(j + 1) % 2])
                absorb(c0 + j, *bufs[j % 2])

        def body(i, carry):
            group(unroll * i, unroll, False)
            return carry

        lax.fori_loop(0, n_chunks // unroll - 1, body, 0)
        group(n_chunks - unroll, unroll, True)

    acc = acc_ref[0:LANES, :] * (1.0 / acc_ref[LANES:LANES + 1, :])
    o = acc[:, :tq] - lam_ref[0] * acc[:, tq:]
    o = o * lax.rsqrt(jnp.mean(o * o, axis=0, keepdims=True) + RMS_EPS)
    o = o * (g_ref[...] * post_scale)
    o_ref[...] = o.T.astype(BF16)


def _diff_attn_call(lam, q, k, vt3, subln_g, lc, post_scale):
    n, d = q.shape
    nh = d // LANES
    nslab, vrows, sub = vt3.shape
    vrows //= nh
    tk = min(DA_TK, (n - lc) // 2)
    assert lc % DA_TQ == 0 and n % DA_TQ == 0 and tk % sub == 0 and lc % sub == 0
    assert (n - lc) % (2 * tk) == 0
    return pl.pallas_call(
        functools.partial(_diff_attn_kernel, tq=DA_TQ, tk=tk, lc=lc, n=n,
                          unroll=4 if (n - lc) % (4 * tk) == 0 else 2, post_scale=post_scale),
        grid=(nh, n // DA_TQ),
        in_specs=[pl.BlockSpec(memory_space=pltpu.SMEM),
                  pl.BlockSpec((DA_TQ, LANES), lambda h, i: (i, h)),
                  pl.BlockSpec((n, LANES), lambda h, i: (0, h)),
                  pl.BlockSpec((nslab, vrows, sub), lambda h, i: (0, h, 0)),
                  pl.BlockSpec((LANES, 1), lambda h, i: (0, 0))],
        out_specs=pl.BlockSpec((DA_TQ, LANES), lambda h, i: (i, h)),
        out_shape=jax.ShapeDtypeStruct((n, d), BF16),
        scratch_shapes=[pltpu.VMEM((1, 2 * DA_TQ), F32),
                        pltpu.VMEM((vrows, 2 * DA_TQ), F32),
                        pltpu.VMEM((tk, 2 * DA_TQ), F32), pltpu.VMEM((tk, 2 * DA_TQ), F32),
                        pltpu.VMEM((1, 2 * DA_TQ), F32), pltpu.VMEM((1, 2 * DA_TQ), F32)],
        compiler_params=_cparams(2),
        name="diff_attention",
    )(lam, q, k, vt3, subln_g)


def _win_attn_kernel(sink_ref, q_ref, kc_ref, kp_ref, ko_ref, kn_ref,
                     vc_ref, vp_ref, vo_ref, vn_ref, o_ref, *, tq, lc, n, rep):
    j = pl.program_id(0)
    g = pl.program_id(1)
    w = WINDOW
    kc = kc_ref[...]
    kwin = jnp.concatenate([kp_ref[...], ko_ref[...], kn_ref[...]], axis=0)
    vtc = jnp.concatenate([vc_ref[s] for s in range(lc // w)], axis=1)
    vtwin = jnp.concatenate([vp_ref[0]] + [vo_ref[s] for s in range(tq // w)] + [vn_ref[0]],
                            axis=1)
    a = lax.broadcasted_iota(jnp.int32, (w, w), 0)
    b = lax.broadcasted_iota(jnp.int32, (w, w), 1)
    lane = lax.broadcasted_iota(jnp.int32, (1, LANES), 1)
    low = lane < HEAD_DIM
    for t in range(tq // w):
        q0 = j * tq + t * w
        latent = q0 >= lc
        own_bias = jnp.where(latent, 0.0, NEG_BIG)
        prev_bias = jnp.where(jnp.logical_and(latent, q0 - w >= lc), 0.0, NEG_BIG)
        next_bias = jnp.where(jnp.logical_and(latent, q0 + w < n), 0.0, NEG_BIG)
        bias = jnp.concatenate(
            [jnp.where(a >= b, prev_bias, NEG_BIG),
             jnp.full((w, w), own_bias, F32),
             jnp.where(a <= b, next_bias, NEG_BIG)], axis=0)
        qs, sinks = [], []
        for r in range(rep):
            qc = q_ref[t * w:(t + 1) * w, (r // 2) * LANES:(r // 2 + 1) * LANES]
            qs.append(jnp.where(low if r % 2 == 0 else ~low, qc, jnp.zeros_like(qc)))
            sinks.append(jnp.full((1, w), sink_ref[g * rep + r] * LOG2E, F32))
        qst = jnp.concatenate(qs, axis=0)
        sink = jnp.concatenate(sinks, axis=1)
        s_c = lax.dot_general(kc, qst, NT_DIMS, preferred_element_type=F32)
        s_w = (lax.dot_general(kwin[t * w:(t + 3) * w, :], qst, NT_DIMS,
                               preferred_element_type=F32)
               + jnp.concatenate([bias] * rep, axis=1))
        mx = jnp.maximum(jnp.maximum(jnp.max(s_c, axis=0, keepdims=True),
                                     jnp.max(s_w, axis=0, keepdims=True)), sink)
        e_c = jnp.exp2(s_c - mx)
        e_w = jnp.exp2(s_w - mx)
        den = (jnp.exp2(sink - mx) + jnp.sum(e_c, axis=0, keepdims=True)
               + jnp.sum(e_w, axis=0, keepdims=True))
        ot = (jnp.dot(vtc, e_c.astype(BF16), preferred_element_type=F32)
              + jnp.dot(vtwin[:, t * w:(t + 3) * w], e_w.astype(BF16),
                        preferred_element_type=F32)) * (1.0 / den)
        heads = [ot[:, r * w:(r + 1) * w].T for r in range(rep)]
        for c in range(rep // 2):
            o_ref[t * w:(t + 1) * w, c * LANES:(c + 1) * LANES] = jnp.where(
                low, heads[2 * c], heads[2 * c + 1]).astype(BF16)


def _win_attn_call(sinks, q, k2, vt2, lc):
    n, d = q.shape
    ng = WA_KV_HEADS
    rep = d // HEAD_DIM // ng
    gw = rep * HEAD_DIM
    tq = WA_TQ
    assert n % tq == 0 and tq % WINDOW == 0 and lc % WINDOW == 0 and rep % 2 == 0
    r = tq // WINDOW
    last = n // WINDOW - 1
    prev = lambda j: jnp.maximum(j * r - 1, 0)
    nxt = lambda j: jnp.minimum((j + 1) * r, last)
    k_specs = [pl.BlockSpec((lc, LANES), lambda j, g: (0, g)),
               pl.BlockSpec((WINDOW, LANES), lambda j, g: (prev(j), g)),
               pl.BlockSpec((tq, LANES), lambda j, g: (j, g)),
               pl.BlockSpec((WINDOW, LANES), lambda j, g: (nxt(j), g))]
    vt_specs = [pl.BlockSpec((lc // WINDOW, LANES, WINDOW), lambda j, g: (0, g, 0)),
                pl.BlockSpec((1, LANES, WINDOW), lambda j, g: (prev(j), g, 0)),
                pl.BlockSpec((r, LANES, WINDOW), lambda j, g: (j, g, 0)),
                pl.BlockSpec((1, LANES, WINDOW), lambda j, g: (nxt(j), g, 0))]
    return pl.pallas_call(
        functools.partial(_win_attn_kernel, tq=tq, lc=lc, n=n, rep=rep),
        grid=(n // tq, ng),
        in_specs=[pl.BlockSpec(memory_space=pltpu.SMEM),
                  pl.BlockSpec((tq, gw), lambda j, g: (j, g))] + k_specs + vt_specs,
        out_specs=pl.BlockSpec((tq, gw), lambda j, g: (j, g)),
        out_shape=jax.ShapeDtypeStruct((n, d), BF16),
        compiler_params=_cparams(2),
        name="window_attention",
    )(sinks, q, k2, k2, k2, k2, vt2, vt2, vt2, vt2)


def _hgrn_consts(chunk):
    n_lv = int(math.log2(chunk))
    walls, masks_all = [], []
    for direction in range(2):
        tau = np.arange(chunk) if direction == 0 else chunk - 1 - np.arange(chunk)
        tr, tc = tau[:, None], tau[None, :]
        blocks = [tc <= tr, tc > tr]
        masks = []
        for lv in range(n_lv):
            h = chunk >> (lv + 1)
            mid = (tr // (2 * h)) * 2 * h + h - 1
            second_r = (tr % (2 * h)) >= h
            second_c = (tc % (2 * h)) >= h
            blocks.append(np.where(second_r, (tc > mid) & (tc <= tr), (tc > tr) & (tc <= mid)))
            masks.append(second_r & ~second_c & ((tr // (2 * h)) == (tc // (2 * h))))
        masks.append(tr == tc)
        w = np.concatenate(blocks, axis=0).astype(np.float32)
        walls.append(np.concatenate([w, w], axis=1))
        masks_all.append(np.stack(masks).astype(np.float32))
    return np.stack(walls), np.stack(masks_all)


def _hgrn_kernel(wall_ref, mask_ref, q_ref, g_ref, v_ref, vt_ref, o_ref, st_ref,
                 *, chunk, n_lv, n_heads):
    @pl.when(pl.program_id(1) == 0)
    def _():
        st_ref[...] = jnp.zeros_like(st_ref)

    g_all = g_ref[0]
    g_hi = g_all.astype(BF16)
    g_lo = (g_all - g_hi.astype(F32)).astype(BF16)
    sums_all = jnp.dot(wall_ref[0], jnp.concatenate([g_hi, g_lo], axis=0),
                       preferred_element_type=F32)
    for hh in range(n_heads):
        sl = slice(hh * LANES, (hh + 1) * LANES)
        sums = sums_all[:, sl]
        g = g_all[:, sl]
        qb16 = q_ref[:, sl]
        q = qb16.astype(F32)
        k = 1.0 - jnp.exp(g)
        st = st_ref[hh]

        qb = (q * jnp.exp(sums[0:chunk])).astype(BF16)
        o = lax.dot_general(qb, st.astype(BF16), NT_DIMS, preferred_element_type=F32)

        att = lax.dot_general(qb16, k.astype(BF16), NT_DIMS,
                              preferred_element_type=F32) * mask_ref[0, n_lv]
        for lv in range(n_lv):
            e = jnp.exp(sums[(2 + lv) * chunk:(3 + lv) * chunk])
            a = lax.dot_general((q * e).astype(BF16), (k * e).astype(BF16), NT_DIMS,
                                preferred_element_type=F32)
            att = att + a * mask_ref[0, lv]
        o_ref[0, :, sl] = o + jnp.dot(att.astype(BF16), v_ref[:, sl], preferred_element_type=F32)

        ks = (k * jnp.exp(sums[chunk:2 * chunk])).astype(BF16)
        decay = jnp.exp(jnp.sum(g, axis=0, keepdims=True))
        st_ref[hh] = st * decay + jnp.dot(vt_ref[0, sl, :], ks, preferred_element_type=F32)


def _hgrn_call(qh, gates, v, vt3, lc):
    n, d = qh.shape
    c = HG_CHUNK
    nh = d // LANES
    nc, ncc = n // c, lc // c
    per_slab = vt3.shape[2] // c
    assert n % c == 0 and lc % c == 0 and vt3.shape[2] % c == 0
    n_lv = int(math.log2(c))
    wall_np, mask_np = _hgrn_consts(c)
    wall = jnp.asarray(wall_np, dtype=BF16)
    masks = jnp.asarray(mask_np, dtype=F32)

    def blk(dr, s):
        back = jnp.where(s < ncc, ncc - 1 - s, nc - 1 - (s - ncc))
        return jnp.where(dr == 0, s, back)

    return pl.pallas_call(
        functools.partial(_hgrn_kernel, chunk=c, n_lv=n_lv, n_heads=nh),
        grid=(2, nc),
        in_specs=[pl.BlockSpec((1,) + wall_np.shape[1:], lambda dr, s: (dr, 0, 0)),
                  pl.BlockSpec((1,) + mask_np.shape[1:], lambda dr, s: (dr, 0, 0, 0)),
                  pl.BlockSpec((c, d), lambda dr, s: (blk(dr, s), 0)),
                  pl.BlockSpec((1, c, d), lambda dr, s: (dr, blk(dr, s), 0)),
                  pl.BlockSpec((c, d), lambda dr, s: (blk(dr, s), 0)),
                  pl.BlockSpec((1, d, c),
                               lambda dr, s: (blk(dr, s) // per_slab, 0, blk(dr, s) % per_slab))],
        out_specs=pl.BlockSpec((1, c, d), lambda dr, s: (dr, blk(dr, s), 0)),
        out_shape=jax.ShapeDtypeStruct((2, n, d), F32),
        scratch_shapes=[pltpu.VMEM((nh, LANES, LANES), F32)],
        compiler_params=_cparams(2),
        name="hgrn_scan",
    )(wall, masks, qh, gates, v, vt3)


def _post_kernel(*refs, mode, final, tm, lc, n, d, f, tile0):
    it = iter(refs)
    x_ref, mod_ref, g2_ref, wo_ref, win_ref, wout_ref = (next(it) for _ in range(6))
    if mode == "plain":
        a_ref = next(it)
    elif mode == "conv":
        b_ref, z_ref, zp_ref, zn_ref, cw_ref = (next(it) for _ in range(5))
    else:
        ofw_ref, obw_ref, og_ref, gn_ref = (next(it) for _ in range(4))
    fin_ref = next(it) if final else None
    o_ref = next(it)

    ti = pl.program_id(0) + tile0
    is_ctx = _is_ctx_rows(ti, tm, lc)
    if mode == "plain":
        a = a_ref[...]
    elif mode == "conv":
        z = z_ref[...]
        rl = lax.broadcasted_iota(jnp.int32, (tm, 1), 0)
        row = ti * tm + rl
        z_prev = jnp.where(rl == 0, zp_ref[7:8, :], pltpu.roll(z, 1, 0))
        z_next = jnp.where(rl == tm - 1, zn_ref[0:1, :], pltpu.roll(z, tm - 1, 0))
        z_prev = jnp.where((row == 0) | (row == lc), 0.0, z_prev)
        z_next = jnp.where((row == lc - 1) | (row == n - 1), 0.0, z_next)
        conv = cw_ref[0:1, :] * z_prev + cw_ref[1:2, :] * z + cw_ref[2:3, :] * z_next
        a = (b_ref[...].astype(F32) * conv).astype(BF16)
    else:
        gn = gn_ref[...]
        heads = []
        for hh in range(d // LANES):
            sl = slice(hh * LANES, (hh + 1) * LANES)
            oh = _rms(ofw_ref[0, :, sl] + obw_ref[0, :, sl])
            heads.append((oh * gn * og_ref[:, sl].astype(F32)).astype(BF16))
        a = jnp.concatenate(heads, axis=1)

    y = jnp.dot(a, wo_ref[...], preferred_element_type=F32)
    x = x_ref[...] + _pick(mod_ref, 2, is_ctx) * y
    h = _norm_mod(x, g2_ref[...], _pick(mod_ref, 3, is_ctx), _pick(mod_ref, 4, is_ctx)).astype(BF16)
    acc = jnp.zeros((tm, d), F32)
    for c0 in range(0, f, FFN_CHUNK):
        u = jnp.dot(h, win_ref[:, c0:c0 + FFN_CHUNK], preferred_element_type=F32)
        w = jnp.dot(h, win_ref[:, f + c0:f + c0 + FFN_CHUNK], preferred_element_type=F32)
        act = (_silu(u) * w).astype(BF16)
        acc = acc + jnp.dot(act, wout_ref[c0:c0 + FFN_CHUNK, :], preferred_element_type=F32)
    x = x + _pick(mod_ref, 5, is_ctx) * acc
    if final:
        x = _rms(x) * fin_ref[...]
    o_ref[...] = x


def _post_call(xs, mod12, g2, wo, win, wout, mixer_inputs, mode, lc, final_g=None):
    n, d = xs.shape
    f = wout.shape[0]
    final = final_g is not None
    tm = TM if final else next(t for t in POST_TM_CHOICES if n % t == 0)
    assert f % FFN_CHUNK == 0 and lc % TM == 0 and n % tm == 0
    tile0 = lc // tm if final else 0
    nt = n // tm - tile0
    row = lambda i: (i + tile0, 0)
    in_specs = [pl.BlockSpec((tm, d), row), _const_spec((2 * N_MOD, d)), _const_spec((1, d)),
                _const_spec(wo.shape), _const_spec(win.shape), _const_spec(wout.shape)]
    args = [xs, mod12, g2, wo, win, wout]
    if mode == "plain":
        in_specs += [pl.BlockSpec((tm, d), row)]
    elif mode == "conv":
        r8 = tm // 8
        last8 = n // 8 - 1
        in_specs += [pl.BlockSpec((tm, d), row), pl.BlockSpec((tm, d), row),
                     pl.BlockSpec((8, d), lambda i: (jnp.maximum((i + tile0) * r8 - 1, 0), 0)),
                     pl.BlockSpec((8, d), lambda i: (jnp.minimum((i + tile0 + 1) * r8, last8), 0)),
                     _const_spec((SC_WIDTH, d))]
    else:
        in_specs += [pl.BlockSpec((1, tm, d), lambda i: (0, i + tile0, 0)),
                     pl.BlockSpec((1, tm, d), lambda i: (1, i + tile0, 0)),
                     pl.BlockSpec((tm, d), row), _const_spec((1, LANES))]
    args += list(mixer_inputs)
    if final:
        in_specs += [_const_spec((1, d))]
        args += [final_g]
    return pl.pallas_call(
        functools.partial(_post_kernel, mode=mode, final=final, tm=tm, lc=lc, n=n, d=d, f=f,
                          tile0=tile0),
        grid=(nt,),
        in_specs=in_specs,
        out_specs=pl.BlockSpec((tm, d), lambda i: (i, 0)),
        out_shape=jax.ShapeDtypeStruct((nt * tm, d), F32),
        compiler_params=_cparams(1),
        name="post_" + mode + ("_final" if final else ""),
    )(*args)


def _rope_tables(l, lc):
    rows = l // GRID_W
    r = jnp.repeat(jnp.arange(rows), GRID_W).astype(F32)
    c = jnp.tile(jnp.arange(GRID_W), rows).astype(F32)
    half = HEAD_DIM // 2
    inv_freq = 1.0 / (ROPE_BASE ** (jnp.arange(0, half, 2, dtype=F32) / half))
    ar, ac = r[:, None] * inv_freq, c[:, None] * inv_freq
    cs = jnp.concatenate([jnp.cos(ar), jnp.cos(ar), jnp.cos(ac), jnp.cos(ac)], axis=-1)
    sn = jnp.concatenate([-jnp.sin(ar), jnp.sin(ar), -jnp.sin(ac), jnp.sin(ac)], axis=-1)
    cs = jnp.concatenate([jnp.ones((lc, HEAD_DIM), F32), cs], axis=0)
    sn = jnp.concatenate([jnp.zeros((lc, HEAD_DIM), F32), sn], axis=0)
    return jnp.tile(cs, (1, 2)), jnp.tile(sn, (1, 2))


def kernel(x, c, ctx, c_ctx, ada_w, ada_b, norm1_g, norm2_g, ffn_w_in, ffn_w_out, final_g,
           da_wqkv, da_lambda, da_subln_g, da_wo, sc_w_in, sc_conv_w, sc_w_out,
           wa_wqkv, wa_sinks, wa_wo, hg_w_in, hg_lb, hg_gnorm_g, hg_wo):
    b, l, d = x.shape
    lc = ctx.shape[1]
    depth = ada_w.shape[0]
    assert b == 1 and d % LANES == 0 and lc % TM == 0 and l % TM == 0
    n = lc + l
    xs = jnp.concatenate([ctx[0], x[0]], axis=0)

    cc = jnp.zeros((8, d), F32).at[0].set(c_ctx).at[1].set(c[0])
    mods = _mod_call(cc, ada_w, ada_b)
    cs, sn = _rope_tables(l, lc)

    n_mix = 4
    for i in range(depth):
        mixer, slot = i % n_mix, i // n_mix
        last = i == depth - 1
        mod12 = jnp.concatenate([mods[i, 0].reshape(N_MOD, d), mods[i, 1].reshape(N_MOD, d)], axis=0)
        g1 = norm1_g[i].reshape(1, d)
        g2 = norm2_g[i].reshape(1, d)
        win = ffn_w_in[i].astype(BF16)
        wout = ffn_w_out[i].astype(BF16)
        fin = final_g.reshape(1, d) if last else None

        if mixer == 0:
            w = da_wqkv[slot]
            lam = da_lambda[slot].astype(F32)
            lam_init = 0.8 - 0.6 * math.exp(-0.3 * i)
            lam_full = (jnp.exp(jnp.sum(lam[0] * lam[1])) - jnp.exp(jnp.sum(lam[2] * lam[3]))
                        + lam_init).reshape(1)
            q, k, vt3 = _proj_da_call(xs, g1, mod12, cs, sn, w[:, :2 * d].astype(BF16),
                                      w[:, 2 * d:].T.astype(BF16), lc)
            a = _diff_attn_call(lam_full, q, k, vt3, da_subln_g[slot].reshape(LANES, 1), lc,
                                1.0 - lam_init)
            xs = _post_call(xs, mod12, g2, da_wo[slot].astype(BF16), win, wout, [a], "plain", lc, fin)
        elif mixer == 1:
            bg, z = _proj_sc_call(xs, g1, mod12, sc_w_in[slot].astype(BF16), lc)
            xs = _post_call(xs, mod12, g2, sc_w_out[slot].astype(BF16), win, wout,
                            [bg, z, z, z, sc_conv_w[slot]], "conv", lc, fin)
        elif mixer == 2:
            w = wa_wqkv[slot]
            kw = WA_KV_HEADS * HEAD_DIM
            dup = lambda m: jnp.tile(m.reshape(d, WA_KV_HEADS, 1, HEAD_DIM), (1, 1, 2, 1)).reshape(d, 2 * kw)
            w2 = jnp.concatenate([w[:, :d], dup(w[:, d:d + kw]), dup(w[:, d + kw:])], axis=1)
            q, k2, v2 = _proj_wa_call(xs, g1, mod12, cs, sn, w2.astype(BF16), lc, 2 * kw)
            a = _win_attn_call(wa_sinks[slot].astype(F32), q, k2, v2, lc)
            xs = _post_call(xs, mod12, g2, wa_wo[slot].astype(BF16), win, wout, [a], "plain", lc, fin)
        else:
            w = hg_w_in[slot]
            p = jax.nn.softmax(hg_lb.astype(F32), axis=1)
            lb = (jnp.cumsum(p, axis=1) - p[:, :1])[:, i]
            qh, gates, v, vt3, og = _proj_hg_call(xs, g1, mod12, jnp.log(lb), jnp.log1p(-lb),
                                                  w.astype(BF16), lc)
            o2 = _hgrn_call(qh, gates, v, vt3, lc)
            xs = _post_call(xs, mod12, g2, hg_wo[slot].astype(BF16), win, wout,
                            [o2, o2, og, hg_gnorm_g[slot].reshape(1, LANES)], "hgrn", lc, fin)
    return xs.reshape(1, l, d)
```

```python
import functools
import math

import numpy as np
import jax
import jax.numpy as jnp
from jax import lax
from jax.experimental import pallas as pl
from jax.experimental.pallas import tpu as pltpu

F32 = jnp.float32
BF16 = jnp.bfloat16

HEAD_DIM = 64
GRID_W = 64
ROPE_BASE = 10000.0
RMS_EPS = 1e-6
N_MOD = 6
WINDOW = 128
WA_KV_HEADS = 4
SC_WIDTH = 3
LANES = 128
NEG_BIG = -1e30
LOG2E = math.log2(math.e)

TM_CHOICES = (640, 256)
SLAB = 128
HG_CHUNK = 128
DA_TQ = 256
DA_TK = 1024
DA_ONES_ROWS = 16
WA_TQ = 256
FFN_CHUNK = 256
VMEM_LIMIT = 56 * 1024 * 1024

NT_DIMS = (((1,), (1,)), ((), ()))


def _cparams(n_axes):
    return pltpu.CompilerParams(
        dimension_semantics=("arbitrary",) * n_axes, vmem_limit_bytes=VMEM_LIMIT)


def _row_tile(n):
    return next(t for t in TM_CHOICES if n % t == 0)


def _const_spec(shape):
    nd = len(shape)
    return pl.BlockSpec(shape, lambda *_: (0,) * nd, pipeline_mode=pl.Buffered(1))


def _is_ctx_rows(tile_idx, tm, lc):
    row = tile_idx * tm + lax.broadcasted_iota(jnp.int32, (tm, 1), 0)
    return row < lc


def _pick(mod_ref, k, is_ctx):
    return jnp.where(is_ctx, mod_ref[k:k + 1, :], mod_ref[N_MOD + k:N_MOD + k + 1, :])


def _rms(x):
    return x * lax.rsqrt(jnp.mean(x * x, axis=-1, keepdims=True) + RMS_EPS)


def _norm_mod(x, g, shift, scale):
    return (_rms(x) * g) * (1.0 + scale) + shift


def _silu(x):
    return x * (1.0 / (1.0 + jnp.exp(-x)))


def _rope(chunk, cs, sn, first_half):
    partner = jnp.where(first_half, pltpu.roll(chunk, LANES - 16, 1), pltpu.roll(chunk, 16, 1))
    return chunk * cs + partner * sn


def _first_half_lanes():
    lane = lax.broadcasted_iota(jnp.int32, (1, LANES), 1)
    return (lane & 31) < 16


def _mod_kernel(cc_ref, w_ref, b_ref, o_ref):
    a = _silu(cc_ref[...])
    o_ref[0] = jnp.dot(a, w_ref[0], precision=lax.Precision.HIGHEST,
                       preferred_element_type=F32) + b_ref[0]


def _mod_call(cc, ada_w, ada_b):
    depth, d, nd = ada_w.shape
    tn = 1536
    assert nd % tn == 0
    return pl.pallas_call(
        _mod_kernel,
        grid=(depth, nd // tn),
        in_specs=[pl.BlockSpec((8, d), lambda i, j: (0, 0)),
                  pl.BlockSpec((1, d, tn), lambda i, j: (i, 0, j)),
                  pl.BlockSpec((1, 1, tn), lambda i, j: (i, 0, j))],
        out_specs=pl.BlockSpec((1, 8, tn), lambda i, j: (i, 0, j)),
        out_shape=jax.ShapeDtypeStruct((depth, 8, nd), F32),
        compiler_params=_cparams(2),
        name="adaln_mod",
    )(cc, ada_w, ada_b.reshape(depth, 1, nd))


def _proj_da_kernel(x_ref, g_ref, mod_ref, cs_ref, sn_ref, wqk_ref, wvt_ref,
                    q_ref, k_ref, vt_ref, *, tm, lc, d):
    is_ctx = _is_ctx_rows(pl.program_id(0), tm, lc)
    h = _norm_mod(x_ref[...], g_ref[...], _pick(mod_ref, 0, is_ctx), _pick(mod_ref, 1, is_ctx))
    hb = h.astype(BF16)
    qk = jnp.dot(hb, wqk_ref[...], preferred_element_type=F32)
    cs, sn, first = cs_ref[...], sn_ref[...], _first_half_lanes()
    nq = d // LANES
    for j in range(2 * nq):
        r = _rope(qk[:, j * LANES:(j + 1) * LANES], cs, sn, first)
        if j < nq:
            q_ref[:, j * LANES:(j + 1) * LANES] = (r * (LOG2E * HEAD_DIM ** -0.5)).astype(BF16)
        else:
            k_ref[:, (j - nq) * LANES:(j - nq + 1) * LANES] = r.astype(BF16)
    vt = lax.dot_general(wvt_ref[...], hb, NT_DIMS, preferred_element_type=F32)
    ones = jnp.ones((DA_ONES_ROWS, SLAB), BF16)
    vr = LANES + DA_ONES_ROWS
    for s in range(tm // SLAB):
        for hh in range(d // LANES):
            vt_ref[s, hh * vr:hh * vr + LANES, :] = (
                vt[hh * LANES:(hh + 1) * LANES, s * SLAB:(s + 1) * SLAB].astype(BF16))
            vt_ref[s, hh * vr + LANES:(hh + 1) * vr, :] = ones


def _proj_da_call(xs, g, mod12, cs, sn, wqk, wvt, lc):
    n, d = xs.shape
    tm = _row_tile(n)
    row = lambda i: (i, 0)
    dv = d // LANES * (LANES + DA_ONES_ROWS)
    return pl.pallas_call(
        functools.partial(_proj_da_kernel, tm=tm, lc=lc, d=d),
        grid=(n // tm,),
        in_specs=[pl.BlockSpec((tm, d), row), _const_spec((1, d)), _const_spec((2 * N_MOD, d)),
                  pl.BlockSpec((tm, LANES), row), pl.BlockSpec((tm, LANES), row),
                  _const_spec(wqk.shape), _const_spec(wvt.shape)],
        out_specs=[pl.BlockSpec((tm, d), row), pl.BlockSpec((tm, d), row),
                   pl.BlockSpec((tm // SLAB, dv, SLAB), lambda i: (i, 0, 0))],
        out_shape=[jax.ShapeDtypeStruct((n, d), BF16), jax.ShapeDtypeStruct((n, d), BF16),
                   jax.ShapeDtypeStruct((n // SLAB, dv, SLAB), BF16)],
        compiler_params=_cparams(1),
        name="proj_diff_attn",
    )(xs, g, mod12, cs, sn, wqk, wvt)


def _proj_sc_kernel(x_ref, g_ref, mod_ref, w_ref, b_ref, z_ref, *, tm, lc, d):
    is_ctx = _is_ctx_rows(pl.program_id(0), tm, lc)
    h = _norm_mod(x_ref[...], g_ref[...], _pick(mod_ref, 0, is_ctx), _pick(mod_ref, 1, is_ctx))
    y = jnp.dot(h.astype(BF16), w_ref[...], preferred_element_type=F32)
    b_ref[...] = y[:, :d].astype(BF16)
    z_ref[...] = y[:, d:2 * d] * y[:, 2 * d:]


def _proj_sc_call(xs, g, mod12, w, lc):
    n, d = xs.shape
    tm = _row_tile(n)
    row = lambda i: (i, 0)
    return pl.pallas_call(
        functools.partial(_proj_sc_kernel, tm=tm, lc=lc, d=d),
        grid=(n // tm,),
        in_specs=[pl.BlockSpec((tm, d), row), _const_spec((1, d)), _const_spec((2 * N_MOD, d)),
                  _const_spec(w.shape)],
        out_specs=[pl.BlockSpec((tm, d), row), pl.BlockSpec((tm, d), row)],
        out_shape=[jax.ShapeDtypeStruct((n, d), BF16), jax.ShapeDtypeStruct((n, d), F32)],
        compiler_params=_cparams(1),
        name="proj_short_conv",
    )(xs, g, mod12, w)


def _proj_wa_kernel(x_ref, g_ref, mod_ref, cs_ref, sn_ref, w_ref, q_ref, k_ref, v_ref,
                    *, tm, lc, d, kw):
    is_ctx = _is_ctx_rows(pl.program_id(0), tm, lc)
    h = _norm_mod(x_ref[...], g_ref[...], _pick(mod_ref, 0, is_ctx), _pick(mod_ref, 1, is_ctx))
    y = jnp.dot(h.astype(BF16), w_ref[...], preferred_element_type=F32)
    cs, sn, first = cs_ref[...], sn_ref[...], _first_half_lanes()
    for j in range(d // LANES):
        r = _rope(y[:, j * LANES:(j + 1) * LANES], cs, sn, first)
        q_ref[:, j * LANES:(j + 1) * LANES] = (r * (LOG2E * HEAD_DIM ** -0.5)).astype(BF16)
    for j in range(kw // LANES):
        r = _rope(y[:, d + j * LANES:d + (j + 1) * LANES], cs, sn, first)
        k_ref[:, j * LANES:(j + 1) * LANES] = r.astype(BF16)
    vt = y[:, d + kw:].T
    for s in range(tm // WINDOW):
        v_ref[s] = vt[:, s * WINDOW:(s + 1) * WINDOW].astype(BF16)


def _proj_wa_call(xs, g, mod12, cs, sn, w, lc, kw):
    n, d = xs.shape
    tm = _row_tile(n)
    row = lambda i: (i, 0)
    per = tm // WINDOW
    return pl.pallas_call(
        functools.partial(_proj_wa_kernel, tm=tm, lc=lc, d=d, kw=kw),
        grid=(n // tm,),
        in_specs=[pl.BlockSpec((tm, d), row), _const_spec((1, d)), _const_spec((2 * N_MOD, d)),
                  pl.BlockSpec((tm, LANES), row), pl.BlockSpec((tm, LANES), row),
                  _const_spec(w.shape)],
        out_specs=[pl.BlockSpec((tm, d), row), pl.BlockSpec((tm, kw), row),
                   pl.BlockSpec((per, kw, WINDOW), lambda i: (i, 0, 0))],
        out_shape=[jax.ShapeDtypeStruct((n, d), BF16), jax.ShapeDtypeStruct((n, kw), BF16),
                   jax.ShapeDtypeStruct((n // WINDOW, kw, WINDOW), BF16)],
        compiler_params=_cparams(1),
        name="proj_window_attn",
    )(xs, g, mod12, cs, sn, w)


def _log1p_unit(x):
    return jnp.log(1.0 + x)


def _proj_hg_kernel(x_ref, g_ref, mod_ref, lla_ref, l1m_ref, w_ref,
                    q_ref, gate_ref, v_ref, vt_ref, og_ref, *, tm, lc, d):
    is_ctx = _is_ctx_rows(pl.program_id(0), tm, lc)
    h = _norm_mod(x_ref[...], g_ref[...], _pick(mod_ref, 0, is_ctx), _pick(mod_ref, 1, is_ctx))
    hb = h.astype(BF16)
    part = lambda j: jnp.dot(hb, w_ref[:, j * d:(j + 1) * d], preferred_element_type=F32)
    q_ref[...] = _silu(part(0)).astype(BF16)
    for dr in range(2):
        f = part(1 + dr)
        log_sig = jnp.minimum(f, 0.0) - _log1p_unit(jnp.exp(-jnp.abs(f)))
        a = lla_ref[dr:dr + 1, :]
        c = l1m_ref[dr:dr + 1, :] + log_sig
        gate_ref[dr] = jnp.maximum(a, c) + _log1p_unit(jnp.exp(-jnp.abs(a - c)))
    v = part(3)
    v_ref[...] = v.astype(BF16)
    vt = v.T
    for s in range(tm // SLAB):
        vt_ref[s] = vt[:, s * SLAB:(s + 1) * SLAB].astype(BF16)
    og_ref[...] = _silu(part(4)).astype(BF16)


def _proj_hg_call(xs, g, mod12, lla, l1m, w, lc):
    n, d = xs.shape
    tm = _row_tile(n)
    row = lambda i: (i, 0)
    return pl.pallas_call(
        functools.partial(_proj_hg_kernel, tm=tm, lc=lc, d=d),
        grid=(n // tm,),
        in_specs=[pl.BlockSpec((tm, d), row), _const_spec((1, d)), _const_spec((2 * N_MOD, d)),
                  _const_spec((2, d)), _const_spec((2, d)), _const_spec(w.shape)],
        out_specs=[pl.BlockSpec((tm, d), row), pl.BlockSpec((2, tm, d), lambda i: (0, i, 0)),
                   pl.BlockSpec((tm, d), row),
                   pl.BlockSpec((tm // SLAB, d, SLAB), lambda i: (i, 0, 0)),
                   pl.BlockSpec((tm, d), row)],
        out_shape=[jax.ShapeDtypeStruct((n, d), BF16), jax.ShapeDtypeStruct((2, n, d), F32),
                   jax.ShapeDtypeStruct((n, d), BF16), jax.ShapeDtypeStruct((n // SLAB, d, SLAB), BF16),
                   jax.ShapeDtypeStruct((n, d), BF16)],
        compiler_params=_cparams(1),
        name="proj_hgrn",
    )(xs, g, mod12, lla, l1m, w)


def _diff_attn_kernel(lam_ref, q_ref, k_ref, vt_ref, g_ref, o_ref, m_ref, acc_ref,
                      sa_ref, sb_ref, ma_ref, mb_ref, *, tq, tk, lc, n, unroll, post_scale):
    qi = pl.program_id(1)
    q = q_ref[...]
    lane = lax.broadcasted_iota(jnp.int32, (1, LANES), 1)
    zero = jnp.zeros_like(q)
    qcat = jnp.concatenate([jnp.where(lane < HEAD_DIM, q, zero),
                            jnp.where(lane >= HEAD_DIM, q, zero)], axis=0)
    sub = 2 * vt_ref.shape[2]
    n_chunks = (n - lc) // tk

    def scores(kc):
        return lax.dot_general(kc, qcat, NT_DIMS, preferred_element_type=F32)

    def pv(pair0, n_pairs, p):
        out = None
        for j in range(n_pairs):
            s0 = 2 * (pair0 + j)
            vt = jnp.concatenate([vt_ref[s0], vt_ref[s0 + 1]], axis=1)
            t = jnp.dot(vt, p[j * sub:(j + 1) * sub, :], preferred_element_type=F32)
            out = t if out is None else out + t
        return out

    def absorb_context():
        s = scores(k_ref[0:lc, :])
        mx = jnp.max(s, axis=0, keepdims=True)
        m_ref[...] = mx
        acc_ref[...] = pv(0, lc // sub, jnp.exp2((s - mx).astype(BF16)))

    def qk(c, s_ref, mx_ref):
        off = pl.multiple_of(lc + c * tk, LANES)
        s = scores(k_ref[pl.ds(off, tk), :])
        s_ref[...] = s
        mx_ref[...] = jnp.max(s, axis=0, keepdims=True)

    def absorb(c, s_ref, mx_ref):
        m_old = m_ref[...]
        m_new = jnp.maximum(m_old, mx_ref[...])
        alpha = jnp.exp2(m_old - m_new)
        p = jnp.exp2(s_ref[...] - m_new).astype(BF16)
        m_ref[...] = m_new
        acc_ref[...] = alpha * acc_ref[...] + pv((lc + c * tk) // sub, tk // sub, p)

    @pl.when(qi * tq < lc)
    def _():
        absorb_context()

    @pl.when(qi * tq >= lc)
    def _():
        qk(0, sa_ref, ma_ref)
        absorb_context()
        bufs = ((sa_ref, ma_ref), (sb_ref, mb_ref))

        def group(c0, count, last):
            for j in range(count):
                if not (last and j == count - 1):
                    qk(c0 + j + 1, *bufs[(j + 1) % 2])
                absorb(c0 + j, *bufs[j % 2])

        def body(i, carry):
            group(unroll * i, unroll, False)
            return carry

        lax.fori_loop(0, n_chunks // unroll - 1, body, 0)
        group(n_chunks - unroll, unroll, True)

    acc = acc_ref[0:LANES, :] * (1.0 / acc_ref[LANES:LANES + 1, :])
    o = acc[:, :tq] - lam_ref[0] * acc[:, tq:]
    o = o * lax.rsqrt(jnp.mean(o * o, axis=0, keepdims=True) + RMS_EPS)
    o = o * (g_ref[...] * post_scale)
    o_ref[...] = o.T.astype(BF16)


def _diff_attn_call(lam, q, k, vt3, subln_g, lc, post_scale):
    n, d = q.shape
    nh = d // LANES
    nslab, vrows, sub = vt3.shape
    vrows //= nh
    tk = min(DA_TK, (n - lc) // 2)
    assert lc % DA_TQ == 0 and n % DA_TQ == 0 and tk % (2 * sub) == 0 and lc % (2 * sub) == 0
    assert (n - lc) % (2 * tk) == 0
    return pl.pallas_call(
        functools.partial(_diff_attn_kernel, tq=DA_TQ, tk=tk, lc=lc, n=n,
                          unroll=4 if (n - lc) % (4 * tk) == 0 else 2, post_scale=post_scale),
        grid=(nh, n // DA_TQ),
        in_specs=[pl.BlockSpec(memory_space=pltpu.SMEM),
                  pl.BlockSpec((DA_TQ, LANES), lambda h, i: (i, h)),
                  pl.BlockSpec((n, LANES), lambda h, i: (0, h)),
                  pl.BlockSpec((nslab, vrows, sub), lambda h, i: (0, h, 0)),
                  pl.BlockSpec((LANES, 1), lambda h, i: (0, 0))],
        out_specs=pl.BlockSpec((DA_TQ, LANES), lambda h, i: (i, h)),
        out_shape=jax.ShapeDtypeStruct((n, d), BF16),
        scratch_shapes=[pltpu.VMEM((1, 2 * DA_TQ), F32),
                        pltpu.VMEM((vrows, 2 * DA_TQ), F32),
                        pltpu.VMEM((tk, 2 * DA_TQ), F32), pltpu.VMEM((tk, 2 * DA_TQ), F32),
                        pltpu.VMEM((1, 2 * DA_TQ), F32), pltpu.VMEM((1, 2 * DA_TQ), F32)],
        compiler_params=_cparams(2),
        name="diff_attention",
    )(lam, q, k, vt3, subln_g)


def _win_attn_kernel(sink_ref, q_ref, kc_ref, kp_ref, ko_ref, kn_ref,
                     vc_ref, vp_ref, vo_ref, vn_ref, o_ref, *, tq, lc, n, rep):
    j = pl.program_id(0)
    g = pl.program_id(1)
    w = WINDOW
    kc = kc_ref[...]
    kwin = jnp.concatenate([kp_ref[...], ko_ref[...], kn_ref[...]], axis=0)
    vtc = jnp.concatenate([vc_ref[s] for s in range(lc // w)], axis=1)
    vtwin = jnp.concatenate([vp_ref[0]] + [vo_ref[s] for s in range(tq // w)] + [vn_ref[0]],
                            axis=1)
    a = lax.broadcasted_iota(jnp.int32, (w, w), 0)
    b = lax.broadcasted_iota(jnp.int32, (w, w), 1)
    lane = lax.broadcasted_iota(jnp.int32, (1, LANES), 1)
    low = lane < HEAD_DIM
    for t in range(tq // w):
        q0 = j * tq + t * w
        latent = q0 >= lc
        own_bias = jnp.where(latent, 0.0, NEG_BIG)
        prev_bias = jnp.where(jnp.logical_and(latent, q0 - w >= lc), 0.0, NEG_BIG)
        next_bias = jnp.where(jnp.logical_and(latent, q0 + w < n), 0.0, NEG_BIG)
        bias = jnp.concatenate(
            [jnp.where(a >= b, prev_bias, NEG_BIG),
             jnp.full((w, w), own_bias, F32),
             jnp.where(a <= b, next_bias, NEG_BIG)], axis=0)
        qs, sinks = [], []
        for r in range(rep):
            qc = q_ref[t * w:(t + 1) * w, (r // 2) * LANES:(r // 2 + 1) * LANES]
            qs.append(jnp.where(low if r % 2 == 0 else ~low, qc, jnp.zeros_like(qc)))
            sinks.append(jnp.full((1, w), sink_ref[g * rep + r] * LOG2E, F32))
        qst = jnp.concatenate(qs, axis=0)
        sink = jnp.concatenate(sinks, axis=1)
        s_c = lax.dot_general(kc, qst, NT_DIMS, preferred_element_type=F32)
        s_w = (lax.dot_general(kwin[t * w:(t + 3) * w, :], qst, NT_DIMS,
                               preferred_element_type=F32)
               + jnp.concatenate([bias] * rep, axis=1))
        mx = jnp.maximum(jnp.maximum(jnp.max(s_c, axis=0, keepdims=True),
                                     jnp.max(s_w, axis=0, keepdims=True)), sink)
        e_c = jnp.exp2(s_c - mx)
        e_w = jnp.exp2(s_w - mx)
        den = (jnp.exp2(sink - mx) + jnp.sum(e_c, axis=0, keepdims=True)
               + jnp.sum(e_w, axis=0, keepdims=True))
        ot = (jnp.dot(vtc, e_c.astype(BF16), preferred_element_type=F32)
              + jnp.dot(vtwin[:, t * w:(t + 3) * w], e_w.astype(BF16),
                        preferred_element_type=F32)) * (1.0 / den)
        heads = [ot[:, r * w:(r + 1) * w].T for r in range(rep)]
        for c in range(rep // 2):
            o_ref[t * w:(t + 1) * w, c * LANES:(c + 1) * LANES] = jnp.where(
                low, heads[2 * c], heads[2 * c + 1]).astype(BF16)


def _win_attn_call(sinks, q, k2, vt2, lc):
    n, d = q.shape
    ng = WA_KV_HEADS
    rep = d // HEAD_DIM // ng
    gw = rep * HEAD_DIM
    tq = WA_TQ
    assert n % tq == 0 and tq % WINDOW == 0 and lc % WINDOW == 0 and rep % 2 == 0
    r = tq // WINDOW
    last = n // WINDOW - 1
    prev = lambda j: jnp.maximum(j * r - 1, 0)
    nxt = lambda j: jnp.minimum((j + 1) * r, last)
    k_specs = [pl.BlockSpec((lc, LANES), lambda j, g: (0, g)),
               pl.BlockSpec((WINDOW, LANES), lambda j, g: (prev(j), g)),
               pl.BlockSpec((tq, LANES), lambda j, g: (j, g)),
               pl.BlockSpec((WINDOW, LANES), lambda j, g: (nxt(j), g))]
    vt_specs = [pl.BlockSpec((lc // WINDOW, LANES, WINDOW), lambda j, g: (0, g, 0)),
                pl.BlockSpec((1, LANES, WINDOW), lambda j, g: (prev(j), g, 0)),
                pl.BlockSpec((r, LANES, WINDOW), lambda j, g: (j, g, 0)),
                pl.BlockSpec((1, LANES, WINDOW), lambda j, g: (nxt(j), g, 0))]
    return pl.pallas_call(
        functools.partial(_win_attn_kernel, tq=tq, lc=lc, n=n, rep=rep),
        grid=(n // tq, ng),
        in_specs=[pl.BlockSpec(memory_space=pltpu.SMEM),
                  pl.BlockSpec((tq, gw), lambda j, g: (j, g))] + k_specs + vt_specs,
        out_specs=pl.BlockSpec((tq, gw), lambda j, g: (j, g)),
        out_shape=jax.ShapeDtypeStruct((n, d), BF16),
        compiler_params=_cparams(2),
        name="window_attention",
    )(sinks, q, k2, k2, k2, k2, vt2, vt2, vt2, vt2)


def _hgrn_consts(chunk):
    n_lv = int(math.log2(chunk))
    walls, masks_all = [], []
    for direction in range(2):
        tau = np.arange(chunk) if direction == 0 else chunk - 1 - np.arange(chunk)
        tr, tc = tau[:, None], tau[None, :]
        blocks = [tc <= tr, tc > tr]
        masks = []
        for lv in range(n_lv):
            h = chunk >> (lv + 1)
            mid = (tr // (2 * h)) * 2 * h + h - 1
            second_r = (tr % (2 * h)) >= h
            second_c = (tc % (2 * h)) >= h
            blocks.append(np.where(second_r, (tc > mid) & (tc <= tr), (tc > tr) & (tc <= mid)))
            masks.append(second_r & ~second_c & ((tr // (2 * h)) == (tc // (2 * h))))
        masks.append(tr == tc)
        w = np.concatenate(blocks, axis=0).astype(np.float32)
        walls.append(np.concatenate([w, w], axis=1))
        masks_all.append(np.stack(masks).astype(np.float32))
    return np.stack(walls), np.stack(masks_all)


def _hgrn_kernel(wall_ref, mask_ref, q_ref, g_ref, v_ref, vt_ref, o_ref, st_ref,
                 *, chunk, n_lv, n_heads):
    @pl.when(pl.program_id(1) == 0)
    def _():
        st_ref[...] = jnp.zeros_like(st_ref)

    g_all = g_ref[0]
    g_hi = g_all.astype(BF16)
    g_lo = (g_all - g_hi.astype(F32)).astype(BF16)
    sums_all = jnp.dot(wall_ref[0], jnp.concatenate([g_hi, g_lo], axis=0),
                       preferred_element_type=F32)
    heads = [slice(hh * LANES, (hh + 1) * LANES) for hh in range(n_heads)]
    nt = lambda x, y: lax.dot_general(x, y, NT_DIMS, preferred_element_type=F32)
    q16 = q_ref[...]
    q_all = q16.astype(F32)
    k_all = 1.0 - jnp.exp(g_all)
    k16 = k_all.astype(BF16)

    qb = (q_all * jnp.exp(sums_all[0:chunk])).astype(BF16)
    o = [nt(qb[:, sl], st_ref[hh].astype(BF16)) for hh, sl in enumerate(heads)]
    att = [nt(q16[:, sl], k16[:, sl]) * mask_ref[0, n_lv] for sl in heads]
    for lv in range(n_lv):
        e = jnp.exp(sums_all[(2 + lv) * chunk:(3 + lv) * chunk])
        qe = (q_all * e).astype(BF16)
        ke = (k_all * e).astype(BF16)
        mask = mask_ref[0, lv]
        att = [a + nt(qe[:, sl], ke[:, sl]) * mask for a, sl in zip(att, heads)]
    for hh, sl in enumerate(heads):
        o_ref[0, :, sl] = o[hh] + jnp.dot(att[hh].astype(BF16), v_ref[:, sl],
                                          preferred_element_type=F32)

    ks = (k_all * jnp.exp(sums_all[chunk:2 * chunk])).astype(BF16)
    decay = jnp.exp(jnp.sum(g_all, axis=0, keepdims=True))
    for hh, sl in enumerate(heads):
        st_ref[hh] = st_ref[hh] * decay[:, sl] + jnp.dot(vt_ref[0, sl, :], ks[:, sl],
                                                         preferred_element_type=F32)


def _hgrn_call(qh, gates, v, vt3, lc):
    n, d = qh.shape
    c = HG_CHUNK
    nh = d // LANES
    nc, ncc = n // c, lc // c
    per_slab = vt3.shape[2] // c
    assert n % c == 0 and lc % c == 0 and vt3.shape[2] % c == 0
    n_lv = int(math.log2(c))
    wall_np, mask_np = _hgrn_consts(c)
    wall = jnp.asarray(wall_np, dtype=BF16)
    masks = jnp.asarray(mask_np, dtype=F32)

    def blk(dr, s):
        back = jnp.where(s < ncc, ncc - 1 - s, nc - 1 - (s - ncc))
        return jnp.where(dr == 0, s, back)

    return pl.pallas_call(
        functools.partial(_hgrn_kernel, chunk=c, n_lv=n_lv, n_heads=nh),
        grid=(2, nc),
        in_specs=[pl.BlockSpec((1,) + wall_np.shape[1:], lambda dr, s: (dr, 0, 0)),
                  pl.BlockSpec((1,) + mask_np.shape[1:], lambda dr, s: (dr, 0, 0, 0)),
                  pl.BlockSpec((c, d), lambda dr, s: (blk(dr, s), 0)),
                  pl.BlockSpec((1, c, d), lambda dr, s: (dr, blk(dr, s), 0)),
                  pl.BlockSpec((c, d), lambda dr, s: (blk(dr, s), 0)),
                  pl.BlockSpec((1, d, c),
                               lambda dr, s: (blk(dr, s) // per_slab, 0, blk(dr, s) % per_slab))],
        out_specs=pl.BlockSpec((1, c, d), lambda dr, s: (dr, blk(dr, s), 0)),
        out_shape=jax.ShapeDtypeStruct((2, n, d), F32),
        scratch_shapes=[pltpu.VMEM((nh, LANES, LANES), F32)],
        compiler_params=_cparams(2),
        name="hgrn_scan",
    )(wall, masks, qh, gates, v, vt3)


def _post_kernel(*refs, mode, final, tm, lc, n, d, f, tile0):
    it = iter(refs)
    x_ref, mod_ref, g2_ref, wo_ref, win_ref, wout_ref = (next(it) for _ in range(6))
    if mode == "plain":
        a_ref = next(it)
    elif mode == "conv":
        b_ref, z_ref, zp_ref, zn_ref, cw_ref = (next(it) for _ in range(5))
    else:
        ofw_ref, obw_ref, og_ref, gn_ref = (next(it) for _ in range(4))
    fin_ref = next(it) if final else None
    o_ref = next(it)

    ti = pl.program_id(0) + tile0
    is_ctx = _is_ctx_rows(ti, tm, lc)
    if mode == "plain":
        a = a_ref[...]
    elif mode == "conv":
        z = z_ref[...]
        rl = lax.broadcasted_iota(jnp.int32, (tm, 1), 0)
        row = ti * tm + rl
        z_prev = jnp.where(rl == 0, zp_ref[7:8, :], pltpu.roll(z, 1, 0))
        z_next = jnp.where(rl == tm - 1, zn_ref[0:1, :], pltpu.roll(z, tm - 1, 0))
        z_prev = jnp.where((row == 0) | (row == lc), 0.0, z_prev)
        z_next = jnp.where((row == lc - 1) | (row == n - 1), 0.0, z_next)
        conv = cw_ref[0:1, :] * z_prev + cw_ref[1:2, :] * z + cw_ref[2:3, :] * z_next
        a = (b_ref[...].astype(F32) * conv).astype(BF16)
    else:
        gn = gn_ref[...]
        heads = []
        for hh in range(d // LANES):
            sl = slice(hh * LANES, (hh + 1) * LANES)
            oh = _rms(ofw_ref[0, :, sl] + obw_ref[0, :, sl])
            heads.append((oh * gn * og_ref[:, sl].astype(F32)).astype(BF16))
        a = jnp.concatenate(heads, axis=1)

    y = jnp.dot(a, wo_ref[...], preferred_element_type=F32)
    x = x_ref[...] + _pick(mod_ref, 2, is_ctx) * y
    h = _norm_mod(x, g2_ref[...], _pick(mod_ref, 3, is_ctx), _pick(mod_ref, 4, is_ctx)).astype(BF16)
    acc = jnp.zeros((tm, d), F32)
    for c0 in range(0, f, FFN_CHUNK):
        u = jnp.dot(h, win_ref[:, c0:c0 + FFN_CHUNK], preferred_element_type=F32)
        w = jnp.dot(h, win_ref[:, f + c0:f + c0 + FFN_CHUNK], preferred_element_type=F32)
        act = (_silu(u) * w).astype(BF16)
        acc = acc + jnp.dot(act, wout_ref[c0:c0 + FFN_CHUNK, :], preferred_element_type=F32)
    x = x + _pick(mod_ref, 5, is_ctx) * acc
    if final:
        x = _rms(x) * fin_ref[...]
    o_ref[...] = x


def _post_call(xs, mod12, g2, wo, win, wout, mixer_inputs, mode, lc, final_g=None):
    n, d = xs.shape
    f = wout.shape[0]
    final = final_g is not None
    tm = _row_tile(n)
    assert f % FFN_CHUNK == 0
    tile0 = lc // tm if (final and lc % tm == 0) else 0
    nt = n // tm - tile0
    row = lambda i: (i + tile0, 0)
    in_specs = [pl.BlockSpec((tm, d), row), _const_spec((2 * N_MOD, d)), _const_spec((1, d)),
                _const_spec(wo.shape), _const_spec(win.shape), _const_spec(wout.shape)]
    args = [xs, mod12, g2, wo, win, wout]
    if mode == "plain":
        in_specs += [pl.BlockSpec((tm, d), row)]
    elif mode == "conv":
        r8 = tm // 8
        last8 = n // 8 - 1
        in_specs += [pl.BlockSpec((tm, d), row), pl.BlockSpec((tm, d), row),
                     pl.BlockSpec((8, d), lambda i: (jnp.maximum((i + tile0) * r8 - 1, 0), 0)),
                     pl.BlockSpec((8, d), lambda i: (jnp.minimum((i + tile0 + 1) * r8, last8), 0)),
                     _const_spec((SC_WIDTH, d))]
    else:
        in_specs += [pl.BlockSpec((1, tm, d), lambda i: (0, i + tile0, 0)),
                     pl.BlockSpec((1, tm, d), lambda i: (1, i + tile0, 0)),
                     pl.BlockSpec((tm, d), row), _const_spec((1, LANES))]
    args += list(mixer_inputs)
    if final:
        in_specs += [_const_spec((1, d))]
        args += [final_g]
    return pl.pallas_call(
        functools.partial(_post_kernel, mode=mode, final=final, tm=tm, lc=lc, n=n, d=d, f=f,
                          tile0=tile0),
        grid=(nt,),
        in_specs=in_specs,
        out_specs=pl.BlockSpec((tm, d), lambda i: (i, 0)),
        out_shape=jax.ShapeDtypeStruct((nt * tm, d), F32),
        compiler_params=_cparams(1),
        name="post_" + mode + ("_final" if final else ""),
    )(*args)


def _rope_tables(l, lc):
    rows = l // GRID_W
    r = jnp.repeat(jnp.arange(rows), GRID_W).astype(F32)
    c = jnp.tile(jnp.arange(GRID_W), rows).astype(F32)
    half = HEAD_DIM // 2
    inv_freq = 1.0 / (ROPE_BASE ** (jnp.arange(0, half, 2, dtype=F32) / half))
    ar, ac = r[:, None] * inv_freq, c[:, None] * inv_freq
    cs = jnp.concatenate([jnp.cos(ar), jnp.cos(ar), jnp.cos(ac), jnp.cos(ac)], axis=-1)
    sn = jnp.concatenate([-jnp.sin(ar), jnp.sin(ar), -jnp.sin(ac), jnp.sin(ac)], axis=-1)
    cs = jnp.concatenate([jnp.ones((lc, HEAD_DIM), F32), cs], axis=0)
    sn = jnp.concatenate([jnp.zeros((lc, HEAD_DIM), F32), sn], axis=0)
    return jnp.tile(cs, (1, 2)), jnp.tile(sn, (1, 2))


def kernel(x, c, ctx, c_ctx, ada_w, ada_b, norm1_g, norm2_g, ffn_w_in, ffn_w_out, final_g,
           da_wqkv, da_lambda, da_subln_g, da_wo, sc_w_in, sc_conv_w, sc_w_out,
           wa_wqkv, wa_sinks, wa_wo, hg_w_in, hg_lb, hg_gnorm_g, hg_wo):
    b, l, d = x.shape
    lc = ctx.shape[1]
    depth = ada_w.shape[0]
    assert b == 1 and d % LANES == 0
    n = lc + l
    xs = jnp.concatenate([ctx[0], x[0]], axis=0)

    cc = jnp.zeros((8, d), F32).at[0].set(c_ctx).at[1].set(c[0])
    mods = _mod_call(cc, ada_w, ada_b)
    cs, sn = _rope_tables(l, lc)

    n_mix = 4
    for i in range(depth):
        mixer, slot = i % n_mix, i // n_mix
        last = i == depth - 1
        mod12 = jnp.concatenate([mods[i, 0].reshape(N_MOD, d), mods[i, 1].reshape(N_MOD, d)], axis=0)
        g1 = norm1_g[i].reshape(1, d)
        g2 = norm2_g[i].reshape(1, d)
        win = ffn_w_in[i].astype(BF16)
        wout = ffn_w_out[i].astype(BF16)
        fin = final_g.reshape(1, d) if last else None

        if mixer == 0:
            w = da_wqkv[slot]
            lam = da_lambda[slot].astype(F32)
            lam_init = 0.8 - 0.6 * math.exp(-0.3 * i)
            lam_full = (jnp.exp(jnp.sum(lam[0] * lam[1])) - jnp.exp(jnp.sum(lam[2] * lam[3]))
                        + lam_init).reshape(1)
            q, k, vt3 = _proj_da_call(xs, g1, mod12, cs, sn, w[:, :2 * d].astype(BF16),
                                      w[:, 2 * d:].T.astype(BF16), lc)
            a = _diff_attn_call(lam_full, q, k, vt3, da_subln_g[slot].reshape(LANES, 1), lc,
                                1.0 - lam_init)
            xs = _post_call(xs, mod12, g2, da_wo[slot].astype(BF16), win, wout, [a], "plain", lc, fin)
        elif mixer == 1:
            bg, z = _proj_sc_call(xs, g1, mod12, sc_w_in[slot].astype(BF16), lc)
            xs = _post_call(xs, mod12, g2, sc_w_out[slot].astype(BF16), win, wout,
                            [bg, z, z, z, sc_conv_w[slot]], "conv", lc, fin)
        elif mixer == 2:
            w = wa_wqkv[slot]
            kw = WA_KV_HEADS * HEAD_DIM
            dup = lambda m: jnp.tile(m.reshape(d, WA_KV_HEADS, 1, HEAD_DIM), (1, 1, 2, 1)).reshape(d, 2 * kw)
            w2 = jnp.concatenate([w[:, :d], dup(w[:, d:d + kw]), dup(w[:, d + kw:])], axis=1)
            q, k2, v2 = _proj_wa_call(xs, g1, mod12, cs, sn, w2.astype(BF16), lc, 2 * kw)
            a = _win_attn_call(wa_sinks[slot].astype(F32), q, k2, v2, lc)
            xs = _post_call(xs, mod12, g2, wa_wo[slot].astype(BF16), win, wout, [a], "plain", lc, fin)
        else:
            w = hg_w_in[slot]
            p = jax.nn.softmax(hg_lb.astype(F32), axis=1)
            lb = (jnp.cumsum(p, axis=1) - p[:, :1])[:, i]
            qh, gates, v, vt3, og = _proj_hg_call(xs, g1, mod12, jnp.log(lb), jnp.log1p(-lb),
                                                  w.astype(BF16), lc)
            o2 = _hgrn_call(qh, gates, v, vt3, lc)
            xs = _post_call(xs, mod12, g2, hg_wo[slot].astype(BF16), win, wout,
                            [o2, o2, og, hg_gnorm_g[slot].reshape(1, LANES)], "hgrn", lc, fin)
    return xs[-l:].reshape(1, l, d)
```

```python
import functools
import math

import numpy as np
import jax
import jax.numpy as jnp
from jax import lax
from jax.experimental import pallas as pl
from jax.experimental.pallas import tpu as pltpu

F32 = jnp.float32
BF16 = jnp.bfloat16

HEAD_DIM = 64
GRID_W = 64
ROPE_BASE = 10000.0
RMS_EPS = 1e-6
N_MOD = 6
WINDOW = 128
WA_KV_HEADS = 4
SC_WIDTH = 3
LANES = 128
NEG_BIG = -1e30
LOG2E = math.log2(math.e)

TM = 512
SLAB = 128
HG_CHUNK = 128
DA_TQ = 512
DA_TK = 1024
DA_ONES_ROWS = 16
WA_TQ = 256
FFN_CHUNK = 256
VMEM_LIMIT = 56 * 1024 * 1024

NT_DIMS = (((1,), (1,)), ((), ()))


def _cparams(n_axes):
    return pltpu.CompilerParams(
        dimension_semantics=("arbitrary",) * n_axes, vmem_limit_bytes=VMEM_LIMIT)


def _const_spec(shape):
    nd = len(shape)
    return pl.BlockSpec(shape, lambda *_: (0,) * nd, pipeline_mode=pl.Buffered(1))


def _is_ctx_rows(tile_idx, tm, l):
    row = tile_idx * tm + lax.broadcasted_iota(jnp.int32, (tm, 1), 0)
    return row >= l


def _pick(mod_ref, k, is_ctx):
    return jnp.where(is_ctx, mod_ref[k:k + 1, :], mod_ref[N_MOD + k:N_MOD + k + 1, :])


def _rms(x):
    return x * lax.rsqrt(jnp.mean(x * x, axis=-1, keepdims=True) + RMS_EPS)


def _norm_mod(x, g, shift, scale):
    return (_rms(x) * g) * (1.0 + scale) + shift


def _silu(x):
    return x * (1.0 / (1.0 + jnp.exp(-x)))


def _rope(chunk, cs, sn, first_half):
    partner = jnp.where(first_half, pltpu.roll(chunk, LANES - 16, 1), pltpu.roll(chunk, 16, 1))
    return chunk * cs + partner * sn


def _first_half_lanes():
    lane = lax.broadcasted_iota(jnp.int32, (1, LANES), 1)
    return (lane & 31) < 16


def _mod_kernel(cc_ref, w_ref, b_ref, o_ref):
    a = _silu(cc_ref[...])
    o_ref[0] = jnp.dot(a, w_ref[0], precision=lax.Precision.HIGHEST,
                       preferred_element_type=F32) + b_ref[0]


def _mod_call(cc, ada_w, ada_b):
    depth, d, nd = ada_w.shape
    tn = 1536
    assert nd % tn == 0
    return pl.pallas_call(
        _mod_kernel,
        grid=(depth, nd // tn),
        in_specs=[pl.BlockSpec((8, d), lambda i, j: (0, 0)),
                  pl.BlockSpec((1, d, tn), lambda i, j: (i, 0, j)),
                  pl.BlockSpec((1, 1, tn), lambda i, j: (i, 0, j))],
        out_specs=pl.BlockSpec((1, 8, tn), lambda i, j: (i, 0, j)),
        out_shape=jax.ShapeDtypeStruct((depth, 8, nd), F32),
        compiler_params=_cparams(2),
        name="adaln_mod",
    )(cc, ada_w, ada_b.reshape(depth, 1, nd))


def _proj_da_kernel(x_ref, g_ref, mod_ref, cs_ref, sn_ref, wqk_ref, wvt_ref,
                    q_ref, k_ref, vt_ref, *, tm, l, d):
    is_ctx = _is_ctx_rows(pl.program_id(0), tm, l)
    h = _norm_mod(x_ref[...], g_ref[...], _pick(mod_ref, 0, is_ctx), _pick(mod_ref, 1, is_ctx))
    hb = h.astype(BF16)
    qk = jnp.dot(hb, wqk_ref[...], preferred_element_type=F32)
    cs, sn, first = cs_ref[...], sn_ref[...], _first_half_lanes()
    nq = d // LANES
    for j in range(2 * nq):
        r = _rope(qk[:, j * LANES:(j + 1) * LANES], cs, sn, first)
        if j < nq:
            q_ref[:, j * LANES:(j + 1) * LANES] = (r * (LOG2E * HEAD_DIM ** -0.5)).astype(BF16)
        else:
            k_ref[:, (j - nq) * LANES:(j - nq + 1) * LANES] = r.astype(BF16)
    vt = lax.dot_general(wvt_ref[...], hb, NT_DIMS, preferred_element_type=F32)
    ones = jnp.ones((DA_ONES_ROWS, SLAB), BF16)
    vr = LANES + DA_ONES_ROWS
    for s in range(tm // SLAB):
        for hh in range(d // LANES):
            vt_ref[s, hh * vr:hh * vr + LANES, :] = (
                vt[hh * LANES:(hh + 1) * LANES, s * SLAB:(s + 1) * SLAB].astype(BF16))
            vt_ref[s, hh * vr + LANES:(hh + 1) * vr, :] = ones


def _proj_da_call(xs, g, mod12, cs, sn, wqk, wvt, l):
    n, d = xs.shape
    row = lambda i: (i, 0)
    dv = d // LANES * (LANES + DA_ONES_ROWS)
    return pl.pallas_call(
        functools.partial(_proj_da_kernel, tm=TM, l=l, d=d),
        grid=(n // TM,),
        in_specs=[pl.BlockSpec((TM, d), row), _const_spec((1, d)), _const_spec((2 * N_MOD, d)),
                  pl.BlockSpec((TM, LANES), row), pl.BlockSpec((TM, LANES), row),
                  _const_spec(wqk.shape), _const_spec(wvt.shape)],
        out_specs=[pl.BlockSpec((TM, d), row), pl.BlockSpec((TM, d), row),
                   pl.BlockSpec((TM // SLAB, dv, SLAB), lambda i: (i, 0, 0))],
        out_shape=[jax.ShapeDtypeStruct((n, d), BF16), jax.ShapeDtypeStruct((n, d), BF16),
                   jax.ShapeDtypeStruct((n // SLAB, dv, SLAB), BF16)],
        compiler_params=_cparams(1),
        name="proj_diff_attn",
    )(xs, g, mod12, cs, sn, wqk, wvt)


def _proj_sc_kernel(x_ref, g_ref, mod_ref, w_ref, b_ref, z_ref, *, tm, l, d):
    is_ctx = _is_ctx_rows(pl.program_id(0), tm, l)
    h = _norm_mod(x_ref[...], g_ref[...], _pick(mod_ref, 0, is_ctx), _pick(mod_ref, 1, is_ctx))
    y = jnp.dot(h.astype(BF16), w_ref[...], preferred_element_type=F32)
    b_ref[...] = y[:, :d].astype(BF16)
    z_ref[...] = y[:, d:2 * d] * y[:, 2 * d:]


def _proj_sc_call(xs, g, mod12, w, l):
    n, d = xs.shape
    row = lambda i: (i, 0)
    return pl.pallas_call(
        functools.partial(_proj_sc_kernel, tm=TM, l=l, d=d),
        grid=(n // TM,),
        in_specs=[pl.BlockSpec((TM, d), row), _const_spec((1, d)), _const_spec((2 * N_MOD, d)),
                  _const_spec(w.shape)],
        out_specs=[pl.BlockSpec((TM, d), row), pl.BlockSpec((TM, d), row)],
        out_shape=[jax.ShapeDtypeStruct((n, d), BF16), jax.ShapeDtypeStruct((n, d), F32)],
        compiler_params=_cparams(1),
        name="proj_short_conv",
    )(xs, g, mod12, w)


def _proj_wa_kernel(x_ref, g_ref, mod_ref, cs_ref, sn_ref, w_ref, q_ref, k_ref, v_ref,
                    *, tm, l, d, kw):
    is_ctx = _is_ctx_rows(pl.program_id(0), tm, l)
    h = _norm_mod(x_ref[...], g_ref[...], _pick(mod_ref, 0, is_ctx), _pick(mod_ref, 1, is_ctx))
    y = jnp.dot(h.astype(BF16), w_ref[...], preferred_element_type=F32)
    cs, sn, first = cs_ref[...], sn_ref[...], _first_half_lanes()
    for j in range(d // LANES):
        r = _rope(y[:, j * LANES:(j + 1) * LANES], cs, sn, first)
        q_ref[:, j * LANES:(j + 1) * LANES] = (r * (LOG2E * HEAD_DIM ** -0.5)).astype(BF16)
    for j in range(kw // LANES):
        r = _rope(y[:, d + j * LANES:d + (j + 1) * LANES], cs, sn, first)
        k_ref[:, j * LANES:(j + 1) * LANES] = r.astype(BF16)
    vt = y[:, d + kw:].T
    for s in range(tm // SLAB):
        v_ref[s] = vt[:, s * SLAB:(s + 1) * SLAB].astype(BF16)


def _proj_wa_call(xs, g, mod12, cs, sn, w, l, kw):
    n, d = xs.shape
    row = lambda i: (i, 0)
    return pl.pallas_call(
        functools.partial(_proj_wa_kernel, tm=TM, l=l, d=d, kw=kw),
        grid=(n // TM,),
        in_specs=[pl.BlockSpec((TM, d), row), _const_spec((1, d)), _const_spec((2 * N_MOD, d)),
                  pl.BlockSpec((TM, LANES), row), pl.BlockSpec((TM, LANES), row),
                  _const_spec(w.shape)],
        out_specs=[pl.BlockSpec((TM, d), row), pl.BlockSpec((TM, kw), row),
                   pl.BlockSpec((TM // SLAB, kw, SLAB), lambda i: (i, 0, 0))],
        out_shape=[jax.ShapeDtypeStruct((n, d), BF16), jax.ShapeDtypeStruct((n, kw), BF16),
                   jax.ShapeDtypeStruct((n // SLAB, kw, SLAB), BF16)],
        compiler_params=_cparams(1),
        name="proj_window_attn",
    )(xs, g, mod12, cs, sn, w)


def _log1p_unit(x):
    return jnp.log(1.0 + x)


def _proj_hg_kernel(x_ref, g_ref, mod_ref, lla_ref, l1m_ref, w_ref,
                    q_ref, gate_ref, v_ref, vt_ref, og_ref, *, tm, l, d):
    is_ctx = _is_ctx_rows(pl.program_id(0), tm, l)
    h = _norm_mod(x_ref[...], g_ref[...], _pick(mod_ref, 0, is_ctx), _pick(mod_ref, 1, is_ctx))
    hb = h.astype(BF16)
    part = lambda j: jnp.dot(hb, w_ref[:, j * d:(j + 1) * d], preferred_element_type=F32)
    q_ref[...] = _silu(part(0)).astype(BF16)
    for dr in range(2):
        f = part(1 + dr)
        log_sig = jnp.minimum(f, 0.0) - _log1p_unit(jnp.exp(-jnp.abs(f)))
        a = lla_ref[dr:dr + 1, :]
        c = l1m_ref[dr:dr + 1, :] + log_sig
        gate_ref[dr] = jnp.maximum(a, c) + _log1p_unit(jnp.exp(-jnp.abs(a - c)))
    v = part(3)
    v_ref[...] = v.astype(BF16)
    vt = v.T
    for s in range(tm // SLAB):
        vt_ref[s] = vt[:, s * SLAB:(s + 1) * SLAB].astype(BF16)
    og_ref[...] = _silu(part(4)).astype(BF16)


def _proj_hg_call(xs, g, mod12, lla, l1m, w, l):
    n, d = xs.shape
    row = lambda i: (i, 0)
    return pl.pallas_call(
        functools.partial(_proj_hg_kernel, tm=TM, l=l, d=d),
        grid=(n // TM,),
        in_specs=[pl.BlockSpec((TM, d), row), _const_spec((1, d)), _const_spec((2 * N_MOD, d)),
                  _const_spec((2, d)), _const_spec((2, d)), _const_spec(w.shape)],
        out_specs=[pl.BlockSpec((TM, d), row), pl.BlockSpec((2, TM, d), lambda i: (0, i, 0)),
                   pl.BlockSpec((TM, d), row),
                   pl.BlockSpec((TM // SLAB, d, SLAB), lambda i: (i, 0, 0)),
                   pl.BlockSpec((TM, d), row)],
        out_shape=[jax.ShapeDtypeStruct((n, d), BF16), jax.ShapeDtypeStruct((2, n, d), F32),
                   jax.ShapeDtypeStruct((n, d), BF16), jax.ShapeDtypeStruct((n // SLAB, d, SLAB), BF16),
                   jax.ShapeDtypeStruct((n, d), BF16)],
        compiler_params=_cparams(1),
        name="proj_hgrn",
    )(xs, g, mod12, lla, l1m, w)


def _diff_attn_kernel(lam_ref, q_ref, k_ref, vt_ref, g_ref, o_ref, m_ref, acc_ref,
                      sa_ref, sb_ref, ma_ref, mb_ref, *, tq, tk, l, lc, unroll, post_scale):
    qi = pl.program_id(1)
    q = q_ref[...]
    lane = lax.broadcasted_iota(jnp.int32, (1, LANES), 1)
    zero = jnp.zeros_like(q)
    qcat = jnp.concatenate([jnp.where(lane < HEAD_DIM, q, zero),
                            jnp.where(lane >= HEAD_DIM, q, zero)], axis=0)
    sub = 2 * vt_ref.shape[2]
    n_chunks = l // tk

    def scores(kc):
        return lax.dot_general(kc, qcat, NT_DIMS, preferred_element_type=F32)

    def pv(pair0, n_pairs, p):
        out = None
        for j in range(n_pairs):
            s0 = 2 * (pair0 + j)
            vt = jnp.concatenate([vt_ref[s0], vt_ref[s0 + 1]], axis=1)
            t = jnp.dot(vt, p[j * sub:(j + 1) * sub, :], preferred_element_type=F32)
            out = t if out is None else out + t
        return out

    def absorb_context():
        s = scores(k_ref[l:l + lc, :])
        mx = jnp.max(s, axis=0, keepdims=True)
        m_ref[...] = mx
        acc_ref[...] = pv(l // sub, lc // sub, jnp.exp2(s - mx).astype(BF16))

    def qk(c, s_ref, mx_ref):
        off = pl.multiple_of(c * tk, LANES)
        s = scores(k_ref[pl.ds(off, tk), :])
        s_ref[...] = s
        mx_ref[...] = jnp.max(s, axis=0, keepdims=True)

    def absorb(c, s_ref, mx_ref):
        m_old = m_ref[...]
        m_new = jnp.maximum(m_old, mx_ref[...])
        alpha = jnp.exp2(m_old - m_new)
        p = jnp.exp2(s_ref[...] - m_new).astype(BF16)
        m_ref[...] = m_new
        acc_ref[...] = alpha * acc_ref[...] + pv((c * tk) // sub, tk // sub, p)

    @pl.when(qi * tq >= l)
    def _():
        absorb_context()

    @pl.when(qi * tq < l)
    def _():
        qk(0, sa_ref, ma_ref)
        absorb_context()
        bufs = ((sa_ref, ma_ref), (sb_ref, mb_ref))

        def group(c0, count, last):
            for j in range(count):
                if not (last and j == count - 1):
                    qk(c0 + j + 1, *bufs[(j + 1) % 2])
                absorb(c0 + j, *bufs[j % 2])

        def body(i, carry):
            group(unroll * i, unroll, False)
            return carry

        lax.fori_loop(0, n_chunks // unroll - 1, body, 0)
        group(n_chunks - unroll, unroll, True)

    acc = acc_ref[0:LANES, :] * (1.0 / acc_ref[LANES:LANES + 1, :])
    o = acc[:, :tq] - lam_ref[0] * acc[:, tq:]
    o = o * lax.rsqrt(jnp.mean(o * o, axis=0, keepdims=True) + RMS_EPS)
    o = o * (g_ref[...] * post_scale)
    o_ref[...] = o.T.astype(BF16)


def _diff_attn_call(lam, q, k, vt3, subln_g, l, lc, post_scale):
    n, d = q.shape
    nh = d // LANES
    nslab, vrows, slab = vt3.shape
    vrows //= nh
    tq = DA_TQ
    tk = min(DA_TK, l // 2)
    assert l % tq == 0 and n % tq == 0 and tk % (2 * slab) == 0
    assert l % (2 * slab) == 0 and lc % (2 * slab) == 0
    assert l % (2 * tk) == 0
    return pl.pallas_call(
        functools.partial(_diff_attn_kernel, tq=tq, tk=tk, l=l, lc=lc,
                          unroll=4 if l % (4 * tk) == 0 else 2, post_scale=post_scale),
        grid=(nh, n // tq),
        in_specs=[pl.BlockSpec(memory_space=pltpu.SMEM),
                  pl.BlockSpec((tq, LANES), lambda h, i: (i, h)),
                  pl.BlockSpec((n, LANES), lambda h, i: (0, h)),
                  pl.BlockSpec((nslab, vrows, slab), lambda h, i: (0, h, 0)),
                  pl.BlockSpec((LANES, 1), lambda h, i: (0, 0))],
        out_specs=pl.BlockSpec((tq, LANES), lambda h, i: (i, h)),
        out_shape=jax.ShapeDtypeStruct((n, d), BF16),
        scratch_shapes=[pltpu.VMEM((1, 2 * tq), F32),
                        pltpu.VMEM((vrows, 2 * tq), F32),
                        pltpu.VMEM((tk, 2 * tq), F32), pltpu.VMEM((tk, 2 * tq), F32),
                        pltpu.VMEM((1, 2 * tq), F32), pltpu.VMEM((1, 2 * tq), F32)],
        compiler_params=_cparams(2),
        name="diff_attention",
    )(lam, q, k, vt3, subln_g)


def _win_attn_kernel(sink_ref, q_ref, kc_ref, kp_ref, ko_ref, kn_ref,
                     vc_ref, vp_ref, vo_ref, vn_ref, o_ref, *, tq, l, lc, rep):
    j = pl.program_id(0)
    g = pl.program_id(1)
    w = WINDOW
    kc = kc_ref[...]
    kwin = jnp.concatenate([kp_ref[...], ko_ref[...], kn_ref[...]], axis=0)
    vtc = jnp.concatenate([vc_ref[s] for s in range(lc // w)], axis=1)
    vtwin = jnp.concatenate([vp_ref[0]] + [vo_ref[s] for s in range(tq // w)] + [vn_ref[0]],
                            axis=1)
    a = lax.broadcasted_iota(jnp.int32, (w, w), 0)
    b = lax.broadcasted_iota(jnp.int32, (w, w), 1)
    lane = lax.broadcasted_iota(jnp.int32, (1, LANES), 1)
    low = lane < HEAD_DIM
    for t in range(tq // w):
        q0 = j * tq + t * w
        latent = q0 < l
        own_bias = jnp.where(latent, 0.0, NEG_BIG)
        prev_bias = jnp.where(jnp.logical_and(latent, q0 >= w), 0.0, NEG_BIG)
        next_bias = jnp.where(q0 + w < l, 0.0, NEG_BIG)
        bias = jnp.concatenate(
            [jnp.where(a >= b, prev_bias, NEG_BIG),
             jnp.full((w, w), own_bias, F32),
             jnp.where(a <= b, next_bias, NEG_BIG)], axis=0)
        qs, sinks = [], []
        for r in range(rep):
            qc = q_ref[t * w:(t + 1) * w, (r // 2) * LANES:(r // 2 + 1) * LANES]
            qs.append(jnp.where(low if r % 2 == 0 else ~low, qc, jnp.zeros_like(qc)))
            sinks.append(jnp.full((1, w), sink_ref[g * rep + r] * LOG2E, F32))
        qst = jnp.concatenate(qs, axis=0)
        sink = jnp.concatenate(sinks, axis=1)
        s_c = lax.dot_general(kc, qst, NT_DIMS, preferred_element_type=F32)
        s_w = (lax.dot_general(kwin[t * w:(t + 3) * w, :], qst, NT_DIMS,
                               preferred_element_type=F32)
               + jnp.concatenate([bias] * rep, axis=1))
        mx = jnp.maximum(jnp.maximum(jnp.max(s_c, axis=0, keepdims=True),
                                     jnp.max(s_w, axis=0, keepdims=True)), sink)
        e_c = jnp.exp2(s_c - mx)
        e_w = jnp.exp2(s_w - mx)
        den = (jnp.exp2(sink - mx) + jnp.sum(e_c, axis=0, keepdims=True)
               + jnp.sum(e_w, axis=0, keepdims=True))
        ot = (jnp.dot(vtc, e_c.astype(BF16), preferred_element_type=F32)
              + jnp.dot(vtwin[:, t * w:(t + 3) * w], e_w.astype(BF16),
                        preferred_element_type=F32)) * (1.0 / den)
        heads = [ot[:, r * w:(r + 1) * w].T for r in range(rep)]
        for c in range(rep // 2):
            o_ref[t * w:(t + 1) * w, c * LANES:(c + 1) * LANES] = jnp.where(
                low, heads[2 * c], heads[2 * c + 1]).astype(BF16)


def _win_attn_call(sinks, q, k2, vt2, l, lc):
    n, d = q.shape
    ng = WA_KV_HEADS
    rep = d // HEAD_DIM // ng
    gw = rep * HEAD_DIM
    tq = WA_TQ
    assert n % tq == 0 and tq % WINDOW == 0 and lc % WINDOW == 0 and l % lc == 0 and rep % 2 == 0
    r = tq // WINDOW
    last = n // WINDOW - 1
    prev = lambda j: jnp.maximum(j * r - 1, 0)
    nxt = lambda j: jnp.minimum((j + 1) * r, last)
    k_specs = [pl.BlockSpec((lc, LANES), lambda j, g: (l // lc, g)),
               pl.BlockSpec((WINDOW, LANES), lambda j, g: (prev(j), g)),
               pl.BlockSpec((tq, LANES), lambda j, g: (j, g)),
               pl.BlockSpec((WINDOW, LANES), lambda j, g: (nxt(j), g))]
    vt_specs = [pl.BlockSpec((lc // WINDOW, LANES, WINDOW), lambda j, g: (l // lc, g, 0)),
                pl.BlockSpec((1, LANES, WINDOW), lambda j, g: (prev(j), g, 0)),
                pl.BlockSpec((r, LANES, WINDOW), lambda j, g: (j, g, 0)),
                pl.BlockSpec((1, LANES, WINDOW), lambda j, g: (nxt(j), g, 0))]
    return pl.pallas_call(
        functools.partial(_win_attn_kernel, tq=tq, l=l, lc=lc, rep=rep),
        grid=(n // tq, ng),
        in_specs=[pl.BlockSpec(memory_space=pltpu.SMEM),
                  pl.BlockSpec((tq, gw), lambda j, g: (j, g))] + k_specs + vt_specs,
        out_specs=pl.BlockSpec((tq, gw), lambda j, g: (j, g)),
        out_shape=jax.ShapeDtypeStruct((n, d), BF16),
        compiler_params=_cparams(2),
        name="window_attention",
    )(sinks, q, k2, k2, k2, k2, vt2, vt2, vt2, vt2)


def _hgrn_consts(chunk):
    n_lv = int(math.log2(chunk))
    walls, masks_all = [], []
    for direction in range(2):
        tau = np.arange(chunk) if direction == 0 else chunk - 1 - np.arange(chunk)
        tr, tc = tau[:, None], tau[None, :]
        blocks = [tc <= tr, tc > tr]
        masks = []
        for lv in range(n_lv):
            h = chunk >> (lv + 1)
            mid = (tr // (2 * h)) * 2 * h + h - 1
            second_r = (tr % (2 * h)) >= h
            second_c = (tc % (2 * h)) >= h
            blocks.append(np.where(second_r, (tc > mid) & (tc <= tr), (tc > tr) & (tc <= mid)))
            masks.append(second_r & ~second_c & ((tr // (2 * h)) == (tc // (2 * h))))
        masks.append(tr == tc)
        w = np.concatenate(blocks, axis=0).astype(np.float32)
        walls.append(np.concatenate([w, w], axis=1))
        masks_all.append(np.stack(masks).astype(np.float32))
    return np.stack(walls), np.stack(masks_all)


def _hgrn_kernel(wall_ref, mask_ref, q_ref, g_ref, v_ref, vt_ref, o_ref, st_ref,
                 *, chunk, n_lv, n_heads, n_steps):
    step = pl.program_id(1)

    @pl.when(step == 0)
    def _():
        st_ref[...] = jnp.zeros_like(st_ref)

    @pl.when(step >= n_steps)
    def _():
        o_ref[...] = jnp.zeros_like(o_ref)

    @pl.when(step < n_steps)
    def _():
        g_all = g_ref[0]
        g_hi = g_all.astype(BF16)
        g_lo = (g_all - g_hi.astype(F32)).astype(BF16)
        sums_all = jnp.dot(wall_ref[0], jnp.concatenate([g_hi, g_lo], axis=0),
                           preferred_element_type=F32)
        heads = [slice(hh * LANES, (hh + 1) * LANES) for hh in range(n_heads)]
        nt = lambda x, y: lax.dot_general(x, y, NT_DIMS, preferred_element_type=F32)
        q16 = q_ref[...]
        q_all = q16.astype(F32)
        k_all = 1.0 - jnp.exp(g_all)
        k16 = k_all.astype(BF16)

        qb = (q_all * jnp.exp(sums_all[0:chunk])).astype(BF16)
        o = [nt(qb[:, sl], st_ref[hh].astype(BF16)) for hh, sl in enumerate(heads)]
        att = [nt(q16[:, sl], k16[:, sl]) * mask_ref[0, n_lv] for sl in heads]
        for lv in range(n_lv):
            e = jnp.exp(sums_all[(2 + lv) * chunk:(3 + lv) * chunk])
            qe = (q_all * e).astype(BF16)
            ke = (k_all * e).astype(BF16)
            mask = mask_ref[0, lv]
            att = [a + nt(qe[:, sl], ke[:, sl]) * mask for a, sl in zip(att, heads)]
        for hh, sl in enumerate(heads):
            o_ref[0, :, sl] = o[hh] + jnp.dot(att[hh].astype(BF16), v_ref[:, sl],
                                              preferred_element_type=F32)

        ks = (k_all * jnp.exp(sums_all[chunk:2 * chunk])).astype(BF16)
        decay = jnp.exp(jnp.sum(g_all, axis=0, keepdims=True))
        for hh, sl in enumerate(heads):
            st_ref[hh] = st_ref[hh] * decay[:, sl] + jnp.dot(vt_ref[0, sl, :], ks[:, sl],
                                                             preferred_element_type=F32)


def _hgrn_call(qh, gates, v, vt3, l, lc):
    n, d = qh.shape
    c = HG_CHUNK
    nh = d // LANES
    nx, ncc, ntot = l // c, lc // c, n // c
    nc = nx + ncc
    assert n % c == 0 and l % c == 0 and lc % c == 0 and vt3.shape[2] == c
    n_lv = int(math.log2(c))
    wall_np, mask_np = _hgrn_consts(c)
    wall = jnp.asarray(wall_np, dtype=BF16)
    masks = jnp.asarray(mask_np, dtype=F32)

    def blk(dr, s):
        fwd = jnp.where(s < ncc, nx + s, s - ncc)
        bwd = jnp.where(s < ncc, nx + ncc - 1 - s, nx - 1 - (s - ncc))
        return jnp.where(s >= nc, s, jnp.where(dr == 0, fwd, bwd))

    return pl.pallas_call(
        functools.partial(_hgrn_kernel, chunk=c, n_lv=n_lv, n_heads=nh, n_steps=nc),
        grid=(2, ntot),
        in_specs=[pl.BlockSpec((1,) + wall_np.shape[1:], lambda dr, s: (dr, 0, 0)),
                  pl.BlockSpec((1,) + mask_np.shape[1:], lambda dr, s: (dr, 0, 0, 0)),
                  pl.BlockSpec((c, d), lambda dr, s: (blk(dr, s), 0)),
                  pl.BlockSpec((1, c, d), lambda dr, s: (dr, blk(dr, s), 0)),
                  pl.BlockSpec((c, d), lambda dr, s: (blk(dr, s), 0)),
                  pl.BlockSpec((1, d, c), lambda dr, s: (blk(dr, s), 0, 0))],
        out_specs=pl.BlockSpec((1, c, d), lambda dr, s: (dr, blk(dr, s), 0)),
        out_shape=jax.ShapeDtypeStruct((2, n, d), F32),
        scratch_shapes=[pltpu.VMEM((nh, LANES, LANES), F32)],
        compiler_params=_cparams(2),
        name="hgrn_scan",
    )(wall, masks, qh, gates, v, vt3)


def _post_kernel(*refs, mode, final, tm, l, lc, d, f):
    it = iter(refs)
    x_ref, mod_ref, g2_ref, wo_ref, win_ref, wout_ref = (next(it) for _ in range(6))
    if mode == "plain":
        a_ref = next(it)
    elif mode == "conv":
        b_ref, z_ref, zp_ref, zn_ref, cw_ref = (next(it) for _ in range(5))
    else:
        ofw_ref, obw_ref, og_ref, gn_ref = (next(it) for _ in range(4))
    fin_ref = next(it) if final else None
    o_ref = next(it)

    ti = pl.program_id(0)
    is_ctx = _is_ctx_rows(ti, tm, l)
    if mode == "plain":
        a = a_ref[...]
    elif mode == "conv":
        z = z_ref[...]
        rl = lax.broadcasted_iota(jnp.int32, (tm, 1), 0)
        row = ti * tm + rl
        z_prev = jnp.where(rl == 0, zp_ref[7:8, :], pltpu.roll(z, 1, 0))
        z_next = jnp.where(rl == tm - 1, zn_ref[0:1, :], pltpu.roll(z, tm - 1, 0))
        z_prev = jnp.where((row == 0) | (row == l), 0.0, z_prev)
        z_next = jnp.where((row == l - 1) | (row == l + lc - 1), 0.0, z_next)
        conv = cw_ref[0:1, :] * z_prev + cw_ref[1:2, :] * z + cw_ref[2:3, :] * z_next
        a = (b_ref[...].astype(F32) * conv).astype(BF16)
    else:
        gn = gn_ref[...]
        heads = []
        for hh in range(d // LANES):
            sl = slice(hh * LANES, (hh + 1) * LANES)
            oh = _rms(ofw_ref[0, :, sl] + obw_ref[0, :, sl])
            heads.append((oh * gn * og_ref[:, sl].astype(F32)).astype(BF16))
        a = jnp.concatenate(heads, axis=1)

    y = jnp.dot(a, wo_ref[...], preferred_element_type=F32)
    x = x_ref[...] + _pick(mod_ref, 2, is_ctx) * y
    h = _norm_mod(x, g2_ref[...], _pick(mod_ref, 3, is_ctx), _pick(mod_ref, 4, is_ctx)).astype(BF16)
    acc = jnp.zeros((tm, d), F32)
    for c0 in range(0, f, FFN_CHUNK):
        u = jnp.dot(h, win_ref[:, c0:c0 + FFN_CHUNK], preferred_element_type=F32)
        w = jnp.dot(h, win_ref[:, f + c0:f + c0 + FFN_CHUNK], preferred_element_type=F32)
        act = (_silu(u) * w).astype(BF16)
        acc = acc + jnp.dot(act, wout_ref[c0:c0 + FFN_CHUNK, :], preferred_element_type=F32)
    x = x + _pick(mod_ref, 5, is_ctx) * acc
    if final:
        x = _rms(x) * fin_ref[...]
    o_ref[...] = x


def _post_call(xs, mod12, g2, wo, win, wout, mixer_inputs, mode, l, lc, final_g=None):
    n, d = xs.shape
    f = wout.shape[0]
    final = final_g is not None
    assert f % FFN_CHUNK == 0 and l % TM == 0 and n % TM == 0
    n_out = l if final else n
    row = lambda i: (i, 0)
    in_specs = [pl.BlockSpec((TM, d), row), _const_spec((2 * N_MOD, d)), _const_spec((1, d)),
                _const_spec(wo.shape), _const_spec(win.shape), _const_spec(wout.shape)]
    args = [xs, mod12, g2, wo, win, wout]
    if mode == "plain":
        in_specs += [pl.BlockSpec((TM, d), row)]
    elif mode == "conv":
        r8 = TM // 8
        last8 = n // 8 - 1
        in_specs += [pl.BlockSpec((TM, d), row), pl.BlockSpec((TM, d), row),
                     pl.BlockSpec((8, d), lambda i: (jnp.maximum(i * r8 - 1, 0), 0)),
                     pl.BlockSpec((8, d), lambda i: (jnp.minimum((i + 1) * r8, last8), 0)),
                     _const_spec((SC_WIDTH, d))]
    else:
        in_specs += [pl.BlockSpec((1, TM, d), lambda i: (0, i, 0)),
                     pl.BlockSpec((1, TM, d), lambda i: (1, i, 0)),
                     pl.BlockSpec((TM, d), row), _const_spec((1, LANES))]
    args += list(mixer_inputs)
    if final:
        in_specs += [_const_spec((1, d))]
        args += [final_g]
    return pl.pallas_call(
        functools.partial(_post_kernel, mode=mode, final=final, tm=TM, l=l, lc=lc, d=d, f=f),
        grid=(n_out // TM,),
        in_specs=in_specs,
        out_specs=pl.BlockSpec((TM, d), row),
        out_shape=jax.ShapeDtypeStruct((n_out, d), F32),
        compiler_params=_cparams(1),
        name="post_" + mode + ("_final" if final else ""),
    )(*args)


def _rope_tables(l, n):
    rows = l // GRID_W
    r = jnp.repeat(jnp.arange(rows), GRID_W).astype(F32)
    c = jnp.tile(jnp.arange(GRID_W), rows).astype(F32)
    half = HEAD_DIM // 2
    inv_freq = 1.0 / (ROPE_BASE ** (jnp.arange(0, half, 2, dtype=F32) / half))
    ar, ac = r[:, None] * inv_freq, c[:, None] * inv_freq
    cs = jnp.concatenate([jnp.cos(ar), jnp.cos(ar), jnp.cos(ac), jnp.cos(ac)], axis=-1)
    sn = jnp.concatenate([-jnp.sin(ar), jnp.sin(ar), -jnp.sin(ac), jnp.sin(ac)], axis=-1)
    cs = jnp.concatenate([cs, jnp.ones((n - l, HEAD_DIM), F32)], axis=0)
    sn = jnp.concatenate([sn, jnp.zeros((n - l, HEAD_DIM), F32)], axis=0)
    return jnp.tile(cs, (1, 2)), jnp.tile(sn, (1, 2))


def kernel(x, c, ctx, c_ctx, ada_w, ada_b, norm1_g, norm2_g, ffn_w_in, ffn_w_out, final_g,
           da_wqkv, da_lambda, da_subln_g, da_wo, sc_w_in, sc_conv_w, sc_w_out,
           wa_wqkv, wa_sinks, wa_wo, hg_w_in, hg_lb, hg_gnorm_g, hg_wo):
    b, l, d = x.shape
    lc = ctx.shape[1]
    depth = ada_w.shape[0]
    assert b == 1 and d % LANES == 0 and l % TM == 0
    n = -(-(l + lc) // TM) * TM
    xs = jnp.concatenate([x[0], ctx[0], jnp.zeros((n - l - lc, d), x.dtype)], axis=0)

    cc = jnp.zeros((8, d), F32).at[0].set(c_ctx).at[1].set(c[0])
    mods = _mod_call(cc, ada_w, ada_b)
    cs, sn = _rope_tables(l, n)

    n_mix = 4
    for i in range(depth):
        mixer, slot = i % n_mix, i // n_mix
        last = i == depth - 1
        mod12 = jnp.concatenate([mods[i, 0].reshape(N_MOD, d), mods[i, 1].reshape(N_MOD, d)], axis=0)
        g1 = norm1_g[i].reshape(1, d)
        g2 = norm2_g[i].reshape(1, d)
        win = ffn_w_in[i].astype(BF16)
        wout = ffn_w_out[i].astype(BF16)
        fin = final_g.reshape(1, d) if last else None

        if mixer == 0:
            w = da_wqkv[slot]
            lam = da_lambda[slot].astype(F32)
            lam_init = 0.8 - 0.6 * math.exp(-0.3 * i)
            lam_full = (jnp.exp(jnp.sum(lam[0] * lam[1])) - jnp.exp(jnp.sum(lam[2] * lam[3]))
                        + lam_init).reshape(1)
            q, k, vt3 = _proj_da_call(xs, g1, mod12, cs, sn, w[:, :2 * d].astype(BF16),
                                      w[:, 2 * d:].T.astype(BF16), l)
            a = _diff_attn_call(lam_full, q, k, vt3, da_subln_g[slot].reshape(LANES, 1), l, lc,
                                1.0 - lam_init)
            xs = _post_call(xs, mod12, g2, da_wo[slot].astype(BF16), win, wout, [a], "plain",
                            l, lc, fin)
        elif mixer == 1:
            bg, z = _proj_sc_call(xs, g1, mod12, sc_w_in[slot].astype(BF16), l)
            xs = _post_call(xs, mod12, g2, sc_w_out[slot].astype(BF16), win, wout,
                            [bg, z, z, z, sc_conv_w[slot]], "conv", l, lc, fin)
        elif mixer == 2:
            w = wa_wqkv[slot]
            kw = WA_KV_HEADS * HEAD_DIM
            dup = lambda m: jnp.tile(m.reshape(d, WA_KV_HEADS, 1, HEAD_DIM), (1, 1, 2, 1)).reshape(d, 2 * kw)
            w2 = jnp.concatenate([w[:, :d], dup(w[:, d:d + kw]), dup(w[:, d + kw:])], axis=1)
            q, k2, vt2 = _proj_wa_call(xs, g1, mod12, cs, sn, w2.astype(BF16), l, 2 * kw)
            a = _win_attn_call(wa_sinks[slot].astype(F32), q, k2, vt2, l, lc)
            xs = _post_call(xs, mod12, g2, wa_wo[slot].astype(BF16), win, wout, [a], "plain",
                            l, lc, fin)
        else:
            w = hg_w_in[slot]
            p = jax.nn.softmax(hg_lb.astype(F32), axis=1)
            lb = (jnp.cumsum(p, axis=1) - p[:, :1])[:, i]
            qh, gates, v, vt3, og = _proj_hg_call(xs, g1, mod12, jnp.log(lb), jnp.log1p(-lb),
                                                  w.astype(BF16), l)
            o2 = _hgrn_call(qh, gates, v, vt3, l, lc)
            xs = _post_call(xs, mod12, g2, hg_wo[slot].astype(BF16), win, wout,
                            [o2, o2, og, hg_gnorm_g[slot].reshape(1, LANES)], "hgrn", l, lc, fin)
    return xs[:l].reshape(1, l, d)
```

```python
import functools
import math

import numpy as np
import jax
import jax.numpy as jnp
from jax import lax
from jax.experimental import pallas as pl
from jax.experimental.pallas import tpu as pltpu

F32 = jnp.float32
BF16 = jnp.bfloat16

HEAD_DIM = 64
GRID_W = 64
ROPE_BASE = 10000.0
RMS_EPS = 1e-6
N_MOD = 6
WINDOW = 128
WA_KV_HEADS = 4
SC_WIDTH = 3
LANES = 128
NEG_BIG = -1e30
LOG2E = math.log2(math.e)

TM = 512
SLAB = 128
HG_CHUNK = 128
HG_PER_STEP = 2
DA_TQ = 256
DA_TK = 1024
DA_ONES_ROWS = 16
DA_UNROLLS = (4, 2)
WA_TQ = 256
FFN_CHUNK = 256
VMEM_LIMIT = 56 * 1024 * 1024

NT_DIMS = (((1,), (1,)), ((), ()))


def _cparams(n_axes):
    return pltpu.CompilerParams(
        dimension_semantics=("arbitrary",) * n_axes, vmem_limit_bytes=VMEM_LIMIT)


def _const_spec(shape):
    nd = len(shape)
    return pl.BlockSpec(shape, lambda *_: (0,) * nd, pipeline_mode=pl.Buffered(1))


def _is_ctx_rows(tile_idx, tm, l):
    row = tile_idx * tm + lax.broadcasted_iota(jnp.int32, (tm, 1), 0)
    return row >= l


def _pick(mod_ref, k, is_ctx):
    return jnp.where(is_ctx, mod_ref[k:k + 1, :], mod_ref[N_MOD + k:N_MOD + k + 1, :])


def _rms(x):
    return x * lax.rsqrt(jnp.mean(x * x, axis=-1, keepdims=True) + RMS_EPS)


def _norm_mod(x, g, shift, scale):
    return (_rms(x) * g) * (1.0 + scale) + shift


def _silu(x):
    return x * (1.0 / (1.0 + jnp.exp(-x)))


def _rope(chunk, cs, sn, first_half):
    partner = jnp.where(first_half, pltpu.roll(chunk, LANES - 16, 1), pltpu.roll(chunk, 16, 1))
    return chunk * cs + partner * sn


def _first_half_lanes():
    lane = lax.broadcasted_iota(jnp.int32, (1, LANES), 1)
    return (lane & 31) < 16


def _mod_kernel(cc_ref, w_ref, b_ref, o_ref):
    a = _silu(cc_ref[...])
    o_ref[0] = jnp.dot(a, w_ref[0], precision=lax.Precision.HIGHEST,
                       preferred_element_type=F32) + b_ref[0]


def _mod_call(cc, ada_w, ada_b):
    depth, d, nd = ada_w.shape
    tn = 1536
    assert nd % tn == 0
    return pl.pallas_call(
        _mod_kernel,
        grid=(depth, nd // tn),
        in_specs=[pl.BlockSpec((8, d), lambda i, j: (0, 0)),
                  pl.BlockSpec((1, d, tn), lambda i, j: (i, 0, j)),
                  pl.BlockSpec((1, 1, tn), lambda i, j: (i, 0, j))],
        out_specs=pl.BlockSpec((1, 8, tn), lambda i, j: (i, 0, j)),
        out_shape=jax.ShapeDtypeStruct((depth, 8, nd), F32),
        compiler_params=_cparams(2),
        name="adaln_mod",
    )(cc, ada_w, ada_b.reshape(depth, 1, nd))


def _proj_da_kernel(x_ref, g_ref, mod_ref, cs_ref, sn_ref, wqk_ref, wvt_ref,
                    q_ref, k_ref, vt_ref, *, tm, l, d):
    is_ctx = _is_ctx_rows(pl.program_id(0), tm, l)
    h = _norm_mod(x_ref[...], g_ref[...], _pick(mod_ref, 0, is_ctx), _pick(mod_ref, 1, is_ctx))
    hb = h.astype(BF16)
    qk = jnp.dot(hb, wqk_ref[...], preferred_element_type=F32)
    cs, sn, first = cs_ref[...], sn_ref[...], _first_half_lanes()
    nq = d // LANES
    for j in range(2 * nq):
        r = _rope(qk[:, j * LANES:(j + 1) * LANES], cs, sn, first)
        if j < nq:
            q_ref[:, j * LANES:(j + 1) * LANES] = (r * (LOG2E * HEAD_DIM ** -0.5)).astype(BF16)
        else:
            k_ref[:, (j - nq) * LANES:(j - nq + 1) * LANES] = r.astype(BF16)
    vt = lax.dot_general(wvt_ref[...], hb, NT_DIMS, preferred_element_type=F32)
    ones = jnp.ones((DA_ONES_ROWS, SLAB), BF16)
    vr = LANES + DA_ONES_ROWS
    for s in range(tm // SLAB):
        for hh in range(d // LANES):
            vt_ref[s, hh * vr:hh * vr + LANES, :] = (
                vt[hh * LANES:(hh + 1) * LANES, s * SLAB:(s + 1) * SLAB].astype(BF16))
            vt_ref[s, hh * vr + LANES:(hh + 1) * vr, :] = ones


def _proj_da_call(xs, g, mod12, cs, sn, wqk, wvt, l):
    n, d = xs.shape
    row = lambda i: (i, 0)
    dv = d // LANES * (LANES + DA_ONES_ROWS)
    return pl.pallas_call(
        functools.partial(_proj_da_kernel, tm=TM, l=l, d=d),
        grid=(n // TM,),
        in_specs=[pl.BlockSpec((TM, d), row), _const_spec((1, d)), _const_spec((2 * N_MOD, d)),
                  pl.BlockSpec((TM, LANES), row), pl.BlockSpec((TM, LANES), row),
                  _const_spec(wqk.shape), _const_spec(wvt.shape)],
        out_specs=[pl.BlockSpec((TM, d), row), pl.BlockSpec((TM, d), row),
                   pl.BlockSpec((TM // SLAB, dv, SLAB), lambda i: (i, 0, 0))],
        out_shape=[jax.ShapeDtypeStruct((n, d), BF16), jax.ShapeDtypeStruct((n, d), BF16),
                   jax.ShapeDtypeStruct((n // SLAB, dv, SLAB), BF16)],
        compiler_params=_cparams(1),
        name="proj_diff_attn",
    )(xs, g, mod12, cs, sn, wqk, wvt)


def _proj_sc_kernel(x_ref, g_ref, mod_ref, w_ref, b_ref, z_ref, *, tm, l, d):
    is_ctx = _is_ctx_rows(pl.program_id(0), tm, l)
    h = _norm_mod(x_ref[...], g_ref[...], _pick(mod_ref, 0, is_ctx), _pick(mod_ref, 1, is_ctx))
    y = jnp.dot(h.astype(BF16), w_ref[...], preferred_element_type=F32)
    b_ref[...] = y[:, :d].astype(BF16)
    z_ref[...] = y[:, d:2 * d] * y[:, 2 * d:]


def _proj_sc_call(xs, g, mod12, w, l):
    n, d = xs.shape
    row = lambda i: (i, 0)
    return pl.pallas_call(
        functools.partial(_proj_sc_kernel, tm=TM, l=l, d=d),
        grid=(n // TM,),
        in_specs=[pl.BlockSpec((TM, d), row), _const_spec((1, d)), _const_spec((2 * N_MOD, d)),
                  _const_spec(w.shape)],
        out_specs=[pl.BlockSpec((TM, d), row), pl.BlockSpec((TM, d), row)],
        out_shape=[jax.ShapeDtypeStruct((n, d), BF16), jax.ShapeDtypeStruct((n, d), F32)],
        compiler_params=_cparams(1),
        name="proj_short_conv",
    )(xs, g, mod12, w)


def _proj_wa_kernel(x_ref, g_ref, mod_ref, cs_ref, sn_ref, w_ref, q_ref, k_ref, v_ref,
                    *, tm, l, d, kw):
    is_ctx = _is_ctx_rows(pl.program_id(0), tm, l)
    h = _norm_mod(x_ref[...], g_ref[...], _pick(mod_ref, 0, is_ctx), _pick(mod_ref, 1, is_ctx))
    y = jnp.dot(h.astype(BF16), w_ref[...], preferred_element_type=F32)
    cs, sn, first = cs_ref[...], sn_ref[...], _first_half_lanes()
    for j in range(d // LANES):
        r = _rope(y[:, j * LANES:(j + 1) * LANES], cs, sn, first)
        q_ref[:, j * LANES:(j + 1) * LANES] = (r * (LOG2E * HEAD_DIM ** -0.5)).astype(BF16)
    for j in range(kw // LANES):
        r = _rope(y[:, d + j * LANES:d + (j + 1) * LANES], cs, sn, first)
        k_ref[:, j * LANES:(j + 1) * LANES] = r.astype(BF16)
    vt = y[:, d + kw:].T
    for s in range(tm // SLAB):
        v_ref[s] = vt[:, s * SLAB:(s + 1) * SLAB].astype(BF16)


def _proj_wa_call(xs, g, mod12, cs, sn, w, l, kw):
    n, d = xs.shape
    row = lambda i: (i, 0)
    return pl.pallas_call(
        functools.partial(_proj_wa_kernel, tm=TM, l=l, d=d, kw=kw),
        grid=(n // TM,),
        in_specs=[pl.BlockSpec((TM, d), row), _const_spec((1, d)), _const_spec((2 * N_MOD, d)),
                  pl.BlockSpec((TM, LANES), row), pl.BlockSpec((TM, LANES), row),
                  _const_spec(w.shape)],
        out_specs=[pl.BlockSpec((TM, d), row), pl.BlockSpec((TM, kw), row),
                   pl.BlockSpec((TM // SLAB, kw, SLAB), lambda i: (i, 0, 0))],
        out_shape=[jax.ShapeDtypeStruct((n, d), BF16), jax.ShapeDtypeStruct((n, kw), BF16),
                   jax.ShapeDtypeStruct((n // SLAB, kw, SLAB), BF16)],
        compiler_params=_cparams(1),
        name="proj_window_attn",
    )(xs, g, mod12, cs, sn, w)


def _log1p_unit(x):
    return jnp.log(1.0 + x)


def _proj_hg_kernel(x_ref, g_ref, mod_ref, lla_ref, l1m_ref, w_ref,
                    q_ref, gate_ref, v_ref, vt_ref, og_ref, *, tm, l, d):
    is_ctx = _is_ctx_rows(pl.program_id(0), tm, l)
    h = _norm_mod(x_ref[...], g_ref[...], _pick(mod_ref, 0, is_ctx), _pick(mod_ref, 1, is_ctx))
    hb = h.astype(BF16)
    part = lambda j: jnp.dot(hb, w_ref[:, j * d:(j + 1) * d], preferred_element_type=F32)
    q_ref[...] = _silu(part(0)).astype(BF16)
    for dr in range(2):
        f = part(1 + dr)
        log_sig = jnp.minimum(f, 0.0) - _log1p_unit(jnp.exp(-jnp.abs(f)))
        a = lla_ref[dr:dr + 1, :]
        c = l1m_ref[dr:dr + 1, :] + log_sig
        gate_ref[dr] = jnp.maximum(a, c) + _log1p_unit(jnp.exp(-jnp.abs(a - c)))
    v = part(3)
    v_ref[...] = v.astype(BF16)
    vt = v.T
    for s in range(tm // SLAB):
        vt_ref[s] = vt[:, s * SLAB:(s + 1) * SLAB].astype(BF16)
    og_ref[...] = _silu(part(4)).astype(BF16)


def _proj_hg_call(xs, g, mod12, lla, l1m, w, l):
    n, d = xs.shape
    row = lambda i: (i, 0)
    return pl.pallas_call(
        functools.partial(_proj_hg_kernel, tm=TM, l=l, d=d),
        grid=(n // TM,),
        in_specs=[pl.BlockSpec((TM, d), row), _const_spec((1, d)), _const_spec((2 * N_MOD, d)),
                  _const_spec((2, d)), _const_spec((2, d)), _const_spec(w.shape)],
        out_specs=[pl.BlockSpec((TM, d), row), pl.BlockSpec((2, TM, d), lambda i: (0, i, 0)),
                   pl.BlockSpec((TM, d), row),
                   pl.BlockSpec((TM // SLAB, d, SLAB), lambda i: (i, 0, 0)),
                   pl.BlockSpec((TM, d), row)],
        out_shape=[jax.ShapeDtypeStruct((n, d), BF16), jax.ShapeDtypeStruct((2, n, d), F32),
                   jax.ShapeDtypeStruct((n, d), BF16), jax.ShapeDtypeStruct((n // SLAB, d, SLAB), BF16),
                   jax.ShapeDtypeStruct((n, d), BF16)],
        compiler_params=_cparams(1),
        name="proj_hgrn",
    )(xs, g, mod12, lla, l1m, w)


def _diff_attn_kernel(lam_ref, q_ref, qn_ref, k_ref, vt_ref, g_ref, o_ref, m_ref, acc_ref,
                      sa_ref, sb_ref, ma_ref, mb_ref, *, tq, tk, l, lc, unroll, post_scale):
    qi = pl.program_id(1)
    slot = qi % 2
    lane = lax.broadcasted_iota(jnp.int32, (1, LANES), 1)
    sub = 2 * vt_ref.shape[2]
    n_chunks = l // tk

    def stacked(ref):
        q = ref[...]
        zero = jnp.zeros_like(q)
        return jnp.concatenate([jnp.where(lane < HEAD_DIM, q, zero),
                                jnp.where(lane >= HEAD_DIM, q, zero)], axis=0)

    def scores(kc, qcat):
        return lax.dot_general(kc, qcat, NT_DIMS, preferred_element_type=F32)

    def pv(pair0, n_pairs, p):
        out = None
        for j in range(n_pairs):
            s0 = 2 * (pair0 + j)
            vt = jnp.concatenate([vt_ref[s0], vt_ref[s0 + 1]], axis=1)
            t = jnp.dot(vt, p[j * sub:(j + 1) * sub, :], preferred_element_type=F32)
            out = t if out is None else out + t
        return out

    def absorb_context(qcat, slot):
        s = scores(k_ref[l:l + lc, :], qcat)
        mx = jnp.max(s, axis=0, keepdims=True)
        m_ref[slot] = mx
        acc_ref[slot] = pv(l // sub, lc // sub, jnp.exp2(s - mx).astype(BF16))

    def qk(c, qcat, s_ref, mx_ref):
        off = pl.multiple_of(c * tk, LANES)
        s = scores(k_ref[pl.ds(off, tk), :], qcat)
        s_ref[...] = s
        mx_ref[...] = jnp.max(s, axis=0, keepdims=True)

    def absorb(c, s_ref, mx_ref):
        m_old = m_ref[slot]
        m_new = jnp.maximum(m_old, mx_ref[...])
        alpha = jnp.exp2(m_old - m_new)
        p = jnp.exp2(s_ref[...] - m_new).astype(BF16)
        m_ref[slot] = m_new
        acc_ref[slot] = alpha * acc_ref[slot] + pv((c * tk) // sub, tk // sub, p)

    def start_sweep(qcat, slot):
        qk(0, qcat, sa_ref, ma_ref)
        absorb_context(qcat, slot)

    @pl.when(qi * tq >= l)
    def _():
        absorb_context(stacked(q_ref), slot)

    @pl.when(qi == 0)
    def _():
        start_sweep(stacked(q_ref), slot)

    @pl.when(qi * tq < l)
    def _():
        qcat = stacked(q_ref)
        bufs = ((sa_ref, ma_ref), (sb_ref, mb_ref))

        def group(c0, count):
            for j in range(count):
                qk(c0 + j + 1, qcat, *bufs[(j + 1) % 2])
                absorb(c0 + j, *bufs[j % 2])

        def body(i, carry):
            group(unroll * i, unroll)
            return carry

        lax.fori_loop(0, n_chunks // unroll - 1, body, 0)
        group(n_chunks - unroll, unroll - 1)
        start_sweep(stacked(qn_ref), 1 - slot)
        absorb(n_chunks - 1, *bufs[(unroll - 1) % 2])

    acc = acc_ref[slot]
    acc = acc[0:LANES, :] * (1.0 / acc[LANES:LANES + 1, :])
    o = acc[:, :tq] - lam_ref[0] * acc[:, tq:]
    o = o * lax.rsqrt(jnp.mean(o * o, axis=0, keepdims=True) + RMS_EPS)
    o = o * (g_ref[...] * post_scale)
    o_ref[...] = o.T.astype(BF16)


def _diff_attn_call(lam, q, k, vt3, subln_g, l, lc, post_scale):
    n, d = q.shape
    nh = d // LANES
    nslab, vrows, slab = vt3.shape
    vrows //= nh
    tq = DA_TQ
    tk = min(DA_TK, l // 2)
    assert l % tq == 0 and n % tq == 0 and tk % (2 * slab) == 0
    assert l % (2 * slab) == 0 and lc % (2 * slab) == 0
    assert l % (2 * tk) == 0
    return pl.pallas_call(
        functools.partial(_diff_attn_kernel, tq=tq, tk=tk, l=l, lc=lc,
                          unroll=next(u for u in DA_UNROLLS if l % (u * tk) == 0),
                          post_scale=post_scale),
        grid=(nh, n // tq),
        in_specs=[pl.BlockSpec(memory_space=pltpu.SMEM),
                  pl.BlockSpec((tq, LANES), lambda h, i: (i, h)),
                  pl.BlockSpec((tq, LANES), lambda h, i: (jnp.minimum(i + 1, n // tq - 1), h)),
                  pl.BlockSpec((n, LANES), lambda h, i: (0, h)),
                  pl.BlockSpec((nslab, vrows, slab), lambda h, i: (0, h, 0)),
                  pl.BlockSpec((LANES, 1), lambda h, i: (0, 0))],
        out_specs=pl.BlockSpec((tq, LANES), lambda h, i: (i, h)),
        out_shape=jax.ShapeDtypeStruct((n, d), BF16),
        scratch_shapes=[pltpu.VMEM((2, 1, 2 * tq), F32),
                        pltpu.VMEM((2, vrows, 2 * tq), F32),
                        pltpu.VMEM((tk, 2 * tq), F32), pltpu.VMEM((tk, 2 * tq), F32),
                        pltpu.VMEM((1, 2 * tq), F32), pltpu.VMEM((1, 2 * tq), F32)],
        compiler_params=_cparams(2),
        name="diff_attention",
    )(lam, q, q, k, vt3, subln_g)


def _win_attn_kernel(sink_ref, q_ref, kc_ref, kp_ref, ko_ref, kn_ref,
                     vc_ref, vp_ref, vo_ref, vn_ref, o_ref, *, tq, l, lc, rep):
    j = pl.program_id(0)
    g = pl.program_id(1)
    w = WINDOW
    kc = kc_ref[...]
    kwin = jnp.concatenate([kp_ref[...], ko_ref[...], kn_ref[...]], axis=0)
    vtc = jnp.concatenate([vc_ref[s] for s in range(lc // w)], axis=1)
    vtwin = jnp.concatenate([vp_ref[0]] + [vo_ref[s] for s in range(tq // w)] + [vn_ref[0]],
                            axis=1)
    a = lax.broadcasted_iota(jnp.int32, (w, w), 0)
    b = lax.broadcasted_iota(jnp.int32, (w, w), 1)
    lane = lax.broadcasted_iota(jnp.int32, (1, LANES), 1)
    low = lane < HEAD_DIM
    for t in range(tq // w):
        q0 = j * tq + t * w
        latent = q0 < l
        own_bias = jnp.where(latent, 0.0, NEG_BIG)
        prev_bias = jnp.where(jnp.logical_and(latent, q0 >= w), 0.0, NEG_BIG)
        next_bias = jnp.where(q0 + w < l, 0.0, NEG_BIG)
        bias = jnp.concatenate(
            [jnp.where(a >= b, prev_bias, NEG_BIG),
             jnp.full((w, w), own_bias, F32),
             jnp.where(a <= b, next_bias, NEG_BIG)], axis=0)
        qs, sinks = [], []
        for r in range(rep):
            qc = q_ref[t * w:(t + 1) * w, (r // 2) * LANES:(r // 2 + 1) * LANES]
            qs.append(jnp.where(low if r % 2 == 0 else ~low, qc, jnp.zeros_like(qc)))
            sinks.append(jnp.full((1, w), sink_ref[g * rep + r] * LOG2E, F32))
        qst = jnp.concatenate(qs, axis=0)
        sink = jnp.concatenate(sinks, axis=1)
        s_c = lax.dot_general(kc, qst, NT_DIMS, preferred_element_type=F32)
        s_w = (lax.dot_general(kwin[t * w:(t + 3) * w, :], qst, NT_DIMS,
                               preferred_element_type=F32)
               + jnp.concatenate([bias] * rep, axis=1))
        mx = jnp.maximum(jnp.maximum(jnp.max(s_c, axis=0, keepdims=True),
                                     jnp.max(s_w, axis=0, keepdims=True)), sink)
        e_c = jnp.exp2(s_c - mx)
        e_w = jnp.exp2(s_w - mx)
        den = (jnp.exp2(sink - mx) + jnp.sum(e_c, axis=0, keepdims=True)
               + jnp.sum(e_w, axis=0, keepdims=True))
        ot = (jnp.dot(vtc, e_c.astype(BF16), preferred_element_type=F32)
              + jnp.dot(vtwin[:, t * w:(t + 3) * w], e_w.astype(BF16),
                        preferred_element_type=F32)) * (1.0 / den)
        heads = [ot[:, r * w:(r + 1) * w].T for r in range(rep)]
        for c in range(rep // 2):
            o_ref[t * w:(t + 1) * w, c * LANES:(c + 1) * LANES] = jnp.where(
                low, heads[2 * c], heads[2 * c + 1]).astype(BF16)


def _win_attn_call(sinks, q, k2, vt2, l, lc):
    n, d = q.shape
    ng = WA_KV_HEADS
    rep = d // HEAD_DIM // ng
    gw = rep * HEAD_DIM
    tq = WA_TQ
    assert n % tq == 0 and tq % WINDOW == 0 and lc % WINDOW == 0 and l % lc == 0 and rep % 2 == 0
    r = tq // WINDOW
    last = n // WINDOW - 1
    prev = lambda j: jnp.maximum(j * r - 1, 0)
    nxt = lambda j: jnp.minimum((j + 1) * r, last)
    k_specs = [pl.BlockSpec((lc, LANES), lambda j, g: (l // lc, g)),
               pl.BlockSpec((WINDOW, LANES), lambda j, g: (prev(j), g)),
               pl.BlockSpec((tq, LANES), lambda j, g: (j, g)),
               pl.BlockSpec((WINDOW, LANES), lambda j, g: (nxt(j), g))]
    vt_specs = [pl.BlockSpec((lc // WINDOW, LANES, WINDOW), lambda j, g: (l // lc, g, 0)),
                pl.BlockSpec((1, LANES, WINDOW), lambda j, g: (prev(j), g, 0)),
                pl.BlockSpec((r, LANES, WINDOW), lambda j, g: (j, g, 0)),
                pl.BlockSpec((1, LANES, WINDOW), lambda j, g: (nxt(j), g, 0))]
    return pl.pallas_call(
        functools.partial(_win_attn_kernel, tq=tq, l=l, lc=lc, rep=rep),
        grid=(n // tq, ng),
        in_specs=[pl.BlockSpec(memory_space=pltpu.SMEM),
                  pl.BlockSpec((tq, gw), lambda j, g: (j, g))] + k_specs + vt_specs,
        out_specs=pl.BlockSpec((tq, gw), lambda j, g: (j, g)),
        out_shape=jax.ShapeDtypeStruct((n, d), BF16),
        compiler_params=_cparams(2),
        name="window_attention",
    )(sinks, q, k2, k2, k2, k2, vt2, vt2, vt2, vt2)


def _hgrn_consts(chunk):
    n_lv = int(math.log2(chunk))
    walls, masks_all = [], []
    for direction in range(2):
        tau = np.arange(chunk) if direction == 0 else chunk - 1 - np.arange(chunk)
        tr, tc = tau[:, None], tau[None, :]
        blocks = [tc <= tr]
        masks = []
        for lv in range(n_lv):
            h = chunk >> (lv + 1)
            mid = (tr // (2 * h)) * 2 * h + h - 1
            second_r = (tr % (2 * h)) >= h
            second_c = (tc % (2 * h)) >= h
            blocks.append(np.where(second_r, (tc > mid) & (tc <= tr), (tc > tr) & (tc <= mid)))
            masks.append(second_r & ~second_c & ((tr // (2 * h)) == (tc // (2 * h))))
        masks.append(tr == tc)
        w = np.concatenate(blocks, axis=0).astype(np.float32)
        walls.append(np.concatenate([w, w], axis=1))
        masks_all.append(np.stack(masks).astype(np.float32))
    return np.stack(walls), np.stack(masks_all)


def _hgrn_kernel(wall_ref, mask_ref, q_ref, g_ref, v_ref, vt_ref, o_ref, st_ref,
                 *, chunk, n_lv, n_heads, n_steps, per_step):
    dr = pl.program_id(0)
    step = pl.program_id(1)

    @pl.when(step == 0)
    def _():
        st_ref[...] = jnp.zeros_like(st_ref)

    @pl.when(step >= n_steps)
    def _():
        o_ref[...] = jnp.zeros_like(o_ref)

    @pl.when(step < n_steps)
    def _():
        heads = [slice(hh * LANES, (hh + 1) * LANES) for hh in range(n_heads)]
        nt = lambda x, y: lax.dot_general(x, y, NT_DIMS, preferred_element_type=F32)
        for j in range(per_step):
            idx = jnp.where(dr == 0, j, per_step - 1 - j)
            rows = pl.ds(pl.multiple_of(idx * chunk, chunk), chunk)
            g_all = g_ref[0, rows, :]
            g_hi = g_all.astype(BF16)
            g_lo = (g_all - g_hi.astype(F32)).astype(BF16)
            sums_all = jnp.dot(wall_ref[0], jnp.concatenate([g_hi, g_lo], axis=0),
                               preferred_element_type=F32)
            q16 = q_ref[rows, :]
            q_all = q16.astype(F32)
            k_all = 1.0 - jnp.exp(g_all)
            k16 = k_all.astype(BF16)
            v_all = v_ref[rows, :]
            vt_all = vt_ref[idx]
            b = sums_all[0:chunk]
            b_end = jnp.sum(g_all, axis=0, keepdims=True)

            qb = (q_all * jnp.exp(b)).astype(BF16)
            o = [nt(qb[:, sl], st_ref[hh].astype(BF16)) for hh, sl in enumerate(heads)]
            att = [nt(q16[:, sl], k16[:, sl]) * mask_ref[0, n_lv] for sl in heads]
            for lv in range(n_lv):
                e = jnp.exp(sums_all[(1 + lv) * chunk:(2 + lv) * chunk])
                qe = (q_all * e).astype(BF16)
                ke = (k_all * e).astype(BF16)
                mask = mask_ref[0, lv]
                att = [a + nt(qe[:, sl], ke[:, sl]) * mask for a, sl in zip(att, heads)]
            out = [o[hh] + jnp.dot(att[hh].astype(BF16), v_all[:, sl], preferred_element_type=F32)
                   for hh, sl in enumerate(heads)]
            o_ref[0, rows, :] = jnp.concatenate(out, axis=1)

            ks = (k_all * jnp.exp(b_end - b)).astype(BF16)
            decay = jnp.exp(b_end)
            for hh, sl in enumerate(heads):
                st_ref[hh] = st_ref[hh] * decay[:, sl] + jnp.dot(vt_all[sl, :], ks[:, sl],
                                                                 preferred_element_type=F32)


def _hgrn_call(qh, gates, v, vt3, l, lc):
    n, d = qh.shape
    c = HG_CHUNK
    per = HG_PER_STEP
    rows = per * c
    nh = d // LANES
    nx, ncc, ntot = l // rows, lc // rows, n // rows
    nc = nx + ncc
    assert n % rows == 0 and l % rows == 0 and lc % rows == 0 and vt3.shape[2] == c
    n_lv = int(math.log2(c))
    wall_np, mask_np = _hgrn_consts(c)
    wall = jnp.asarray(wall_np, dtype=BF16)
    masks = jnp.asarray(mask_np, dtype=F32)

    def blk(dr, s):
        fwd = jnp.where(s < ncc, nx + s, s - ncc)
        bwd = jnp.where(s < ncc, nx + ncc - 1 - s, nx - 1 - (s - ncc))
        return jnp.where(s >= nc, s, jnp.where(dr == 0, fwd, bwd))

    return pl.pallas_call(
        functools.partial(_hgrn_kernel, chunk=c, n_lv=n_lv, n_heads=nh, n_steps=nc, per_step=per),
        grid=(2, ntot),
        in_specs=[pl.BlockSpec((1,) + wall_np.shape[1:], lambda dr, s: (dr, 0, 0)),
                  pl.BlockSpec((1,) + mask_np.shape[1:], lambda dr, s: (dr, 0, 0, 0)),
                  pl.BlockSpec((rows, d), lambda dr, s: (blk(dr, s), 0)),
                  pl.BlockSpec((1, rows, d), lambda dr, s: (dr, blk(dr, s), 0)),
                  pl.BlockSpec((rows, d), lambda dr, s: (blk(dr, s), 0)),
                  pl.BlockSpec((per, d, c), lambda dr, s: (blk(dr, s), 0, 0))],
        out_specs=pl.BlockSpec((1, rows, d), lambda dr, s: (dr, blk(dr, s), 0)),
        out_shape=jax.ShapeDtypeStruct((2, n, d), F32),
        scratch_shapes=[pltpu.VMEM((nh, LANES, LANES), F32)],
        compiler_params=_cparams(2),
        name="hgrn_scan",
    )(wall, masks, qh, gates, v, vt3)


def _post_kernel(*refs, mode, final, tm, l, lc, d, f):
    it = iter(refs)
    x_ref, mod_ref, g2_ref, wo_ref, win_ref, wout_ref = (next(it) for _ in range(6))
    if mode == "plain":
        a_ref = next(it)
    elif mode == "conv":
        b_ref, z_ref, zp_ref, zn_ref, cw_ref = (next(it) for _ in range(5))
    else:
        ofw_ref, obw_ref, og_ref, gn_ref = (next(it) for _ in range(4))
    fin_ref = next(it) if final else None
    o_ref = next(it)

    ti = pl.program_id(0)
    is_ctx = _is_ctx_rows(ti, tm, l)
    if mode == "plain":
        a = a_ref[...]
    elif mode == "conv":
        z = z_ref[...]
        rl = lax.broadcasted_iota(jnp.int32, (tm, 1), 0)
        row = ti * tm + rl
        z_prev = jnp.where(rl == 0, zp_ref[7:8, :], pltpu.roll(z, 1, 0))
        z_next = jnp.where(rl == tm - 1, zn_ref[0:1, :], pltpu.roll(z, tm - 1, 0))
        z_prev = jnp.where((row == 0) | (row == l), 0.0, z_prev)
        z_next = jnp.where((row == l - 1) | (row == l + lc - 1), 0.0, z_next)
        conv = cw_ref[0:1, :] * z_prev + cw_ref[1:2, :] * z + cw_ref[2:3, :] * z_next
        a = (b_ref[...].astype(F32) * conv).astype(BF16)
    else:
        gn = gn_ref[...]
        heads = []
        for hh in range(d // LANES):
            sl = slice(hh * LANES, (hh + 1) * LANES)
            oh = _rms(ofw_ref[0, :, sl] + obw_ref[0, :, sl])
            heads.append((oh * gn * og_ref[:, sl].astype(F32)).astype(BF16))
        a = jnp.concatenate(heads, axis=1)

    y = jnp.dot(a, wo_ref[...], preferred_element_type=F32)
    x = x_ref[...] + _pick(mod_ref, 2, is_ctx) * y
    h = _norm_mod(x, g2_ref[...], _pick(mod_ref, 3, is_ctx), _pick(mod_ref, 4, is_ctx)).astype(BF16)
    acc = jnp.zeros((tm, d), F32)
    for c0 in range(0, f, FFN_CHUNK):
        u = jnp.dot(h, win_ref[:, c0:c0 + FFN_CHUNK], preferred_element_type=F32)
        w = jnp.dot(h, win_ref[:, f + c0:f + c0 + FFN_CHUNK], preferred_element_type=F32)
        act = (_silu(u) * w).astype(BF16)
        acc = acc + jnp.dot(act, wout_ref[c0:c0 + FFN_CHUNK, :], preferred_element_type=F32)
    x = x + _pick(mod_ref, 5, is_ctx) * acc
    if final:
        x = _rms(x) * fin_ref[...]
    o_ref[...] = x


def _post_call(xs, mod12, g2, wo, win, wout, mixer_inputs, mode, l, lc, final_g=None):
    n, d = xs.shape
    f = wout.shape[0]
    final = final_g is not None
    assert f % FFN_CHUNK == 0 and l % TM == 0 and n % TM == 0
    n_out = l if final else n
    row = lambda i: (i, 0)
    in_specs = [pl.BlockSpec((TM, d), row), _const_spec((2 * N_MOD, d)), _const_spec((1, d)),
                _const_spec(wo.shape), _const_spec(win.shape), _const_spec(wout.shape)]
    args = [xs, mod12, g2, wo, win, wout]
    if mode == "plain":
        in_specs += [pl.BlockSpec((TM, d), row)]
    elif mode == "conv":
        r8 = TM // 8
        last8 = n // 8 - 1
        in_specs += [pl.BlockSpec((TM, d), row), pl.BlockSpec((TM, d), row),
                     pl.BlockSpec((8, d), lambda i: (jnp.maximum(i * r8 - 1, 0), 0)),
                     pl.BlockSpec((8, d), lambda i: (jnp.minimum((i + 1) * r8, last8), 0)),
                     _const_spec((SC_WIDTH, d))]
    else:
        in_specs += [pl.BlockSpec((1, TM, d), lambda i: (0, i, 0)),
                     pl.BlockSpec((1, TM, d), lambda i: (1, i, 0)),
                     pl.BlockSpec((TM, d), row), _const_spec((1, LANES))]
    args += list(mixer_inputs)
    if final:
        in_specs += [_const_spec((1, d))]
        args += [final_g]
    return pl.pallas_call(
        functools.partial(_post_kernel, mode=mode, final=final, tm=TM, l=l, lc=lc, d=d, f=f),
        grid=(n_out // TM,),
        in_specs=in_specs,
        out_specs=pl.BlockSpec((TM, d), row),
        out_shape=jax.ShapeDtypeStruct((n_out, d), F32),
        compiler_params=_cparams(1),
        name="post_" + mode + ("_final" if final else ""),
    )(*args)


def _rope_tables(l, n):
    rows = l // GRID_W
    r = jnp.repeat(jnp.arange(rows), GRID_W).astype(F32)
    c = jnp.tile(jnp.arange(GRID_W), rows).astype(F32)
    half = HEAD_DIM // 2
    inv_freq = 1.0 / (ROPE_BASE ** (jnp.arange(0, half, 2, dtype=F32) / half))
    ar, ac = r[:, None] * inv_freq, c[:, None] * inv_freq
    cs = jnp.concatenate([jnp.cos(ar), jnp.cos(ar), jnp.cos(ac), jnp.cos(ac)], axis=-1)
    sn = jnp.concatenate([-jnp.sin(ar), jnp.sin(ar), -jnp.sin(ac), jnp.sin(ac)], axis=-1)
    cs = jnp.concatenate([cs, jnp.ones((n - l, HEAD_DIM), F32)], axis=0)
    sn = jnp.concatenate([sn, jnp.zeros((n - l, HEAD_DIM), F32)], axis=0)
    return jnp.tile(cs, (1, 2)), jnp.tile(sn, (1, 2))


def kernel(x, c, ctx, c_ctx, ada_w, ada_b, norm1_g, norm2_g, ffn_w_in, ffn_w_out, final_g,
           da_wqkv, da_lambda, da_subln_g, da_wo, sc_w_in, sc_conv_w, sc_w_out,
           wa_wqkv, wa_sinks, wa_wo, hg_w_in, hg_lb, hg_gnorm_g, hg_wo):
    b, l, d = x.shape
    lc = ctx.shape[1]
    depth = ada_w.shape[0]
    assert b == 1 and d % LANES == 0 and l % TM == 0
    n = -(-(l + lc) // TM) * TM
    xs = jnp.concatenate([x[0], ctx[0], jnp.zeros((n - l - lc, d), x.dtype)], axis=0)

    cc = jnp.zeros((8, d), F32).at[0].set(c_ctx).at[1].set(c[0])
    mods = _mod_call(cc, ada_w, ada_b)
    cs, sn = _rope_tables(l, n)

    n_mix = 4
    for i in range(depth):
        mixer, slot = i % n_mix, i // n_mix
        last = i == depth - 1
        mod12 = jnp.concatenate([mods[i, 0].reshape(N_MOD, d), mods[i, 1].reshape(N_MOD, d)], axis=0)
        g1 = norm1_g[i].reshape(1, d)
        g2 = norm2_g[i].reshape(1, d)
        win = ffn_w_in[i].astype(BF16)
        wout = ffn_w_out[i].astype(BF16)
        fin = final_g.reshape(1, d) if last else None

        if mixer == 0:
            w = da_wqkv[slot]
            lam = da_lambda[slot].astype(F32)
            lam_init = 0.8 - 0.6 * math.exp(-0.3 * i)
            lam_full = (jnp.exp(jnp.sum(lam[0] * lam[1])) - jnp.exp(jnp.sum(lam[2] * lam[3]))
                        + lam_init).reshape(1)
            q, k, vt3 = _proj_da_call(xs, g1, mod12, cs, sn, w[:, :2 * d].astype(BF16),
                                      w[:, 2 * d:].T.astype(BF16), l)
            a = _diff_attn_call(lam_full, q, k, vt3, da_subln_g[slot].reshape(LANES, 1), l, lc,
                                1.0 - lam_init)
            xs = _post_call(xs, mod12, g2, da_wo[slot].astype(BF16), win, wout, [a], "plain",
                            l, lc, fin)
        elif mixer == 1:
            bg, z = _proj_sc_call(xs, g1, mod12, sc_w_in[slot].astype(BF16), l)
            xs = _post_call(xs, mod12, g2, sc_w_out[slot].astype(BF16), win, wout,
                            [bg, z, z, z, sc_conv_w[slot]], "conv", l, lc, fin)
        elif mixer == 2:
            w = wa_wqkv[slot]
            kw = WA_KV_HEADS * HEAD_DIM
            dup = lambda m: jnp.tile(m.reshape(d, WA_KV_HEADS, 1, HEAD_DIM), (1, 1, 2, 1)).reshape(d, 2 * kw)
            w2 = jnp.concatenate([w[:, :d], dup(w[:, d:d + kw]), dup(w[:, d + kw:])], axis=1)
            q, k2, vt2 = _proj_wa_call(xs, g1, mod12, cs, sn, w2.astype(BF16), l, 2 * kw)
            a = _win_attn_call(wa_sinks[slot].astype(F32), q, k2, vt2, l, lc)
            xs = _post_call(xs, mod12, g2, wa_wo[slot].astype(BF16), win, wout, [a], "plain",
                            l, lc, fin)
        else:
            w = hg_w_in[slot]
            p = jax.nn.softmax(hg_lb.astype(F32), axis=1)
            lb = (jnp.cumsum(p, axis=1) - p[:, :1])[:, i]
            qh, gates, v, vt3, og = _proj_hg_call(xs, g1, mod12, jnp.log(lb), jnp.log1p(-lb),
                                                  w.astype(BF16), l)
            o2 = _hgrn_call(qh, gates, v, vt3, l, lc)
            xs = _post_call(xs, mod12, g2, hg_wo[slot].astype(BF16), win, wout,
                            [o2, o2, og, hg_gnorm_g[slot].reshape(1, LANES)], "hgrn", l, lc, fin)
    return xs[:l].reshape(1, l, d)
```

```python
import functools
import math

import numpy as np
import jax
import jax.numpy as jnp
from jax import lax
from jax.experimental import pallas as pl
from jax.experimental.pallas import tpu as pltpu

F32 = jnp.float32
BF16 = jnp.bfloat16

HEAD_DIM = 64
GRID_W = 64
ROPE_BASE = 10000.0
RMS_EPS = 1e-6
N_MOD = 6
WINDOW = 128
WA_KV_HEADS = 4
SC_WIDTH = 3
LANES = 128
NEG_BIG = -1e30
LOG2E = math.log2(math.e)

TM = 512
SLAB = 128
HG_CHUNK = 128
HG_PER_STEP = 2
DA_TQ = 256
DA_TK = 1024
DA_ONES_ROWS = 16
DA_UNROLLS = (4, 2)
WA_TQ = 256
FFN_CHUNK = 256
VMEM_LIMIT = 56 * 1024 * 1024

NT_DIMS = (((1,), (1,)), ((), ()))


def _cparams(n_axes):
    return pltpu.CompilerParams(
        dimension_semantics=("arbitrary",) * n_axes, vmem_limit_bytes=VMEM_LIMIT)


def _const_spec(shape):
    nd = len(shape)
    return pl.BlockSpec(shape, lambda *_: (0,) * nd, pipeline_mode=pl.Buffered(1))


def _is_ctx_rows(tile_idx, tm, l):
    row = tile_idx * tm + lax.broadcasted_iota(jnp.int32, (tm, 1), 0)
    return row >= l


def _pick(mod_ref, k, is_ctx):
    return jnp.where(is_ctx, mod_ref[k:k + 1, :], mod_ref[N_MOD + k:N_MOD + k + 1, :])


def _rms(x):
    return x * lax.rsqrt(jnp.mean(x * x, axis=-1, keepdims=True) + RMS_EPS)


def _norm_mod(x, g, shift, scale):
    return (_rms(x) * g) * (1.0 + scale) + shift


def _silu(x):
    return x * (1.0 / (1.0 + jnp.exp(-x)))


def _rope(chunk, cs, sn, first_half):
    partner = jnp.where(first_half, pltpu.roll(chunk, LANES - 16, 1), pltpu.roll(chunk, 16, 1))
    return chunk * cs + partner * sn


def _stream_specs(xs, l):
    if not isinstance(xs, tuple):
        n, d = xs.shape
        return [pl.BlockSpec((TM, d), lambda i: (i, 0))], [xs], n, d
    lat, tail = xs
    d = lat.shape[1]
    nl = l // TM
    assert lat.shape[0] == l and tail.shape[0] % TM == 0
    return ([pl.BlockSpec((TM, d), lambda i: (jnp.minimum(i, nl - 1), 0)),
             pl.BlockSpec((TM, d), lambda i: (jnp.maximum(i - nl, 0), 0))],
            [lat, tail], l + tail.shape[0], d)


def _stream_tile(x_refs, tile_idx, tm, l):
    if len(x_refs) == 1:
        return x_refs[0][...]
    return jnp.where(tile_idx * tm < l, x_refs[0][...], x_refs[1][...])


def _first_half_lanes():
    lane = lax.broadcasted_iota(jnp.int32, (1, LANES), 1)
    return (lane & 31) < 16


def _mod_kernel(cc_ref, w_ref, b_ref, o_ref):
    a = _silu(cc_ref[...])
    o_ref[0] = jnp.dot(a, w_ref[0], precision=lax.Precision.HIGHEST,
                       preferred_element_type=F32) + b_ref[0]


def _mod_call(cc, ada_w, ada_b):
    depth, d, nd = ada_w.shape
    tn = 1536
    assert nd % tn == 0
    return pl.pallas_call(
        _mod_kernel,
        grid=(depth, nd // tn),
        in_specs=[pl.BlockSpec((8, d), lambda i, j: (0, 0)),
                  pl.BlockSpec((1, d, tn), lambda i, j: (i, 0, j)),
                  pl.BlockSpec((1, 1, tn), lambda i, j: (i, 0, j))],
        out_specs=pl.BlockSpec((1, 8, tn), lambda i, j: (i, 0, j)),
        out_shape=jax.ShapeDtypeStruct((depth, 8, nd), F32),
        compiler_params=_cparams(2),
        name="adaln_mod",
    )(cc, ada_w, ada_b.reshape(depth, 1, nd))


def _proj_da_kernel(*refs, tm, l, d, n_x):
    x_refs = refs[:n_x]
    g_ref, mod_ref, cs_ref, sn_ref, wqk_ref, wvt_ref, q_ref, k_ref, vt_ref = refs[n_x:]
    is_ctx = _is_ctx_rows(pl.program_id(0), tm, l)
    x = _stream_tile(x_refs, pl.program_id(0), tm, l)
    h = _norm_mod(x, g_ref[...], _pick(mod_ref, 0, is_ctx), _pick(mod_ref, 1, is_ctx))
    hb = h.astype(BF16)
    qk = jnp.dot(hb, wqk_ref[...], preferred_element_type=F32)
    cs, sn, first = cs_ref[...], sn_ref[...], _first_half_lanes()
    nq = d // LANES
    for j in range(2 * nq):
        r = _rope(qk[:, j * LANES:(j + 1) * LANES], cs, sn, first)
        if j < nq:
            q_ref[:, j * LANES:(j + 1) * LANES] = (r * (LOG2E * HEAD_DIM ** -0.5)).astype(BF16)
        else:
            k_ref[:, (j - nq) * LANES:(j - nq + 1) * LANES] = r.astype(BF16)
    vt = lax.dot_general(wvt_ref[...], hb, NT_DIMS, preferred_element_type=F32)
    ones = jnp.ones((DA_ONES_ROWS, SLAB), BF16)
    vr = LANES + DA_ONES_ROWS
    for s in range(tm // SLAB):
        for hh in range(d // LANES):
            vt_ref[s, hh * vr:hh * vr + LANES, :] = (
                vt[hh * LANES:(hh + 1) * LANES, s * SLAB:(s + 1) * SLAB].astype(BF16))
            vt_ref[s, hh * vr + LANES:(hh + 1) * vr, :] = ones


def _proj_da_call(xs, g, mod12, cs, sn, wqk, wvt, l):
    x_specs, x_args, n, d = _stream_specs(xs, l)
    row = lambda i: (i, 0)
    dv = d // LANES * (LANES + DA_ONES_ROWS)
    return pl.pallas_call(
        functools.partial(_proj_da_kernel, tm=TM, l=l, d=d, n_x=len(x_args)),
        grid=(n // TM,),
        in_specs=x_specs + [_const_spec((1, d)), _const_spec((2 * N_MOD, d)),
                            pl.BlockSpec((TM, LANES), row), pl.BlockSpec((TM, LANES), row),
                            _const_spec(wqk.shape), _const_spec(wvt.shape)],
        out_specs=[pl.BlockSpec((TM, d), row), pl.BlockSpec((TM, d), row),
                   pl.BlockSpec((TM // SLAB, dv, SLAB), lambda i: (i, 0, 0))],
        out_shape=[jax.ShapeDtypeStruct((n, d), BF16), jax.ShapeDtypeStruct((n, d), BF16),
                   jax.ShapeDtypeStruct((n // SLAB, dv, SLAB), BF16)],
        compiler_params=_cparams(1),
        name="proj_diff_attn",
    )(*x_args, g, mod12, cs, sn, wqk, wvt)


def _proj_sc_kernel(x_ref, g_ref, mod_ref, w_ref, b_ref, z_ref, *, tm, l, d):
    is_ctx = _is_ctx_rows(pl.program_id(0), tm, l)
    h = _norm_mod(x_ref[...], g_ref[...], _pick(mod_ref, 0, is_ctx), _pick(mod_ref, 1, is_ctx))
    y = jnp.dot(h.astype(BF16), w_ref[...], preferred_element_type=F32)
    b_ref[...] = y[:, :d].astype(BF16)
    z_ref[...] = y[:, d:2 * d] * y[:, 2 * d:]


def _proj_sc_call(xs, g, mod12, w, l):
    n, d = xs.shape
    row = lambda i: (i, 0)
    return pl.pallas_call(
        functools.partial(_proj_sc_kernel, tm=TM, l=l, d=d),
        grid=(n // TM,),
        in_specs=[pl.BlockSpec((TM, d), row), _const_spec((1, d)), _const_spec((2 * N_MOD, d)),
                  _const_spec(w.shape)],
        out_specs=[pl.BlockSpec((TM, d), row), pl.BlockSpec((TM, d), row)],
        out_shape=[jax.ShapeDtypeStruct((n, d), BF16), jax.ShapeDtypeStruct((n, d), F32)],
        compiler_params=_cparams(1),
        name="proj_short_conv",
    )(xs, g, mod12, w)


def _proj_wa_kernel(x_ref, g_ref, mod_ref, cs_ref, sn_ref, w_ref, q_ref, k_ref, v_ref,
                    *, tm, l, d, kw):
    is_ctx = _is_ctx_rows(pl.program_id(0), tm, l)
    h = _norm_mod(x_ref[...], g_ref[...], _pick(mod_ref, 0, is_ctx), _pick(mod_ref, 1, is_ctx))
    y = jnp.dot(h.astype(BF16), w_ref[...], preferred_element_type=F32)
    cs, sn, first = cs_ref[...], sn_ref[...], _first_half_lanes()
    for j in range(d // LANES):
        r = _rope(y[:, j * LANES:(j + 1) * LANES], cs, sn, first)
        q_ref[:, j * LANES:(j + 1) * LANES] = (r * (LOG2E * HEAD_DIM ** -0.5)).astype(BF16)
    for j in range(kw // LANES):
        r = _rope(y[:, d + j * LANES:d + (j + 1) * LANES], cs, sn, first)
        k_ref[:, j * LANES:(j + 1) * LANES] = r.astype(BF16)
    vt = y[:, d + kw:].T
    for s in range(tm // SLAB):
        v_ref[s] = vt[:, s * SLAB:(s + 1) * SLAB].astype(BF16)


def _proj_wa_call(xs, g, mod12, cs, sn, w, l, kw):
    n, d = xs.shape
    row = lambda i: (i, 0)
    return pl.pallas_call(
        functools.partial(_proj_wa_kernel, tm=TM, l=l, d=d, kw=kw),
        grid=(n // TM,),
        in_specs=[pl.BlockSpec((TM, d), row), _const_spec((1, d)), _const_spec((2 * N_MOD, d)),
                  pl.BlockSpec((TM, LANES), row), pl.BlockSpec((TM, LANES), row),
                  _const_spec(w.shape)],
        out_specs=[pl.BlockSpec((TM, d), row), pl.BlockSpec((TM, kw), row),
                   pl.BlockSpec((TM // SLAB, kw, SLAB), lambda i: (i, 0, 0))],
        out_shape=[jax.ShapeDtypeStruct((n, d), BF16), jax.ShapeDtypeStruct((n, kw), BF16),
                   jax.ShapeDtypeStruct((n // SLAB, kw, SLAB), BF16)],
        compiler_params=_cparams(1),
        name="proj_window_attn",
    )(xs, g, mod12, cs, sn, w)


def _log1p_unit(x):
    return jnp.log(1.0 + x)


def _proj_hg_kernel(x_ref, g_ref, mod_ref, lla_ref, l1m_ref, w_ref,
                    q_ref, gate_ref, v_ref, vt_ref, og_ref, *, tm, l, d):
    is_ctx = _is_ctx_rows(pl.program_id(0), tm, l)
    h = _norm_mod(x_ref[...], g_ref[...], _pick(mod_ref, 0, is_ctx), _pick(mod_ref, 1, is_ctx))
    hb = h.astype(BF16)
    part = lambda j: jnp.dot(hb, w_ref[:, j * d:(j + 1) * d], preferred_element_type=F32)
    q_ref[...] = _silu(part(0)).astype(BF16)
    for dr in range(2):
        f = part(1 + dr)
        log_sig = jnp.minimum(f, 0.0) - _log1p_unit(jnp.exp(-jnp.abs(f)))
        a = lla_ref[dr:dr + 1, :]
        c = l1m_ref[dr:dr + 1, :] + log_sig
        gate_ref[dr] = jnp.maximum(a, c) + _log1p_unit(jnp.exp(-jnp.abs(a - c)))
    v = part(3)
    v_ref[...] = v.astype(BF16)
    vt = v.T
    for s in range(tm // SLAB):
        vt_ref[s] = vt[:, s * SLAB:(s + 1) * SLAB].astype(BF16)
    og_ref[...] = _silu(part(4)).astype(BF16)


def _proj_hg_call(xs, g, mod12, lla, l1m, w, l):
    n, d = xs.shape
    row = lambda i: (i, 0)
    return pl.pallas_call(
        functools.partial(_proj_hg_kernel, tm=TM, l=l, d=d),
        grid=(n // TM,),
        in_specs=[pl.BlockSpec((TM, d), row), _const_spec((1, d)), _const_spec((2 * N_MOD, d)),
                  _const_spec((2, d)), _const_spec((2, d)), _const_spec(w.shape)],
        out_specs=[pl.BlockSpec((TM, d), row), pl.BlockSpec((2, TM, d), lambda i: (0, i, 0)),
                   pl.BlockSpec((TM, d), row),
                   pl.BlockSpec((TM // SLAB, d, SLAB), lambda i: (i, 0, 0)),
                   pl.BlockSpec((TM, d), row)],
        out_shape=[jax.ShapeDtypeStruct((n, d), BF16), jax.ShapeDtypeStruct((2, n, d), F32),
                   jax.ShapeDtypeStruct((n, d), BF16), jax.ShapeDtypeStruct((n // SLAB, d, SLAB), BF16),
                   jax.ShapeDtypeStruct((n, d), BF16)],
        compiler_params=_cparams(1),
        name="proj_hgrn",
    )(xs, g, mod12, lla, l1m, w)


def _diff_attn_kernel(lam_ref, q_ref, qn_ref, k_ref, vt_ref, g_ref, o_ref, m_ref, acc_ref,
                      sa_ref, sb_ref, ma_ref, mb_ref, *, tq, tk, l, lc, unroll, post_scale):
    qi = pl.program_id(1)
    slot = qi % 2
    lane = lax.broadcasted_iota(jnp.int32, (1, LANES), 1)
    sub = 2 * vt_ref.shape[2]
    n_chunks = l // tk

    def stacked(ref):
        q = ref[...]
        zero = jnp.zeros_like(q)
        return jnp.concatenate([jnp.where(lane < HEAD_DIM, q, zero),
                                jnp.where(lane >= HEAD_DIM, q, zero)], axis=0)

    def scores(kc, qcat):
        return lax.dot_general(kc, qcat, NT_DIMS, preferred_element_type=F32)

    def pv(pair0, n_pairs, p):
        out = None
        for j in range(n_pairs):
            s0 = 2 * (pair0 + j)
            vt = jnp.concatenate([vt_ref[s0], vt_ref[s0 + 1]], axis=1)
            t = jnp.dot(vt, p[j * sub:(j + 1) * sub, :], preferred_element_type=F32)
            out = t if out is None else out + t
        return out

    def absorb_context(qcat, slot):
        s = scores(k_ref[l:l + lc, :], qcat)
        mx = jnp.max(s, axis=0, keepdims=True)
        m_ref[slot] = mx
        acc_ref[slot] = pv(l // sub, lc // sub, jnp.exp2(s - mx).astype(BF16))

    def qk(c, qcat, s_ref, mx_ref):
        off = pl.multiple_of(c * tk, LANES)
        s = scores(k_ref[pl.ds(off, tk), :], qcat)
        s_ref[...] = s
        mx_ref[...] = jnp.max(s, axis=0, keepdims=True)

    def absorb(c, s_ref, mx_ref):
        m_old = m_ref[slot]
        m_new = jnp.maximum(m_old, mx_ref[...])
        alpha = jnp.exp2(m_old - m_new)
        p = jnp.exp2(s_ref[...] - m_new).astype(BF16)
        m_ref[slot] = m_new
        acc_ref[slot] = alpha * acc_ref[slot] + pv((c * tk) // sub, tk // sub, p)

    def start_sweep(qcat, slot):
        qk(0, qcat, sa_ref, ma_ref)
        absorb_context(qcat, slot)

    @pl.when(qi * tq >= l)
    def _():
        absorb_context(stacked(q_ref), slot)

    @pl.when(qi == 0)
    def _():
        start_sweep(stacked(q_ref), slot)

    @pl.when(qi * tq < l)
    def _():
        qcat = stacked(q_ref)
        bufs = ((sa_ref, ma_ref), (sb_ref, mb_ref))

        def group(c0, count):
            for j in range(count):
                qk(c0 + j + 1, qcat, *bufs[(j + 1) % 2])
                absorb(c0 + j, *bufs[j % 2])

        def body(i, carry):
            group(unroll * i, unroll)
            return carry

        lax.fori_loop(0, n_chunks // unroll - 1, body, 0)
        group(n_chunks - unroll, unroll - 1)
        start_sweep(stacked(qn_ref), 1 - slot)
        absorb(n_chunks - 1, *bufs[(unroll - 1) % 2])

    acc = acc_ref[slot]
    acc = acc[0:LANES, :] * (1.0 / acc[LANES:LANES + 1, :])
    o = acc[:, :tq] - lam_ref[0] * acc[:, tq:]
    o = o * lax.rsqrt(jnp.mean(o * o, axis=0, keepdims=True) + RMS_EPS)
    o = o * (g_ref[...] * post_scale)
    o_ref[...] = o.T.astype(BF16)


def _diff_attn_call(lam, q, k, vt3, subln_g, l, lc, post_scale):
    n, d = q.shape
    nh = d // LANES
    nslab, vrows, slab = vt3.shape
    vrows //= nh
    tq = DA_TQ
    tk = min(DA_TK, l // 2)
    assert l % tq == 0 and n % tq == 0 and tk % (2 * slab) == 0
    assert l % (2 * slab) == 0 and lc % (2 * slab) == 0
    assert l % (2 * tk) == 0
    return pl.pallas_call(
        functools.partial(_diff_attn_kernel, tq=tq, tk=tk, l=l, lc=lc,
                          unroll=next(u for u in DA_UNROLLS if l % (u * tk) == 0),
                          post_scale=post_scale),
        grid=(nh, n // tq),
        in_specs=[pl.BlockSpec(memory_space=pltpu.SMEM),
                  pl.BlockSpec((tq, LANES), lambda h, i: (i, h)),
                  pl.BlockSpec((tq, LANES), lambda h, i: (jnp.minimum(i + 1, n // tq - 1), h)),
                  pl.BlockSpec((n, LANES), lambda h, i: (0, h)),
                  pl.BlockSpec((nslab, vrows, slab), lambda h, i: (0, h, 0)),
                  pl.BlockSpec((LANES, 1), lambda h, i: (0, 0))],
        out_specs=pl.BlockSpec((tq, LANES), lambda h, i: (i, h)),
        out_shape=jax.ShapeDtypeStruct((n, d), BF16),
        scratch_shapes=[pltpu.VMEM((2, 1, 2 * tq), F32),
                        pltpu.VMEM((2, vrows, 2 * tq), F32),
                        pltpu.VMEM((tk, 2 * tq), F32), pltpu.VMEM((tk, 2 * tq), F32),
                        pltpu.VMEM((1, 2 * tq), F32), pltpu.VMEM((1, 2 * tq), F32)],
        compiler_params=_cparams(2),
        name="diff_attention",
    )(lam, q, q, k, vt3, subln_g)


def _win_attn_kernel(sink_ref, q_ref, kc_ref, kp_ref, ko_ref, kn_ref,
                     vc_ref, vp_ref, vo_ref, vn_ref, o_ref, *, tq, l, lc, rep, ng):
    j = pl.program_id(0)
    w = WINDOW
    gw = rep * HEAD_DIM
    a = lax.broadcasted_iota(jnp.int32, (w, w), 0)
    b = lax.broadcasted_iota(jnp.int32, (w, w), 1)
    lane = lax.broadcasted_iota(jnp.int32, (1, LANES), 1)
    low = lane < HEAD_DIM
    biases = []
    for t in range(tq // w):
        q0 = j * tq + t * w
        latent = q0 < l
        own_bias = jnp.where(latent, 0.0, NEG_BIG)
        prev_bias = jnp.where(jnp.logical_and(latent, q0 >= w), 0.0, NEG_BIG)
        next_bias = jnp.where(q0 + w < l, 0.0, NEG_BIG)
        bias = jnp.concatenate(
            [jnp.where(a >= b, prev_bias, NEG_BIG),
             jnp.full((w, w), own_bias, F32),
             jnp.where(a <= b, next_bias, NEG_BIG)], axis=0)
        biases.append(jnp.concatenate([bias] * rep, axis=1))
    for g in range(ng):
        gl = slice(g * LANES, (g + 1) * LANES)
        kc = kc_ref[:, gl]
        kwin = jnp.concatenate([kp_ref[:, gl], ko_ref[:, gl], kn_ref[:, gl]], axis=0)
        vtc = jnp.concatenate([vc_ref[s, gl, :] for s in range(lc // w)], axis=1)
        vtwin = jnp.concatenate([vp_ref[0, gl, :]] + [vo_ref[s, gl, :] for s in range(tq // w)]
                                + [vn_ref[0, gl, :]], axis=1)
        for t in range(tq // w):
            qs, sinks = [], []
            for r in range(rep):
                c0 = g * gw + (r // 2) * LANES
                qc = q_ref[t * w:(t + 1) * w, c0:c0 + LANES]
                qs.append(jnp.where(low if r % 2 == 0 else ~low, qc, jnp.zeros_like(qc)))
                sinks.append(jnp.full((1, w), sink_ref[g * rep + r] * LOG2E, F32))
            qst = jnp.concatenate(qs, axis=0)
            sink = jnp.concatenate(sinks, axis=1)
            s_c = lax.dot_general(kc, qst, NT_DIMS, preferred_element_type=F32)
            s_w = (lax.dot_general(kwin[t * w:(t + 3) * w, :], qst, NT_DIMS,
                                   preferred_element_type=F32) + biases[t])
            mx = jnp.maximum(jnp.maximum(jnp.max(s_c, axis=0, keepdims=True),
                                         jnp.max(s_w, axis=0, keepdims=True)), sink)
            e_c = jnp.exp2(s_c - mx)
            e_w = jnp.exp2(s_w - mx)
            den = (jnp.exp2(sink - mx) + jnp.sum(e_c, axis=0, keepdims=True)
                   + jnp.sum(e_w, axis=0, keepdims=True))
            ot = (jnp.dot(vtc, e_c.astype(BF16), preferred_element_type=F32)
                  + jnp.dot(vtwin[:, t * w:(t + 3) * w], e_w.astype(BF16),
                            preferred_element_type=F32)) * (1.0 / den)
            heads = [ot[:, r * w:(r + 1) * w].T for r in range(rep)]
            for c in range(rep // 2):
                c0 = g * gw + c * LANES
                o_ref[t * w:(t + 1) * w, c0:c0 + LANES] = jnp.where(
                    low, heads[2 * c], heads[2 * c + 1]).astype(BF16)


def _win_attn_call(sinks, q, k2, vt2, l, lc):
    n, d = q.shape
    ng = WA_KV_HEADS
    rep = d // HEAD_DIM // ng
    kw = k2.shape[1]
    tq = WA_TQ
    assert n % tq == 0 and tq % WINDOW == 0 and lc % WINDOW == 0 and l % lc == 0 and rep % 2 == 0
    assert kw == ng * LANES
    r = tq // WINDOW
    last = n // WINDOW - 1
    prev = lambda j: jnp.maximum(j * r - 1, 0)
    nxt = lambda j: jnp.minimum((j + 1) * r, last)
    k_specs = [pl.BlockSpec((lc, kw), lambda j: (l // lc, 0)),
               pl.BlockSpec((WINDOW, kw), lambda j: (prev(j), 0)),
               pl.BlockSpec((tq, kw), lambda j: (j, 0)),
               pl.BlockSpec((WINDOW, kw), lambda j: (nxt(j), 0))]
    vt_specs = [pl.BlockSpec((lc // WINDOW, kw, WINDOW), lambda j: (l // lc, 0, 0)),
                pl.BlockSpec((1, kw, WINDOW), lambda j: (prev(j), 0, 0)),
                pl.BlockSpec((r, kw, WINDOW), lambda j: (j, 0, 0)),
                pl.BlockSpec((1, kw, WINDOW), lambda j: (nxt(j), 0, 0))]
    return pl.pallas_call(
        functools.partial(_win_attn_kernel, tq=tq, l=l, lc=lc, rep=rep, ng=ng),
        grid=(n // tq,),
        in_specs=[pl.BlockSpec(memory_space=pltpu.SMEM),
                  pl.BlockSpec((tq, d), lambda j: (j, 0))] + k_specs + vt_specs,
        out_specs=pl.BlockSpec((tq, d), lambda j: (j, 0)),
        out_shape=jax.ShapeDtypeStruct((n, d), BF16),
        compiler_params=_cparams(1),
        name="window_attention",
    )(sinks, q, k2, k2, k2, k2, vt2, vt2, vt2, vt2)


def _hgrn_consts(chunk):
    n_lv = int(math.log2(chunk))
    walls, masks_all = [], []
    for direction in range(2):
        tau = np.arange(chunk) if direction == 0 else chunk - 1 - np.arange(chunk)
        tr, tc = tau[:, None], tau[None, :]
        blocks = [tc <= tr]
        masks = []
        for lv in range(n_lv):
            h = chunk >> (lv + 1)
            mid = (tr // (2 * h)) * 2 * h + h - 1
            second_r = (tr % (2 * h)) >= h
            second_c = (tc % (2 * h)) >= h
            blocks.append(np.where(second_r, (tc > mid) & (tc <= tr), (tc > tr) & (tc <= mid)))
            masks.append(second_r & ~second_c & ((tr // (2 * h)) == (tc // (2 * h))))
        masks.append(tr == tc)
        w = np.concatenate(blocks, axis=0).astype(np.float32)
        walls.append(np.concatenate([w, w], axis=1))
        masks_all.append(np.stack(masks).astype(np.float32))
    return np.stack(walls), np.stack(masks_all)


def _hgrn_kernel(wall_ref, mask_ref, q_ref, g_ref, v_ref, vt_ref, o_ref, st_ref,
                 *, chunk, n_lv, n_heads, n_steps, per_step):
    dr = pl.program_id(0)
    step = pl.program_id(1)

    @pl.when(step == 0)
    def _():
        st_ref[...] = jnp.zeros_like(st_ref)

    @pl.when(step >= n_steps)
    def _():
        o_ref[...] = jnp.zeros_like(o_ref)

    @pl.when(step < n_steps)
    def _():
        heads = [slice(hh * LANES, (hh + 1) * LANES) for hh in range(n_heads)]
        nt = lambda x, y: lax.dot_general(x, y, NT_DIMS, preferred_element_type=F32)
        for j in range(per_step):
            idx = jnp.where(dr == 0, j, per_step - 1 - j)
            rows = pl.ds(pl.multiple_of(idx * chunk, chunk), chunk)
            g_all = g_ref[0, rows, :]
            g_hi = g_all.astype(BF16)
            g_lo = (g_all - g_hi.astype(F32)).astype(BF16)
            sums_all = jnp.dot(wall_ref[0], jnp.concatenate([g_hi, g_lo], axis=0),
                               preferred_element_type=F32)
            q16 = q_ref[rows, :]
            q_all = q16.astype(F32)
            k_all = 1.0 - jnp.exp(g_all)
            k16 = k_all.astype(BF16)
            v_all = v_ref[rows, :]
            vt_all = vt_ref[idx]
            b = sums_all[0:chunk]
            b_end = jnp.sum(g_all, axis=0, keepdims=True)

            qb = (q_all * jnp.exp(b)).astype(BF16)
            o = [nt(qb[:, sl], st_ref[hh].astype(BF16)) for hh, sl in enumerate(heads)]
            att = [nt(q16[:, sl], k16[:, sl]) * mask_ref[0, n_lv] for sl in heads]
            for lv in range(n_lv):
                e = jnp.exp(sums_all[(1 + lv) * chunk:(2 + lv) * chunk])
                qe = (q_all * e).astype(BF16)
                ke = (k_all * e).astype(BF16)
                mask = mask_ref[0, lv]
                att = [a + nt(qe[:, sl], ke[:, sl]) * mask for a, sl in zip(att, heads)]
            out = [o[hh] + jnp.dot(att[hh].astype(BF16), v_all[:, sl], preferred_element_type=F32)
                   for hh, sl in enumerate(heads)]
            o_ref[0, rows, :] = jnp.concatenate(out, axis=1)

            ks = (k_all * jnp.exp(b_end - b)).astype(BF16)
            decay = jnp.exp(b_end)
            for hh, sl in enumerate(heads):
                st_ref[hh] = st_ref[hh] * decay[:, sl] + jnp.dot(vt_all[sl, :], ks[:, sl],
                                                                 preferred_element_type=F32)


def _hgrn_call(qh, gates, v, vt3, l, lc):
    n, d = qh.shape
    c = HG_CHUNK
    per = HG_PER_STEP
    rows = per * c
    nh = d // LANES
    nx, ncc, ntot = l // rows, lc // rows, n // rows
    nc = nx + ncc
    assert n % rows == 0 and l % rows == 0 and lc % rows == 0 and vt3.shape[2] == c
    n_lv = int(math.log2(c))
    wall_np, mask_np = _hgrn_consts(c)
    wall = jnp.asarray(wall_np, dtype=BF16)
    masks = jnp.asarray(mask_np, dtype=F32)

    def blk(dr, s):
        fwd = jnp.where(s < ncc, nx + s, s - ncc)
        bwd = jnp.where(s < ncc, nx + ncc - 1 - s, nx - 1 - (s - ncc))
        return jnp.where(s >= nc, s, jnp.where(dr == 0, fwd, bwd))

    return pl.pallas_call(
        functools.partial(_hgrn_kernel, chunk=c, n_lv=n_lv, n_heads=nh, n_steps=nc, per_step=per),
        grid=(2, ntot),
        in_specs=[pl.BlockSpec((1,) + wall_np.shape[1:], lambda dr, s: (dr, 0, 0)),
                  pl.BlockSpec((1,) + mask_np.shape[1:], lambda dr, s: (dr, 0, 0, 0)),
                  pl.BlockSpec((rows, d), lambda dr, s: (blk(dr, s), 0)),
                  pl.BlockSpec((1, rows, d), lambda dr, s: (dr, blk(dr, s), 0)),
                  pl.BlockSpec((rows, d), lambda dr, s: (blk(dr, s), 0)),
                  pl.BlockSpec((per, d, c), lambda dr, s: (blk(dr, s), 0, 0))],
        out_specs=pl.BlockSpec((1, rows, d), lambda dr, s: (dr, blk(dr, s), 0)),
        out_shape=jax.ShapeDtypeStruct((2, n, d), F32),
        scratch_shapes=[pltpu.VMEM((nh, LANES, LANES), F32)],
        compiler_params=_cparams(2),
        name="hgrn_scan",
    )(wall, masks, qh, gates, v, vt3)


def _post_kernel(*refs, mode, final, tm, l, lc, d, f, n_x):
    it = iter(refs)
    x_refs = [next(it) for _ in range(n_x)]
    mod_ref, g2_ref, wo_ref, win_ref, wout_ref = (next(it) for _ in range(5))
    if mode == "plain":
        a_ref = next(it)
    elif mode == "conv":
        b_ref, z_ref, zp_ref, zn_ref, cw_ref = (next(it) for _ in range(5))
    else:
        ofw_ref, obw_ref, og_ref, gn_ref = (next(it) for _ in range(4))
    fin_ref = next(it) if final else None
    o_ref = next(it)

    ti = pl.program_id(0)
    is_ctx = _is_ctx_rows(ti, tm, l)
    if mode == "plain":
        a = a_ref[...]
    elif mode == "conv":
        z = z_ref[...]
        rl = lax.broadcasted_iota(jnp.int32, (tm, 1), 0)
        row = ti * tm + rl
        z_prev = jnp.where(rl == 0, zp_ref[7:8, :], pltpu.roll(z, 1, 0))
        z_next = jnp.where(rl == tm - 1, zn_ref[0:1, :], pltpu.roll(z, tm - 1, 0))
        z_prev = jnp.where((row == 0) | (row == l), 0.0, z_prev)
        z_next = jnp.where((row == l - 1) | (row == l + lc - 1), 0.0, z_next)
        conv = cw_ref[0:1, :] * z_prev + cw_ref[1:2, :] * z + cw_ref[2:3, :] * z_next
        a = (b_ref[...].astype(F32) * conv).astype(BF16)
    else:
        gn = gn_ref[...]
        heads = []
        for hh in range(d // LANES):
            sl = slice(hh * LANES, (hh + 1) * LANES)
            oh = _rms(ofw_ref[0, :, sl] + obw_ref[0, :, sl])
            heads.append((oh * gn * og_ref[:, sl].astype(F32)).astype(BF16))
        a = jnp.concatenate(heads, axis=1)

    y = jnp.dot(a, wo_ref[...], preferred_element_type=F32)
    x = _stream_tile(x_refs, ti, tm, l) + _pick(mod_ref, 2, is_ctx) * y
    h = _norm_mod(x, g2_ref[...], _pick(mod_ref, 3, is_ctx), _pick(mod_ref, 4, is_ctx)).astype(BF16)
    acc = jnp.zeros((tm, d), F32)
    for c0 in range(0, f, FFN_CHUNK):
        u = jnp.dot(h, win_ref[:, c0:c0 + FFN_CHUNK], preferred_element_type=F32)
        w = jnp.dot(h, win_ref[:, f + c0:f + c0 + FFN_CHUNK], preferred_element_type=F32)
        act = (_silu(u) * w).astype(BF16)
        acc = acc + jnp.dot(act, wout_ref[c0:c0 + FFN_CHUNK, :], preferred_element_type=F32)
    x = x + _pick(mod_ref, 5, is_ctx) * acc
    if final:
        x = _rms(x) * fin_ref[...]
    o_ref[...] = x


def _post_call(xs, mod12, g2, wo, win, wout, mixer_inputs, mode, l, lc, final_g=None):
    x_specs, x_args, n, d = _stream_specs(xs, l)
    f = wout.shape[0]
    final = final_g is not None
    assert f % FFN_CHUNK == 0 and l % TM == 0 and n % TM == 0
    n_out = l if final else n
    row = lambda i: (i, 0)
    in_specs = x_specs + [_const_spec((2 * N_MOD, d)), _const_spec((1, d)),
                          _const_spec(wo.shape), _const_spec(win.shape), _const_spec(wout.shape)]
    args = x_args + [mod12, g2, wo, win, wout]
    if mode == "plain":
        in_specs += [pl.BlockSpec((TM, d), row)]
    elif mode == "conv":
        r8 = TM // 8
        last8 = n // 8 - 1
        in_specs += [pl.BlockSpec((TM, d), row), pl.BlockSpec((TM, d), row),
                     pl.BlockSpec((8, d), lambda i: (jnp.maximum(i * r8 - 1, 0), 0)),
                     pl.BlockSpec((8, d), lambda i: (jnp.minimum((i + 1) * r8, last8), 0)),
                     _const_spec((SC_WIDTH, d))]
    else:
        in_specs += [pl.BlockSpec((1, TM, d), lambda i: (0, i, 0)),
                     pl.BlockSpec((1, TM, d), lambda i: (1, i, 0)),
                     pl.BlockSpec((TM, d), row), _const_spec((1, LANES))]
    args += list(mixer_inputs)
    if final:
        in_specs += [_const_spec((1, d))]
        args += [final_g]
    return pl.pallas_call(
        functools.partial(_post_kernel, mode=mode, final=final, tm=TM, l=l, lc=lc, d=d, f=f,
                          n_x=len(x_args)),
        grid=(n_out // TM,),
        in_specs=in_specs,
        out_specs=pl.BlockSpec((TM, d), row),
        out_shape=jax.ShapeDtypeStruct((n_out, d), F32),
        compiler_params=_cparams(1),
        name="post_" + mode + ("_final" if final else ""),
    )(*args)


def _rope_tables(l, n):
    rows = l // GRID_W
    r = jnp.repeat(jnp.arange(rows), GRID_W).astype(F32)
    c = jnp.tile(jnp.arange(GRID_W), rows).astype(F32)
    half = HEAD_DIM // 2
    inv_freq = 1.0 / (ROPE_BASE ** (jnp.arange(0, half, 2, dtype=F32) / half))
    ar, ac = r[:, None] * inv_freq, c[:, None] * inv_freq
    cs = jnp.concatenate([jnp.cos(ar), jnp.cos(ar), jnp.cos(ac), jnp.cos(ac)], axis=-1)
    sn = jnp.concatenate([-jnp.sin(ar), jnp.sin(ar), -jnp.sin(ac), jnp.sin(ac)], axis=-1)
    cs = jnp.concatenate([cs, jnp.ones((n - l, HEAD_DIM), F32)], axis=0)
    sn = jnp.concatenate([sn, jnp.zeros((n - l, HEAD_DIM), F32)], axis=0)
    return jnp.tile(cs, (1, 2)), jnp.tile(sn, (1, 2))


def kernel(x, c, ctx, c_ctx, ada_w, ada_b, norm1_g, norm2_g, ffn_w_in, ffn_w_out, final_g,
           da_wqkv, da_lambda, da_subln_g, da_wo, sc_w_in, sc_conv_w, sc_w_out,
           wa_wqkv, wa_sinks, wa_wo, hg_w_in, hg_lb, hg_gnorm_g, hg_wo):
    b, l, d = x.shape
    lc = ctx.shape[1]
    depth = ada_w.shape[0]
    assert b == 1 and d % LANES == 0 and l % TM == 0
    n = -(-(l + lc) // TM) * TM
    xs = (x[0], jnp.concatenate([ctx[0], jnp.zeros((n - l - lc, d), x.dtype)], axis=0))

    cc = jnp.zeros((8, d), F32).at[0].set(c_ctx).at[1].set(c[0])
    mods = _mod_call(cc, ada_w, ada_b)
    cs, sn = _rope_tables(l, n)

    n_mix = 4
    for i in range(depth):
        mixer, slot = i % n_mix, i // n_mix
        last = i == depth - 1
        mod12 = jnp.concatenate([mods[i, 0].reshape(N_MOD, d), mods[i, 1].reshape(N_MOD, d)], axis=0)
        g1 = norm1_g[i].reshape(1, d)
        g2 = norm2_g[i].reshape(1, d)
        win = ffn_w_in[i].astype(BF16)
        wout = ffn_w_out[i].astype(BF16)
        fin = final_g.reshape(1, d) if last else None
        if isinstance(xs, tuple) and mixer != 0:
            xs = jnp.concatenate(xs, axis=0)

        if mixer == 0:
            w = da_wqkv[slot]
            lam = da_lambda[slot].astype(F32)
            lam_init = 0.8 - 0.6 * math.exp(-0.3 * i)
            lam_full = (jnp.exp(jnp.sum(lam[0] * lam[1])) - jnp.exp(jnp.sum(lam[2] * lam[3]))
                        + lam_init).reshape(1)
            q, k, vt3 = _proj_da_call(xs, g1, mod12, cs, sn, w[:, :2 * d].astype(BF16),
                                      w[:, 2 * d:].T.astype(BF16), l)
            a = _diff_attn_call(lam_full, q, k, vt3, da_subln_g[slot].reshape(LANES, 1), l, lc,
                                1.0 - lam_init)
            xs = _post_call(xs, mod12, g2, da_wo[slot].astype(BF16), win, wout, [a], "plain",
                            l, lc, fin)
        elif mixer == 1:
            bg, z = _proj_sc_call(xs, g1, mod12, sc_w_in[slot].astype(BF16), l)
            xs = _post_call(xs, mod12, g2, sc_w_out[slot].astype(BF16), win, wout,
                            [bg, z, z, z, sc_conv_w[slot]], "conv", l, lc, fin)
        elif mixer == 2:
            w = wa_wqkv[slot]
            kw = WA_KV_HEADS * HEAD_DIM
            dup = lambda m: jnp.tile(m.reshape(d, WA_KV_HEADS, 1, HEAD_DIM), (1, 1, 2, 1)).reshape(d, 2 * kw)
            w2 = jnp.concatenate([w[:, :d], dup(w[:, d:d + kw]), dup(w[:, d + kw:])], axis=1)
            q, k2, vt2 = _proj_wa_call(xs, g1, mod12, cs, sn, w2.astype(BF16), l, 2 * kw)
            a = _win_attn_call(wa_sinks[slot].astype(F32), q, k2, vt2, l, lc)
            xs = _post_call(xs, mod12, g2, wa_wo[slot].astype(BF16), win, wout, [a], "plain",
                            l, lc, fin)
        else:
            w = hg_w_in[slot]
            p = jax.nn.softmax(hg_lb.astype(F32), axis=1)
            lb = (jnp.cumsum(p, axis=1) - p[:, :1])[:, i]
            qh, gates, v, vt3, og = _proj_hg_call(xs, g1, mod12, jnp.log(lb), jnp.log1p(-lb),
                                                  w.astype(BF16), l)
            o2 = _hgrn_call(qh, gates, v, vt3, l, lc)
            xs = _post_call(xs, mod12, g2, hg_wo[slot].astype(BF16), win, wout,
                            [o2, o2, og, hg_gnorm_g[slot].reshape(1, LANES)], "hgrn", l, lc, fin)
    return xs[:l].reshape(1, l, d)
```

```python
import functools
import math

import numpy as np
import jax
import jax.numpy as jnp
from jax import lax
from jax.experimental import pallas as pl
from jax.experimental.pallas import tpu as pltpu

F32 = jnp.float32
BF16 = jnp.bfloat16

HEAD_DIM = 64
GRID_W = 64
ROPE_BASE = 10000.0
RMS_EPS = 1e-6
N_MOD = 6
WINDOW = 128
WA_KV_HEADS = 4
SC_WIDTH = 3
LANES = 128
NEG_BIG = -1e30
LOG2E = math.log2(math.e)

TM = 512
SLAB = 128
HG_CHUNK = 128
HG_PER_STEP = 2
DA_TQ = 256
DA_TILES_PER_STEP = 2
DA_TK = 1024
DA_ONES_ROWS = 16
DA_UNROLLS = (4, 2)
WA_TQ = 512
FFN_CHUNK = 256
VMEM_LIMIT = 56 * 1024 * 1024

NT_DIMS = (((1,), (1,)), ((), ()))


def _cparams(n_axes):
    return pltpu.CompilerParams(
        dimension_semantics=("arbitrary",) * n_axes, vmem_limit_bytes=VMEM_LIMIT)


def _const_spec(shape):
    nd = len(shape)
    return pl.BlockSpec(shape, lambda *_: (0,) * nd, pipeline_mode=pl.Buffered(1))


def _is_ctx_rows(tile_idx, tm, l):
    row = tile_idx * tm + lax.broadcasted_iota(jnp.int32, (tm, 1), 0)
    return row >= l


def _pick(mod_ref, k, is_ctx):
    return jnp.where(is_ctx, mod_ref[k:k + 1, :], mod_ref[N_MOD + k:N_MOD + k + 1, :])


def _rms(x):
    return x * lax.rsqrt(jnp.mean(x * x, axis=-1, keepdims=True) + RMS_EPS)


def _norm_mod(x, g, shift, scale):
    return (_rms(x) * g) * (1.0 + scale) + shift


def _silu(x):
    h = 0.5 * x
    return h + h * jnp.tanh(h)


def _rope(chunk, cs, sn, first_half):
    partner = jnp.where(first_half, pltpu.roll(chunk, LANES - 16, 1), pltpu.roll(chunk, 16, 1))
    return chunk * cs + partner * sn


def _stream_specs(xs, l):
    if not isinstance(xs, tuple):
        n, d = xs.shape
        return [pl.BlockSpec((TM, d), lambda i: (i, 0))], [xs], n, d
    lat, tail = xs
    d = lat.shape[1]
    nl = l // TM
    assert lat.shape[0] == l and tail.shape[0] % TM == 0
    return ([pl.BlockSpec((TM, d), lambda i: (jnp.minimum(i, nl - 1), 0)),
             pl.BlockSpec((TM, d), lambda i: (jnp.maximum(i - nl, 0), 0))],
            [lat, tail], l + tail.shape[0], d)


def _stream_tile(x_refs, tile_idx, tm, l):
    if len(x_refs) == 1:
        return x_refs[0][...]
    return jnp.where(tile_idx * tm < l, x_refs[0][...], x_refs[1][...])


def _first_half_lanes():
    lane = lax.broadcasted_iota(jnp.int32, (1, LANES), 1)
    return (lane & 31) < 16


def _mod_kernel(cc_ref, w_ref, b_ref, o_ref):
    a = _silu(cc_ref[...])
    o_ref[0] = jnp.dot(a, w_ref[0], precision=lax.Precision.HIGHEST,
                       preferred_element_type=F32) + b_ref[0]


def _mod_call(cc, ada_w, ada_b):
    depth, d, nd = ada_w.shape
    tn = 1536
    assert nd % tn == 0
    return pl.pallas_call(
        _mod_kernel,
        grid=(depth, nd // tn),
        in_specs=[pl.BlockSpec((8, d), lambda i, j: (0, 0)),
                  pl.BlockSpec((1, d, tn), lambda i, j: (i, 0, j)),
                  pl.BlockSpec((1, 1, tn), lambda i, j: (i, 0, j))],
        out_specs=pl.BlockSpec((1, 8, tn), lambda i, j: (i, 0, j)),
        out_shape=jax.ShapeDtypeStruct((depth, 8, nd), F32),
        compiler_params=_cparams(2),
        name="adaln_mod",
    )(cc, ada_w, ada_b.reshape(depth, 1, nd))


def _proj_da_kernel(*refs, tm, l, d, n_x):
    x_refs = refs[:n_x]
    g_ref, mod_ref, cs_ref, sn_ref, wqk_ref, wvt_ref, q_ref, k_ref, vt_ref = refs[n_x:]
    is_ctx = _is_ctx_rows(pl.program_id(0), tm, l)
    x = _stream_tile(x_refs, pl.program_id(0), tm, l)
    h = _norm_mod(x, g_ref[...], _pick(mod_ref, 0, is_ctx), _pick(mod_ref, 1, is_ctx))
    hb = h.astype(BF16)
    qk = jnp.dot(hb, wqk_ref[...], preferred_element_type=F32)
    cs, sn, first = cs_ref[...], sn_ref[...], _first_half_lanes()
    nq = d // LANES
    for j in range(2 * nq):
        r = _rope(qk[:, j * LANES:(j + 1) * LANES], cs, sn, first)
        if j < nq:
            q_ref[:, j * LANES:(j + 1) * LANES] = (r * (LOG2E * HEAD_DIM ** -0.5)).astype(BF16)
        else:
            k_ref[:, (j - nq) * LANES:(j - nq + 1) * LANES] = r.astype(BF16)
    vt = lax.dot_general(wvt_ref[...], hb, NT_DIMS, preferred_element_type=F32)
    ones = jnp.ones((DA_ONES_ROWS, SLAB), BF16)
    vr = LANES + DA_ONES_ROWS
    for s in range(tm // SLAB):
        for hh in range(d // LANES):
            vt_ref[s, hh * vr:hh * vr + LANES, :] = (
                vt[hh * LANES:(hh + 1) * LANES, s * SLAB:(s + 1) * SLAB].astype(BF16))
            vt_ref[s, hh * vr + LANES:(hh + 1) * vr, :] = ones


def _proj_da_call(xs, g, mod12, cs, sn, wqk, wvt, l):
    x_specs, x_args, n, d = _stream_specs(xs, l)
    row = lambda i: (i, 0)
    dv = d // LANES * (LANES + DA_ONES_ROWS)
    return pl.pallas_call(
        functools.partial(_proj_da_kernel, tm=TM, l=l, d=d, n_x=len(x_args)),
        grid=(n // TM,),
        in_specs=x_specs + [_const_spec((1, d)), _const_spec((2 * N_MOD, d)),
                            pl.BlockSpec((TM, LANES), row), pl.BlockSpec((TM, LANES), row),
                            _const_spec(wqk.shape), _const_spec(wvt.shape)],
        out_specs=[pl.BlockSpec((TM, d), row), pl.BlockSpec((TM, d), row),
                   pl.BlockSpec((TM // SLAB, dv, SLAB), lambda i: (i, 0, 0))],
        out_shape=[jax.ShapeDtypeStruct((n, d), BF16), jax.ShapeDtypeStruct((n, d), BF16),
                   jax.ShapeDtypeStruct((n // SLAB, dv, SLAB), BF16)],
        compiler_params=_cparams(1),
        name="proj_diff_attn",
    )(*x_args, g, mod12, cs, sn, wqk, wvt)


def _proj_sc_kernel(x_ref, g_ref, mod_ref, w_ref, b_ref, z_ref, *, tm, l, d):
    is_ctx = _is_ctx_rows(pl.program_id(0), tm, l)
    h = _norm_mod(x_ref[...], g_ref[...], _pick(mod_ref, 0, is_ctx), _pick(mod_ref, 1, is_ctx))
    y = jnp.dot(h.astype(BF16), w_ref[...], preferred_element_type=F32)
    b_ref[...] = y[:, :d].astype(BF16)
    z_ref[...] = y[:, d:2 * d] * y[:, 2 * d:]


def _proj_sc_call(xs, g, mod12, w, l):
    n, d = xs.shape
    row = lambda i: (i, 0)
    return pl.pallas_call(
        functools.partial(_proj_sc_kernel, tm=TM, l=l, d=d),
        grid=(n // TM,),
        in_specs=[pl.BlockSpec((TM, d), row), _const_spec((1, d)), _const_spec((2 * N_MOD, d)),
                  _const_spec(w.shape)],
        out_specs=[pl.BlockSpec((TM, d), row), pl.BlockSpec((TM, d), row)],
        out_shape=[jax.ShapeDtypeStruct((n, d), BF16), jax.ShapeDtypeStruct((n, d), F32)],
        compiler_params=_cparams(1),
        name="proj_short_conv",
    )(xs, g, mod12, w)


def _proj_wa_kernel(x_ref, g_ref, mod_ref, cs_ref, sn_ref, w_ref, q_ref, k_ref, v_ref,
                    *, tm, l, d, kw):
    is_ctx = _is_ctx_rows(pl.program_id(0), tm, l)
    h = _norm_mod(x_ref[...], g_ref[...], _pick(mod_ref, 0, is_ctx), _pick(mod_ref, 1, is_ctx))
    y = jnp.dot(h.astype(BF16), w_ref[...], preferred_element_type=F32)
    cs, sn, first = cs_ref[...], sn_ref[...], _first_half_lanes()
    for j in range(d // LANES):
        r = _rope(y[:, j * LANES:(j + 1) * LANES], cs, sn, first)
        q_ref[:, j * LANES:(j + 1) * LANES] = (r * (LOG2E * HEAD_DIM ** -0.5)).astype(BF16)
    for j in range(kw // LANES):
        r = _rope(y[:, d + j * LANES:d + (j + 1) * LANES], cs, sn, first)
        k_ref[:, j * LANES:(j + 1) * LANES] = r.astype(BF16)
    vt = y[:, d + kw:].T
    for s in range(tm // SLAB):
        v_ref[s] = vt[:, s * SLAB:(s + 1) * SLAB].astype(BF16)


def _proj_wa_call(xs, g, mod12, cs, sn, w, l, kw):
    n, d = xs.shape
    row = lambda i: (i, 0)
    return pl.pallas_call(
        functools.partial(_proj_wa_kernel, tm=TM, l=l, d=d, kw=kw),
        grid=(n // TM,),
        in_specs=[pl.BlockSpec((TM, d), row), _const_spec((1, d)), _const_spec((2 * N_MOD, d)),
                  pl.BlockSpec((TM, LANES), row), pl.BlockSpec((TM, LANES), row),
                  _const_spec(w.shape)],
        out_specs=[pl.BlockSpec((TM, d), row), pl.BlockSpec((TM, kw), row),
                   pl.BlockSpec((TM // SLAB, kw, SLAB), lambda i: (i, 0, 0))],
        out_shape=[jax.ShapeDtypeStruct((n, d), BF16), jax.ShapeDtypeStruct((n, kw), BF16),
                   jax.ShapeDtypeStruct((n // SLAB, kw, SLAB), BF16)],
        compiler_params=_cparams(1),
        name="proj_window_attn",
    )(xs, g, mod12, cs, sn, w)


def _log1p_unit(x):
    return jnp.log(1.0 + x)


def _proj_hg_kernel(x_ref, g_ref, mod_ref, lla_ref, l1m_ref, w_ref,
                    q_ref, gate_ref, v_ref, vt_ref, og_ref, *, tm, l, d):
    is_ctx = _is_ctx_rows(pl.program_id(0), tm, l)
    h = _norm_mod(x_ref[...], g_ref[...], _pick(mod_ref, 0, is_ctx), _pick(mod_ref, 1, is_ctx))
    hb = h.astype(BF16)
    part = lambda j: jnp.dot(hb, w_ref[:, j * d:(j + 1) * d], preferred_element_type=F32)
    q_ref[...] = _silu(part(0)).astype(BF16)
    for dr in range(2):
        f = part(1 + dr)
        log_sig = jnp.minimum(f, 0.0) - _log1p_unit(jnp.exp(-jnp.abs(f)))
        a = lla_ref[dr:dr + 1, :]
        c = l1m_ref[dr:dr + 1, :] + log_sig
        gate_ref[dr] = jnp.maximum(a, c) + _log1p_unit(jnp.exp(-jnp.abs(a - c)))
    v = part(3)
    v_ref[...] = v.astype(BF16)
    vt = v.T
    for s in range(tm // SLAB):
        vt_ref[s] = vt[:, s * SLAB:(s + 1) * SLAB].astype(BF16)
    og_ref[...] = _silu(part(4)).astype(BF16)


def _proj_hg_call(xs, g, mod12, lla, l1m, w, l):
    n, d = xs.shape
    row = lambda i: (i, 0)
    return pl.pallas_call(
        functools.partial(_proj_hg_kernel, tm=TM, l=l, d=d),
        grid=(n // TM,),
        in_specs=[pl.BlockSpec((TM, d), row), _const_spec((1, d)), _const_spec((2 * N_MOD, d)),
                  _const_spec((2, d)), _const_spec((2, d)), _const_spec(w.shape)],
        out_specs=[pl.BlockSpec((TM, d), row), pl.BlockSpec((2, TM, d), lambda i: (0, i, 0)),
                   pl.BlockSpec((TM, d), row),
                   pl.BlockSpec((TM // SLAB, d, SLAB), lambda i: (i, 0, 0)),
                   pl.BlockSpec((TM, d), row)],
        out_shape=[jax.ShapeDtypeStruct((n, d), BF16), jax.ShapeDtypeStruct((2, n, d), F32),
                   jax.ShapeDtypeStruct((n, d), BF16), jax.ShapeDtypeStruct((n // SLAB, d, SLAB), BF16),
                   jax.ShapeDtypeStruct((n, d), BF16)],
        compiler_params=_cparams(1),
        name="proj_hgrn",
    )(xs, g, mod12, lla, l1m, w)


def _diff_attn_kernel(lam_ref, q_ref, qn_ref, k_ref, vt_ref, g_ref, o_ref, m_ref, acc_ref,
                      sa_ref, sb_ref, ma_ref, mb_ref, *, tq, tiles, tk, l, lc, unroll, post_scale):
    qi = pl.program_id(1)
    lane = lax.broadcasted_iota(jnp.int32, (1, LANES), 1)
    sub = 2 * vt_ref.shape[2]
    n_chunks = l // tk
    step_rows = tiles * tq

    def stacked(ref, t):
        q = ref[t * tq:(t + 1) * tq, :]
        zero = jnp.zeros_like(q)
        return jnp.concatenate([jnp.where(lane < HEAD_DIM, q, zero),
                                jnp.where(lane >= HEAD_DIM, q, zero)], axis=0)

    def scores(kc, qcat):
        return lax.dot_general(kc, qcat, NT_DIMS, preferred_element_type=F32)

    def pv(pair0, n_pairs, p):
        out = None
        for j in range(n_pairs):
            s0 = 2 * (pair0 + j)
            vt = jnp.concatenate([vt_ref[s0], vt_ref[s0 + 1]], axis=1)
            t = jnp.dot(vt, p[j * sub:(j + 1) * sub, :], preferred_element_type=F32)
            out = t if out is None else out + t
        return out

    def absorb_context(qcat, slot):
        s = scores(k_ref[l:l + lc, :], qcat)
        mx = jnp.max(s, axis=0, keepdims=True)
        m_ref[slot] = mx
        acc_ref[slot] = pv(l // sub, lc // sub, jnp.exp2(s - mx).astype(BF16))

    def qk(c, qcat, s_ref, mx_ref):
        off = pl.multiple_of(c * tk, LANES)
        s = scores(k_ref[pl.ds(off, tk), :], qcat)
        s_ref[...] = s
        mx_ref[...] = jnp.max(s, axis=0, keepdims=True)

    def absorb(c, s_ref, mx_ref, slot):
        m_old = m_ref[slot]
        m_new = jnp.maximum(m_old, mx_ref[...])
        alpha = jnp.exp2(m_old - m_new)
        p = jnp.exp2(s_ref[...] - m_new).astype(BF16)
        m_ref[slot] = m_new
        acc_ref[slot] = alpha * acc_ref[slot] + pv((c * tk) // sub, tk // sub, p)

    def start_sweep(qcat, slot):
        qk(0, qcat, sa_ref, ma_ref)
        absorb_context(qcat, slot)

    def finish(t):
        acc = acc_ref[t]
        acc = acc[0:LANES, :] * (1.0 / acc[LANES:LANES + 1, :])
        o = acc[:, :tq] - lam_ref[0] * acc[:, tq:]
        o = o * lax.rsqrt(jnp.mean(o * o, axis=0, keepdims=True) + RMS_EPS)
        o = o * (g_ref[...] * post_scale)
        o_ref[t * tq:(t + 1) * tq, :] = o.T.astype(BF16)

    bufs = ((sa_ref, ma_ref), (sb_ref, mb_ref))

    def sweep(t):
        qcat = stacked(q_ref, t)

        def group(c0, count):
            for j in range(count):
                qk(c0 + j + 1, qcat, *bufs[(j + 1) % 2])
                absorb(c0 + j, *bufs[j % 2], t)

        def body(i, carry):
            group(unroll * i, unroll)
            return carry

        lax.fori_loop(0, n_chunks // unroll - 1, body, 0)
        group(n_chunks - unroll, unroll - 1)
        if t + 1 < tiles:
            start_sweep(stacked(q_ref, t + 1), t + 1)
        else:
            start_sweep(stacked(qn_ref, 0), 0)
        absorb(n_chunks - 1, *bufs[(unroll - 1) % 2], t)
        finish(t)

    @pl.when(qi * step_rows >= l)
    def _():
        for t in range(tiles):
            absorb_context(stacked(q_ref, t), t)
            finish(t)

    @pl.when(qi == 0)
    def _():
        start_sweep(stacked(q_ref, 0), 0)

    @pl.when(qi * step_rows < l)
    def _():
        for t in range(tiles):
            sweep(t)


def _diff_attn_call(lam, q, k, vt3, subln_g, l, lc, post_scale):
    n, d = q.shape
    nh = d // LANES
    nslab, vrows, slab = vt3.shape
    vrows //= nh
    tq = DA_TQ
    tiles = DA_TILES_PER_STEP
    rows = tiles * tq
    tk = min(DA_TK, l // 2)
    assert l % rows == 0 and n % rows == 0 and tk % (2 * slab) == 0
    assert l % (2 * slab) == 0 and lc % (2 * slab) == 0
    assert l % (2 * tk) == 0
    return pl.pallas_call(
        functools.partial(_diff_attn_kernel, tq=tq, tiles=tiles, tk=tk, l=l, lc=lc,
                          unroll=next(u for u in DA_UNROLLS if l % (u * tk) == 0),
                          post_scale=post_scale),
        grid=(nh, n // rows),
        in_specs=[pl.BlockSpec(memory_space=pltpu.SMEM),
                  pl.BlockSpec((rows, LANES), lambda h, i: (i, h)),
                  pl.BlockSpec((rows, LANES), lambda h, i: (jnp.minimum(i + 1, n // rows - 1), h)),
                  pl.BlockSpec((n, LANES), lambda h, i: (0, h)),
                  pl.BlockSpec((nslab, vrows, slab), lambda h, i: (0, h, 0)),
                  pl.BlockSpec((LANES, 1), lambda h, i: (0, 0))],
        out_specs=pl.BlockSpec((rows, LANES), lambda h, i: (i, h)),
        out_shape=jax.ShapeDtypeStruct((n, d), BF16),
        scratch_shapes=[pltpu.VMEM((tiles, 1, 2 * tq), F32),
                        pltpu.VMEM((tiles, vrows, 2 * tq), F32),
                        pltpu.VMEM((tk, 2 * tq), F32), pltpu.VMEM((tk, 2 * tq), F32),
                        pltpu.VMEM((1, 2 * tq), F32), pltpu.VMEM((1, 2 * tq), F32)],
        compiler_params=_cparams(2),
        name="diff_attention",
    )(lam, q, q, k, vt3, subln_g)


def _win_attn_kernel(sink_ref, q_ref, kc_ref, kp_ref, ko_ref, kn_ref,
                     vc_ref, vp_ref, vo_ref, vn_ref, o_ref, *, tq, l, lc, rep, ng):
    j = pl.program_id(0)
    w = WINDOW
    gw = rep * HEAD_DIM
    a = lax.broadcasted_iota(jnp.int32, (w, w), 0)
    b = lax.broadcasted_iota(jnp.int32, (w, w), 1)
    lane = lax.broadcasted_iota(jnp.int32, (1, LANES), 1)
    low = lane < HEAD_DIM
    biases = []
    for t in range(tq // w):
        q0 = j * tq + t * w
        latent = q0 < l
        own_bias = jnp.where(latent, 0.0, NEG_BIG)
        prev_bias = jnp.where(jnp.logical_and(latent, q0 >= w), 0.0, NEG_BIG)
        next_bias = jnp.where(q0 + w < l, 0.0, NEG_BIG)
        bias = jnp.concatenate(
            [jnp.where(a >= b, prev_bias, NEG_BIG),
             jnp.full((w, w), own_bias, F32),
             jnp.where(a <= b, next_bias, NEG_BIG)], axis=0)
        biases.append(jnp.concatenate([bias] * rep, axis=1))
    for g in range(ng):
        gl = slice(g * LANES, (g + 1) * LANES)
        kc = kc_ref[:, gl]
        kwin = jnp.concatenate([kp_ref[:, gl], ko_ref[:, gl], kn_ref[:, gl]], axis=0)
        vtc = jnp.concatenate([vc_ref[s, gl, :] for s in range(lc // w)], axis=1)
        vtwin = jnp.concatenate([vp_ref[0, gl, :]] + [vo_ref[s, gl, :] for s in range(tq // w)]
                                + [vn_ref[0, gl, :]], axis=1)
        for t in range(tq // w):
            qs, sinks = [], []
            for r in range(rep):
                c0 = g * gw + (r // 2) * LANES
                qc = q_ref[t * w:(t + 1) * w, c0:c0 + LANES]
                qs.append(jnp.where(low if r % 2 == 0 else ~low, qc, jnp.zeros_like(qc)))
                sinks.append(jnp.full((1, w), sink_ref[g * rep + r] * LOG2E, F32))
            qst = jnp.concatenate(qs, axis=0)
            sink = jnp.concatenate(sinks, axis=1)
            s_c = lax.dot_general(kc, qst, NT_DIMS, preferred_element_type=F32)
            s_w = (lax.dot_general(kwin[t * w:(t + 3) * w, :], qst, NT_DIMS,
                                   preferred_element_type=F32) + biases[t])
            mx = jnp.maximum(jnp.maximum(jnp.max(s_c, axis=0, keepdims=True),
                                         jnp.max(s_w, axis=0, keepdims=True)), sink)
            e_c = jnp.exp2(s_c - mx)
            e_w = jnp.exp2(s_w - mx)
            den = (jnp.exp2(sink - mx) + jnp.sum(e_c, axis=0, keepdims=True)
                   + jnp.sum(e_w, axis=0, keepdims=True))
            ot = (jnp.dot(vtc, e_c.astype(BF16), preferred_element_type=F32)
                  + jnp.dot(vtwin[:, t * w:(t + 3) * w], e_w.astype(BF16),
                            preferred_element_type=F32)) * (1.0 / den)
            heads = [ot[:, r * w:(r + 1) * w].T for r in range(rep)]
            for c in range(rep // 2):
                c0 = g * gw + c * LANES
                o_ref[t * w:(t + 1) * w, c0:c0 + LANES] = jnp.where(
                    low, heads[2 * c], heads[2 * c + 1]).astype(BF16)


def _win_attn_call(sinks, q, k2, vt2, l, lc):
    n, d = q.shape
    ng = WA_KV_HEADS
    rep = d // HEAD_DIM // ng
    kw = k2.shape[1]
    tq = WA_TQ
    assert n % tq == 0 and tq % WINDOW == 0 and lc % WINDOW == 0 and l % lc == 0 and rep % 2 == 0
    assert kw == ng * LANES
    r = tq // WINDOW
    last = n // WINDOW - 1
    prev = lambda j: jnp.maximum(j * r - 1, 0)
    nxt = lambda j: jnp.minimum((j + 1) * r, last)
    k_specs = [pl.BlockSpec((lc, kw), lambda j: (l // lc, 0)),
               pl.BlockSpec((WINDOW, kw), lambda j: (prev(j), 0)),
               pl.BlockSpec((tq, kw), lambda j: (j, 0)),
               pl.BlockSpec((WINDOW, kw), lambda j: (nxt(j), 0))]
    vt_specs = [pl.BlockSpec((lc // WINDOW, kw, WINDOW), lambda j: (l // lc, 0, 0)),
                pl.BlockSpec((1, kw, WINDOW), lambda j: (prev(j), 0, 0)),
                pl.BlockSpec((r, kw, WINDOW), lambda j: (j, 0, 0)),
                pl.BlockSpec((1, kw, WINDOW), lambda j: (nxt(j), 0, 0))]
    return pl.pallas_call(
        functools.partial(_win_attn_kernel, tq=tq, l=l, lc=lc, rep=rep, ng=ng),
        grid=(n // tq,),
        in_specs=[pl.BlockSpec(memory_space=pltpu.SMEM),
                  pl.BlockSpec((tq, d), lambda j: (j, 0))] + k_specs + vt_specs,
        out_specs=pl.BlockSpec((tq, d), lambda j: (j, 0)),
        out_shape=jax.ShapeDtypeStruct((n, d), BF16),
        compiler_params=_cparams(1),
        name="window_attention",
    )(sinks, q, k2, k2, k2, k2, vt2, vt2, vt2, vt2)


def _hgrn_consts(chunk):
    n_lv = int(math.log2(chunk))
    walls, masks_all = [], []
    for direction in range(2):
        tau = np.arange(chunk) if direction == 0 else chunk - 1 - np.arange(chunk)
        tr, tc = tau[:, None], tau[None, :]
        blocks = [tc <= tr]
        masks = []
        for lv in range(n_lv):
            h = chunk >> (lv + 1)
            mid = (tr // (2 * h)) * 2 * h + h - 1
            second_r = (tr % (2 * h)) >= h
            second_c = (tc % (2 * h)) >= h
            blocks.append(np.where(second_r, (tc > mid) & (tc <= tr), (tc > tr) & (tc <= mid)))
            masks.append(second_r & ~second_c & ((tr // (2 * h)) == (tc // (2 * h))))
        masks.append(tr == tc)
        w = np.concatenate(blocks, axis=0).astype(np.float32)
        walls.append(np.concatenate([w, w], axis=1))
        masks_all.append(np.stack(masks).astype(np.float32))
    return np.stack(walls), np.stack(masks_all)


def _hgrn_kernel(wall_ref, mask_ref, q_ref, g_ref, v_ref, vt_ref, o_ref, st_ref,
                 *, chunk, n_lv, n_heads, n_steps, per_step):
    dr = pl.program_id(0)
    step = pl.program_id(1)

    @pl.when(step == 0)
    def _():
        st_ref[...] = jnp.zeros_like(st_ref)

    @pl.when(step >= n_steps)
    def _():
        o_ref[...] = jnp.zeros_like(o_ref)

    @pl.when(step < n_steps)
    def _():
        heads = [slice(hh * LANES, (hh + 1) * LANES) for hh in range(n_heads)]
        nt = lambda x, y: lax.dot_general(x, y, NT_DIMS, preferred_element_type=F32)
        for j in range(per_step):
            idx = jnp.where(dr == 0, j, per_step - 1 - j)
            rows = pl.ds(pl.multiple_of(idx * chunk, chunk), chunk)
            g_all = g_ref[0, rows, :]
            g_hi = g_all.astype(BF16)
            g_lo = (g_all - g_hi.astype(F32)).astype(BF16)
            sums_all = jnp.dot(wall_ref[0], jnp.concatenate([g_hi, g_lo], axis=0),
                               preferred_element_type=F32)
            q16 = q_ref[rows, :]
            q_all = q16.astype(F32)
            k_all = 1.0 - jnp.exp(g_all)
            k16 = k_all.astype(BF16)
            v_all = v_ref[rows, :]
            vt_all = vt_ref[idx]
            b = sums_all[0:chunk]
            b_end = jnp.sum(g_all, axis=0, keepdims=True)

            qb = (q_all * jnp.exp(b)).astype(BF16)
            o = [nt(qb[:, sl], st_ref[hh].astype(BF16)) for hh, sl in enumerate(heads)]
            att = [nt(q16[:, sl], k16[:, sl]) * mask_ref[0, n_lv] for sl in heads]
            for lv in range(n_lv):
                e = jnp.exp(sums_all[(1 + lv) * chunk:(2 + lv) * chunk])
                qe = (q_all * e).astype(BF16)
                ke = (k_all * e).astype(BF16)
                mask = mask_ref[0, lv]
                att = [a + nt(qe[:, sl], ke[:, sl]) * mask for a, sl in zip(att, heads)]
            out = [o[hh] + jnp.dot(att[hh].astype(BF16), v_all[:, sl], preferred_element_type=F32)
                   for hh, sl in enumerate(heads)]
            o_ref[0, rows, :] = jnp.concatenate(out, axis=1)

            ks = (k_all * jnp.exp(b_end - b)).astype(BF16)
            decay = jnp.exp(b_end)
            for hh, sl in enumerate(heads):
                st_ref[hh] = st_ref[hh] * decay[:, sl] + jnp.dot(vt_all[sl, :], ks[:, sl],
                                                                 preferred_element_type=F32)


def _hgrn_call(qh, gates, v, vt3, l, lc):
    n, d = qh.shape
    c = HG_CHUNK
    per = HG_PER_STEP
    rows = per * c
    nh = d // LANES
    nx, ncc, ntot = l // rows, lc // rows, n // rows
    nc = nx + ncc
    assert n % rows == 0 and l % rows == 0 and lc % rows == 0 and vt3.shape[2] == c
    n_lv = int(math.log2(c))
    wall_np, mask_np = _hgrn_consts(c)
    wall = jnp.asarray(wall_np, dtype=BF16)
    masks = jnp.asarray(mask_np, dtype=F32)

    def blk(dr, s):
        fwd = jnp.where(s < ncc, nx + s, s - ncc)
        bwd = jnp.where(s < ncc, nx + ncc - 1 - s, nx - 1 - (s - ncc))
        return jnp.where(s >= nc, s, jnp.where(dr == 0, fwd, bwd))

    return pl.pallas_call(
        functools.partial(_hgrn_kernel, chunk=c, n_lv=n_lv, n_heads=nh, n_steps=nc, per_step=per),
        grid=(2, ntot),
        in_specs=[pl.BlockSpec((1,) + wall_np.shape[1:], lambda dr, s: (dr, 0, 0)),
                  pl.BlockSpec((1,) + mask_np.shape[1:], lambda dr, s: (dr, 0, 0, 0)),
                  pl.BlockSpec((rows, d), lambda dr, s: (blk(dr, s), 0)),
                  pl.BlockSpec((1, rows, d), lambda dr, s: (dr, blk(dr, s), 0)),
                  pl.BlockSpec((rows, d), lambda dr, s: (blk(dr, s), 0)),
                  pl.BlockSpec((per, d, c), lambda dr, s: (blk(dr, s), 0, 0))],
        out_specs=pl.BlockSpec((1, rows, d), lambda dr, s: (dr, blk(dr, s), 0)),
        out_shape=jax.ShapeDtypeStruct((2, n, d), F32),
        scratch_shapes=[pltpu.VMEM((nh, LANES, LANES), F32)],
        compiler_params=_cparams(2),
        name="hgrn_scan",
    )(wall, masks, qh, gates, v, vt3)


def _post_kernel(*refs, mode, final, tm, l, lc, d, f, n_x):
    it = iter(refs)
    x_refs = [next(it) for _ in range(n_x)]
    mod_ref, g2_ref, wo_ref, win_ref, wout_ref = (next(it) for _ in range(5))
    if mode == "plain":
        a_ref = next(it)
    elif mode == "conv":
        b_ref, z_ref, zp_ref, zn_ref, cw_ref = (next(it) for _ in range(5))
    else:
        ofw_ref, obw_ref, og_ref, gn_ref = (next(it) for _ in range(4))
    fin_ref = next(it) if final else None
    o_ref = next(it)

    ti = pl.program_id(0)
    is_ctx = _is_ctx_rows(ti, tm, l)
    if mode == "plain":
        a = a_ref[...]
    elif mode == "conv":
        z = z_ref[...]
        rl = lax.broadcasted_iota(jnp.int32, (tm, 1), 0)
        row = ti * tm + rl
        z_prev = jnp.where(rl == 0, zp_ref[7:8, :], pltpu.roll(z, 1, 0))
        z_next = jnp.where(rl == tm - 1, zn_ref[0:1, :], pltpu.roll(z, tm - 1, 0))
        z_prev = jnp.where((row == 0) | (row == l), 0.0, z_prev)
        z_next = jnp.where((row == l - 1) | (row == l + lc - 1), 0.0, z_next)
        conv = cw_ref[0:1, :] * z_prev + cw_ref[1:2, :] * z + cw_ref[2:3, :] * z_next
        a = (b_ref[...].astype(F32) * conv).astype(BF16)
    else:
        gn = gn_ref[...]
        heads = []
        for hh in range(d // LANES):
            sl = slice(hh * LANES, (hh + 1) * LANES)
            oh = _rms(ofw_ref[0, :, sl] + obw_ref[0, :, sl])
            heads.append((oh * gn * og_ref[:, sl].astype(F32)).astype(BF16))
        a = jnp.concatenate(heads, axis=1)

    y = jnp.dot(a, wo_ref[...], preferred_element_type=F32)
    x = _stream_tile(x_refs, ti, tm, l) + _pick(mod_ref, 2, is_ctx) * y
    h = _norm_mod(x, g2_ref[...], _pick(mod_ref, 3, is_ctx), _pick(mod_ref, 4, is_ctx)).astype(BF16)
    acc = jnp.zeros((tm, d), F32)
    for c0 in range(0, f, FFN_CHUNK):
        u = jnp.dot(h, win_ref[:, c0:c0 + FFN_CHUNK], preferred_element_type=F32)
        w = jnp.dot(h, win_ref[:, f + c0:f + c0 + FFN_CHUNK], preferred_element_type=F32)
        act = (_silu(u) * w).astype(BF16)
        acc = acc + jnp.dot(act, wout_ref[c0:c0 + FFN_CHUNK, :], preferred_element_type=F32)
    x = x + _pick(mod_ref, 5, is_ctx) * acc
    if final:
        x = _rms(x) * fin_ref[...]
    o_ref[...] = x


def _post_call(xs, mod12, g2, wo, win, wout, mixer_inputs, mode, l, lc, final_g=None):
    x_specs, x_args, n, d = _stream_specs(xs, l)
    f = wout.shape[0]
    final = final_g is not None
    assert f % FFN_CHUNK == 0 and l % TM == 0 and n % TM == 0
    n_out = l if final else n
    row = lambda i: (i, 0)
    in_specs = x_specs + [_const_spec((2 * N_MOD, d)), _const_spec((1, d)),
                          _const_spec(wo.shape), _const_spec(win.shape), _const_spec(wout.shape)]
    args = x_args + [mod12, g2, wo, win, wout]
    if mode == "plain":
        in_specs += [pl.BlockSpec((TM, d), row)]
    elif mode == "conv":
        r8 = TM // 8
        last8 = n // 8 - 1
        in_specs += [pl.BlockSpec((TM, d), row), pl.BlockSpec((TM, d), row),
                     pl.BlockSpec((8, d), lambda i: (jnp.maximum(i * r8 - 1, 0), 0)),
                     pl.BlockSpec((8, d), lambda i: (jnp.minimum((i + 1) * r8, last8), 0)),
                     _const_spec((SC_WIDTH, d))]
    else:
        in_specs += [pl.BlockSpec((1, TM, d), lambda i: (0, i, 0)),
                     pl.BlockSpec((1, TM, d), lambda i: (1, i, 0)),
                     pl.BlockSpec((TM, d), row), _const_spec((1, LANES))]
    args += list(mixer_inputs)
    if final:
        in_specs += [_const_spec((1, d))]
        args += [final_g]
    return pl.pallas_call(
        functools.partial(_post_kernel, mode=mode, final=final, tm=TM, l=l, lc=lc, d=d, f=f,
                          n_x=len(x_args)),
        grid=(n_out // TM,),
        in_specs=in_specs,
        out_specs=pl.BlockSpec((TM, d), row),
        out_shape=jax.ShapeDtypeStruct((n_out, d), F32),
        compiler_params=_cparams(1),
        name="post_" + mode + ("_final" if final else ""),
    )(*args)


def _rope_tables(l, n):
    rows = l // GRID_W
    r = jnp.repeat(jnp.arange(rows), GRID_W).astype(F32)
    c = jnp.tile(jnp.arange(GRID_W), rows).astype(F32)
    half = HEAD_DIM // 2
    inv_freq = 1.0 / (ROPE_BASE ** (jnp.arange(0, half, 2, dtype=F32) / half))
    ar, ac = r[:, None] * inv_freq, c[:, None] * inv_freq
    cs = jnp.concatenate([jnp.cos(ar), jnp.cos(ar), jnp.cos(ac), jnp.cos(ac)], axis=-1)
    sn = jnp.concatenate([-jnp.sin(ar), jnp.sin(ar), -jnp.sin(ac), jnp.sin(ac)], axis=-1)
    cs = jnp.concatenate([cs, jnp.ones((n - l, HEAD_DIM), F32)], axis=0)
    sn = jnp.concatenate([sn, jnp.zeros((n - l, HEAD_DIM), F32)], axis=0)
    return jnp.tile(cs, (1, 2)), jnp.tile(sn, (1, 2))


def kernel(x, c, ctx, c_ctx, ada_w, ada_b, norm1_g, norm2_g, ffn_w_in, ffn_w_out, final_g,
           da_wqkv, da_lambda, da_subln_g, da_wo, sc_w_in, sc_conv_w, sc_w_out,
           wa_wqkv, wa_sinks, wa_wo, hg_w_in, hg_lb, hg_gnorm_g, hg_wo):
    b, l, d = x.shape
    lc = ctx.shape[1]
    depth = ada_w.shape[0]
    assert b == 1 and d % LANES == 0 and l % TM == 0
    n = -(-(l + lc) // TM) * TM
    xs = (x[0], jnp.concatenate([ctx[0], jnp.zeros((n - l - lc, d), x.dtype)], axis=0))

    cc = jnp.zeros((8, d), F32).at[0].set(c_ctx).at[1].set(c[0])
    mods = _mod_call(cc, ada_w, ada_b)
    cs, sn = _rope_tables(l, n)

    n_mix = 4
    for i in range(depth):
        mixer, slot = i % n_mix, i // n_mix
        last = i == depth - 1
        mod12 = jnp.concatenate([mods[i, 0].reshape(N_MOD, d), mods[i, 1].reshape(N_MOD, d)], axis=0)
        g1 = norm1_g[i].reshape(1, d)
        g2 = norm2_g[i].reshape(1, d)
        win = ffn_w_in[i].astype(BF16)
        wout = ffn_w_out[i].astype(BF16)
        fin = final_g.reshape(1, d) if last else None
        if isinstance(xs, tuple) and mixer != 0:
            xs = jnp.concatenate(xs, axis=0)

        if mixer == 0:
            w = da_wqkv[slot]
            lam = da_lambda[slot].astype(F32)
            lam_init = 0.8 - 0.6 * math.exp(-0.3 * i)
            lam_full = (jnp.exp(jnp.sum(lam[0] * lam[1])) - jnp.exp(jnp.sum(lam[2] * lam[3]))
                        + lam_init).reshape(1)
            q, k, vt3 = _proj_da_call(xs, g1, mod12, cs, sn, w[:, :2 * d].astype(BF16),
                                      w[:, 2 * d:].T.astype(BF16), l)
            a = _diff_attn_call(lam_full, q, k, vt3, da_subln_g[slot].reshape(LANES, 1), l, lc,
                                1.0 - lam_init)
            xs = _post_call(xs, mod12, g2, da_wo[slot].astype(BF16), win, wout, [a], "plain",
                            l, lc, fin)
        elif mixer == 1:
            bg, z = _proj_sc_call(xs, g1, mod12, sc_w_in[slot].astype(BF16), l)
            xs = _post_call(xs, mod12, g2, sc_w_out[slot].astype(BF16), win, wout,
                            [bg, z, z, z, sc_conv_w[slot]], "conv", l, lc, fin)
        elif mixer == 2:
            w = wa_wqkv[slot]
            kw = WA_KV_HEADS * HEAD_DIM
            dup = lambda m: jnp.tile(m.reshape(d, WA_KV_HEADS, 1, HEAD_DIM), (1, 1, 2, 1)).reshape(d, 2 * kw)
            w2 = jnp.concatenate([w[:, :d], dup(w[:, d:d + kw]), dup(w[:, d + kw:])], axis=1)
            q, k2, vt2 = _proj_wa_call(xs, g1, mod12, cs, sn, w2.astype(BF16), l, 2 * kw)
            a = _win_attn_call(wa_sinks[slot].astype(F32), q, k2, vt2, l, lc)
            xs = _post_call(xs, mod12, g2, wa_wo[slot].astype(BF16), win, wout, [a], "plain",
                            l, lc, fin)
        else:
            w = hg_w_in[slot]
            p = jax.nn.softmax(hg_lb.astype(F32), axis=1)
            lb = (jnp.cumsum(p, axis=1) - p[:, :1])[:, i]
            qh, gates, v, vt3, og = _proj_hg_call(xs, g1, mod12, jnp.log(lb), jnp.log1p(-lb),
                                                  w.astype(BF16), l)
            o2 = _hgrn_call(qh, gates, v, vt3, l, lc)
            xs = _post_call(xs, mod12, g2, hg_wo[slot].astype(BF16), win, wout,
                            [o2, o2, og, hg_gnorm_g[slot].reshape(1, LANES)], "hgrn", l, lc, fin)
    return xs[:l].reshape(1, l, d)
```

```python
import functools
import math

import numpy as np
import jax
import jax.numpy as jnp
from jax import lax
from jax.experimental import pallas as pl
from jax.experimental.pallas import tpu as pltpu

F32 = jnp.float32
BF16 = jnp.bfloat16

HEAD_DIM = 64
GRID_W = 64
ROPE_BASE = 10000.0
RMS_EPS = 1e-6
N_MOD = 6
WINDOW = 128
WA_KV_HEADS = 4
SC_WIDTH = 3
LANES = 128
NEG_BIG = -1e30
LOG2E = math.log2(math.e)

TM = 512
SLAB = 128
HG_CHUNK = 128
HG_PER_STEP = 2
DA_TQ = 256
DA_TK = 1024
DA_ONES_ROWS = 16
DA_UNROLLS = (4, 2)
WA_TQ = 512
FFN_CHUNK = 256
VMEM_LIMIT = 56 * 1024 * 1024

NT_DIMS = (((1,), (1,)), ((), ()))


def _cparams(n_axes):
    return pltpu.CompilerParams(
        dimension_semantics=("arbitrary",) * n_axes, vmem_limit_bytes=VMEM_LIMIT)


def _const_spec(shape):
    nd = len(shape)
    return pl.BlockSpec(shape, lambda *_: (0,) * nd, pipeline_mode=pl.Buffered(1))


def _is_ctx_rows(tile_idx, tm, l):
    row = tile_idx * tm + lax.broadcasted_iota(jnp.int32, (tm, 1), 0)
    return row >= l


def _pick(mod_ref, k, is_ctx):
    return jnp.where(is_ctx, mod_ref[k:k + 1, :], mod_ref[N_MOD + k:N_MOD + k + 1, :])


def _rms(x):
    return x * lax.rsqrt(jnp.mean(x * x, axis=-1, keepdims=True) + RMS_EPS)


def _norm_mod(x, g, shift, scale):
    return (_rms(x) * g) * (1.0 + scale) + shift


def _silu(x):
    h = 0.5 * x
    return h + h * jnp.tanh(h)


def _rope(chunk, cs, sn, first_half):
    partner = jnp.where(first_half, pltpu.roll(chunk, LANES - 16, 1), pltpu.roll(chunk, 16, 1))
    return chunk * cs + partner * sn


def _stream_specs(xs, l):
    if not isinstance(xs, tuple):
        n, d = xs.shape
        return [pl.BlockSpec((TM, d), lambda i: (i, 0))], [xs], n, d
    lat, tail = xs
    d = lat.shape[1]
    nl = l // TM
    assert lat.shape[0] == l and tail.shape[0] % TM == 0
    return ([pl.BlockSpec((TM, d), lambda i: (jnp.minimum(i, nl - 1), 0)),
             pl.BlockSpec((TM, d), lambda i: (jnp.maximum(i - nl, 0), 0))],
            [lat, tail], l + tail.shape[0], d)


def _stream_tile(x_refs, tile_idx, tm, l):
    if len(x_refs) == 1:
        return x_refs[0][...]
    return jnp.where(tile_idx * tm < l, x_refs[0][...], x_refs[1][...])


def _first_half_lanes():
    lane = lax.broadcasted_iota(jnp.int32, (1, LANES), 1)
    return (lane & 31) < 16


def _mod_kernel(cc_ref, w_ref, b_ref, o_ref):
    a = _silu(cc_ref[...])
    o_ref[0] = jnp.dot(a, w_ref[0], precision=lax.Precision.HIGHEST,
                       preferred_element_type=F32) + b_ref[0]


def _mod_call(cc, ada_w, ada_b):
    depth, d, nd = ada_w.shape
    tn = 1536
    assert nd % tn == 0
    return pl.pallas_call(
        _mod_kernel,
        grid=(depth, nd // tn),
        in_specs=[pl.BlockSpec((8, d), lambda i, j: (0, 0)),
                  pl.BlockSpec((1, d, tn), lambda i, j: (i, 0, j)),
                  pl.BlockSpec((1, 1, tn), lambda i, j: (i, 0, j))],
        out_specs=pl.BlockSpec((1, 8, tn), lambda i, j: (i, 0, j)),
        out_shape=jax.ShapeDtypeStruct((depth, 8, nd), F32),
        compiler_params=_cparams(2),
        name="adaln_mod",
    )(cc, ada_w, ada_b.reshape(depth, 1, nd))


def _proj_da_kernel(*refs, tm, l, d, n_x):
    x_refs = refs[:n_x]
    g_ref, mod_ref, cs_ref, sn_ref, wqk_ref, wvt_ref, q_ref, k_ref, vt_ref = refs[n_x:]
    is_ctx = _is_ctx_rows(pl.program_id(0), tm, l)
    x = _stream_tile(x_refs, pl.program_id(0), tm, l)
    h = _norm_mod(x, g_ref[...], _pick(mod_ref, 0, is_ctx), _pick(mod_ref, 1, is_ctx))
    hb = h.astype(BF16)
    qk = jnp.dot(hb, wqk_ref[...], preferred_element_type=F32)
    cs, sn, first = cs_ref[...], sn_ref[...], _first_half_lanes()
    nq = d // LANES
    for j in range(2 * nq):
        r = _rope(qk[:, j * LANES:(j + 1) * LANES], cs, sn, first)
        if j < nq:
            q_ref[:, j * LANES:(j + 1) * LANES] = (r * (LOG2E * HEAD_DIM ** -0.5)).astype(BF16)
        else:
            k_ref[:, (j - nq) * LANES:(j - nq + 1) * LANES] = r.astype(BF16)
    vt = lax.dot_general(wvt_ref[...], hb, NT_DIMS, preferred_element_type=F32)
    ones = jnp.ones((DA_ONES_ROWS, SLAB), BF16)
    vr = LANES + DA_ONES_ROWS
    for s in range(tm // SLAB):
        for hh in range(d // LANES):
            vt_ref[s, hh * vr:hh * vr + LANES, :] = (
                vt[hh * LANES:(hh + 1) * LANES, s * SLAB:(s + 1) * SLAB].astype(BF16))
            vt_ref[s, hh * vr + LANES:(hh + 1) * vr, :] = ones


def _proj_da_call(xs, g, mod12, cs, sn, wqk, wvt, l):
    x_specs, x_args, n, d = _stream_specs(xs, l)
    row = lambda i: (i, 0)
    dv = d // LANES * (LANES + DA_ONES_ROWS)
    return pl.pallas_call(
        functools.partial(_proj_da_kernel, tm=TM, l=l, d=d, n_x=len(x_args)),
        grid=(n // TM,),
        in_specs=x_specs + [_const_spec((1, d)), _const_spec((2 * N_MOD, d)),
                            pl.BlockSpec((TM, LANES), row), pl.BlockSpec((TM, LANES), row),
                            _const_spec(wqk.shape), _const_spec(wvt.shape)],
        out_specs=[pl.BlockSpec((TM, d), row), pl.BlockSpec((TM, d), row),
                   pl.BlockSpec((TM // SLAB, dv, SLAB), lambda i: (i, 0, 0))],
        out_shape=[jax.ShapeDtypeStruct((n, d), BF16), jax.ShapeDtypeStruct((n, d), BF16),
                   jax.ShapeDtypeStruct((n // SLAB, dv, SLAB), BF16)],
        compiler_params=_cparams(1),
        name="proj_diff_attn",
    )(*x_args, g, mod12, cs, sn, wqk, wvt)


def _proj_sc_kernel(x_ref, g_ref, mod_ref, w_ref, b_ref, z_ref, *, tm, l, d):
    is_ctx = _is_ctx_rows(pl.program_id(0), tm, l)
    h = _norm_mod(x_ref[...], g_ref[...], _pick(mod_ref, 0, is_ctx), _pick(mod_ref, 1, is_ctx))
    y = jnp.dot(h.astype(BF16), w_ref[...], preferred_element_type=F32)
    b_ref[...] = y[:, :d].astype(BF16)
    z_ref[...] = y[:, d:2 * d] * y[:, 2 * d:]


def _proj_sc_call(xs, g, mod12, w, l):
    n, d = xs.shape
    row = lambda i: (i, 0)
    return pl.pallas_call(
        functools.partial(_proj_sc_kernel, tm=TM, l=l, d=d),
        grid=(n // TM,),
        in_specs=[pl.BlockSpec((TM, d), row), _const_spec((1, d)), _const_spec((2 * N_MOD, d)),
                  _const_spec(w.shape)],
        out_specs=[pl.BlockSpec((TM, d), row), pl.BlockSpec((TM, d), row)],
        out_shape=[jax.ShapeDtypeStruct((n, d), BF16), jax.ShapeDtypeStruct((n, d), F32)],
        compiler_params=_cparams(1),
        name="proj_short_conv",
    )(xs, g, mod12, w)


def _proj_wa_kernel(x_ref, g_ref, mod_ref, cs_ref, sn_ref, w_ref, q_ref, k_ref, v_ref,
                    *, tm, l, d, kw):
    is_ctx = _is_ctx_rows(pl.program_id(0), tm, l)
    h = _norm_mod(x_ref[...], g_ref[...], _pick(mod_ref, 0, is_ctx), _pick(mod_ref, 1, is_ctx))
    y = jnp.dot(h.astype(BF16), w_ref[...], preferred_element_type=F32)
    cs, sn, first = cs_ref[...], sn_ref[...], _first_half_lanes()
    for j in range(d // LANES):
        r = _rope(y[:, j * LANES:(j + 1) * LANES], cs, sn, first)
        q_ref[:, j * LANES:(j + 1) * LANES] = (r * (LOG2E * HEAD_DIM ** -0.5)).astype(BF16)
    for j in range(kw // LANES):
        r = _rope(y[:, d + j * LANES:d + (j + 1) * LANES], cs, sn, first)
        k_ref[:, j * LANES:(j + 1) * LANES] = r.astype(BF16)
    vt = y[:, d + kw:].T
    for s in range(tm // SLAB):
        v_ref[s] = vt[:, s * SLAB:(s + 1) * SLAB].astype(BF16)


def _proj_wa_call(xs, g, mod12, cs, sn, w, l, kw):
    n, d = xs.shape
    row = lambda i: (i, 0)
    return pl.pallas_call(
        functools.partial(_proj_wa_kernel, tm=TM, l=l, d=d, kw=kw),
        grid=(n // TM,),
        in_specs=[pl.BlockSpec((TM, d), row), _const_spec((1, d)), _const_spec((2 * N_MOD, d)),
                  pl.BlockSpec((TM, LANES), row), pl.BlockSpec((TM, LANES), row),
                  _const_spec(w.shape)],
        out_specs=[pl.BlockSpec((TM, d), row), pl.BlockSpec((TM, kw), row),
                   pl.BlockSpec((TM // SLAB, kw, SLAB), lambda i: (i, 0, 0))],
        out_shape=[jax.ShapeDtypeStruct((n, d), BF16), jax.ShapeDtypeStruct((n, kw), BF16),
                   jax.ShapeDtypeStruct((n // SLAB, kw, SLAB), BF16)],
        compiler_params=_cparams(1),
        name="proj_window_attn",
    )(xs, g, mod12, cs, sn, w)


def _log1p_unit(x):
    return jnp.log(1.0 + x)


def _proj_hg_kernel(x_ref, g_ref, mod_ref, lla_ref, l1m_ref, w_ref,
                    q_ref, gate_ref, v_ref, vt_ref, og_ref, *, tm, l, d):
    is_ctx = _is_ctx_rows(pl.program_id(0), tm, l)
    h = _norm_mod(x_ref[...], g_ref[...], _pick(mod_ref, 0, is_ctx), _pick(mod_ref, 1, is_ctx))
    hb = h.astype(BF16)
    part = lambda j: jnp.dot(hb, w_ref[:, j * d:(j + 1) * d], preferred_element_type=F32)
    q_ref[...] = _silu(part(0)).astype(BF16)
    for dr in range(2):
        f = part(1 + dr)
        log_sig = jnp.minimum(f, 0.0) - _log1p_unit(jnp.exp(-jnp.abs(f)))
        a = lla_ref[dr:dr + 1, :]
        c = l1m_ref[dr:dr + 1, :] + log_sig
        gate_ref[dr] = jnp.maximum(a, c) + _log1p_unit(jnp.exp(-jnp.abs(a - c)))
    v = part(3)
    v_ref[...] = v.astype(BF16)
    vt = v.T
    for s in range(tm // SLAB):
        vt_ref[s] = vt[:, s * SLAB:(s + 1) * SLAB].astype(BF16)
    og_ref[...] = _silu(part(4)).astype(BF16)


def _proj_hg_call(xs, g, mod12, lla, l1m, w, l):
    n, d = xs.shape
    row = lambda i: (i, 0)
    return pl.pallas_call(
        functools.partial(_proj_hg_kernel, tm=TM, l=l, d=d),
        grid=(n // TM,),
        in_specs=[pl.BlockSpec((TM, d), row), _const_spec((1, d)), _const_spec((2 * N_MOD, d)),
                  _const_spec((2, d)), _const_spec((2, d)), _const_spec(w.shape)],
        out_specs=[pl.BlockSpec((TM, d), row), pl.BlockSpec((2, TM, d), lambda i: (0, i, 0)),
                   pl.BlockSpec((TM, d), row),
                   pl.BlockSpec((TM // SLAB, d, SLAB), lambda i: (i, 0, 0)),
                   pl.BlockSpec((TM, d), row)],
        out_shape=[jax.ShapeDtypeStruct((n, d), BF16), jax.ShapeDtypeStruct((2, n, d), F32),
                   jax.ShapeDtypeStruct((n, d), BF16), jax.ShapeDtypeStruct((n // SLAB, d, SLAB), BF16),
                   jax.ShapeDtypeStruct((n, d), BF16)],
        compiler_params=_cparams(1),
        name="proj_hgrn",
    )(xs, g, mod12, lla, l1m, w)


def _diff_attn_kernel(lam_ref, q_ref, k_ref, vt_ref, g_ref, o_ref, m_ref, acc_ref,
                      sa_ref, sb_ref, ma_ref, mb_ref, *, tq, tk, l, lc, n, unroll, post_scale):
    lane = lax.broadcasted_iota(jnp.int32, (1, LANES), 1)
    sub = 2 * vt_ref.shape[2]
    n_chunks = l // tk

    def tile_rows(t):
        if isinstance(t, int):
            return slice(t * tq, (t + 1) * tq)
        return pl.ds(pl.multiple_of(t * tq, tq), tq)

    def stacked(t):
        q = q_ref[tile_rows(t), :]
        zero = jnp.zeros_like(q)
        return jnp.concatenate([jnp.where(lane < HEAD_DIM, q, zero),
                                jnp.where(lane >= HEAD_DIM, q, zero)], axis=0)

    def scores(kc, qcat):
        return lax.dot_general(kc, qcat, NT_DIMS, preferred_element_type=F32)

    def pv(pair0, n_pairs, p):
        out = None
        for j in range(n_pairs):
            s0 = 2 * (pair0 + j)
            vt = jnp.concatenate([vt_ref[s0], vt_ref[s0 + 1]], axis=1)
            t = jnp.dot(vt, p[j * sub:(j + 1) * sub, :], preferred_element_type=F32)
            out = t if out is None else out + t
        return out

    def absorb_context(qcat, slot):
        s = scores(k_ref[l:l + lc, :], qcat)
        mx = jnp.max(s, axis=0, keepdims=True)
        m_ref[slot] = mx
        acc_ref[slot] = pv(l // sub, lc // sub, jnp.exp2(s - mx).astype(BF16))

    def qk(c, qcat, s_ref, mx_ref):
        off = pl.multiple_of(c * tk, LANES)
        s = scores(k_ref[pl.ds(off, tk), :], qcat)
        s_ref[...] = s
        mx_ref[...] = jnp.max(s, axis=0, keepdims=True)

    def absorb(c, s_ref, mx_ref, slot):
        m_old = m_ref[slot]
        m_new = jnp.maximum(m_old, mx_ref[...])
        alpha = jnp.exp2(m_old - m_new)
        p = jnp.exp2(s_ref[...] - m_new).astype(BF16)
        m_ref[slot] = m_new
        acc_ref[slot] = alpha * acc_ref[slot] + pv((c * tk) // sub, tk // sub, p)

    def start_sweep(qcat, slot):
        qk(0, qcat, sa_ref, ma_ref)
        absorb_context(qcat, slot)

    def finish(t, slot):
        acc = acc_ref[slot]
        acc = acc[0:LANES, :] * (1.0 / acc[LANES:LANES + 1, :])
        o = acc[:, :tq] - lam_ref[0] * acc[:, tq:]
        o = o * lax.rsqrt(jnp.mean(o * o, axis=0, keepdims=True) + RMS_EPS)
        o = o * (g_ref[...] * post_scale)
        o_ref[tile_rows(t), :] = o.T.astype(BF16)

    bufs = ((sa_ref, ma_ref), (sb_ref, mb_ref))

    def sweep(t, carry):
        slot = t % 2
        qcat = stacked(t)

        def group(c0, count):
            for j in range(count):
                qk(c0 + j + 1, qcat, *bufs[(j + 1) % 2])
                absorb(c0 + j, *bufs[j % 2], slot)

        def body(i, inner):
            group(unroll * i, unroll)
            return inner

        lax.fori_loop(0, n_chunks // unroll - 1, body, 0)
        group(n_chunks - unroll, unroll - 1)
        start_sweep(stacked(t + 1), 1 - slot)
        absorb(n_chunks - 1, *bufs[(unroll - 1) % 2], slot)
        finish(t, slot)
        return carry

    start_sweep(stacked(0), 0)
    lax.fori_loop(0, l // tq, sweep, 0)
    for t in range(l // tq, n // tq):
        absorb_context(stacked(t), 0)
        finish(t, 0)


def _diff_attn_call(lam, q, k, vt3, subln_g, l, lc, post_scale):
    n, d = q.shape
    nh = d // LANES
    nslab, vrows, slab = vt3.shape
    vrows //= nh
    tq = DA_TQ
    tk = min(DA_TK, l // 2)
    assert l % tq == 0 and n % tq == 0 and n > l and tk % (2 * slab) == 0
    assert l % (2 * slab) == 0 and lc % (2 * slab) == 0
    assert l % (2 * tk) == 0
    return pl.pallas_call(
        functools.partial(_diff_attn_kernel, tq=tq, tk=tk, l=l, lc=lc, n=n,
                          unroll=next(u for u in DA_UNROLLS if l % (u * tk) == 0),
                          post_scale=post_scale),
        grid=(nh,),
        in_specs=[pl.BlockSpec(memory_space=pltpu.SMEM),
                  pl.BlockSpec((n, LANES), lambda h: (0, h)),
                  pl.BlockSpec((n, LANES), lambda h: (0, h)),
                  pl.BlockSpec((nslab, vrows, slab), lambda h: (0, h, 0)),
                  pl.BlockSpec((LANES, 1), lambda h: (0, 0))],
        out_specs=pl.BlockSpec((n, LANES), lambda h: (0, h)),
        out_shape=jax.ShapeDtypeStruct((n, d), BF16),
        scratch_shapes=[pltpu.VMEM((2, 1, 2 * tq), F32),
                        pltpu.VMEM((2, vrows, 2 * tq), F32),
                        pltpu.VMEM((tk, 2 * tq), F32), pltpu.VMEM((tk, 2 * tq), F32),
                        pltpu.VMEM((1, 2 * tq), F32), pltpu.VMEM((1, 2 * tq), F32)],
        compiler_params=_cparams(1),
        name="diff_attention",
    )(lam, q, k, vt3, subln_g)


def _win_attn_kernel(sink_ref, q_ref, kc_ref, kp_ref, ko_ref, kn_ref,
                     vc_ref, vp_ref, vo_ref, vn_ref, o_ref, *, tq, l, lc, rep, ng):
    j = pl.program_id(0)
    w = WINDOW
    gw = rep * HEAD_DIM
    a = lax.broadcasted_iota(jnp.int32, (w, w), 0)
    b = lax.broadcasted_iota(jnp.int32, (w, w), 1)
    lane = lax.broadcasted_iota(jnp.int32, (1, LANES), 1)
    low = lane < HEAD_DIM
    biases = []
    for t in range(tq // w):
        q0 = j * tq + t * w
        latent = q0 < l
        own_bias = jnp.where(latent, 0.0, NEG_BIG)
        prev_bias = jnp.where(jnp.logical_and(latent, q0 >= w), 0.0, NEG_BIG)
        next_bias = jnp.where(q0 + w < l, 0.0, NEG_BIG)
        bias = jnp.concatenate(
            [jnp.where(a >= b, prev_bias, NEG_BIG),
             jnp.full((w, w), own_bias, F32),
             jnp.where(a <= b, next_bias, NEG_BIG)], axis=0)
        biases.append(jnp.concatenate([bias] * rep, axis=1))
    for g in range(ng):
        gl = slice(g * LANES, (g + 1) * LANES)
        kc = kc_ref[:, gl]
        kwin = jnp.concatenate([kp_ref[:, gl], ko_ref[:, gl], kn_ref[:, gl]], axis=0)
        vtc = jnp.concatenate([vc_ref[s, gl, :] for s in range(lc // w)], axis=1)
        vtwin = jnp.concatenate([vp_ref[0, gl, :]] + [vo_ref[s, gl, :] for s in range(tq // w)]
                                + [vn_ref[0, gl, :]], axis=1)
        for t in range(tq // w):
            qs, sinks = [], []
            for r in range(rep):
                c0 = g * gw + (r // 2) * LANES
                qc = q_ref[t * w:(t + 1) * w, c0:c0 + LANES]
                qs.append(jnp.where(low if r % 2 == 0 else ~low, qc, jnp.zeros_like(qc)))
                sinks.append(jnp.full((1, w), sink_ref[g * rep + r] * LOG2E, F32))
            qst = jnp.concatenate(qs, axis=0)
            sink = jnp.concatenate(sinks, axis=1)
            s_c = lax.dot_general(kc, qst, NT_DIMS, preferred_element_type=F32)
            s_w = (lax.dot_general(kwin[t * w:(t + 3) * w, :], qst, NT_DIMS,
                                   preferred_element_type=F32) + biases[t])
            mx = jnp.maximum(jnp.maximum(jnp.max(s_c, axis=0, keepdims=True),
                                         jnp.max(s_w, axis=0, keepdims=True)), sink)
            e_c = jnp.exp2(s_c - mx)
            e_w = jnp.exp2(s_w - mx)
            den = (jnp.exp2(sink - mx) + jnp.sum(e_c, axis=0, keepdims=True)
                   + jnp.sum(e_w, axis=0, keepdims=True))
            ot = (jnp.dot(vtc, e_c.astype(BF16), preferred_element_type=F32)
                  + jnp.dot(vtwin[:, t * w:(t + 3) * w], e_w.astype(BF16),
                            preferred_element_type=F32)) * (1.0 / den)
            heads = [ot[:, r * w:(r + 1) * w].T for r in range(rep)]
            for c in range(rep // 2):
                c0 = g * gw + c * LANES
                o_ref[t * w:(t + 1) * w, c0:c0 + LANES] = jnp.where(
                    low, heads[2 * c], heads[2 * c + 1]).astype(BF16)


def _win_attn_call(sinks, q, k2, vt2, l, lc):
    n, d = q.shape
    ng = WA_KV_HEADS
    rep = d // HEAD_DIM // ng
    kw = k2.shape[1]
    tq = WA_TQ
    assert n % tq == 0 and tq % WINDOW == 0 and lc % WINDOW == 0 and l % lc == 0 and rep % 2 == 0
    assert kw == ng * LANES
    r = tq // WINDOW
    last = n // WINDOW - 1
    prev = lambda j: jnp.maximum(j * r - 1, 0)
    nxt = lambda j: jnp.minimum((j + 1) * r, last)
    k_specs = [pl.BlockSpec((lc, kw), lambda j: (l // lc, 0)),
               pl.BlockSpec((WINDOW, kw), lambda j: (prev(j), 0)),
               pl.BlockSpec((tq, kw), lambda j: (j, 0)),
               pl.BlockSpec((WINDOW, kw), lambda j: (nxt(j), 0))]
    vt_specs = [pl.BlockSpec((lc // WINDOW, kw, WINDOW), lambda j: (l // lc, 0, 0)),
                pl.BlockSpec((1, kw, WINDOW), lambda j: (prev(j), 0, 0)),
                pl.BlockSpec((r, kw, WINDOW), lambda j: (j, 0, 0)),
                pl.BlockSpec((1, kw, WINDOW), lambda j: (nxt(j), 0, 0))]
    return pl.pallas_call(
        functools.partial(_win_attn_kernel, tq=tq, l=l, lc=lc, rep=rep, ng=ng),
        grid=(n // tq,),
        in_specs=[pl.BlockSpec(memory_space=pltpu.SMEM),
                  pl.BlockSpec((tq, d), lambda j: (j, 0))] + k_specs + vt_specs,
        out_specs=pl.BlockSpec((tq, d), lambda j: (j, 0)),
        out_shape=jax.ShapeDtypeStruct((n, d), BF16),
        compiler_params=_cparams(1),
        name="window_attention",
    )(sinks, q, k2, k2, k2, k2, vt2, vt2, vt2, vt2)


def _hgrn_consts(chunk):
    n_lv = int(math.log2(chunk))
    walls, masks_all = [], []
    for direction in range(2):
        tau = np.arange(chunk) if direction == 0 else chunk - 1 - np.arange(chunk)
        tr, tc = tau[:, None], tau[None, :]
        blocks = [tc <= tr]
        masks = []
        for lv in range(n_lv):
            h = chunk >> (lv + 1)
            mid = (tr // (2 * h)) * 2 * h + h - 1
            second_r = (tr % (2 * h)) >= h
            second_c = (tc % (2 * h)) >= h
            blocks.append(np.where(second_r, (tc > mid) & (tc <= tr), (tc > tr) & (tc <= mid)))
            masks.append(second_r & ~second_c & ((tr // (2 * h)) == (tc // (2 * h))))
        masks.append(tr == tc)
        w = np.concatenate(blocks, axis=0).astype(np.float32)
        walls.append(np.concatenate([w, w], axis=1))
        masks_all.append(np.stack(masks).astype(np.float32))
    return np.stack(walls), np.stack(masks_all)


def _hgrn_kernel(wall_ref, mask_ref, q_ref, g_ref, v_ref, vt_ref, o_ref, st_ref,
                 *, chunk, n_lv, n_heads, n_steps, per_step):
    dr = pl.program_id(0)
    step = pl.program_id(1)

    @pl.when(step == 0)
    def _():
        st_ref[...] = jnp.zeros_like(st_ref)

    @pl.when(step >= n_steps)
    def _():
        o_ref[...] = jnp.zeros_like(o_ref)

    @pl.when(step < n_steps)
    def _():
        heads = [slice(hh * LANES, (hh + 1) * LANES) for hh in range(n_heads)]
        nt = lambda x, y: lax.dot_general(x, y, NT_DIMS, preferred_element_type=F32)
        for j in range(per_step):
            idx = jnp.where(dr == 0, j, per_step - 1 - j)
            rows = pl.ds(pl.multiple_of(idx * chunk, chunk), chunk)
            g_all = g_ref[0, rows, :]
            g_hi = g_all.astype(BF16)
            g_lo = (g_all - g_hi.astype(F32)).astype(BF16)
            sums_all = jnp.dot(wall_ref[0], jnp.concatenate([g_hi, g_lo], axis=0),
                               preferred_element_type=F32)
            q16 = q_ref[rows, :]
            q_all = q16.astype(F32)
            k_all = 1.0 - jnp.exp(g_all)
            k16 = k_all.astype(BF16)
            v_all = v_ref[rows, :]
            vt_all = vt_ref[idx]
            b = sums_all[0:chunk]
            b_end = jnp.sum(g_all, axis=0, keepdims=True)

            qb = (q_all * jnp.exp(b)).astype(BF16)
            o = [nt(qb[:, sl], st_ref[hh].astype(BF16)) for hh, sl in enumerate(heads)]
            att = [nt(q16[:, sl], k16[:, sl]) * mask_ref[0, n_lv] for sl in heads]
            for lv in range(n_lv):
                e = jnp.exp(sums_all[(1 + lv) * chunk:(2 + lv) * chunk])
                qe = (q_all * e).astype(BF16)
                ke = (k_all * e).astype(BF16)
                mask = mask_ref[0, lv]
                att = [a + nt(qe[:, sl], ke[:, sl]) * mask for a, sl in zip(att, heads)]
            out = [o[hh] + jnp.dot(att[hh].astype(BF16), v_all[:, sl], preferred_element_type=F32)
                   for hh, sl in enumerate(heads)]
            o_ref[0, rows, :] = jnp.concatenate(out, axis=1)

            ks = (k_all * jnp.exp(b_end - b)).astype(BF16)
            decay = jnp.exp(b_end)
            for hh, sl in enumerate(heads):
                st_ref[hh] = st_ref[hh] * decay[:, sl] + jnp.dot(vt_all[sl, :], ks[:, sl],
                                                                 preferred_element_type=F32)


def _hgrn_call(qh, gates, v, vt3, l, lc):
    n, d = qh.shape
    c = HG_CHUNK
    per = HG_PER_STEP
    rows = per * c
    nh = d // LANES
    nx, ncc, ntot = l // rows, lc // rows, n // rows
    nc = nx + ncc
    assert n % rows == 0 and l % rows == 0 and lc % rows == 0 and vt3.shape[2] == c
    n_lv = int(math.log2(c))
    wall_np, mask_np = _hgrn_consts(c)
    wall = jnp.asarray(wall_np, dtype=BF16)
    masks = jnp.asarray(mask_np, dtype=F32)

    def blk(dr, s):
        fwd = jnp.where(s < ncc, nx + s, s - ncc)
        bwd = jnp.where(s < ncc, nx + ncc - 1 - s, nx - 1 - (s - ncc))
        return jnp.where(s >= nc, s, jnp.where(dr == 0, fwd, bwd))

    return pl.pallas_call(
        functools.partial(_hgrn_kernel, chunk=c, n_lv=n_lv, n_heads=nh, n_steps=nc, per_step=per),
        grid=(2, ntot),
        in_specs=[pl.BlockSpec((1,) + wall_np.shape[1:], lambda dr, s: (dr, 0, 0)),
                  pl.BlockSpec((1,) + mask_np.shape[1:], lambda dr, s: (dr, 0, 0, 0)),
                  pl.BlockSpec((rows, d), lambda dr, s: (blk(dr, s), 0)),
                  pl.BlockSpec((1, rows, d), lambda dr, s: (dr, blk(dr, s), 0)),
                  pl.BlockSpec((rows, d), lambda dr, s: (blk(dr, s), 0)),
                  pl.BlockSpec((per, d, c), lambda dr, s: (blk(dr, s), 0, 0))],
        out_specs=pl.BlockSpec((1, rows, d), lambda dr, s: (dr, blk(dr, s), 0)),
        out_shape=jax.ShapeDtypeStruct((2, n, d), F32),
        scratch_shapes=[pltpu.VMEM((nh, LANES, LANES), F32)],
        compiler_params=_cparams(2),
        name="hgrn_scan",
    )(wall, masks, qh, gates, v, vt3)


def _post_kernel(*refs, mode, final, tm, l, lc, d, f, n_x):
    it = iter(refs)
    x_refs = [next(it) for _ in range(n_x)]
    mod_ref, g2_ref, wo_ref, win_ref, wout_ref = (next(it) for _ in range(5))
    if mode == "plain":
        a_ref = next(it)
    elif mode == "conv":
        b_ref, z_ref, zp_ref, zn_ref, cw_ref = (next(it) for _ in range(5))
    else:
        ofw_ref, obw_ref, og_ref, gn_ref = (next(it) for _ in range(4))
    fin_ref = next(it) if final else None
    o_ref = next(it)

    ti = pl.program_id(0)
    is_ctx = _is_ctx_rows(ti, tm, l)
    if mode == "plain":
        a = a_ref[...]
    elif mode == "conv":
        z = z_ref[...]
        rl = lax.broadcasted_iota(jnp.int32, (tm, 1), 0)
        row = ti * tm + rl
        z_prev = jnp.where(rl == 0, zp_ref[7:8, :], pltpu.roll(z, 1, 0))
        z_next = jnp.where(rl == tm - 1, zn_ref[0:1, :], pltpu.roll(z, tm - 1, 0))
        z_prev = jnp.where((row == 0) | (row == l), 0.0, z_prev)
        z_next = jnp.where((row == l - 1) | (row == l + lc - 1), 0.0, z_next)
        conv = cw_ref[0:1, :] * z_prev + cw_ref[1:2, :] * z + cw_ref[2:3, :] * z_next
        a = (b_ref[...].astype(F32) * conv).astype(BF16)
    else:
        gn = gn_ref[...]
        heads = []
        for hh in range(d // LANES):
            sl = slice(hh * LANES, (hh + 1) * LANES)
            oh = _rms(ofw_ref[0, :, sl] + obw_ref[0, :, sl])
            heads.append((oh * gn * og_ref[:, sl].astype(F32)).astype(BF16))
        a = jnp.concatenate(heads, axis=1)

    y = jnp.dot(a, wo_ref[...], preferred_element_type=F32)
    x = _stream_tile(x_refs, ti, tm, l) + _pick(mod_ref, 2, is_ctx) * y
    h = _norm_mod(x, g2_ref[...], _pick(mod_ref, 3, is_ctx), _pick(mod_ref, 4, is_ctx)).astype(BF16)
    acc = jnp.zeros((tm, d), F32)
    for c0 in range(0, f, FFN_CHUNK):
        u = jnp.dot(h, win_ref[:, c0:c0 + FFN_CHUNK], preferred_element_type=F32)
        w = jnp.dot(h, win_ref[:, f + c0:f + c0 + FFN_CHUNK], preferred_element_type=F32)
        act = (_silu(u) * w).astype(BF16)
        acc = acc + jnp.dot(act, wout_ref[c0:c0 + FFN_CHUNK, :], preferred_element_type=F32)
    x = x + _pick(mod_ref, 5, is_ctx) * acc
    if final:
        x = _rms(x) * fin_ref[...]
    o_ref[...] = x


def _post_call(xs, mod12, g2, wo, win, wout, mixer_inputs, mode, l, lc, final_g=None):
    x_specs, x_args, n, d = _stream_specs(xs, l)
    f = wout.shape[0]
    final = final_g is not None
    assert f % FFN_CHUNK == 0 and l % TM == 0 and n % TM == 0
    n_out = l if final else n
    row = lambda i: (i, 0)
    in_specs = x_specs + [_const_spec((2 * N_MOD, d)), _const_spec((1, d)),
                          _const_spec(wo.shape), _const_spec(win.shape), _const_spec(wout.shape)]
    args = x_args + [mod12, g2, wo, win, wout]
    if mode == "plain":
        in_specs += [pl.BlockSpec((TM, d), row)]
    elif mode == "conv":
        r8 = TM // 8
        last8 = n // 8 - 1
        in_specs += [pl.BlockSpec((TM, d), row), pl.BlockSpec((TM, d), row),
                     pl.BlockSpec((8, d), lambda i: (jnp.maximum(i * r8 - 1, 0), 0)),
                     pl.BlockSpec((8, d), lambda i: (jnp.minimum((i + 1) * r8, last8), 0)),
                     _const_spec((SC_WIDTH, d))]
    else:
        in_specs += [pl.BlockSpec((1, TM, d), lambda i: (0, i, 0)),
                     pl.BlockSpec((1, TM, d), lambda i: (1, i, 0)),
                     pl.BlockSpec((TM, d), row), _const_spec((1, LANES))]
    args += list(mixer_inputs)
    if final:
        in_specs += [_const_spec((1, d))]
        args += [final_g]
    return pl.pallas_call(
        functools.partial(_post_kernel, mode=mode, final=final, tm=TM, l=l, lc=lc, d=d, f=f,
                          n_x=len(x_args)),
        grid=(n_out // TM,),
        in_specs=in_specs,
        out_specs=pl.BlockSpec((TM, d), row),
        out_shape=jax.ShapeDtypeStruct((n_out, d), F32),
        compiler_params=_cparams(1),
        name="post_" + mode + ("_final" if final else ""),
    )(*args)


def _rope_tables(l, n):
    rows = l // GRID_W
    r = jnp.repeat(jnp.arange(rows), GRID_W).astype(F32)
    c = jnp.tile(jnp.arange(GRID_W), rows).astype(F32)
    half = HEAD_DIM // 2
    inv_freq = 1.0 / (ROPE_BASE ** (jnp.arange(0, half, 2, dtype=F32) / half))
    ar, ac = r[:, None] * inv_freq, c[:, None] * inv_freq
    cs = jnp.concatenate([jnp.cos(ar), jnp.cos(ar), jnp.cos(ac), jnp.cos(ac)], axis=-1)
    sn = jnp.concatenate([-jnp.sin(ar), jnp.sin(ar), -jnp.sin(ac), jnp.sin(ac)], axis=-1)
    cs = jnp.concatenate([cs, jnp.ones((n - l, HEAD_DIM), F32)], axis=0)
    sn = jnp.concatenate([sn, jnp.zeros((n - l, HEAD_DIM), F32)], axis=0)
    return jnp.tile(cs, (1, 2)), jnp.tile(sn, (1, 2))


def kernel(x, c, ctx, c_ctx, ada_w, ada_b, norm1_g, norm2_g, ffn_w_in, ffn_w_out, final_g,
           da_wqkv, da_lambda, da_subln_g, da_wo, sc_w_in, sc_conv_w, sc_w_out,
           wa_wqkv, wa_sinks, wa_wo, hg_w_in, hg_lb, hg_gnorm_g, hg_wo):
    b, l, d = x.shape
    lc = ctx.shape[1]
    depth = ada_w.shape[0]
    assert b == 1 and d % LANES == 0 and l % TM == 0
    n = -(-(l + lc) // TM) * TM
    xs = (x[0], jnp.concatenate([ctx[0], jnp.zeros((n - l - lc, d), x.dtype)], axis=0))

    cc = jnp.zeros((8, d), F32).at[0].set(c_ctx).at[1].set(c[0])
    mods = _mod_call(cc, ada_w, ada_b)
    cs, sn = _rope_tables(l, n)

    n_mix = 4
    for i in range(depth):
        mixer, slot = i % n_mix, i // n_mix
        last = i == depth - 1
        mod12 = jnp.concatenate([mods[i, 0].reshape(N_MOD, d), mods[i, 1].reshape(N_MOD, d)], axis=0)
        g1 = norm1_g[i].reshape(1, d)
        g2 = norm2_g[i].reshape(1, d)
        win = ffn_w_in[i].astype(BF16)
        wout = ffn_w_out[i].astype(BF16)
        fin = final_g.reshape(1, d) if last else None
        if isinstance(xs, tuple) and mixer != 0:
            xs = jnp.concatenate(xs, axis=0)

        if mixer == 0:
            w = da_wqkv[slot]
            lam = da_lambda[slot].astype(F32)
            lam_init = 0.8 - 0.6 * math.exp(-0.3 * i)
            lam_full = (jnp.exp(jnp.sum(lam[0] * lam[1])) - jnp.exp(jnp.sum(lam[2] * lam[3]))
                        + lam_init).reshape(1)
            q, k, vt3 = _proj_da_call(xs, g1, mod12, cs, sn, w[:, :2 * d].astype(BF16),
                                      w[:, 2 * d:].T.astype(BF16), l)
            a = _diff_attn_call(lam_full, q, k, vt3, da_subln_g[slot].reshape(LANES, 1), l, lc,
                                1.0 - lam_init)
            xs = _post_call(xs, mod12, g2, da_wo[slot].astype(BF16), win, wout, [a], "plain",
                            l, lc, fin)
        elif mixer == 1:
            bg, z = _proj_sc_call(xs, g1, mod12, sc_w_in[slot].astype(BF16), l)
            xs = _post_call(xs, mod12, g2, sc_w_out[slot].astype(BF16), win, wout,
                            [bg, z, z, z, sc_conv_w[slot]], "conv", l, lc, fin)
        elif mixer == 2:
            w = wa_wqkv[slot]
            kw = WA_KV_HEADS * HEAD_DIM
            dup = lambda m: jnp.tile(m.reshape(d, WA_KV_HEADS, 1, HEAD_DIM), (1, 1, 2, 1)).reshape(d, 2 * kw)
            w2 = jnp.concatenate([w[:, :d], dup(w[:, d:d + kw]), dup(w[:, d + kw:])], axis=1)
            q, k2, vt2 = _proj_wa_call(xs, g1, mod12, cs, sn, w2.astype(BF16), l, 2 * kw)
            a = _win_attn_call(wa_sinks[slot].astype(F32), q, k2, vt2, l, lc)
            xs = _post_call(xs, mod12, g2, wa_wo[slot].astype(BF16), win, wout, [a], "plain",
                            l, lc, fin)
        else:
            w = hg_w_in[slot]
            p = jax.nn.softmax(hg_lb.astype(F32), axis=1)
            lb = (jnp.cumsum(p, axis=1) - p[:, :1])[:, i]
            qh, gates, v, vt3, og = _proj_hg_call(xs, g1, mod12, jnp.log(lb), jnp.log1p(-lb),
                                                  w.astype(BF16), l)
            o2 = _hgrn_call(qh, gates, v, vt3, l, lc)
            xs = _post_call(xs, mod12, g2, hg_wo[slot].astype(BF16), win, wout,
                            [o2, o2, og, hg_gnorm_g[slot].reshape(1, LANES)], "hgrn", l, lc, fin)
    return xs[:l].reshape(1, l, d)
```

```python
import functools
import math

import numpy as np
import jax
import jax.numpy as jnp
from jax import lax
from jax.experimental import pallas as pl
from jax.experimental.pallas import tpu as pltpu

F32 = jnp.float32
BF16 = jnp.bfloat16

HEAD_DIM = 64
GRID_W = 64
ROPE_BASE = 10000.0
RMS_EPS = 1e-6
N_MOD = 6
WINDOW = 128
WA_KV_HEADS = 4
SC_WIDTH = 3
LANES = 128
NEG_BIG = -1e30
LOG2E = math.log2(math.e)

TM = 512
SLAB = 128
HG_CHUNK = 128
HG_PER_STEP = 2
DA_TQ = 256
DA_TK = 1024
DA_ONES_ROWS = 16
DA_UNROLLS = (4, 2)
WA_TQ = 512
FFN_CHUNK = 256
VMEM_LIMIT = 56 * 1024 * 1024

NT_DIMS = (((1,), (1,)), ((), ()))


def _cparams(n_axes):
    return pltpu.CompilerParams(
        dimension_semantics=("arbitrary",) * n_axes, vmem_limit_bytes=VMEM_LIMIT)


def _const_spec(shape):
    nd = len(shape)
    return pl.BlockSpec(shape, lambda *_: (0,) * nd, pipeline_mode=pl.Buffered(1))


def _is_ctx_rows(tile_idx, tm, l):
    row = tile_idx * tm + lax.broadcasted_iota(jnp.int32, (tm, 1), 0)
    return row >= l


def _pick(mod_ref, k, is_ctx):
    return jnp.where(is_ctx, mod_ref[k:k + 1, :], mod_ref[N_MOD + k:N_MOD + k + 1, :])


def _rms(x):
    return x * lax.rsqrt(jnp.mean(x * x, axis=-1, keepdims=True) + RMS_EPS)


def _norm_mod(x, g, shift, scale):
    return (_rms(x) * g) * (1.0 + scale) + shift


def _silu(x):
    h = 0.5 * x
    return h + h * jnp.tanh(h)


def _rope(chunk, cs, sn, first_half):
    partner = jnp.where(first_half, pltpu.roll(chunk, LANES - 16, 1), pltpu.roll(chunk, 16, 1))
    return chunk * cs + partner * sn


def _stream_specs(xs, l):
    if not isinstance(xs, tuple):
        n, d = xs.shape
        return [pl.BlockSpec((TM, d), lambda i: (i, 0))], [xs], n, d
    lat, tail = xs
    d = lat.shape[1]
    nl = l // TM
    assert lat.shape[0] == l and tail.shape[0] % TM == 0
    return ([pl.BlockSpec((TM, d), lambda i: (jnp.minimum(i, nl - 1), 0)),
             pl.BlockSpec((TM, d), lambda i: (jnp.maximum(i - nl, 0), 0))],
            [lat, tail], l + tail.shape[0], d)


def _stream_tile(x_refs, tile_idx, tm, l):
    if len(x_refs) == 1:
        return x_refs[0][...]
    return jnp.where(tile_idx * tm < l, x_refs[0][...], x_refs[1][...])


def _first_half_lanes():
    lane = lax.broadcasted_iota(jnp.int32, (1, LANES), 1)
    return (lane & 31) < 16


def _mod_kernel(cc_ref, w_ref, b_ref, o_ref):
    a = _silu(cc_ref[...])
    o_ref[0] = jnp.dot(a, w_ref[0], precision=lax.Precision.HIGHEST,
                       preferred_element_type=F32) + b_ref[0]


def _mod_call(cc, ada_w, ada_b):
    depth, d, nd = ada_w.shape
    tn = 1536
    assert nd % tn == 0
    return pl.pallas_call(
        _mod_kernel,
        grid=(depth, nd // tn),
        in_specs=[pl.BlockSpec((8, d), lambda i, j: (0, 0)),
                  pl.BlockSpec((1, d, tn), lambda i, j: (i, 0, j)),
                  pl.BlockSpec((1, 1, tn), lambda i, j: (i, 0, j))],
        out_specs=pl.BlockSpec((1, 8, tn), lambda i, j: (i, 0, j)),
        out_shape=jax.ShapeDtypeStruct((depth, 8, nd), F32),
        compiler_params=_cparams(2),
        name="adaln_mod",
    )(cc, ada_w, ada_b.reshape(depth, 1, nd))


def _proj_da_kernel(*refs, tm, l, d, n_x):
    x_refs = refs[:n_x]
    g_ref, mod_ref, cs_ref, sn_ref, wqk_ref, wvt_ref, q_ref, k_ref, vt_ref = refs[n_x:]
    is_ctx = _is_ctx_rows(pl.program_id(0), tm, l)
    x = _stream_tile(x_refs, pl.program_id(0), tm, l)
    h = _norm_mod(x, g_ref[...], _pick(mod_ref, 0, is_ctx), _pick(mod_ref, 1, is_ctx))
    hb = h.astype(BF16)
    qk = jnp.dot(hb, wqk_ref[...], preferred_element_type=F32)
    cs, sn, first = cs_ref[...], sn_ref[...], _first_half_lanes()
    nq = d // LANES
    for j in range(2 * nq):
        r = _rope(qk[:, j * LANES:(j + 1) * LANES], cs, sn, first)
        if j < nq:
            q_ref[:, j * LANES:(j + 1) * LANES] = (r * (LOG2E * HEAD_DIM ** -0.5)).astype(BF16)
        else:
            k_ref[:, (j - nq) * LANES:(j - nq + 1) * LANES] = r.astype(BF16)
    vt = lax.dot_general(wvt_ref[...], hb, NT_DIMS, preferred_element_type=F32)
    ones = jnp.ones((DA_ONES_ROWS, SLAB), BF16)
    vr = LANES + DA_ONES_ROWS
    for s in range(tm // SLAB):
        for hh in range(d // LANES):
            vt_ref[s, hh * vr:hh * vr + LANES, :] = (
                vt[hh * LANES:(hh + 1) * LANES, s * SLAB:(s + 1) * SLAB].astype(BF16))
            vt_ref[s, hh * vr + LANES:(hh + 1) * vr, :] = ones


def _proj_da_call(xs, g, mod12, cs, sn, wqk, wvt, l):
    x_specs, x_args, n, d = _stream_specs(xs, l)
    row = lambda i: (i, 0)
    dv = d // LANES * (LANES + DA_ONES_ROWS)
    return pl.pallas_call(
        functools.partial(_proj_da_kernel, tm=TM, l=l, d=d, n_x=len(x_args)),
        grid=(n // TM,),
        in_specs=x_specs + [_const_spec((1, d)), _const_spec((2 * N_MOD, d)),
                            pl.BlockSpec((TM, LANES), row), pl.BlockSpec((TM, LANES), row),
                            _const_spec(wqk.shape), _const_spec(wvt.shape)],
        out_specs=[pl.BlockSpec((TM, d), row), pl.BlockSpec((TM, d), row),
                   pl.BlockSpec((TM // SLAB, dv, SLAB), lambda i: (i, 0, 0))],
        out_shape=[jax.ShapeDtypeStruct((n, d), BF16), jax.ShapeDtypeStruct((n, d), BF16),
                   jax.ShapeDtypeStruct((n // SLAB, dv, SLAB), BF16)],
        compiler_params=_cparams(1),
        name="proj_diff_attn",
    )(*x_args, g, mod12, cs, sn, wqk, wvt)


def _proj_sc_kernel(x_ref, g_ref, mod_ref, w_ref, b_ref, z_ref, *, tm, l, d):
    is_ctx = _is_ctx_rows(pl.program_id(0), tm, l)
    h = _norm_mod(x_ref[...], g_ref[...], _pick(mod_ref, 0, is_ctx), _pick(mod_ref, 1, is_ctx))
    y = jnp.dot(h.astype(BF16), w_ref[...], preferred_element_type=F32)
    b_ref[...] = y[:, :d].astype(BF16)
    z_ref[...] = y[:, d:2 * d] * y[:, 2 * d:]


def _proj_sc_call(xs, g, mod12, w, l):
    n, d = xs.shape
    row = lambda i: (i, 0)
    return pl.pallas_call(
        functools.partial(_proj_sc_kernel, tm=TM, l=l, d=d),
        grid=(n // TM,),
        in_specs=[pl.BlockSpec((TM, d), row), _const_spec((1, d)), _const_spec((2 * N_MOD, d)),
                  _const_spec(w.shape)],
        out_specs=[pl.BlockSpec((TM, d), row), pl.BlockSpec((TM, d), row)],
        out_shape=[jax.ShapeDtypeStruct((n, d), BF16), jax.ShapeDtypeStruct((n, d), F32)],
        compiler_params=_cparams(1),
        name="proj_short_conv",
    )(xs, g, mod12, w)


def _proj_wa_kernel(x_ref, g_ref, mod_ref, cs_ref, sn_ref, w_ref, q_ref, k_ref, v_ref,
                    *, tm, l, d, kw):
    is_ctx = _is_ctx_rows(pl.program_id(0), tm, l)
    h = _norm_mod(x_ref[...], g_ref[...], _pick(mod_ref, 0, is_ctx), _pick(mod_ref, 1, is_ctx))
    y = jnp.dot(h.astype(BF16), w_ref[...], preferred_element_type=F32)
    cs, sn, first = cs_ref[...], sn_ref[...], _first_half_lanes()
    for j in range(d // LANES):
        r = _rope(y[:, j * LANES:(j + 1) * LANES], cs, sn, first)
        q_ref[:, j * LANES:(j + 1) * LANES] = (r * (LOG2E * HEAD_DIM ** -0.5)).astype(BF16)
    for j in range(kw // LANES):
        r = _rope(y[:, d + j * LANES:d + (j + 1) * LANES], cs, sn, first)
        k_ref[:, j * LANES:(j + 1) * LANES] = r.astype(BF16)
    vt = y[:, d + kw:].T
    for s in range(tm // SLAB):
        v_ref[s] = vt[:, s * SLAB:(s + 1) * SLAB].astype(BF16)


def _proj_wa_call(xs, g, mod12, cs, sn, w, l, kw):
    n, d = xs.shape
    row = lambda i: (i, 0)
    return pl.pallas_call(
        functools.partial(_proj_wa_kernel, tm=TM, l=l, d=d, kw=kw),
        grid=(n // TM,),
        in_specs=[pl.BlockSpec((TM, d), row), _const_spec((1, d)), _const_spec((2 * N_MOD, d)),
                  pl.BlockSpec((TM, LANES), row), pl.BlockSpec((TM, LANES), row),
                  _const_spec(w.shape)],
        out_specs=[pl.BlockSpec((TM, d), row), pl.BlockSpec((TM, kw), row),
                   pl.BlockSpec((TM // SLAB, kw, SLAB), lambda i: (i, 0, 0))],
        out_shape=[jax.ShapeDtypeStruct((n, d), BF16), jax.ShapeDtypeStruct((n, kw), BF16),
                   jax.ShapeDtypeStruct((n // SLAB, kw, SLAB), BF16)],
        compiler_params=_cparams(1),
        name="proj_window_attn",
    )(xs, g, mod12, cs, sn, w)


def _log1p_unit(x):
    return jnp.log(1.0 + x)


def _proj_hg_kernel(x_ref, g_ref, mod_ref, lla_ref, l1m_ref, w_ref,
                    q_ref, gate_ref, v_ref, vt_ref, og_ref, *, tm, l, d):
    is_ctx = _is_ctx_rows(pl.program_id(0), tm, l)
    h = _norm_mod(x_ref[...], g_ref[...], _pick(mod_ref, 0, is_ctx), _pick(mod_ref, 1, is_ctx))
    hb = h.astype(BF16)
    part = lambda j: jnp.dot(hb, w_ref[:, j * d:(j + 1) * d], preferred_element_type=F32)
    q_ref[...] = _silu(part(0)).astype(BF16)
    for dr in range(2):
        f = part(1 + dr)
        log_sig = jnp.minimum(f, 0.0) - _log1p_unit(jnp.exp(-jnp.abs(f)))
        a = lla_ref[dr:dr + 1, :]
        c = l1m_ref[dr:dr + 1, :] + log_sig
        gate_ref[dr] = jnp.maximum(a, c) + _log1p_unit(jnp.exp(-jnp.abs(a - c)))
    v = part(3)
    v_ref[...] = v.astype(BF16)
    vt = v.T
    for s in range(tm // SLAB):
        vt_ref[s] = vt[:, s * SLAB:(s + 1) * SLAB].astype(BF16)
    og_ref[...] = _silu(part(4)).astype(BF16)


def _proj_hg_call(xs, g, mod12, lla, l1m, w, l):
    n, d = xs.shape
    row = lambda i: (i, 0)
    return pl.pallas_call(
        functools.partial(_proj_hg_kernel, tm=TM, l=l, d=d),
        grid=(n // TM,),
        in_specs=[pl.BlockSpec((TM, d), row), _const_spec((1, d)), _const_spec((2 * N_MOD, d)),
                  _const_spec((2, d)), _const_spec((2, d)), _const_spec(w.shape)],
        out_specs=[pl.BlockSpec((TM, d), row), pl.BlockSpec((2, TM, d), lambda i: (0, i, 0)),
                   pl.BlockSpec((TM, d), row),
                   pl.BlockSpec((TM // SLAB, d, SLAB), lambda i: (i, 0, 0)),
                   pl.BlockSpec((TM, d), row)],
        out_shape=[jax.ShapeDtypeStruct((n, d), BF16), jax.ShapeDtypeStruct((2, n, d), F32),
                   jax.ShapeDtypeStruct((n, d), BF16), jax.ShapeDtypeStruct((n // SLAB, d, SLAB), BF16),
                   jax.ShapeDtypeStruct((n, d), BF16)],
        compiler_params=_cparams(1),
        name="proj_hgrn",
    )(xs, g, mod12, lla, l1m, w)


def _diff_attn_kernel(lam_ref, q_ref, k_ref, vt_ref, g_ref, o_ref, m_ref, acc_ref,
                      sa_ref, sb_ref, ma_ref, mb_ref, *, tq, tk, l, lc, n, unroll, post_scale):
    lane = lax.broadcasted_iota(jnp.int32, (1, LANES), 1)
    sub = 2 * vt_ref.shape[2]
    n_chunks = l // tk

    def tile_rows(t):
        if isinstance(t, int):
            return slice(t * tq, (t + 1) * tq)
        return pl.ds(pl.multiple_of(t * tq, tq), tq)

    def stacked(t):
        q = q_ref[tile_rows(t), :]
        zero = jnp.zeros_like(q)
        return jnp.concatenate([jnp.where(lane < HEAD_DIM, q, zero),
                                jnp.where(lane >= HEAD_DIM, q, zero)], axis=0)

    def scores(kc, qcat):
        return lax.dot_general(kc, qcat, NT_DIMS, preferred_element_type=F32)

    def pv(pair0, n_pairs, p):
        out = None
        for j in range(n_pairs):
            s0 = 2 * (pair0 + j)
            vt = jnp.concatenate([vt_ref[s0], vt_ref[s0 + 1]], axis=1)
            t = jnp.dot(vt, p[j * sub:(j + 1) * sub, :], preferred_element_type=F32)
            out = t if out is None else out + t
        return out

    def absorb_context(qcat, slot):
        s = scores(k_ref[l:l + lc, :], qcat)
        mx = jnp.max(s, axis=0, keepdims=True)
        m_ref[slot] = mx
        acc_ref[slot] = pv(l // sub, lc // sub, jnp.exp2(s - mx).astype(BF16))

    def qk(c, qcat, s_ref, mx_ref):
        off = pl.multiple_of(c * tk, LANES)
        s = scores(k_ref[pl.ds(off, tk), :], qcat)
        s_ref[...] = s
        mx_ref[...] = jnp.max(s, axis=0, keepdims=True)

    def absorb(c, s_ref, mx_ref, slot):
        m_old = m_ref[slot]
        m_new = jnp.maximum(m_old, mx_ref[...])
        alpha = jnp.exp2(m_old - m_new)
        p = jnp.exp2(s_ref[...] - m_new).astype(BF16)
        m_ref[slot] = m_new
        acc_ref[slot] = alpha * acc_ref[slot] + pv((c * tk) // sub, tk // sub, p)

    def start_sweep(qcat, slot):
        qk(0, qcat, sa_ref, ma_ref)
        absorb_context(qcat, slot)

    def finish(t, slot):
        acc = acc_ref[slot]
        acc = acc[0:LANES, :] * (1.0 / acc[LANES:LANES + 1, :])
        o = acc[:, :tq] - lam_ref[0] * acc[:, tq:]
        o = o * lax.rsqrt(jnp.mean(o * o, axis=0, keepdims=True) + RMS_EPS)
        o = o * (g_ref[...] * post_scale)
        o_ref[tile_rows(t), :] = o.T.astype(BF16)

    bufs = ((sa_ref, ma_ref), (sb_ref, mb_ref))

    def sweep(t, slot):
        qcat = stacked(t)

        def group(c0, count):
            for j in range(count):
                qk(c0 + j + 1, qcat, *bufs[(j + 1) % 2])
                absorb(c0 + j, *bufs[j % 2], slot)

        def body(i, inner):
            group(unroll * i, unroll)
            return inner

        lax.fori_loop(0, n_chunks // unroll - 1, body, 0)
        group(n_chunks - unroll, unroll - 1)
        start_sweep(stacked(t + 1), 1 - slot)
        absorb(n_chunks - 1, *bufs[(unroll - 1) % 2], slot)
        finish(t, slot)

    def tile_pair(i, carry):
        sweep(2 * i, 0)
        sweep(2 * i + 1, 1)
        return carry

    start_sweep(stacked(0), 0)
    lax.fori_loop(0, l // (2 * tq), tile_pair, 0)
    for t in range(l // tq, n // tq):
        absorb_context(stacked(t), 0)
        finish(t, 0)


def _diff_attn_call(lam, q, k, vt3, subln_g, l, lc, post_scale):
    n, d = q.shape
    nh = d // LANES
    nslab, vrows, slab = vt3.shape
    vrows //= nh
    tq = DA_TQ
    tk = min(DA_TK, l // 2)
    assert l % (2 * tq) == 0 and n % tq == 0 and n > l and tk % (2 * slab) == 0
    assert l % (2 * slab) == 0 and lc % (2 * slab) == 0
    assert l % (2 * tk) == 0
    return pl.pallas_call(
        functools.partial(_diff_attn_kernel, tq=tq, tk=tk, l=l, lc=lc, n=n,
                          unroll=next(u for u in DA_UNROLLS if l % (u * tk) == 0),
                          post_scale=post_scale),
        grid=(nh,),
        in_specs=[pl.BlockSpec(memory_space=pltpu.SMEM),
                  pl.BlockSpec((n, LANES), lambda h: (0, h)),
                  pl.BlockSpec((n, LANES), lambda h: (0, h)),
                  pl.BlockSpec((nslab, vrows, slab), lambda h: (0, h, 0)),
                  pl.BlockSpec((LANES, 1), lambda h: (0, 0))],
        out_specs=pl.BlockSpec((n, LANES), lambda h: (0, h)),
        out_shape=jax.ShapeDtypeStruct((n, d), BF16),
        scratch_shapes=[pltpu.VMEM((2, 1, 2 * tq), F32),
                        pltpu.VMEM((2, vrows, 2 * tq), F32),
                        pltpu.VMEM((tk, 2 * tq), F32), pltpu.VMEM((tk, 2 * tq), F32),
                        pltpu.VMEM((1, 2 * tq), F32), pltpu.VMEM((1, 2 * tq), F32)],
        compiler_params=_cparams(1),
        name="diff_attention",
    )(lam, q, k, vt3, subln_g)


def _win_attn_kernel(sink_ref, q_ref, kc_ref, kp_ref, ko_ref, kn_ref,
                     vc_ref, vp_ref, vo_ref, vn_ref, o_ref, *, tq, l, lc, rep, ng):
    j = pl.program_id(0)
    w = WINDOW
    gw = rep * HEAD_DIM
    a = lax.broadcasted_iota(jnp.int32, (w, w), 0)
    b = lax.broadcasted_iota(jnp.int32, (w, w), 1)
    lane = lax.broadcasted_iota(jnp.int32, (1, LANES), 1)
    low = lane < HEAD_DIM
    biases = []
    for t in range(tq // w):
        q0 = j * tq + t * w
        latent = q0 < l
        own_bias = jnp.where(latent, 0.0, NEG_BIG)
        prev_bias = jnp.where(jnp.logical_and(latent, q0 >= w), 0.0, NEG_BIG)
        next_bias = jnp.where(q0 + w < l, 0.0, NEG_BIG)
        bias = jnp.concatenate(
            [jnp.where(a >= b, prev_bias, NEG_BIG),
             jnp.full((w, w), own_bias, F32),
             jnp.where(a <= b, next_bias, NEG_BIG)], axis=0)
        biases.append(jnp.concatenate([bias] * rep, axis=1))
    for g in range(ng):
        gl = slice(g * LANES, (g + 1) * LANES)
        kc = kc_ref[:, gl]
        kwin = jnp.concatenate([kp_ref[:, gl], ko_ref[:, gl], kn_ref[:, gl]], axis=0)
        vtc = jnp.concatenate([vc_ref[s, gl, :] for s in range(lc // w)], axis=1)
        vtwin = jnp.concatenate([vp_ref[0, gl, :]] + [vo_ref[s, gl, :] for s in range(tq // w)]
                                + [vn_ref[0, gl, :]], axis=1)
        for t in range(tq // w):
            qs, sinks = [], []
            for r in range(rep):
                c0 = g * gw + (r // 2) * LANES
                qc = q_ref[t * w:(t + 1) * w, c0:c0 + LANES]
                qs.append(jnp.where(low if r % 2 == 0 else ~low, qc, jnp.zeros_like(qc)))
                sinks.append(jnp.full((1, w), sink_ref[g * rep + r] * LOG2E, F32))
            qst = jnp.concatenate(qs, axis=0)
            sink = jnp.concatenate(sinks, axis=1)
            s_c = lax.dot_general(kc, qst, NT_DIMS, preferred_element_type=F32)
            s_w = (lax.dot_general(kwin[t * w:(t + 3) * w, :], qst, NT_DIMS,
                                   preferred_element_type=F32) + biases[t])
            mx = jnp.maximum(jnp.maximum(jnp.max(s_c, axis=0, keepdims=True),
                                         jnp.max(s_w, axis=0, keepdims=True)), sink)
            e_c = jnp.exp2(s_c - mx)
            e_w = jnp.exp2(s_w - mx)
            den = (jnp.exp2(sink - mx) + jnp.sum(e_c, axis=0, keepdims=True)
                   + jnp.sum(e_w, axis=0, keepdims=True))
            ot = (jnp.dot(vtc, e_c.astype(BF16), preferred_element_type=F32)
                  + jnp.dot(vtwin[:, t * w:(t + 3) * w], e_w.astype(BF16),
                            preferred_element_type=F32)) * (1.0 / den)
            heads = [ot[:, r * w:(r + 1) * w].T for r in range(rep)]
            for c in range(rep // 2):
                c0 = g * gw + c * LANES
                o_ref[t * w:(t + 1) * w, c0:c0 + LANES] = jnp.where(
                    low, heads[2 * c], heads[2 * c + 1]).astype(BF16)


def _win_attn_call(sinks, q, k2, vt2, l, lc):
    n, d = q.shape
    ng = WA_KV_HEADS
    rep = d // HEAD_DIM // ng
    kw = k2.shape[1]
    tq = WA_TQ
    assert n % tq == 0 and tq % WINDOW == 0 and lc % WINDOW == 0 and l % lc == 0 and rep % 2 == 0
    assert kw == ng * LANES
    r = tq // WINDOW
    last = n // WINDOW - 1
    prev = lambda j: jnp.maximum(j * r - 1, 0)
    nxt = lambda j: jnp.minimum((j + 1) * r, last)
    k_specs = [pl.BlockSpec((lc, kw), lambda j: (l // lc, 0)),
               pl.BlockSpec((WINDOW, kw), lambda j: (prev(j), 0)),
               pl.BlockSpec((tq, kw), lambda j: (j, 0)),
               pl.BlockSpec((WINDOW, kw), lambda j: (nxt(j), 0))]
    vt_specs = [pl.BlockSpec((lc // WINDOW, kw, WINDOW), lambda j: (l // lc, 0, 0)),
                pl.BlockSpec((1, kw, WINDOW), lambda j: (prev(j), 0, 0)),
                pl.BlockSpec((r, kw, WINDOW), lambda j: (j, 0, 0)),
                pl.BlockSpec((1, kw, WINDOW), lambda j: (nxt(j), 0, 0))]
    return pl.pallas_call(
        functools.partial(_win_attn_kernel, tq=tq, l=l, lc=lc, rep=rep, ng=ng),
        grid=(n // tq,),
        in_specs=[pl.BlockSpec(memory_space=pltpu.SMEM),
                  pl.BlockSpec((tq, d), lambda j: (j, 0))] + k_specs + vt_specs,
        out_specs=pl.BlockSpec((tq, d), lambda j: (j, 0)),
        out_shape=jax.ShapeDtypeStruct((n, d), BF16),
        compiler_params=_cparams(1),
        name="window_attention",
    )(sinks, q, k2, k2, k2, k2, vt2, vt2, vt2, vt2)


def _hgrn_consts(chunk):
    n_lv = int(math.log2(chunk))
    walls, masks_all = [], []
    for direction in range(2):
        tau = np.arange(chunk) if direction == 0 else chunk - 1 - np.arange(chunk)
        tr, tc = tau[:, None], tau[None, :]
        blocks = [tc <= tr]
        masks = []
        for lv in range(n_lv):
            h = chunk >> (lv + 1)
            mid = (tr // (2 * h)) * 2 * h + h - 1
            second_r = (tr % (2 * h)) >= h
            second_c = (tc % (2 * h)) >= h
            blocks.append(np.where(second_r, (tc > mid) & (tc <= tr), (tc > tr) & (tc <= mid)))
            masks.append(second_r & ~second_c & ((tr // (2 * h)) == (tc // (2 * h))))
        masks.append(tr == tc)
        w = np.concatenate(blocks, axis=0).astype(np.float32)
        walls.append(np.concatenate([w, w], axis=1))
        masks_all.append(np.stack(masks).astype(np.float32))
    return np.stack(walls), np.stack(masks_all)


def _hgrn_kernel(wall_ref, mask_ref, q_ref, g_ref, v_ref, vt_ref, o_ref, st_ref,
                 *, chunk, n_lv, n_heads, n_steps, per_step):
    dr = pl.program_id(0)
    step = pl.program_id(1)

    @pl.when(step == 0)
    def _():
        st_ref[...] = jnp.zeros_like(st_ref)

    @pl.when(step >= n_steps)
    def _():
        o_ref[...] = jnp.zeros_like(o_ref)

    @pl.when(step < n_steps)
    def _():
        heads = [slice(hh * LANES, (hh + 1) * LANES) for hh in range(n_heads)]
        nt = lambda x, y: lax.dot_general(x, y, NT_DIMS, preferred_element_type=F32)
        for j in range(per_step):
            idx = jnp.where(dr == 0, j, per_step - 1 - j)
            rows = pl.ds(pl.multiple_of(idx * chunk, chunk), chunk)
            g_all = g_ref[0, rows, :]
            g_hi = g_all.astype(BF16)
            g_lo = (g_all - g_hi.astype(F32)).astype(BF16)
            sums_all = jnp.dot(wall_ref[0], jnp.concatenate([g_hi, g_lo], axis=0),
                               preferred_element_type=F32)
            q16 = q_ref[rows, :]
            q_all = q16.astype(F32)
            k_all = 1.0 - jnp.exp(g_all)
            k16 = k_all.astype(BF16)
            v_all = v_ref[rows, :]
            vt_all = vt_ref[idx]
            b = sums_all[0:chunk]
            b_end = jnp.sum(g_all, axis=0, keepdims=True)

            qb = (q_all * jnp.exp(b)).astype(BF16)
            o = [nt(qb[:, sl], st_ref[hh].astype(BF16)) for hh, sl in enumerate(heads)]
            att = [nt(q16[:, sl], k16[:, sl]) * mask_ref[0, n_lv] for sl in heads]
            for lv in range(n_lv):
                e = jnp.exp(sums_all[(1 + lv) * chunk:(2 + lv) * chunk])
                qe = (q_all * e).astype(BF16)
                ke = (k_all * e).astype(BF16)
                mask = mask_ref[0, lv]
                att = [a + nt(qe[:, sl], ke[:, sl]) * mask for a, sl in zip(att, heads)]
            out = [o[hh] + jnp.dot(att[hh].astype(BF16), v_all[:, sl], preferred_element_type=F32)
                   for hh, sl in enumerate(heads)]
            o_ref[0, rows, :] = jnp.concatenate(out, axis=1)

            ks = (k_all * jnp.exp(b_end - b)).astype(BF16)
            decay = jnp.exp(b_end)
            for hh, sl in enumerate(heads):
                st_ref[hh] = st_ref[hh] * decay[:, sl] + jnp.dot(vt_all[sl, :], ks[:, sl],
                                                                 preferred_element_type=F32)


def _hgrn_call(qh, gates, v, vt3, l, lc):
    n, d = qh.shape
    c = HG_CHUNK
    per = HG_PER_STEP
    rows = per * c
    nh = d // LANES
    nx, ncc, ntot = l // rows, lc // rows, n // rows
    nc = nx + ncc
    assert n % rows == 0 and l % rows == 0 and lc % rows == 0 and vt3.shape[2] == c
    n_lv = int(math.log2(c))
    wall_np, mask_np = _hgrn_consts(c)
    wall = jnp.asarray(wall_np, dtype=BF16)
    masks = jnp.asarray(mask_np, dtype=F32)

    def blk(dr, s):
        fwd = jnp.where(s < ncc, nx + s, s - ncc)
        bwd = jnp.where(s < ncc, nx + ncc - 1 - s, nx - 1 - (s - ncc))
        return jnp.where(s >= nc, s, jnp.where(dr == 0, fwd, bwd))

    return pl.pallas_call(
        functools.partial(_hgrn_kernel, chunk=c, n_lv=n_lv, n_heads=nh, n_steps=nc, per_step=per),
        grid=(2, ntot),
        in_specs=[pl.BlockSpec((1,) + wall_np.shape[1:], lambda dr, s: (dr, 0, 0)),
                  pl.BlockSpec((1,) + mask_np.shape[1:], lambda dr, s: (dr, 0, 0, 0)),
                  pl.BlockSpec((rows, d), lambda dr, s: (blk(dr, s), 0)),
                  pl.BlockSpec((1, rows, d), lambda dr, s: (dr, blk(dr, s), 0)),
                  pl.BlockSpec((rows, d), lambda dr, s: (blk(dr, s), 0)),
                  pl.BlockSpec((per, d, c), lambda dr, s: (blk(dr, s), 0, 0))],
        out_specs=pl.BlockSpec((1, rows, d), lambda dr, s: (dr, blk(dr, s), 0)),
        out_shape=jax.ShapeDtypeStruct((2, n, d), F32),
        scratch_shapes=[pltpu.VMEM((nh, LANES, LANES), F32)],
        compiler_params=_cparams(2),
        name="hgrn_scan",
    )(wall, masks, qh, gates, v, vt3)


def _post_kernel(*refs, mode, final, tm, l, lc, d, f, n_x):
    it = iter(refs)
    x_refs = [next(it) for _ in range(n_x)]
    mod_ref, g2_ref, wo_ref, win_ref, wout_ref = (next(it) for _ in range(5))
    if mode == "plain":
        a_ref = next(it)
    elif mode == "conv":
        b_ref, z_ref, zp_ref, zn_ref, cw_ref = (next(it) for _ in range(5))
    else:
        ofw_ref, obw_ref, og_ref, gn_ref = (next(it) for _ in range(4))
    fin_ref = next(it) if final else None
    o_ref = next(it)

    ti = pl.program_id(0)
    is_ctx = _is_ctx_rows(ti, tm, l)
    if mode == "plain":
        a = a_ref[...]
    elif mode == "conv":
        z = z_ref[...]
        rl = lax.broadcasted_iota(jnp.int32, (tm, 1), 0)
        row = ti * tm + rl
        z_prev = jnp.where(rl == 0, zp_ref[7:8, :], pltpu.roll(z, 1, 0))
        z_next = jnp.where(rl == tm - 1, zn_ref[0:1, :], pltpu.roll(z, tm - 1, 0))
        z_prev = jnp.where((row == 0) | (row == l), 0.0, z_prev)
        z_next = jnp.where((row == l - 1) | (row == l + lc - 1), 0.0, z_next)
        conv = cw_ref[0:1, :] * z_prev + cw_ref[1:2, :] * z + cw_ref[2:3, :] * z_next
        a = (b_ref[...].astype(F32) * conv).astype(BF16)
    else:
        gn = gn_ref[...]
        heads = []
        for hh in range(d // LANES):
            sl = slice(hh * LANES, (hh + 1) * LANES)
            oh = _rms(ofw_ref[0, :, sl] + obw_ref[0, :, sl])
            heads.append((oh * gn * og_ref[:, sl].astype(F32)).astype(BF16))
        a = jnp.concatenate(heads, axis=1)

    y = jnp.dot(a, wo_ref[...], preferred_element_type=F32)
    x = _stream_tile(x_refs, ti, tm, l) + _pick(mod_ref, 2, is_ctx) * y
    h = _norm_mod(x, g2_ref[...], _pick(mod_ref, 3, is_ctx), _pick(mod_ref, 4, is_ctx)).astype(BF16)
    acc = jnp.zeros((tm, d), F32)
    for c0 in range(0, f, FFN_CHUNK):
        u = jnp.dot(h, win_ref[:, c0:c0 + FFN_CHUNK], preferred_element_type=F32)
        w = jnp.dot(h, win_ref[:, f + c0:f + c0 + FFN_CHUNK], preferred_element_type=F32)
        act = (_silu(u) * w).astype(BF16)
        acc = acc + jnp.dot(act, wout_ref[c0:c0 + FFN_CHUNK, :], preferred_element_type=F32)
    x = x + _pick(mod_ref, 5, is_ctx) * acc
    if final:
        x = _rms(x) * fin_ref[...]
    o_ref[...] = x


def _post_call(xs, mod12, g2, wo, win, wout, mixer_inputs, mode, l, lc, final_g=None):
    x_specs, x_args, n, d = _stream_specs(xs, l)
    f = wout.shape[0]
    final = final_g is not None
    assert f % FFN_CHUNK == 0 and l % TM == 0 and n % TM == 0
    n_out = l if final else n
    row = lambda i: (i, 0)
    in_specs = x_specs + [_const_spec((2 * N_MOD, d)), _const_spec((1, d)),
                          _const_spec(wo.shape), _const_spec(win.shape), _const_spec(wout.shape)]
    args = x_args + [mod12, g2, wo, win, wout]
    if mode == "plain":
        in_specs += [pl.BlockSpec((TM, d), row)]
    elif mode == "conv":
        r8 = TM // 8
        last8 = n // 8 - 1
        in_specs += [pl.BlockSpec((TM, d), row), pl.BlockSpec((TM, d), row),
                     pl.BlockSpec((8, d), lambda i: (jnp.maximum(i * r8 - 1, 0), 0)),
                     pl.BlockSpec((8, d), lambda i: (jnp.minimum((i + 1) * r8, last8), 0)),
                     _const_spec((SC_WIDTH, d))]
    else:
        in_specs += [pl.BlockSpec((1, TM, d), lambda i: (0, i, 0)),
                     pl.BlockSpec((1, TM, d), lambda i: (1, i, 0)),
                     pl.BlockSpec((TM, d), row), _const_spec((1, LANES))]
    args += list(mixer_inputs)
    if final:
        in_specs += [_const_spec((1, d))]
        args += [final_g]
    return pl.pallas_call(
        functools.partial(_post_kernel, mode=mode, final=final, tm=TM, l=l, lc=lc, d=d, f=f,
                          n_x=len(x_args)),
        grid=(n_out // TM,),
        in_specs=in_specs,
        out_specs=pl.BlockSpec((TM, d), row),
        out_shape=jax.ShapeDtypeStruct((n_out, d), F32),
        compiler_params=_cparams(1),
        name="post_" + mode + ("_final" if final else ""),
    )(*args)


def _rope_tables(l, n):
    rows = l // GRID_W
    r = jnp.repeat(jnp.arange(rows), GRID_W).astype(F32)
    c = jnp.tile(jnp.arange(GRID_W), rows).astype(F32)
    half = HEAD_DIM // 2
    inv_freq = 1.0 / (ROPE_BASE ** (jnp.arange(0, half, 2, dtype=F32) / half))
    ar, ac = r[:, None] * inv_freq, c[:, None] * inv_freq
    cs = jnp.concatenate([jnp.cos(ar), jnp.cos(ar), jnp.cos(ac), jnp.cos(ac)], axis=-1)
    sn = jnp.concatenate([-jnp.sin(ar), jnp.sin(ar), -jnp.sin(ac), jnp.sin(ac)], axis=-1)
    cs = jnp.concatenate([cs, jnp.ones((n - l, HEAD_DIM), F32)], axis=0)
    sn = jnp.concatenate([sn, jnp.zeros((n - l, HEAD_DIM), F32)], axis=0)
    return jnp.tile(cs, (1, 2)), jnp.tile(sn, (1, 2))


def kernel(x, c, ctx, c_ctx, ada_w, ada_b, norm1_g, norm2_g, ffn_w_in, ffn_w_out, final_g,
           da_wqkv, da_lambda, da_subln_g, da_wo, sc_w_in, sc_conv_w, sc_w_out,
           wa_wqkv, wa_sinks, wa_wo, hg_w_in, hg_lb, hg_gnorm_g, hg_wo):
    b, l, d = x.shape
    lc = ctx.shape[1]
    depth = ada_w.shape[0]
    assert b == 1 and d % LANES == 0 and l % TM == 0
    n = -(-(l + lc) // TM) * TM
    xs = (x[0], jnp.concatenate([ctx[0], jnp.zeros((n - l - lc, d), x.dtype)], axis=0))

    cc = jnp.zeros((8, d), F32).at[0].set(c_ctx).at[1].set(c[0])
    mods = _mod_call(cc, ada_w, ada_b)
    cs, sn = _rope_tables(l, n)

    n_mix = 4
    for i in range(depth):
        mixer, slot = i % n_mix, i // n_mix
        last = i == depth - 1
        mod12 = jnp.concatenate([mods[i, 0].reshape(N_MOD, d), mods[i, 1].reshape(N_MOD, d)], axis=0)
        g1 = norm1_g[i].reshape(1, d)
        g2 = norm2_g[i].reshape(1, d)
        win = ffn_w_in[i].astype(BF16)
        wout = ffn_w_out[i].astype(BF16)
        fin = final_g.reshape(1, d) if last else None
        if isinstance(xs, tuple) and mixer != 0:
            xs = jnp.concatenate(xs, axis=0)

        if mixer == 0:
            w = da_wqkv[slot]
            lam = da_lambda[slot].astype(F32)
            lam_init = 0.8 - 0.6 * math.exp(-0.3 * i)
            lam_full = (jnp.exp(jnp.sum(lam[0] * lam[1])) - jnp.exp(jnp.sum(lam[2] * lam[3]))
                        + lam_init).reshape(1)
            q, k, vt3 = _proj_da_call(xs, g1, mod12, cs, sn, w[:, :2 * d].astype(BF16),
                                      w[:, 2 * d:].T.astype(BF16), l)
            a = _diff_attn_call(lam_full, q, k, vt3, da_subln_g[slot].reshape(LANES, 1), l, lc,
                                1.0 - lam_init)
            xs = _post_call(xs, mod12, g2, da_wo[slot].astype(BF16), win, wout, [a], "plain",
                            l, lc, fin)
        elif mixer == 1:
            bg, z = _proj_sc_call(xs, g1, mod12, sc_w_in[slot].astype(BF16), l)
            xs = _post_call(xs, mod12, g2, sc_w_out[slot].astype(BF16), win, wout,
                            [bg, z, z, z, sc_conv_w[slot]], "conv", l, lc, fin)
        elif mixer == 2:
            w = wa_wqkv[slot]
            kw = WA_KV_HEADS * HEAD_DIM
            dup = lambda m: jnp.tile(m.reshape(d, WA_KV_HEADS, 1, HEAD_DIM), (1, 1, 2, 1)).reshape(d, 2 * kw)
            w2 = jnp.concatenate([w[:, :d], dup(w[:, d:d + kw]), dup(w[:, d + kw:])], axis=1)
            q, k2, vt2 = _proj_wa_call(xs, g1, mod12, cs, sn, w2.astype(BF16), l, 2 * kw)
            a = _win_attn_call(wa_sinks[slot].astype(F32), q, k2, vt2, l, lc)
            xs = _post_call(xs, mod12, g2, wa_wo[slot].astype(BF16), win, wout, [a], "plain",
                            l, lc, fin)
        else:
            w = hg_w_in[slot]
            p = jax.nn.softmax(hg_lb.astype(F32), axis=1)
            lb = (jnp.cumsum(p, axis=1) - p[:, :1])[:, i]
            qh, gates, v, vt3, og = _proj_hg_call(xs, g1, mod12, jnp.log(lb), jnp.log1p(-lb),
                                                  w.astype(BF16), l)
            o2 = _hgrn_call(qh, gates, v, vt3, l, lc)
            xs = _post_call(xs, mod12, g2, hg_wo[slot].astype(BF16), win, wout,
                            [o2, o2, og, hg_gnorm_g[slot].reshape(1, LANES)], "hgrn", l, lc, fin)
    return xs[:l].reshape(1, l, d)
```

```python
import functools
import math

import numpy as np
import jax
import jax.numpy as jnp
from jax import lax
from jax.experimental import pallas as pl
from jax.experimental.pallas import tpu as pltpu

F32 = jnp.float32
BF16 = jnp.bfloat16

HEAD_DIM = 64
GRID_W = 64
ROPE_BASE = 10000.0
RMS_EPS = 1e-6
N_MOD = 6
WINDOW = 128
WA_KV_HEADS = 4
SC_WIDTH = 3
LANES = 128
NEG_BIG = -1e30
LOG2E = math.log2(math.e)

TM = 512
SLAB = 128
HG_CHUNK = 128
HG_PER_STEP = 2
DA_TQ = 256
DA_TK = 1024
DA_ONES_ROWS = 16
DA_UNROLLS = (4, 2)
WA_TQ = 512
FFN_CHUNK = 256
VMEM_LIMIT = 56 * 1024 * 1024

NT_DIMS = (((1,), (1,)), ((), ()))


def _cparams(n_axes):
    return pltpu.CompilerParams(
        dimension_semantics=("arbitrary",) * n_axes, vmem_limit_bytes=VMEM_LIMIT)


def _const_spec(shape):
    nd = len(shape)
    return pl.BlockSpec(shape, lambda *_: (0,) * nd, pipeline_mode=pl.Buffered(1))


def _is_ctx_rows(tile_idx, tm, l):
    row = tile_idx * tm + lax.broadcasted_iota(jnp.int32, (tm, 1), 0)
    return row >= l


def _pick(mod_ref, k, is_ctx):
    return jnp.where(is_ctx, mod_ref[k:k + 1, :], mod_ref[N_MOD + k:N_MOD + k + 1, :])


def _rms(x):
    return x * lax.rsqrt(jnp.mean(x * x, axis=-1, keepdims=True) + RMS_EPS)


def _norm_mod(x, g, shift, scale):
    return (_rms(x) * g) * (1.0 + scale) + shift


def _silu(x):
    h = 0.5 * x
    return h + h * jnp.tanh(h)


def _rope(chunk, cs, sn, first_half):
    partner = jnp.where(first_half, pltpu.roll(chunk, LANES - 16, 1), pltpu.roll(chunk, 16, 1))
    return chunk * cs + partner * sn


def _stream_specs(xs, l):
    if not isinstance(xs, tuple):
        n, d = xs.shape
        return [pl.BlockSpec((TM, d), lambda i: (i, 0))], [xs], n, d
    lat, tail = xs
    d = lat.shape[1]
    nl = l // TM
    assert lat.shape[0] == l and tail.shape[0] % TM == 0
    return ([pl.BlockSpec((TM, d), lambda i: (jnp.minimum(i, nl - 1), 0)),
             pl.BlockSpec((TM, d), lambda i: (jnp.maximum(i - nl, 0), 0))],
            [lat, tail], l + tail.shape[0], d)


def _stream_tile(x_refs, tile_idx, tm, l):
    if len(x_refs) == 1:
        return x_refs[0][...]
    return jnp.where(tile_idx * tm < l, x_refs[0][...], x_refs[1][...])


def _first_half_lanes():
    lane = lax.broadcasted_iota(jnp.int32, (1, LANES), 1)
    return (lane & 31) < 16


def _mod_kernel(cc_ref, w_ref, b_ref, o_ref):
    a = _silu(cc_ref[...])
    o_ref[0] = jnp.dot(a, w_ref[0], precision=lax.Precision.HIGHEST,
                       preferred_element_type=F32) + b_ref[0]


def _mod_call(cc, ada_w, ada_b):
    depth, d, nd = ada_w.shape
    tn = 1536
    assert nd % tn == 0
    return pl.pallas_call(
        _mod_kernel,
        grid=(depth, nd // tn),
        in_specs=[pl.BlockSpec((8, d), lambda i, j: (0, 0)),
                  pl.BlockSpec((1, d, tn), lambda i, j: (i, 0, j)),
                  pl.BlockSpec((1, 1, tn), lambda i, j: (i, 0, j))],
        out_specs=pl.BlockSpec((1, 8, tn), lambda i, j: (i, 0, j)),
        out_shape=jax.ShapeDtypeStruct((depth, 8, nd), F32),
        compiler_params=_cparams(2),
        name="adaln_mod",
    )(cc, ada_w, ada_b.reshape(depth, 1, nd))


def _proj_da_kernel(*refs, tm, l, d, n_x):
    x_refs = refs[:n_x]
    g_ref, mod_ref, cs_ref, sn_ref, wqk_ref, wvt_ref, q_ref, k_ref, vt_ref = refs[n_x:]
    is_ctx = _is_ctx_rows(pl.program_id(0), tm, l)
    x = _stream_tile(x_refs, pl.program_id(0), tm, l)
    h = _norm_mod(x, g_ref[...], _pick(mod_ref, 0, is_ctx), _pick(mod_ref, 1, is_ctx))
    hb = h.astype(BF16)
    qk = jnp.dot(hb, wqk_ref[...], preferred_element_type=F32)
    cs, sn, first = cs_ref[...], sn_ref[...], _first_half_lanes()
    nq = d // LANES
    for j in range(2 * nq):
        r = _rope(qk[:, j * LANES:(j + 1) * LANES], cs, sn, first)
        if j < nq:
            q_ref[:, j * LANES:(j + 1) * LANES] = (r * (LOG2E * HEAD_DIM ** -0.5)).astype(BF16)
        else:
            k_ref[:, (j - nq) * LANES:(j - nq + 1) * LANES] = r.astype(BF16)
    vt = lax.dot_general(wvt_ref[...], hb, NT_DIMS, preferred_element_type=F32)
    ones = jnp.ones((DA_ONES_ROWS, SLAB), BF16)
    vr = LANES + DA_ONES_ROWS
    for s in range(tm // SLAB):
        for hh in range(d // LANES):
            vt_ref[s, hh * vr:hh * vr + LANES, :] = (
                vt[hh * LANES:(hh + 1) * LANES, s * SLAB:(s + 1) * SLAB].astype(BF16))
            vt_ref[s, hh * vr + LANES:(hh + 1) * vr, :] = ones


def _proj_da_call(xs, g, mod12, cs, sn, wqk, wvt, l):
    x_specs, x_args, n, d = _stream_specs(xs, l)
    row = lambda i: (i, 0)
    dv = d // LANES * (LANES + DA_ONES_ROWS)
    return pl.pallas_call(
        functools.partial(_proj_da_kernel, tm=TM, l=l, d=d, n_x=len(x_args)),
        grid=(n // TM,),
        in_specs=x_specs + [_const_spec((1, d)), _const_spec((2 * N_MOD, d)),
                            pl.BlockSpec((TM, LANES), row), pl.BlockSpec((TM, LANES), row),
                            _const_spec(wqk.shape), _const_spec(wvt.shape)],
        out_specs=[pl.BlockSpec((TM, d), row), pl.BlockSpec((TM, d), row),
                   pl.BlockSpec((TM // SLAB, dv, SLAB), lambda i: (i, 0, 0))],
        out_shape=[jax.ShapeDtypeStruct((n, d), BF16), jax.ShapeDtypeStruct((n, d), BF16),
                   jax.ShapeDtypeStruct((n // SLAB, dv, SLAB), BF16)],
        compiler_params=_cparams(1),
        name="proj_diff_attn",
    )(*x_args, g, mod12, cs, sn, wqk, wvt)


def _proj_sc_kernel(x_ref, g_ref, mod_ref, w_ref, b_ref, z_ref, *, tm, l, d):
    is_ctx = _is_ctx_rows(pl.program_id(0), tm, l)
    h = _norm_mod(x_ref[...], g_ref[...], _pick(mod_ref, 0, is_ctx), _pick(mod_ref, 1, is_ctx))
    y = jnp.dot(h.astype(BF16), w_ref[...], preferred_element_type=F32)
    b_ref[...] = y[:, :d].astype(BF16)
    z_ref[...] = y[:, d:2 * d] * y[:, 2 * d:]


def _proj_sc_call(xs, g, mod12, w, l):
    n, d = xs.shape
    row = lambda i: (i, 0)
    return pl.pallas_call(
        functools.partial(_proj_sc_kernel, tm=TM, l=l, d=d),
        grid=(n // TM,),
        in_specs=[pl.BlockSpec((TM, d), row), _const_spec((1, d)), _const_spec((2 * N_MOD, d)),
                  _const_spec(w.shape)],
        out_specs=[pl.BlockSpec((TM, d), row), pl.BlockSpec((TM, d), row)],
        out_shape=[jax.ShapeDtypeStruct((n, d), BF16), jax.ShapeDtypeStruct((n, d), F32)],
        compiler_params=_cparams(1),
        name="proj_short_conv",
    )(xs, g, mod12, w)


def _proj_wa_kernel(x_ref, g_ref, mod_ref, cs_ref, sn_ref, w_ref, q_ref, k_ref, v_ref,
                    *, tm, l, d, kw):
    is_ctx = _is_ctx_rows(pl.program_id(0), tm, l)
    h = _norm_mod(x_ref[...], g_ref[...], _pick(mod_ref, 0, is_ctx), _pick(mod_ref, 1, is_ctx))
    y = jnp.dot(h.astype(BF16), w_ref[...], preferred_element_type=F32)
    cs, sn, first = cs_ref[...], sn_ref[...], _first_half_lanes()
    for j in range(d // LANES):
        r = _rope(y[:, j * LANES:(j + 1) * LANES], cs, sn, first)
        q_ref[:, j * LANES:(j + 1) * LANES] = (r * (LOG2E * HEAD_DIM ** -0.5)).astype(BF16)
    for j in range(kw // LANES):
        r = _rope(y[:, d + j * LANES:d + (j + 1) * LANES], cs, sn, first)
        k_ref[:, j * LANES:(j + 1) * LANES] = r.astype(BF16)
    vt = y[:, d + kw:].T
    for s in range(tm // SLAB):
        v_ref[s] = vt[:, s * SLAB:(s + 1) * SLAB].astype(BF16)


def _proj_wa_call(xs, g, mod12, cs, sn, w, l, kw):
    n, d = xs.shape
    row = lambda i: (i, 0)
    return pl.pallas_call(
        functools.partial(_proj_wa_kernel, tm=TM, l=l, d=d, kw=kw),
        grid=(n // TM,),
        in_specs=[pl.BlockSpec((TM, d), row), _const_spec((1, d)), _const_spec((2 * N_MOD, d)),
                  pl.BlockSpec((TM, LANES), row), pl.BlockSpec((TM, LANES), row),
                  _const_spec(w.shape)],
        out_specs=[pl.BlockSpec((TM, d), row), pl.BlockSpec((TM, kw), row),
                   pl.BlockSpec((TM // SLAB, kw, SLAB), lambda i: (i, 0, 0))],
        out_shape=[jax.ShapeDtypeStruct((n, d), BF16), jax.ShapeDtypeStruct((n, kw), BF16),
                   jax.ShapeDtypeStruct((n // SLAB, kw, SLAB), BF16)],
        compiler_params=_cparams(1),
        name="proj_window_attn",
    )(xs, g, mod12, cs, sn, w)


def _log1p_unit(x):
    return jnp.log(1.0 + x)


def _proj_hg_kernel(x_ref, g_ref, mod_ref, lla_ref, l1m_ref, w_ref,
                    q_ref, gate_ref, v_ref, vt_ref, og_ref, *, tm, l, d):
    is_ctx = _is_ctx_rows(pl.program_id(0), tm, l)
    h = _norm_mod(x_ref[...], g_ref[...], _pick(mod_ref, 0, is_ctx), _pick(mod_ref, 1, is_ctx))
    hb = h.astype(BF16)
    part = lambda j: jnp.dot(hb, w_ref[:, j * d:(j + 1) * d], preferred_element_type=F32)
    q_ref[...] = _silu(part(0)).astype(BF16)
    for dr in range(2):
        f = part(1 + dr)
        log_sig = jnp.minimum(f, 0.0) - _log1p_unit(jnp.exp(-jnp.abs(f)))
        a = lla_ref[dr:dr + 1, :]
        c = l1m_ref[dr:dr + 1, :] + log_sig
        gate_ref[dr] = jnp.maximum(a, c) + _log1p_unit(jnp.exp(-jnp.abs(a - c)))
    v = part(3)
    v_ref[...] = v.astype(BF16)
    vt = v.T
    for s in range(tm // SLAB):
        vt_ref[s] = vt[:, s * SLAB:(s + 1) * SLAB].astype(BF16)
    og_ref[...] = _silu(part(4)).astype(BF16)


def _proj_hg_call(xs, g, mod12, lla, l1m, w, l):
    n, d = xs.shape
    row = lambda i: (i, 0)
    return pl.pallas_call(
        functools.partial(_proj_hg_kernel, tm=TM, l=l, d=d),
        grid=(n // TM,),
        in_specs=[pl.BlockSpec((TM, d), row), _const_spec((1, d)), _const_spec((2 * N_MOD, d)),
                  _const_spec((2, d)), _const_spec((2, d)), _const_spec(w.shape)],
        out_specs=[pl.BlockSpec((TM, d), row), pl.BlockSpec((2, TM, d), lambda i: (0, i, 0)),
                   pl.BlockSpec((TM, d), row),
                   pl.BlockSpec((TM // SLAB, d, SLAB), lambda i: (i, 0, 0)),
                   pl.BlockSpec((TM, d), row)],
        out_shape=[jax.ShapeDtypeStruct((n, d), BF16), jax.ShapeDtypeStruct((2, n, d), F32),
                   jax.ShapeDtypeStruct((n, d), BF16), jax.ShapeDtypeStruct((n // SLAB, d, SLAB), BF16),
                   jax.ShapeDtypeStruct((n, d), BF16)],
        compiler_params=_cparams(1),
        name="proj_hgrn",
    )(xs, g, mod12, lla, l1m, w)


def _diff_attn_kernel(lam_ref, q_ref, k_ref, vt_ref, g_ref, o_ref, m_ref, acc_ref,
                      sa_ref, sb_ref, ma_ref, mb_ref, *, tq, tk, l, lc, n, unroll, post_scale):
    lane = lax.broadcasted_iota(jnp.int32, (1, LANES), 1)
    sub = 2 * vt_ref.shape[2]
    n_chunks = l // tk

    def tile_rows(t):
        if isinstance(t, int):
            return slice(t * tq, (t + 1) * tq)
        return pl.ds(pl.multiple_of(t * tq, tq), tq)

    def stacked(t):
        q = q_ref[tile_rows(t), :]
        zero = jnp.zeros_like(q)
        return jnp.concatenate([jnp.where(lane < HEAD_DIM, q, zero),
                                jnp.where(lane >= HEAD_DIM, q, zero)], axis=0)

    def scores(kc, qcat):
        return lax.dot_general(kc, qcat, NT_DIMS, preferred_element_type=F32)

    def pv(pair0, n_pairs, p):
        out = None
        for j in range(n_pairs):
            s0 = 2 * (pair0 + j)
            vt = jnp.concatenate([vt_ref[s0], vt_ref[s0 + 1]], axis=1)
            t = jnp.dot(vt, p[j * sub:(j + 1) * sub, :], preferred_element_type=F32)
            out = t if out is None else out + t
        return out

    def absorb_context(qcat, slot):
        s = scores(k_ref[l:l + lc, :], qcat)
        mx = jnp.max(s, axis=0, keepdims=True)
        m_ref[slot] = mx
        acc_ref[slot] = pv(l // sub, lc // sub, jnp.exp2(s - mx).astype(BF16))

    def qk(c, qcat, s_ref, mx_ref):
        off = pl.multiple_of(c * tk, LANES)
        s = scores(k_ref[pl.ds(off, tk), :], qcat)
        s_ref[...] = s
        mx_ref[...] = jnp.max(s, axis=0, keepdims=True)

    def absorb(c, s_ref, mx_ref, slot):
        m_old = m_ref[slot]
        m_new = jnp.maximum(m_old, mx_ref[...])
        alpha = jnp.exp2(m_old - m_new)
        p = jnp.exp2(s_ref[...] - m_new).astype(BF16)
        m_ref[slot] = m_new
        acc_ref[slot] = alpha * acc_ref[slot] + pv((c * tk) // sub, tk // sub, p)

    def start_sweep(qcat, slot):
        qk(0, qcat, sa_ref, ma_ref)
        absorb_context(qcat, slot)

    def finish(t, slot):
        acc = acc_ref[slot]
        acc = acc[0:LANES, :] * (1.0 / acc[LANES:LANES + 1, :])
        o = acc[:, :tq] - lam_ref[0] * acc[:, tq:]
        o = o * lax.rsqrt(jnp.mean(o * o, axis=0, keepdims=True) + RMS_EPS)
        o = o * (g_ref[...] * post_scale)
        o_ref[tile_rows(t), :] = o.T.astype(BF16)

    bufs = ((sa_ref, ma_ref), (sb_ref, mb_ref))

    def sweep(t, slot):
        qcat = stacked(t)

        def group(c0, count):
            for j in range(count):
                qk(c0 + j + 1, qcat, *bufs[(j + 1) % 2])
                absorb(c0 + j, *bufs[j % 2], slot)

        def body(i, inner):
            group(unroll * i, unroll)
            return inner

        lax.fori_loop(0, n_chunks // unroll - 1, body, 0)
        group(n_chunks - unroll, unroll - 1)
        start_sweep(stacked(t + 1), 1 - slot)
        absorb(n_chunks - 1, *bufs[(unroll - 1) % 2], slot)
        finish(t, slot)

    def tile_pair(i, carry):
        sweep(2 * i, 0)
        sweep(2 * i + 1, 1)
        return carry

    start_sweep(stacked(0), 0)
    lax.fori_loop(0, l // (2 * tq), tile_pair, 0)
    for t in range(l // tq, n // tq):
        absorb_context(stacked(t), 0)
        finish(t, 0)


def _diff_attn_call(lam, q, k, vt3, subln_g, l, lc, post_scale):
    n, d = q.shape
    nh = d // LANES
    nslab, vrows, slab = vt3.shape
    vrows //= nh
    tq = DA_TQ
    tk = min(DA_TK, l // 2)
    assert l % (2 * tq) == 0 and n % tq == 0 and n > l and tk % (2 * slab) == 0
    assert l % (2 * slab) == 0 and lc % (2 * slab) == 0
    assert l % (2 * tk) == 0
    return pl.pallas_call(
        functools.partial(_diff_attn_kernel, tq=tq, tk=tk, l=l, lc=lc, n=n,
                          unroll=next(u for u in DA_UNROLLS if l % (u * tk) == 0),
                          post_scale=post_scale),
        grid=(nh,),
        in_specs=[pl.BlockSpec(memory_space=pltpu.SMEM),
                  pl.BlockSpec((n, LANES), lambda h: (0, h)),
                  pl.BlockSpec((n, LANES), lambda h: (0, h)),
                  pl.BlockSpec((nslab, vrows, slab), lambda h: (0, h, 0)),
                  pl.BlockSpec((LANES, 1), lambda h: (0, 0))],
        out_specs=pl.BlockSpec((n, LANES), lambda h: (0, h)),
        out_shape=jax.ShapeDtypeStruct((n, d), BF16),
        scratch_shapes=[pltpu.VMEM((2, 1, 2 * tq), F32),
                        pltpu.VMEM((2, vrows, 2 * tq), F32),
                        pltpu.VMEM((tk, 2 * tq), F32), pltpu.VMEM((tk, 2 * tq), F32),
                        pltpu.VMEM((1, 2 * tq), F32), pltpu.VMEM((1, 2 * tq), F32)],
        compiler_params=_cparams(1),
        name="diff_attention",
    )(lam, q, k, vt3, subln_g)


def _win_attn_kernel(sink_ref, q_ref, kc_ref, kp_ref, ko_ref, kn_ref,
                     vc_ref, vp_ref, vo_ref, vn_ref, o_ref, *, tq, l, lc, rep, ng):
    j = pl.program_id(0)
    w = WINDOW
    gw = rep * HEAD_DIM
    a = lax.broadcasted_iota(jnp.int32, (w, w), 0)
    b = lax.broadcasted_iota(jnp.int32, (w, w), 1)
    lane = lax.broadcasted_iota(jnp.int32, (1, LANES), 1)
    low = lane < HEAD_DIM
    biases = []
    for t in range(tq // w):
        q0 = j * tq + t * w
        latent = q0 < l
        own_bias = jnp.where(latent, 0.0, NEG_BIG)
        prev_bias = jnp.where(jnp.logical_and(latent, q0 >= w), 0.0, NEG_BIG)
        next_bias = jnp.where(q0 + w < l, 0.0, NEG_BIG)
        bias = jnp.concatenate(
            [jnp.where(a >= b, prev_bias, NEG_BIG),
             jnp.full((w, w), own_bias, F32),
             jnp.where(a <= b, next_bias, NEG_BIG)], axis=0)
        biases.append(jnp.concatenate([bias] * rep, axis=1))
    for g in range(ng):
        gl = slice(g * LANES, (g + 1) * LANES)
        kc = kc_ref[:, gl]
        kwin = jnp.concatenate([kp_ref[:, gl], ko_ref[:, gl], kn_ref[:, gl]], axis=0)
        vtc = jnp.concatenate([vc_ref[s, gl, :] for s in range(lc // w)], axis=1)
        vtwin = jnp.concatenate([vp_ref[0, gl, :]] + [vo_ref[s, gl, :] for s in range(tq // w)]
                                + [vn_ref[0, gl, :]], axis=1)
        for t in range(tq // w):
            qs, sinks = [], []
            for r in range(rep):
                c0 = g * gw + (r // 2) * LANES
                qc = q_ref[t * w:(t + 1) * w, c0:c0 + LANES]
                qs.append(jnp.where(low if r % 2 == 0 else ~low, qc, jnp.zeros_like(qc)))
                sinks.append(jnp.full((1, w), sink_ref[g * rep + r] * LOG2E, F32))
            qst = jnp.concatenate(qs, axis=0)
            sink = jnp.concatenate(sinks, axis=1)
            s_c = lax.dot_general(kc, qst, NT_DIMS, preferred_element_type=F32)
            s_w = (lax.dot_general(kwin[t * w:(t + 3) * w, :], qst, NT_DIMS,
                                   preferred_element_type=F32) + biases[t])
            mx = jnp.maximum(jnp.maximum(jnp.max(s_c, axis=0, keepdims=True),
                                         jnp.max(s_w, axis=0, keepdims=True)), sink)
            e_c = jnp.exp2(s_c - mx)
            e_w = jnp.exp2(s_w - mx)
            den = (jnp.exp2(sink - mx) + jnp.sum(e_c, axis=0, keepdims=True)
                   + jnp.sum(e_w, axis=0, keepdims=True))
            ot = (jnp.dot(vtc, e_c.astype(BF16), preferred_element_type=F32)
                  + jnp.dot(vtwin[:, t * w:(t + 3) * w], e_w.astype(BF16),
                            preferred_element_type=F32)) * (1.0 / den)
            heads = [ot[:, r * w:(r + 1) * w].T for r in range(rep)]
            for c in range(rep // 2):
                c0 = g * gw + c * LANES
                o_ref[t * w:(t + 1) * w, c0:c0 + LANES] = jnp.where(
                    low, heads[2 * c], heads[2 * c + 1]).astype(BF16)


def _win_attn_call(sinks, q, k2, vt2, l, lc):
    n, d = q.shape
    ng = WA_KV_HEADS
    rep = d // HEAD_DIM // ng
    kw = k2.shape[1]
    tq = WA_TQ
    assert n % tq == 0 and tq % WINDOW == 0 and lc % WINDOW == 0 and l % lc == 0 and rep % 2 == 0
    assert kw == ng * LANES
    r = tq // WINDOW
    last = n // WINDOW - 1
    prev = lambda j: jnp.maximum(j * r - 1, 0)
    nxt = lambda j: jnp.minimum((j + 1) * r, last)
    k_specs = [pl.BlockSpec((lc, kw), lambda j: (l // lc, 0)),
               pl.BlockSpec((WINDOW, kw), lambda j: (prev(j), 0)),
               pl.BlockSpec((tq, kw), lambda j: (j, 0)),
               pl.BlockSpec((WINDOW, kw), lambda j: (nxt(j), 0))]
    vt_specs = [pl.BlockSpec((lc // WINDOW, kw, WINDOW), lambda j: (l // lc, 0, 0)),
                pl.BlockSpec((1, kw, WINDOW), lambda j: (prev(j), 0, 0)),
                pl.BlockSpec((r, kw, WINDOW), lambda j: (j, 0, 0)),
                pl.BlockSpec((1, kw, WINDOW), lambda j: (nxt(j), 0, 0))]
    return pl.pallas_call(
        functools.partial(_win_attn_kernel, tq=tq, l=l, lc=lc, rep=rep, ng=ng),
        grid=(n // tq,),
        in_specs=[pl.BlockSpec(memory_space=pltpu.SMEM),
                  pl.BlockSpec((tq, d), lambda j: (j, 0))] + k_specs + vt_specs,
        out_specs=pl.BlockSpec((tq, d), lambda j: (j, 0)),
        out_shape=jax.ShapeDtypeStruct((n, d), BF16),
        compiler_params=_cparams(1),
        name="window_attention",
    )(sinks, q, k2, k2, k2, k2, vt2, vt2, vt2, vt2)


def _hgrn_consts(chunk):
    n_lv = int(math.log2(chunk))
    walls, masks_all = [], []
    for direction in range(2):
        tau = np.arange(chunk) if direction == 0 else chunk - 1 - np.arange(chunk)
        tr, tc = tau[:, None], tau[None, :]
        blocks = [tc <= tr]
        masks = []
        for lv in range(n_lv):
            h = chunk >> (lv + 1)
            mid = (tr // (2 * h)) * 2 * h + h - 1
            second_r = (tr % (2 * h)) >= h
            second_c = (tc % (2 * h)) >= h
            blocks.append(np.where(second_r, (tc > mid) & (tc <= tr), (tc > tr) & (tc <= mid)))
            masks.append(second_r & ~second_c & ((tr // (2 * h)) == (tc // (2 * h))))
        masks.append(tr == tc)
        w = np.concatenate(blocks, axis=0).astype(np.float32)
        walls.append(np.concatenate([w, w], axis=1))
        masks_all.append(np.stack(masks).astype(np.float32))
    return np.stack(walls), np.stack(masks_all)


def _hgrn_kernel(wall_ref, mask_ref, q_ref, g_ref, v_ref, vt_ref, o_ref, st_ref,
                 *, chunk, n_lv, n_heads, n_steps, per_step):
    dr = pl.program_id(0)
    step = pl.program_id(1)

    @pl.when(step == 0)
    def _():
        st_ref[...] = jnp.zeros_like(st_ref)

    @pl.when(step >= n_steps)
    def _():
        o_ref[...] = jnp.zeros_like(o_ref)

    def scan_block(order):
        heads = [slice(hh * LANES, (hh + 1) * LANES) for hh in range(n_heads)]
        nt = lambda x, y: lax.dot_general(x, y, NT_DIMS, preferred_element_type=F32)
        for idx in order:
            rows = slice(idx * chunk, (idx + 1) * chunk)
            g_all = g_ref[0, rows, :]
            g_hi = g_all.astype(BF16)
            g_lo = (g_all - g_hi.astype(F32)).astype(BF16)
            sums_all = jnp.dot(wall_ref[0], jnp.concatenate([g_hi, g_lo], axis=0),
                               preferred_element_type=F32)
            q16 = q_ref[rows, :]
            q_all = q16.astype(F32)
            k_all = 1.0 - jnp.exp(g_all)
            k16 = k_all.astype(BF16)
            v_all = v_ref[rows, :]
            vt_all = vt_ref[idx]
            b = sums_all[0:chunk]
            b_end = jnp.sum(g_all, axis=0, keepdims=True)

            qb = (q_all * jnp.exp(b)).astype(BF16)
            o = [nt(qb[:, sl], st_ref[hh].astype(BF16)) for hh, sl in enumerate(heads)]
            att = [nt(q16[:, sl], k16[:, sl]) * mask_ref[0, n_lv] for sl in heads]
            for lv in range(n_lv):
                e = jnp.exp(sums_all[(1 + lv) * chunk:(2 + lv) * chunk])
                qe = (q_all * e).astype(BF16)
                ke = (k_all * e).astype(BF16)
                mask = mask_ref[0, lv]
                att = [a + nt(qe[:, sl], ke[:, sl]) * mask for a, sl in zip(att, heads)]
            out = [o[hh] + jnp.dot(att[hh].astype(BF16), v_all[:, sl], preferred_element_type=F32)
                   for hh, sl in enumerate(heads)]
            o_ref[0, rows, :] = jnp.concatenate(out, axis=1)

            ks = (k_all * jnp.exp(b_end - b)).astype(BF16)
            decay = jnp.exp(b_end)
            for hh, sl in enumerate(heads):
                st_ref[hh] = st_ref[hh] * decay[:, sl] + jnp.dot(vt_all[sl, :], ks[:, sl],
                                                                 preferred_element_type=F32)

    @pl.when(jnp.logical_and(step < n_steps, dr == 0))
    def _():
        scan_block(range(per_step))

    @pl.when(jnp.logical_and(step < n_steps, dr == 1))
    def _():
        scan_block(range(per_step - 1, -1, -1))


def _hgrn_call(qh, gates, v, vt3, l, lc):
    n, d = qh.shape
    c = HG_CHUNK
    per = HG_PER_STEP
    rows = per * c
    nh = d // LANES
    nx, ncc, ntot = l // rows, lc // rows, n // rows
    nc = nx + ncc
    assert n % rows == 0 and l % rows == 0 and lc % rows == 0 and vt3.shape[2] == c
    n_lv = int(math.log2(c))
    wall_np, mask_np = _hgrn_consts(c)
    wall = jnp.asarray(wall_np, dtype=BF16)
    masks = jnp.asarray(mask_np, dtype=F32)

    def blk(dr, s):
        fwd = jnp.where(s < ncc, nx + s, s - ncc)
        bwd = jnp.where(s < ncc, nx + ncc - 1 - s, nx - 1 - (s - ncc))
        return jnp.where(s >= nc, s, jnp.where(dr == 0, fwd, bwd))

    return pl.pallas_call(
        functools.partial(_hgrn_kernel, chunk=c, n_lv=n_lv, n_heads=nh, n_steps=nc, per_step=per),
        grid=(2, ntot),
        in_specs=[pl.BlockSpec((1,) + wall_np.shape[1:], lambda dr, s: (dr, 0, 0)),
                  pl.BlockSpec((1,) + mask_np.shape[1:], lambda dr, s: (dr, 0, 0, 0)),
                  pl.BlockSpec((rows, d), lambda dr, s: (blk(dr, s), 0)),
                  pl.BlockSpec((1, rows, d), lambda dr, s: (dr, blk(dr, s), 0)),
                  pl.BlockSpec((rows, d), lambda dr, s: (blk(dr, s), 0)),
                  pl.BlockSpec((per, d, c), lambda dr, s: (blk(dr, s), 0, 0))],
        out_specs=pl.BlockSpec((1, rows, d), lambda dr, s: (dr, blk(dr, s), 0)),
        out_shape=jax.ShapeDtypeStruct((2, n, d), F32),
        scratch_shapes=[pltpu.VMEM((nh, LANES, LANES), F32)],
        compiler_params=_cparams(2),
        name="hgrn_scan",
    )(wall, masks, qh, gates, v, vt3)


def _post_kernel(*refs, mode, final, tm, l, lc, d, f, n_x):
    it = iter(refs)
    x_refs = [next(it) for _ in range(n_x)]
    mod_ref, g2_ref, wo_ref, win_ref, wout_ref = (next(it) for _ in range(5))
    if mode == "plain":
        a_ref = next(it)
    elif mode == "conv":
        b_ref, z_ref, zp_ref, zn_ref, cw_ref = (next(it) for _ in range(5))
    else:
        ofw_ref, obw_ref, og_ref, gn_ref = (next(it) for _ in range(4))
    fin_ref = next(it) if final else None
    o_ref = next(it)

    ti = pl.program_id(0)
    is_ctx = _is_ctx_rows(ti, tm, l)
    if mode == "plain":
        a = a_ref[...]
    elif mode == "conv":
        z = z_ref[...]
        rl = lax.broadcasted_iota(jnp.int32, (tm, 1), 0)
        row = ti * tm + rl
        z_prev = jnp.where(rl == 0, zp_ref[7:8, :], pltpu.roll(z, 1, 0))
        z_next = jnp.where(rl == tm - 1, zn_ref[0:1, :], pltpu.roll(z, tm - 1, 0))
        z_prev = jnp.where((row == 0) | (row == l), 0.0, z_prev)
        z_next = jnp.where((row == l - 1) | (row == l + lc - 1), 0.0, z_next)
        conv = cw_ref[0:1, :] * z_prev + cw_ref[1:2, :] * z + cw_ref[2:3, :] * z_next
        a = (b_ref[...].astype(F32) * conv).astype(BF16)
    else:
        gn = gn_ref[...]
        heads = []
        for hh in range(d // LANES):
            sl = slice(hh * LANES, (hh + 1) * LANES)
            oh = _rms(ofw_ref[0, :, sl] + obw_ref[0, :, sl])
            heads.append((oh * gn * og_ref[:, sl].astype(F32)).astype(BF16))
        a = jnp.concatenate(heads, axis=1)

    y = jnp.dot(a, wo_ref[...], preferred_element_type=F32)
    x = _stream_tile(x_refs, ti, tm, l) + _pick(mod_ref, 2, is_ctx) * y
    h = _norm_mod(x, g2_ref[...], _pick(mod_ref, 3, is_ctx), _pick(mod_ref, 4, is_ctx)).astype(BF16)
    acc = jnp.zeros((tm, d), F32)
    for c0 in range(0, f, FFN_CHUNK):
        u = jnp.dot(h, win_ref[:, c0:c0 + FFN_CHUNK], preferred_element_type=F32)
        w = jnp.dot(h, win_ref[:, f + c0:f + c0 + FFN_CHUNK], preferred_element_type=F32)
        act = (_silu(u) * w).astype(BF16)
        acc = acc + jnp.dot(act, wout_ref[c0:c0 + FFN_CHUNK, :], preferred_element_type=F32)
    x = x + _pick(mod_ref, 5, is_ctx) * acc
    if final:
        x = _rms(x) * fin_ref[...]
    o_ref[...] = x


def _post_call(xs, mod12, g2, wo, win, wout, mixer_inputs, mode, l, lc, final_g=None):
    x_specs, x_args, n, d = _stream_specs(xs, l)
    f = wout.shape[0]
    final = final_g is not None
    assert f % FFN_CHUNK == 0 and l % TM == 0 and n % TM == 0
    n_out = l if final else n
    row = lambda i: (i, 0)
    in_specs = x_specs + [_const_spec((2 * N_MOD, d)), _const_spec((1, d)),
                          _const_spec(wo.shape), _const_spec(win.shape), _const_spec(wout.shape)]
    args = x_args + [mod12, g2, wo, win, wout]
    if mode == "plain":
        in_specs += [pl.BlockSpec((TM, d), row)]
    elif mode == "conv":
        r8 = TM // 8
        last8 = n // 8 - 1
        in_specs += [pl.BlockSpec((TM, d), row), pl.BlockSpec((TM, d), row),
                     pl.BlockSpec((8, d), lambda i: (jnp.maximum(i * r8 - 1, 0), 0)),
                     pl.BlockSpec((8, d), lambda i: (jnp.minimum((i + 1) * r8, last8), 0)),
                     _const_spec((SC_WIDTH, d))]
    else:
        in_specs += [pl.BlockSpec((1, TM, d), lambda i: (0, i, 0)),
                     pl.BlockSpec((1, TM, d), lambda i: (1, i, 0)),
                     pl.BlockSpec((TM, d), row), _const_spec((1, LANES))]
    args += list(mixer_inputs)
    if final:
        in_specs += [_const_spec((1, d))]
        args += [final_g]
    return pl.pallas_call(
        functools.partial(_post_kernel, mode=mode, final=final, tm=TM, l=l, lc=lc, d=d, f=f,
                          n_x=len(x_args)),
        grid=(n_out // TM,),
        in_specs=in_specs,
        out_specs=pl.BlockSpec((TM, d), row),
        out_shape=jax.ShapeDtypeStruct((n_out, d), F32),
        compiler_params=_cparams(1),
        name="post_" + mode + ("_final" if final else ""),
    )(*args)


def _rope_tables(l, n):
    rows = l // GRID_W
    r = jnp.repeat(jnp.arange(rows), GRID_W).astype(F32)
    c = jnp.tile(jnp.arange(GRID_W), rows).astype(F32)
    half = HEAD_DIM // 2
    inv_freq = 1.0 / (ROPE_BASE ** (jnp.arange(0, half, 2, dtype=F32) / half))
    ar, ac = r[:, None] * inv_freq, c[:, None] * inv_freq
    cs = jnp.concatenate([jnp.cos(ar), jnp.cos(ar), jnp.cos(ac), jnp.cos(ac)], axis=-1)
    sn = jnp.concatenate([-jnp.sin(ar), jnp.sin(ar), -jnp.sin(ac), jnp.sin(ac)], axis=-1)
    cs = jnp.concatenate([cs, jnp.ones((n - l, HEAD_DIM), F32)], axis=0)
    sn = jnp.concatenate([sn, jnp.zeros((n - l, HEAD_DIM), F32)], axis=0)
    return jnp.tile(cs, (1, 2)), jnp.tile(sn, (1, 2))


def kernel(x, c, ctx, c_ctx, ada_w, ada_b, norm1_g, norm2_g, ffn_w_in, ffn_w_out, final_g,
           da_wqkv, da_lambda, da_subln_g, da_wo, sc_w_in, sc_conv_w, sc_w_out,
           wa_wqkv, wa_sinks, wa_wo, hg_w_in, hg_lb, hg_gnorm_g, hg_wo):
    b, l, d = x.shape
    lc = ctx.shape[1]
    depth = ada_w.shape[0]
    assert b == 1 and d % LANES == 0 and l % TM == 0
    n = -(-(l + lc) // TM) * TM
    xs = (x[0], jnp.concatenate([ctx[0], jnp.zeros((n - l - lc, d), x.dtype)], axis=0))

    cc = jnp.zeros((8, d), F32).at[0].set(c_ctx).at[1].set(c[0])
    mods = _mod_call(cc, ada_w, ada_b)
    cs, sn = _rope_tables(l, n)

    n_mix = 4
    for i in range(depth):
        mixer, slot = i % n_mix, i // n_mix
        last = i == depth - 1
        mod12 = jnp.concatenate([mods[i, 0].reshape(N_MOD, d), mods[i, 1].reshape(N_MOD, d)], axis=0)
        g1 = norm1_g[i].reshape(1, d)
        g2 = norm2_g[i].reshape(1, d)
        win = ffn_w_in[i].astype(BF16)
        wout = ffn_w_out[i].astype(BF16)
        fin = final_g.reshape(1, d) if last else None
        if isinstance(xs, tuple) and mixer != 0:
            xs = jnp.concatenate(xs, axis=0)

        if mixer == 0:
            w = da_wqkv[slot]
            lam = da_lambda[slot].astype(F32)
            lam_init = 0.8 - 0.6 * math.exp(-0.3 * i)
            lam_full = (jnp.exp(jnp.sum(lam[0] * lam[1])) - jnp.exp(jnp.sum(lam[2] * lam[3]))
                        + lam_init).reshape(1)
            q, k, vt3 = _proj_da_call(xs, g1, mod12, cs, sn, w[:, :2 * d].astype(BF16),
                                      w[:, 2 * d:].T.astype(BF16), l)
            a = _diff_attn_call(lam_full, q, k, vt3, da_subln_g[slot].reshape(LANES, 1), l, lc,
                                1.0 - lam_init)
            xs = _post_call(xs, mod12, g2, da_wo[slot].astype(BF16), win, wout, [a], "plain",
                            l, lc, fin)
        elif mixer == 1:
            bg, z = _proj_sc_call(xs, g1, mod12, sc_w_in[slot].astype(BF16), l)
            xs = _post_call(xs, mod12, g2, sc_w_out[slot].astype(BF16), win, wout,
                            [bg, z, z, z, sc_conv_w[slot]], "conv", l, lc, fin)
        elif mixer == 2:
            w = wa_wqkv[slot]
            kw = WA_KV_HEADS * HEAD_DIM
            dup = lambda m: jnp.tile(m.reshape(d, WA_KV_HEADS, 1, HEAD_DIM), (1, 1, 2, 1)).reshape(d, 2 * kw)
            w2 = jnp.concatenate([w[:, :d], dup(w[:, d:d + kw]), dup(w[:, d + kw:])], axis=1)
            q, k2, vt2 = _proj_wa_call(xs, g1, mod12, cs, sn, w2.astype(BF16), l, 2 * kw)
            a = _win_attn_call(wa_sinks[slot].astype(F32), q, k2, vt2, l, lc)
            xs = _post_call(xs, mod12, g2, wa_wo[slot].astype(BF16), win, wout, [a], "plain",
                            l, lc, fin)
        else:
            w = hg_w_in[slot]
            p = jax.nn.softmax(hg_lb.astype(F32), axis=1)
            lb = (jnp.cumsum(p, axis=1) - p[:, :1])[:, i]
            qh, gates, v, vt3, og = _proj_hg_call(xs, g1, mod12, jnp.log(lb), jnp.log1p(-lb),
                                                  w.astype(BF16), l)
            o2 = _hgrn_call(qh, gates, v, vt3, l, lc)
            xs = _post_call(xs, mod12, g2, hg_wo[slot].astype(BF16), win, wout,
                            [o2, o2, og, hg_gnorm_g[slot].reshape(1, LANES)], "hgrn", l, lc, fin)
    return xs[:l].reshape(1, l, d)
```

```python
import functools
import math

import numpy as np
import jax
import jax.numpy as jnp
from jax import lax
from jax.experimental import pallas as pl
from jax.experimental.pallas import tpu as pltpu

F32 = jnp.float32
BF16 = jnp.bfloat16

HEAD_DIM = 64
GRID_W = 64
ROPE_BASE = 10000.0
RMS_EPS = 1e-6
N_MOD = 6
WINDOW = 128
WA_KV_HEADS = 4
SC_WIDTH = 3
LANES = 128
NEG_BIG = -1e30
LOG2E = math.log2(math.e)

TM = 512
SLAB = 128
HG_CHUNK = 128
HG_PER_STEP = 4
DA_TQ = 256
DA_TK = 1024
DA_ONES_ROWS = 16
DA_UNROLLS = (4, 2)
WA_TQ = 512
FFN_CHUNK = 256
VMEM_LIMIT = 56 * 1024 * 1024

NT_DIMS = (((1,), (1,)), ((), ()))


def _cparams(n_axes):
    return pltpu.CompilerParams(
        dimension_semantics=("arbitrary",) * n_axes, vmem_limit_bytes=VMEM_LIMIT)


def _const_spec(shape):
    nd = len(shape)
    return pl.BlockSpec(shape, lambda *_: (0,) * nd, pipeline_mode=pl.Buffered(1))


def _is_ctx_rows(tile_idx, tm, l):
    row = tile_idx * tm + lax.broadcasted_iota(jnp.int32, (tm, 1), 0)
    return row >= l


def _pick(mod_ref, k, is_ctx):
    return jnp.where(is_ctx, mod_ref[k:k + 1, :], mod_ref[N_MOD + k:N_MOD + k + 1, :])


def _rms(x):
    return x * lax.rsqrt(jnp.mean(x * x, axis=-1, keepdims=True) + RMS_EPS)


def _norm_mod(x, g, shift, scale):
    return (_rms(x) * g) * (1.0 + scale) + shift


def _silu(x):
    h = 0.5 * x
    return h + h * jnp.tanh(h)


def _rope(chunk, cs, sn, first_half):
    partner = jnp.where(first_half, pltpu.roll(chunk, LANES - 16, 1), pltpu.roll(chunk, 16, 1))
    return chunk * cs + partner * sn


def _stream_specs(xs, l):
    if not isinstance(xs, tuple):
        n, d = xs.shape
        return [pl.BlockSpec((TM, d), lambda i: (i, 0))], [xs], n, d
    lat, tail = xs
    d = lat.shape[1]
    nl = l // TM
    assert lat.shape[0] == l and tail.shape[0] % TM == 0
    return ([pl.BlockSpec((TM, d), lambda i: (jnp.minimum(i, nl - 1), 0)),
             pl.BlockSpec((TM, d), lambda i: (jnp.maximum(i - nl, 0), 0))],
            [lat, tail], l + tail.shape[0], d)


def _stream_tile(x_refs, tile_idx, tm, l):
    if len(x_refs) == 1:
        return x_refs[0][...]
    return jnp.where(tile_idx * tm < l, x_refs[0][...], x_refs[1][...])


def _first_half_lanes():
    lane = lax.broadcasted_iota(jnp.int32, (1, LANES), 1)
    return (lane & 31) < 16


def _mod_kernel(cc_ref, w_ref, b_ref, o_ref):
    a = _silu(cc_ref[...])
    o_ref[0] = jnp.dot(a, w_ref[0], precision=lax.Precision.HIGHEST,
                       preferred_element_type=F32) + b_ref[0]


def _mod_call(cc, ada_w, ada_b):
    depth, d, nd = ada_w.shape
    tn = 3072
    assert nd % tn == 0
    return pl.pallas_call(
        _mod_kernel,
        grid=(depth, nd // tn),
        in_specs=[pl.BlockSpec((8, d), lambda i, j: (0, 0)),
                  pl.BlockSpec((1, d, tn), lambda i, j: (i, 0, j)),
                  pl.BlockSpec((1, 1, tn), lambda i, j: (i, 0, j))],
        out_specs=pl.BlockSpec((1, 8, tn), lambda i, j: (i, 0, j)),
        out_shape=jax.ShapeDtypeStruct((depth, 8, nd), F32),
        compiler_params=_cparams(2),
        name="adaln_mod",
    )(cc, ada_w, ada_b.reshape(depth, 1, nd))


def _proj_da_kernel(*refs, tm, l, d, n_x):
    x_refs = refs[:n_x]
    g_ref, mod_ref, cs_ref, sn_ref, wqk_ref, wvt_ref, q_ref, k_ref, vt_ref = refs[n_x:]
    is_ctx = _is_ctx_rows(pl.program_id(0), tm, l)
    x = _stream_tile(x_refs, pl.program_id(0), tm, l)
    h = _norm_mod(x, g_ref[...], _pick(mod_ref, 0, is_ctx), _pick(mod_ref, 1, is_ctx))
    hb = h.astype(BF16)
    qk = jnp.dot(hb, wqk_ref[...], preferred_element_type=F32)
    cs, sn, first = cs_ref[...], sn_ref[...], _first_half_lanes()
    nq = d // LANES
    for j in range(2 * nq):
        r = _rope(qk[:, j * LANES:(j + 1) * LANES], cs, sn, first)
        if j < nq:
            q_ref[:, j * LANES:(j + 1) * LANES] = (r * (LOG2E * HEAD_DIM ** -0.5)).astype(BF16)
        else:
            k_ref[:, (j - nq) * LANES:(j - nq + 1) * LANES] = r.astype(BF16)
    vt = lax.dot_general(wvt_ref[...], hb, NT_DIMS, preferred_element_type=F32)
    ones = jnp.ones((DA_ONES_ROWS, SLAB), BF16)
    vr = LANES + DA_ONES_ROWS
    for s in range(tm // SLAB):
        for hh in range(d // LANES):
            vt_ref[s, hh * vr:hh * vr + LANES, :] = (
                vt[hh * LANES:(hh + 1) * LANES, s * SLAB:(s + 1) * SLAB].astype(BF16))
            vt_ref[s, hh * vr + LANES:(hh + 1) * vr, :] = ones


def _proj_da_call(xs, g, mod12, cs, sn, wqk, wvt, l):
    x_specs, x_args, n, d = _stream_specs(xs, l)
    row = lambda i: (i, 0)
    dv = d // LANES * (LANES + DA_ONES_ROWS)
    return pl.pallas_call(
        functools.partial(_proj_da_kernel, tm=TM, l=l, d=d, n_x=len(x_args)),
        grid=(n // TM,),
        in_specs=x_specs + [_const_spec((1, d)), _const_spec((2 * N_MOD, d)),
                            pl.BlockSpec((TM, LANES), row), pl.BlockSpec((TM, LANES), row),
                            _const_spec(wqk.shape), _const_spec(wvt.shape)],
        out_specs=[pl.BlockSpec((TM, d), row), pl.BlockSpec((TM, d), row),
                   pl.BlockSpec((TM // SLAB, dv, SLAB), lambda i: (i, 0, 0))],
        out_shape=[jax.ShapeDtypeStruct((n, d), BF16), jax.ShapeDtypeStruct((n, d), BF16),
                   jax.ShapeDtypeStruct((n // SLAB, dv, SLAB), BF16)],
        compiler_params=_cparams(1),
        name="proj_diff_attn",
    )(*x_args, g, mod12, cs, sn, wqk, wvt)


def _proj_sc_kernel(x_ref, g_ref, mod_ref, w_ref, b_ref, z_ref, *, tm, l, d):
    is_ctx = _is_ctx_rows(pl.program_id(0), tm, l)
    h = _norm_mod(x_ref[...], g_ref[...], _pick(mod_ref, 0, is_ctx), _pick(mod_ref, 1, is_ctx))
    y = jnp.dot(h.astype(BF16), w_ref[...], preferred_element_type=F32)
    b_ref[...] = y[:, :d].astype(BF16)
    z_ref[...] = y[:, d:2 * d] * y[:, 2 * d:]


def _proj_sc_call(xs, g, mod12, w, l):
    n, d = xs.shape
    row = lambda i: (i, 0)
    return pl.pallas_call(
        functools.partial(_proj_sc_kernel, tm=TM, l=l, d=d),
        grid=(n // TM,),
        in_specs=[pl.BlockSpec((TM, d), row), _const_spec((1, d)), _const_spec((2 * N_MOD, d)),
                  _const_spec(w.shape)],
        out_specs=[pl.BlockSpec((TM, d), row), pl.BlockSpec((TM, d), row)],
        out_shape=[jax.ShapeDtypeStruct((n, d), BF16), jax.ShapeDtypeStruct((n, d), F32)],
        compiler_params=_cparams(1),
        name="proj_short_conv",
    )(xs, g, mod12, w)


def _proj_wa_kernel(x_ref, g_ref, mod_ref, cs_ref, sn_ref, w_ref, q_ref, k_ref, v_ref,
                    *, tm, l, d, kw):
    is_ctx = _is_ctx_rows(pl.program_id(0), tm, l)
    h = _norm_mod(x_ref[...], g_ref[...], _pick(mod_ref, 0, is_ctx), _pick(mod_ref, 1, is_ctx))
    y = jnp.dot(h.astype(BF16), w_ref[...], preferred_element_type=F32)
    cs, sn, first = cs_ref[...], sn_ref[...], _first_half_lanes()
    for j in range(d // LANES):
        r = _rope(y[:, j * LANES:(j + 1) * LANES], cs, sn, first)
        q_ref[:, j * LANES:(j + 1) * LANES] = (r * (LOG2E * HEAD_DIM ** -0.5)).astype(BF16)
    for j in range(kw // LANES):
        r = _rope(y[:, d + j * LANES:d + (j + 1) * LANES], cs, sn, first)
        k_ref[:, j * LANES:(j + 1) * LANES] = r.astype(BF16)
    vt = y[:, d + kw:].T
    for s in range(tm // SLAB):
        v_ref[s] = vt[:, s * SLAB:(s + 1) * SLAB].astype(BF16)


def _proj_wa_call(xs, g, mod12, cs, sn, w, l, kw):
    n, d = xs.shape
    row = lambda i: (i, 0)
    return pl.pallas_call(
        functools.partial(_proj_wa_kernel, tm=TM, l=l, d=d, kw=kw),
        grid=(n // TM,),
        in_specs=[pl.BlockSpec((TM, d), row), _const_spec((1, d)), _const_spec((2 * N_MOD, d)),
                  pl.BlockSpec((TM, LANES), row), pl.BlockSpec((TM, LANES), row),
                  _const_spec(w.shape)],
        out_specs=[pl.BlockSpec((TM, d), row), pl.BlockSpec((TM, kw), row),
                   pl.BlockSpec((TM // SLAB, kw, SLAB), lambda i: (i, 0, 0))],
        out_shape=[jax.ShapeDtypeStruct((n, d), BF16), jax.ShapeDtypeStruct((n, kw), BF16),
                   jax.ShapeDtypeStruct((n // SLAB, kw, SLAB), BF16)],
        compiler_params=_cparams(1),
        name="proj_window_attn",
    )(xs, g, mod12, cs, sn, w)


def _log1p_unit(x):
    return jnp.log(1.0 + x)


def _proj_hg_kernel(x_ref, g_ref, mod_ref, lla_ref, l1m_ref, w_ref,
                    q_ref, gate_ref, v_ref, vt_ref, og_ref, *, tm, l, d):
    is_ctx = _is_ctx_rows(pl.program_id(0), tm, l)
    h = _norm_mod(x_ref[...], g_ref[...], _pick(mod_ref, 0, is_ctx), _pick(mod_ref, 1, is_ctx))
    hb = h.astype(BF16)
    part = lambda j: jnp.dot(hb, w_ref[:, j * d:(j + 1) * d], preferred_element_type=F32)
    q_ref[...] = _silu(part(0)).astype(BF16)
    for dr in range(2):
        f = part(1 + dr)
        log_sig = jnp.minimum(f, 0.0) - _log1p_unit(jnp.exp(-jnp.abs(f)))
        a = lla_ref[dr:dr + 1, :]
        c = l1m_ref[dr:dr + 1, :] + log_sig
        gate_ref[dr] = jnp.maximum(a, c) + _log1p_unit(jnp.exp(-jnp.abs(a - c)))
    v = part(3)
    v_ref[...] = v.astype(BF16)
    vt = v.T
    for s in range(tm // SLAB):
        vt_ref[s] = vt[:, s * SLAB:(s + 1) * SLAB].astype(BF16)
    og_ref[...] = _silu(part(4)).astype(BF16)


def _proj_hg_call(xs, g, mod12, lla, l1m, w, l):
    n, d = xs.shape
    row = lambda i: (i, 0)
    return pl.pallas_call(
        functools.partial(_proj_hg_kernel, tm=TM, l=l, d=d),
        grid=(n // TM,),
        in_specs=[pl.BlockSpec((TM, d), row), _const_spec((1, d)), _const_spec((2 * N_MOD, d)),
                  _const_spec((2, d)), _const_spec((2, d)), _const_spec(w.shape)],
        out_specs=[pl.BlockSpec((TM, d), row), pl.BlockSpec((2, TM, d), lambda i: (0, i, 0)),
                   pl.BlockSpec((TM, d), row),
                   pl.BlockSpec((TM // SLAB, d, SLAB), lambda i: (i, 0, 0)),
                   pl.BlockSpec((TM, d), row)],
        out_shape=[jax.ShapeDtypeStruct((n, d), BF16), jax.ShapeDtypeStruct((2, n, d), F32),
                   jax.ShapeDtypeStruct((n, d), BF16), jax.ShapeDtypeStruct((n // SLAB, d, SLAB), BF16),
                   jax.ShapeDtypeStruct((n, d), BF16)],
        compiler_params=_cparams(1),
        name="proj_hgrn",
    )(xs, g, mod12, lla, l1m, w)


def _diff_attn_kernel(lam_ref, q_ref, k_ref, vt_ref, g_ref, o_ref, m_ref, acc_ref,
                      sa_ref, sb_ref, ma_ref, mb_ref, *, tq, tk, l, lc, n, unroll, post_scale):
    lane = lax.broadcasted_iota(jnp.int32, (1, LANES), 1)
    sub = 2 * vt_ref.shape[2]
    n_chunks = l // tk

    def tile_rows(t):
        if isinstance(t, int):
            return slice(t * tq, (t + 1) * tq)
        return pl.ds(pl.multiple_of(t * tq, tq), tq)

    def stacked(t):
        q = q_ref[tile_rows(t), :]
        zero = jnp.zeros_like(q)
        return jnp.concatenate([jnp.where(lane < HEAD_DIM, q, zero),
                                jnp.where(lane >= HEAD_DIM, q, zero)], axis=0)

    def scores(kc, qcat):
        return lax.dot_general(kc, qcat, NT_DIMS, preferred_element_type=F32)

    def pv(pair0, n_pairs, p):
        out = None
        for j in range(n_pairs):
            s0 = 2 * (pair0 + j)
            vt = jnp.concatenate([vt_ref[s0], vt_ref[s0 + 1]], axis=1)
            t = jnp.dot(vt, p[j * sub:(j + 1) * sub, :], preferred_element_type=F32)
            out = t if out is None else out + t
        return out

    def absorb_context(qcat, slot):
        s = scores(k_ref[l:l + lc, :], qcat)
        mx = jnp.max(s, axis=0, keepdims=True)
        m_ref[slot] = mx
        acc_ref[slot] = pv(l // sub, lc // sub, jnp.exp2(s - mx).astype(BF16))

    def qk(c, qcat, s_ref, mx_ref):
        off = pl.multiple_of(c * tk, LANES)
        s = scores(k_ref[pl.ds(off, tk), :], qcat)
        s_ref[...] = s
        mx_ref[...] = jnp.max(s, axis=0, keepdims=True)

    def absorb(c, s_ref, mx_ref, slot):
        m_old = m_ref[slot]
        m_new = jnp.maximum(m_old, mx_ref[...])
        alpha = jnp.exp2(m_old - m_new)
        p = jnp.exp2(s_ref[...] - m_new).astype(BF16)
        m_ref[slot] = m_new
        acc_ref[slot] = alpha * acc_ref[slot] + pv((c * tk) // sub, tk // sub, p)

    def start_sweep(qcat, slot):
        qk(0, qcat, sa_ref, ma_ref)
        absorb_context(qcat, slot)

    def finish(t, slot):
        acc = acc_ref[slot]
        acc = acc[0:LANES, :] * (1.0 / acc[LANES:LANES + 1, :])
        o = acc[:, :tq] - lam_ref[0] * acc[:, tq:]
        o = o * lax.rsqrt(jnp.mean(o * o, axis=0, keepdims=True) + RMS_EPS)
        o = o * (g_ref[...] * post_scale)
        o_ref[tile_rows(t), :] = o.T.astype(BF16)

    bufs = ((sa_ref, ma_ref), (sb_ref, mb_ref))

    def sweep(t, slot):
        qcat = stacked(t)

        def group(c0, count):
            for j in range(count):
                qk(c0 + j + 1, qcat, *bufs[(j + 1) % 2])
                absorb(c0 + j, *bufs[j % 2], slot)

        def body(i, inner):
            group(unroll * i, unroll)
            return inner

        lax.fori_loop(0, n_chunks // unroll - 1, body, 0)
        group(n_chunks - unroll, unroll - 1)
        start_sweep(stacked(t + 1), 1 - slot)
        absorb(n_chunks - 1, *bufs[(unroll - 1) % 2], slot)
        finish(t, slot)

    def tile_pair(i, carry):
        sweep(2 * i, 0)
        sweep(2 * i + 1, 1)
        return carry

    start_sweep(stacked(0), 0)
    lax.fori_loop(0, l // (2 * tq), tile_pair, 0)
    for t in range(l // tq, n // tq):
        absorb_context(stacked(t), 0)
        finish(t, 0)


def _diff_attn_call(lam, q, k, vt3, subln_g, l, lc, post_scale):
    n, d = q.shape
    nh = d // LANES
    nslab, vrows, slab = vt3.shape
    vrows //= nh
    tq = DA_TQ
    tk = min(DA_TK, l // 2)
    assert l % (2 * tq) == 0 and n % tq == 0 and n > l and tk % (2 * slab) == 0
    assert l % (2 * slab) == 0 and lc % (2 * slab) == 0
    assert l % (2 * tk) == 0
    return pl.pallas_call(
        functools.partial(_diff_attn_kernel, tq=tq, tk=tk, l=l, lc=lc, n=n,
                          unroll=next(u for u in DA_UNROLLS if l % (u * tk) == 0),
                          post_scale=post_scale),
        grid=(nh,),
        in_specs=[pl.BlockSpec(memory_space=pltpu.SMEM),
                  pl.BlockSpec((n, LANES), lambda h: (0, h)),
                  pl.BlockSpec((n, LANES), lambda h: (0, h)),
                  pl.BlockSpec((nslab, vrows, slab), lambda h: (0, h, 0)),
                  pl.BlockSpec((LANES, 1), lambda h: (0, 0))],
        out_specs=pl.BlockSpec((n, LANES), lambda h: (0, h)),
        out_shape=jax.ShapeDtypeStruct((n, d), BF16),
        scratch_shapes=[pltpu.VMEM((2, 1, 2 * tq), F32),
                        pltpu.VMEM((2, vrows, 2 * tq), F32),
                        pltpu.VMEM((tk, 2 * tq), F32), pltpu.VMEM((tk, 2 * tq), F32),
                        pltpu.VMEM((1, 2 * tq), F32), pltpu.VMEM((1, 2 * tq), F32)],
        compiler_params=_cparams(1),
        name="diff_attention",
    )(lam, q, k, vt3, subln_g)


def _win_attn_kernel(sink_ref, q_ref, kc_ref, kp_ref, ko_ref, kn_ref,
                     vc_ref, vp_ref, vo_ref, vn_ref, o_ref, *, tq, l, lc, rep, ng):
    j = pl.program_id(0)
    w = WINDOW
    gw = rep * HEAD_DIM
    a = lax.broadcasted_iota(jnp.int32, (w, w), 0)
    b = lax.broadcasted_iota(jnp.int32, (w, w), 1)
    lane = lax.broadcasted_iota(jnp.int32, (1, LANES), 1)
    low = lane < HEAD_DIM
    biases = []
    for t in range(tq // w):
        q0 = j * tq + t * w
        latent = q0 < l
        own_bias = jnp.where(latent, 0.0, NEG_BIG)
        prev_bias = jnp.where(jnp.logical_and(latent, q0 >= w), 0.0, NEG_BIG)
        next_bias = jnp.where(q0 + w < l, 0.0, NEG_BIG)
        bias = jnp.concatenate(
            [jnp.where(a >= b, prev_bias, NEG_BIG),
             jnp.full((w, w), own_bias, F32),
             jnp.where(a <= b, next_bias, NEG_BIG)], axis=0)
        biases.append(jnp.concatenate([bias] * rep, axis=1))
    for g in range(ng):
        gl = slice(g * LANES, (g + 1) * LANES)
        kc = kc_ref[:, gl]
        kwin = jnp.concatenate([kp_ref[:, gl], ko_ref[:, gl], kn_ref[:, gl]], axis=0)
        vtc = jnp.concatenate([vc_ref[s, gl, :] for s in range(lc // w)], axis=1)
        vtwin = jnp.concatenate([vp_ref[0, gl, :]] + [vo_ref[s, gl, :] for s in range(tq // w)]
                                + [vn_ref[0, gl, :]], axis=1)
        for t in range(tq // w):
            qs, sinks = [], []
            for r in range(rep):
                c0 = g * gw + (r // 2) * LANES
                qc = q_ref[t * w:(t + 1) * w, c0:c0 + LANES]
                qs.append(jnp.where(low if r % 2 == 0 else ~low, qc, jnp.zeros_like(qc)))
                sinks.append(jnp.full((1, w), sink_ref[g * rep + r] * LOG2E, F32))
            qst = jnp.concatenate(qs, axis=0)
            sink = jnp.concatenate(sinks, axis=1)
            s_c = lax.dot_general(kc, qst, NT_DIMS, preferred_element_type=F32)
            s_w = (lax.dot_general(kwin[t * w:(t + 3) * w, :], qst, NT_DIMS,
                                   preferred_element_type=F32) + biases[t])
            mx = jnp.maximum(jnp.maximum(jnp.max(s_c, axis=0, keepdims=True),
                                         jnp.max(s_w, axis=0, keepdims=True)), sink)
            e_c = jnp.exp2(s_c - mx)
            e_w = jnp.exp2(s_w - mx)
            den = (jnp.exp2(sink - mx) + jnp.sum(e_c, axis=0, keepdims=True)
                   + jnp.sum(e_w, axis=0, keepdims=True))
            ot = (jnp.dot(vtc, e_c.astype(BF16), preferred_element_type=F32)
                  + jnp.dot(vtwin[:, t * w:(t + 3) * w], e_w.astype(BF16),
                            preferred_element_type=F32)) * (1.0 / den)
            heads = [ot[:, r * w:(r + 1) * w].T for r in range(rep)]
            for c in range(rep // 2):
                c0 = g * gw + c * LANES
                o_ref[t * w:(t + 1) * w, c0:c0 + LANES] = jnp.where(
                    low, heads[2 * c], heads[2 * c + 1]).astype(BF16)


def _win_attn_call(sinks, q, k2, vt2, l, lc):
    n, d = q.shape
    ng = WA_KV_HEADS
    rep = d // HEAD_DIM // ng
    kw = k2.shape[1]
    tq = WA_TQ
    assert n % tq == 0 and tq % WINDOW == 0 and lc % WINDOW == 0 and l % lc == 0 and rep % 2 == 0
    assert kw == ng * LANES
    r = tq // WINDOW
    last = n // WINDOW - 1
    prev = lambda j: jnp.maximum(j * r - 1, 0)
    nxt = lambda j: jnp.minimum((j + 1) * r, last)
    k_specs = [pl.BlockSpec((lc, kw), lambda j: (l // lc, 0)),
               pl.BlockSpec((WINDOW, kw), lambda j: (prev(j), 0)),
               pl.BlockSpec((tq, kw), lambda j: (j, 0)),
               pl.BlockSpec((WINDOW, kw), lambda j: (nxt(j), 0))]
    vt_specs = [pl.BlockSpec((lc // WINDOW, kw, WINDOW), lambda j: (l // lc, 0, 0)),
                pl.BlockSpec((1, kw, WINDOW), lambda j: (prev(j), 0, 0)),
                pl.BlockSpec((r, kw, WINDOW), lambda j: (j, 0, 0)),
                pl.BlockSpec((1, kw, WINDOW), lambda j: (nxt(j), 0, 0))]
    return pl.pallas_call(
        functools.partial(_win_attn_kernel, tq=tq, l=l, lc=lc, rep=rep, ng=ng),
        grid=(n // tq,),
        in_specs=[pl.BlockSpec(memory_space=pltpu.SMEM),
                  pl.BlockSpec((tq, d), lambda j: (j, 0))] + k_specs + vt_specs,
        out_specs=pl.BlockSpec((tq, d), lambda j: (j, 0)),
        out_shape=jax.ShapeDtypeStruct((n, d), BF16),
        compiler_params=_cparams(1),
        name="window_attention",
    )(sinks, q, k2, k2, k2, k2, vt2, vt2, vt2, vt2)


def _hgrn_consts(chunk):
    n_lv = int(math.log2(chunk))
    walls, masks_all = [], []
    for direction in range(2):
        tau = np.arange(chunk) if direction == 0 else chunk - 1 - np.arange(chunk)
        tr, tc = tau[:, None], tau[None, :]
        blocks = [tc <= tr]
        masks = []
        for lv in range(n_lv):
            h = chunk >> (lv + 1)
            mid = (tr // (2 * h)) * 2 * h + h - 1
            second_r = (tr % (2 * h)) >= h
            second_c = (tc % (2 * h)) >= h
            blocks.append(np.where(second_r, (tc > mid) & (tc <= tr), (tc > tr) & (tc <= mid)))
            masks.append(second_r & ~second_c & ((tr // (2 * h)) == (tc // (2 * h))))
        masks.append(tr == tc)
        w = np.concatenate(blocks, axis=0).astype(np.float32)
        walls.append(np.concatenate([w, w], axis=1))
        masks_all.append(np.stack(masks).astype(np.float32))
    return np.stack(walls), np.stack(masks_all)


def _hgrn_block(dr, s, n_lat, n_tail):
    fwd = jnp.where(s < n_tail, n_lat + s, s - n_tail)
    bwd = jnp.where(s < n_tail, n_lat + n_tail - 1 - s, n_lat - 1 - (s - n_tail))
    return jnp.where(dr == 0, fwd, bwd)


def _hgrn_kernel(wall_ref, mask_ref, q_ref, g_ref, v_ref, vt_ref, o_ref, st_ref,
                 *, chunk, n_lv, n_heads, n_real, per_step, n_lat, n_tail):
    dr = pl.program_id(0)
    step = pl.program_id(1)
    chunk0 = _hgrn_block(dr, step, n_lat, n_tail) * per_step

    @pl.when(step == 0)
    def _():
        st_ref[...] = jnp.zeros_like(st_ref)

    heads = [slice(hh * LANES, (hh + 1) * LANES) for hh in range(n_heads)]
    nt = lambda x, y: lax.dot_general(x, y, NT_DIMS, preferred_element_type=F32)

    def one_chunk(j, carry):
        idx = jnp.where(dr == 0, j, per_step - 1 - j)
        rows = pl.ds(pl.multiple_of(idx * chunk, chunk), chunk)

        @pl.when(chunk0 + idx >= n_real)
        def _():
            o_ref[0, rows, :] = jnp.zeros((chunk, n_heads * LANES), F32)

        @pl.when(chunk0 + idx < n_real)
        def _():
            g_all = g_ref[0, rows, :]
            g_hi = g_all.astype(BF16)
            g_lo = (g_all - g_hi.astype(F32)).astype(BF16)
            sums_all = jnp.dot(wall_ref[0], jnp.concatenate([g_hi, g_lo], axis=0),
                               preferred_element_type=F32)
            q16 = q_ref[rows, :]
            q_all = q16.astype(F32)
            k_all = 1.0 - jnp.exp(g_all)
            k16 = k_all.astype(BF16)
            v_all = v_ref[rows, :]
            vt_all = vt_ref[idx]
            b = sums_all[0:chunk]
            b_end = jnp.sum(g_all, axis=0, keepdims=True)

            qb = (q_all * jnp.exp(b)).astype(BF16)
            o = [nt(qb[:, sl], st_ref[hh].astype(BF16)) for hh, sl in enumerate(heads)]
            att = [nt(q16[:, sl], k16[:, sl]) * mask_ref[0, n_lv] for sl in heads]
            for lv in range(n_lv):
                e = jnp.exp(sums_all[(1 + lv) * chunk:(2 + lv) * chunk])
                qe = (q_all * e).astype(BF16)
                ke = (k_all * e).astype(BF16)
                mask = mask_ref[0, lv]
                att = [a + nt(qe[:, sl], ke[:, sl]) * mask for a, sl in zip(att, heads)]
            out = [o[hh] + jnp.dot(att[hh].astype(BF16), v_all[:, sl], preferred_element_type=F32)
                   for hh, sl in enumerate(heads)]
            o_ref[0, rows, :] = jnp.concatenate(out, axis=1)

            ks = (k_all * jnp.exp(b_end - b)).astype(BF16)
            decay = jnp.exp(b_end)
            for hh, sl in enumerate(heads):
                st_ref[hh] = st_ref[hh] * decay[:, sl] + jnp.dot(vt_all[sl, :], ks[:, sl],
                                                                 preferred_element_type=F32)

        return carry

    lax.fori_loop(0, per_step, one_chunk, 0)


def _hgrn_call(qh, gates, v, vt3, l, lc):
    n, d = qh.shape
    c = HG_CHUNK
    per = HG_PER_STEP
    rows = per * c
    nh = d // LANES
    n_lat, n_tail = l // rows, (n - l) // rows
    assert n % rows == 0 and l % rows == 0 and (l + lc) % c == 0 and vt3.shape[2] == c
    n_lv = int(math.log2(c))
    wall_np, mask_np = _hgrn_consts(c)
    wall = jnp.asarray(wall_np, dtype=BF16)
    masks = jnp.asarray(mask_np, dtype=F32)
    blk = lambda dr, s: _hgrn_block(dr, s, n_lat, n_tail)

    return pl.pallas_call(
        functools.partial(_hgrn_kernel, chunk=c, n_lv=n_lv, n_heads=nh, n_real=(l + lc) // c,
                          per_step=per, n_lat=n_lat, n_tail=n_tail),
        grid=(2, n_lat + n_tail),
        in_specs=[pl.BlockSpec((1,) + wall_np.shape[1:], lambda dr, s: (dr, 0, 0)),
                  pl.BlockSpec((1,) + mask_np.shape[1:], lambda dr, s: (dr, 0, 0, 0)),
                  pl.BlockSpec((rows, d), lambda dr, s: (blk(dr, s), 0)),
                  pl.BlockSpec((1, rows, d), lambda dr, s: (dr, blk(dr, s), 0)),
                  pl.BlockSpec((rows, d), lambda dr, s: (blk(dr, s), 0)),
                  pl.BlockSpec((per, d, c), lambda dr, s: (blk(dr, s), 0, 0))],
        out_specs=pl.BlockSpec((1, rows, d), lambda dr, s: (dr, blk(dr, s), 0)),
        out_shape=jax.ShapeDtypeStruct((2, n, d), F32),
        scratch_shapes=[pltpu.VMEM((nh, LANES, LANES), F32)],
        compiler_params=_cparams(2),
        name="hgrn_scan",
    )(wall, masks, qh, gates, v, vt3)


def _post_kernel(*refs, mode, final, tm, l, lc, d, f, n_x):
    it = iter(refs)
    x_refs = [next(it) for _ in range(n_x)]
    mod_ref, g2_ref, wo_ref, win_ref, wout_ref = (next(it) for _ in range(5))
    if mode == "plain":
        a_ref = next(it)
    elif mode == "conv":
        b_ref, z_ref, zp_ref, zn_ref, cw_ref = (next(it) for _ in range(5))
    else:
        ofw_ref, obw_ref, og_ref, gn_ref = (next(it) for _ in range(4))
    fin_ref = next(it) if final else None
    o_ref = next(it)

    ti = pl.program_id(0)
    is_ctx = _is_ctx_rows(ti, tm, l)
    if mode == "plain":
        a = a_ref[...]
    elif mode == "conv":
        z = z_ref[...]
        rl = lax.broadcasted_iota(jnp.int32, (tm, 1), 0)
        row = ti * tm + rl
        z_prev = jnp.where(rl == 0, zp_ref[7:8, :], pltpu.roll(z, 1, 0))
        z_next = jnp.where(rl == tm - 1, zn_ref[0:1, :], pltpu.roll(z, tm - 1, 0))
        z_prev = jnp.where((row == 0) | (row == l), 0.0, z_prev)
        z_next = jnp.where((row == l - 1) | (row == l + lc - 1), 0.0, z_next)
        conv = cw_ref[0:1, :] * z_prev + cw_ref[1:2, :] * z + cw_ref[2:3, :] * z_next
        a = (b_ref[...].astype(F32) * conv).astype(BF16)
    else:
        gn = gn_ref[...]
        heads = []
        for hh in range(d // LANES):
            sl = slice(hh * LANES, (hh + 1) * LANES)
            oh = _rms(ofw_ref[0, :, sl] + obw_ref[0, :, sl])
            heads.append((oh * gn * og_ref[:, sl].astype(F32)).astype(BF16))
        a = jnp.concatenate(heads, axis=1)

    y = jnp.dot(a, wo_ref[...], preferred_element_type=F32)
    x = _stream_tile(x_refs, ti, tm, l) + _pick(mod_ref, 2, is_ctx) * y
    h = _norm_mod(x, g2_ref[...], _pick(mod_ref, 3, is_ctx), _pick(mod_ref, 4, is_ctx)).astype(BF16)
    acc = jnp.zeros((tm, d), F32)
    for c0 in range(0, f, FFN_CHUNK):
        u = jnp.dot(h, win_ref[:, c0:c0 + FFN_CHUNK], preferred_element_type=F32)
        w = jnp.dot(h, win_ref[:, f + c0:f + c0 + FFN_CHUNK], preferred_element_type=F32)
        act = (_silu(u) * w).astype(BF16)
        acc = acc + jnp.dot(act, wout_ref[c0:c0 + FFN_CHUNK, :], preferred_element_type=F32)
    x = x + _pick(mod_ref, 5, is_ctx) * acc
    if final:
        x = _rms(x) * fin_ref[...]
    o_ref[...] = x


def _post_call(xs, mod12, g2, wo, win, wout, mixer_inputs, mode, l, lc, final_g=None):
    x_specs, x_args, n, d = _stream_specs(xs, l)
    f = wout.shape[0]
    final = final_g is not None
    assert f % FFN_CHUNK == 0 and l % TM == 0 and n % TM == 0
    n_out = l if final else n
    row = lambda i: (i, 0)
    in_specs = x_specs + [_const_spec((2 * N_MOD, d)), _const_spec((1, d)),
                          _const_spec(wo.shape), _const_spec(win.shape), _const_spec(wout.shape)]
    args = x_args + [mod12, g2, wo, win, wout]
    if mode == "plain":
        in_specs += [pl.BlockSpec((TM, d), row)]
    elif mode == "conv":
        r8 = TM // 8
        last8 = n // 8 - 1
        in_specs += [pl.BlockSpec((TM, d), row), pl.BlockSpec((TM, d), row),
                     pl.BlockSpec((8, d), lambda i: (jnp.maximum(i * r8 - 1, 0), 0)),
                     pl.BlockSpec((8, d), lambda i: (jnp.minimum((i + 1) * r8, last8), 0)),
                     _const_spec((SC_WIDTH, d))]
    else:
        in_specs += [pl.BlockSpec((1, TM, d), lambda i: (0, i, 0)),
                     pl.BlockSpec((1, TM, d), lambda i: (1, i, 0)),
                     pl.BlockSpec((TM, d), row), _const_spec((1, LANES))]
    args += list(mixer_inputs)
    if final:
        in_specs += [_const_spec((1, d))]
        args += [final_g]
    return pl.pallas_call(
        functools.partial(_post_kernel, mode=mode, final=final, tm=TM, l=l, lc=lc, d=d, f=f,
                          n_x=len(x_args)),
        grid=(n_out // TM,),
        in_specs=in_specs,
        out_specs=pl.BlockSpec((TM, d), row),
        out_shape=jax.ShapeDtypeStruct((n_out, d), F32),
        compiler_params=_cparams(1),
        name="post_" + mode + ("_final" if final else ""),
    )(*args)


def _rope_tables(l, n):
    rows = l // GRID_W
    r = jnp.repeat(jnp.arange(rows), GRID_W).astype(F32)
    c = jnp.tile(jnp.arange(GRID_W), rows).astype(F32)
    half = HEAD_DIM // 2
    inv_freq = 1.0 / (ROPE_BASE ** (jnp.arange(0, half, 2, dtype=F32) / half))
    ar, ac = r[:, None] * inv_freq, c[:, None] * inv_freq
    cs = jnp.concatenate([jnp.cos(ar), jnp.cos(ar), jnp.cos(ac), jnp.cos(ac)], axis=-1)
    sn = jnp.concatenate([-jnp.sin(ar), jnp.sin(ar), -jnp.sin(ac), jnp.sin(ac)], axis=-1)
    cs = jnp.concatenate([cs, jnp.ones((n - l, HEAD_DIM), F32)], axis=0)
    sn = jnp.concatenate([sn, jnp.zeros((n - l, HEAD_DIM), F32)], axis=0)
    return jnp.tile(cs, (1, 2)), jnp.tile(sn, (1, 2))


def kernel(x, c, ctx, c_ctx, ada_w, ada_b, norm1_g, norm2_g, ffn_w_in, ffn_w_out, final_g,
           da_wqkv, da_lambda, da_subln_g, da_wo, sc_w_in, sc_conv_w, sc_w_out,
           wa_wqkv, wa_sinks, wa_wo, hg_w_in, hg_lb, hg_gnorm_g, hg_wo):
    b, l, d = x.shape
    lc = ctx.shape[1]
    depth = ada_w.shape[0]
    assert b == 1 and d % LANES == 0 and l % TM == 0
    n = -(-(l + lc) // TM) * TM
    xs = (x[0], jnp.concatenate([ctx[0], jnp.zeros((n - l - lc, d), x.dtype)], axis=0))

    cc = jnp.zeros((8, d), F32).at[0].set(c_ctx).at[1].set(c[0])
    mods = _mod_call(cc, ada_w, ada_b)
    cs, sn = _rope_tables(l, n)

    n_mix = 4
    for i in range(depth):
        mixer, slot = i % n_mix, i // n_mix
        last = i == depth - 1
        mod12 = jnp.concatenate([mods[i, 0].reshape(N_MOD, d), mods[i, 1].reshape(N_MOD, d)], axis=0)
        g1 = norm1_g[i].reshape(1, d)
        g2 = norm2_g[i].reshape(1, d)
        win = ffn_w_in[i].astype(BF16)
        wout = ffn_w_out[i].astype(BF16)
        fin = final_g.reshape(1, d) if last else None
        if isinstance(xs, tuple) and mixer != 0:
            xs = jnp.concatenate(xs, axis=0)

        if mixer == 0:
            w = da_wqkv[slot]
            lam = da_lambda[slot].astype(F32)
            lam_init = 0.8 - 0.6 * math.exp(-0.3 * i)
            lam_full = (jnp.exp(jnp.sum(lam[0] * lam[1])) - jnp.exp(jnp.sum(lam[2] * lam[3]))
                        + lam_init).reshape(1)
            q, k, vt3 = _proj_da_call(xs, g1, mod12, cs, sn, w[:, :2 * d].astype(BF16),
                                      w[:, 2 * d:].T.astype(BF16), l)
            a = _diff_attn_call(lam_full, q, k, vt3, da_subln_g[slot].reshape(LANES, 1), l, lc,
                                1.0 - lam_init)
            xs = _post_call(xs, mod12, g2, da_wo[slot].astype(BF16), win, wout, [a], "plain",
                            l, lc, fin)
        elif mixer == 1:
            bg, z = _proj_sc_call(xs, g1, mod12, sc_w_in[slot].astype(BF16), l)
            xs = _post_call(xs, mod12, g2, sc_w_out[slot].astype(BF16), win, wout,
                            [bg, z, z, z, sc_conv_w[slot]], "conv", l, lc, fin)
        elif mixer == 2:
            w = wa_wqkv[slot]
            kw = WA_KV_HEADS * HEAD_DIM
            dup = lambda m: jnp.tile(m.reshape(d, WA_KV_HEADS, 1, HEAD_DIM), (1, 1, 2, 1)).reshape(d, 2 * kw)
            w2 = jnp.concatenate([w[:, :d], dup(w[:, d:d + kw]), dup(w[:, d + kw:])], axis=1)
            q, k2, vt2 = _proj_wa_call(xs, g1, mod12, cs, sn, w2.astype(BF16), l, 2 * kw)
            a = _win_attn_call(wa_sinks[slot].astype(F32), q, k2, vt2, l, lc)
            xs = _post_call(xs, mod12, g2, wa_wo[slot].astype(BF16), win, wout, [a], "plain",
                            l, lc, fin)
        else:
            w = hg_w_in[slot]
            p = jax.nn.softmax(hg_lb.astype(F32), axis=1)
            lb = (jnp.cumsum(p, axis=1) - p[:, :1])[:, i]
            qh, gates, v, vt3, og = _proj_hg_call(xs, g1, mod12, jnp.log(lb), jnp.log1p(-lb),
                                                  w.astype(BF16), l)
            o2 = _hgrn_call(qh, gates, v, vt3, l, lc)
            xs = _post_call(xs, mod12, g2, hg_wo[slot].astype(BF16), win, wout,
                            [o2, o2, og, hg_gnorm_g[slot].reshape(1, LANES)], "hgrn", l, lc, fin)
    return xs[:l].reshape(1, l, d)
```

```python
import functools
import math

import numpy as np
import jax
import jax.numpy as jnp
from jax import lax
from jax.experimental import pallas as pl
from jax.experimental.pallas import tpu as pltpu

F32 = jnp.float32
BF16 = jnp.bfloat16

HEAD_DIM = 64
GRID_W = 64
ROPE_BASE = 10000.0
RMS_EPS = 1e-6
N_MOD = 6
WINDOW = 128
WA_KV_HEADS = 4
SC_WIDTH = 3
LANES = 128
NEG_BIG = -1e30
LOG2E = math.log2(math.e)

TM = 512
SLAB = 128
HG_CHUNK = 128
HG_PER_STEP = 2
DA_TQ = 256
DA_TK = 1024
DA_ONES_ROWS = 16
DA_UNROLLS = (4, 2)
WA_TQ = 512
FFN_CHUNK = 256
VMEM_LIMIT = 56 * 1024 * 1024

NT_DIMS = (((1,), (1,)), ((), ()))


def _cparams(n_axes):
    return pltpu.CompilerParams(
        dimension_semantics=("arbitrary",) * n_axes, vmem_limit_bytes=VMEM_LIMIT)


def _const_spec(shape):
    nd = len(shape)
    return pl.BlockSpec(shape, lambda *_: (0,) * nd, pipeline_mode=pl.Buffered(1))


def _is_ctx_rows(tile_idx, tm, l):
    row = tile_idx * tm + lax.broadcasted_iota(jnp.int32, (tm, 1), 0)
    return row >= l


def _pick(mod_ref, k, is_ctx):
    return jnp.where(is_ctx, mod_ref[k:k + 1, :], mod_ref[N_MOD + k:N_MOD + k + 1, :])


def _rms(x):
    return x * lax.rsqrt(jnp.mean(x * x, axis=-1, keepdims=True) + RMS_EPS)


def _norm_mod(x, g, shift, scale):
    return (_rms(x) * g) * (1.0 + scale) + shift


def _silu(x):
    h = 0.5 * x
    return h + h * jnp.tanh(h)


def _rope(chunk, cs, sn, first_half):
    partner = jnp.where(first_half, pltpu.roll(chunk, LANES - 16, 1), pltpu.roll(chunk, 16, 1))
    return chunk * cs + partner * sn


def _stream_specs(xs, l):
    if not isinstance(xs, tuple):
        n, d = xs.shape
        return [pl.BlockSpec((TM, d), lambda i: (i, 0))], [xs], n, d
    lat, tail = xs
    d = lat.shape[1]
    nl = l // TM
    assert lat.shape[0] == l and tail.shape[0] % TM == 0
    return ([pl.BlockSpec((TM, d), lambda i: (jnp.minimum(i, nl - 1), 0)),
             pl.BlockSpec((TM, d), lambda i: (jnp.maximum(i - nl, 0), 0))],
            [lat, tail], l + tail.shape[0], d)


def _stream_tile(x_refs, tile_idx, tm, l):
    if len(x_refs) == 1:
        return x_refs[0][...]
    return jnp.where(tile_idx * tm < l, x_refs[0][...], x_refs[1][...])


def _first_half_lanes():
    lane = lax.broadcasted_iota(jnp.int32, (1, LANES), 1)
    return (lane & 31) < 16


def _mod_kernel(cc_ref, w_ref, b_ref, o_ref):
    a = _silu(cc_ref[...])
    o_ref[0] = jnp.dot(a, w_ref[0], precision=lax.Precision.HIGHEST,
                       preferred_element_type=F32) + b_ref[0]


def _mod_call(cc, ada_w, ada_b):
    depth, d, nd = ada_w.shape
    tn = 1536
    assert nd % tn == 0
    return pl.pallas_call(
        _mod_kernel,
        grid=(depth, nd // tn),
        in_specs=[pl.BlockSpec((8, d), lambda i, j: (0, 0)),
                  pl.BlockSpec((1, d, tn), lambda i, j: (i, 0, j)),
                  pl.BlockSpec((1, 1, tn), lambda i, j: (i, 0, j))],
        out_specs=pl.BlockSpec((1, 8, tn), lambda i, j: (i, 0, j)),
        out_shape=jax.ShapeDtypeStruct((depth, 8, nd), F32),
        compiler_params=_cparams(2),
        name="adaln_mod",
    )(cc, ada_w, ada_b.reshape(depth, 1, nd))


def _proj_da_kernel(*refs, tm, l, d, n_x):
    x_refs = refs[:n_x]
    g_ref, mod_ref, cs_ref, sn_ref, wqk_ref, wvt_ref, q_ref, k_ref, vt_ref = refs[n_x:]
    is_ctx = _is_ctx_rows(pl.program_id(0), tm, l)
    x = _stream_tile(x_refs, pl.program_id(0), tm, l)
    h = _norm_mod(x, g_ref[...], _pick(mod_ref, 0, is_ctx), _pick(mod_ref, 1, is_ctx))
    hb = h.astype(BF16)
    qk = jnp.dot(hb, wqk_ref[...], preferred_element_type=F32)
    cs, sn, first = cs_ref[...], sn_ref[...], _first_half_lanes()
    nq = d // LANES
    for j in range(2 * nq):
        r = _rope(qk[:, j * LANES:(j + 1) * LANES], cs, sn, first)
        if j < nq:
            q_ref[:, j * LANES:(j + 1) * LANES] = (r * (LOG2E * HEAD_DIM ** -0.5)).astype(BF16)
        else:
            k_ref[:, (j - nq) * LANES:(j - nq + 1) * LANES] = r.astype(BF16)
    vt = lax.dot_general(wvt_ref[...], hb, NT_DIMS, preferred_element_type=F32)
    ones = jnp.ones((DA_ONES_ROWS, SLAB), BF16)
    vr = LANES + DA_ONES_ROWS
    for s in range(tm // SLAB):
        for hh in range(d // LANES):
            vt_ref[s, hh * vr:hh * vr + LANES, :] = (
                vt[hh * LANES:(hh + 1) * LANES, s * SLAB:(s + 1) * SLAB].astype(BF16))
            vt_ref[s, hh * vr + LANES:(hh + 1) * vr, :] = ones


def _proj_da_call(xs, g, mod12, cs, sn, wqk, wvt, l):
    x_specs, x_args, n, d = _stream_specs(xs, l)
    row = lambda i: (i, 0)
    dv = d // LANES * (LANES + DA_ONES_ROWS)
    return pl.pallas_call(
        functools.partial(_proj_da_kernel, tm=TM, l=l, d=d, n_x=len(x_args)),
        grid=(n // TM,),
        in_specs=x_specs + [_const_spec((1, d)), _const_spec((2 * N_MOD, d)),
                            pl.BlockSpec((TM, LANES), row), pl.BlockSpec((TM, LANES), row),
                            _const_spec(wqk.shape), _const_spec(wvt.shape)],
        out_specs=[pl.BlockSpec((TM, d), row), pl.BlockSpec((TM, d), row),
                   pl.BlockSpec((TM // SLAB, dv, SLAB), lambda i: (i, 0, 0))],
        out_shape=[jax.ShapeDtypeStruct((n, d), BF16), jax.ShapeDtypeStruct((n, d), BF16),
                   jax.ShapeDtypeStruct((n // SLAB, dv, SLAB), BF16)],
        compiler_params=_cparams(1),
        name="proj_diff_attn",
    )(*x_args, g, mod12, cs, sn, wqk, wvt)


def _proj_sc_kernel(x_ref, g_ref, mod_ref, w_ref, b_ref, z_ref, *, tm, l, d):
    is_ctx = _is_ctx_rows(pl.program_id(0), tm, l)
    h = _norm_mod(x_ref[...], g_ref[...], _pick(mod_ref, 0, is_ctx), _pick(mod_ref, 1, is_ctx))
    y = jnp.dot(h.astype(BF16), w_ref[...], preferred_element_type=F32)
    b_ref[...] = y[:, :d].astype(BF16)
    z_ref[...] = y[:, d:2 * d] * y[:, 2 * d:]


def _proj_sc_call(xs, g, mod12, w, l):
    n, d = xs.shape
    row = lambda i: (i, 0)
    return pl.pallas_call(
        functools.partial(_proj_sc_kernel, tm=TM, l=l, d=d),
        grid=(n // TM,),
        in_specs=[pl.BlockSpec((TM, d), row), _const_spec((1, d)), _const_spec((2 * N_MOD, d)),
                  _const_spec(w.shape)],
        out_specs=[pl.BlockSpec((TM, d), row), pl.BlockSpec((TM, d), row)],
        out_shape=[jax.ShapeDtypeStruct((n, d), BF16), jax.ShapeDtypeStruct((n, d), F32)],
        compiler_params=_cparams(1),
        name="proj_short_conv",
    )(xs, g, mod12, w)


def _proj_wa_kernel(x_ref, g_ref, mod_ref, cs_ref, sn_ref, w_ref, q_ref, k_ref, v_ref,
                    *, tm, l, d, kw):
    is_ctx = _is_ctx_rows(pl.program_id(0), tm, l)
    h = _norm_mod(x_ref[...], g_ref[...], _pick(mod_ref, 0, is_ctx), _pick(mod_ref, 1, is_ctx))
    y = jnp.dot(h.astype(BF16), w_ref[...], preferred_element_type=F32)
    cs, sn, first = cs_ref[...], sn_ref[...], _first_half_lanes()
    for j in range(d // LANES):
        r = _rope(y[:, j * LANES:(j + 1) * LANES], cs, sn, first)
        q_ref[:, j * LANES:(j + 1) * LANES] = (r * (LOG2E * HEAD_DIM ** -0.5)).astype(BF16)
    for j in range(kw // LANES):
        r = _rope(y[:, d + j * LANES:d + (j + 1) * LANES], cs, sn, first)
        k_ref[:, j * LANES:(j + 1) * LANES] = r.astype(BF16)
    vt = y[:, d + kw:].T
    for s in range(tm // SLAB):
        v_ref[s] = vt[:, s * SLAB:(s + 1) * SLAB].astype(BF16)


def _proj_wa_call(xs, g, mod12, cs, sn, w, l, kw):
    n, d = xs.shape
    row = lambda i: (i, 0)
    return pl.pallas_call(
        functools.partial(_proj_wa_kernel, tm=TM, l=l, d=d, kw=kw),
        grid=(n // TM,),
        in_specs=[pl.BlockSpec((TM, d), row), _const_spec((1, d)), _const_spec((2 * N_MOD, d)),
                  pl.BlockSpec((TM, LANES), row), pl.BlockSpec((TM, LANES), row),
                  _const_spec(w.shape)],
        out_specs=[pl.BlockSpec((TM, d), row), pl.BlockSpec((TM, kw), row),
                   pl.BlockSpec((TM // SLAB, kw, SLAB), lambda i: (i, 0, 0))],
        out_shape=[jax.ShapeDtypeStruct((n, d), BF16), jax.ShapeDtypeStruct((n, kw), BF16),
                   jax.ShapeDtypeStruct((n // SLAB, kw, SLAB), BF16)],
        compiler_params=_cparams(1),
        name="proj_window_attn",
    )(xs, g, mod12, cs, sn, w)


def _log1p_unit(x):
    return jnp.log(1.0 + x)


def _proj_hg_kernel(x_ref, g_ref, mod_ref, lla_ref, l1m_ref, w_ref,
                    q_ref, gate_ref, v_ref, vt_ref, og_ref, *, tm, l, d):
    is_ctx = _is_ctx_rows(pl.program_id(0), tm, l)
    h = _norm_mod(x_ref[...], g_ref[...], _pick(mod_ref, 0, is_ctx), _pick(mod_ref, 1, is_ctx))
    hb = h.astype(BF16)
    part = lambda j: jnp.dot(hb, w_ref[:, j * d:(j + 1) * d], preferred_element_type=F32)
    q_ref[...] = _silu(part(0)).astype(BF16)
    for dr in range(2):
        f = part(1 + dr)
        log_sig = jnp.minimum(f, 0.0) - _log1p_unit(jnp.exp(-jnp.abs(f)))
        a = lla_ref[dr:dr + 1, :]
        c = l1m_ref[dr:dr + 1, :] + log_sig
        gate_ref[dr] = jnp.maximum(a, c) + _log1p_unit(jnp.exp(-jnp.abs(a - c)))
    v = part(3)
    v_ref[...] = v.astype(BF16)
    vt = v.T
    for s in range(tm // SLAB):
        vt_ref[s] = vt[:, s * SLAB:(s + 1) * SLAB].astype(BF16)
    og_ref[...] = _silu(part(4)).astype(BF16)


def _proj_hg_call(xs, g, mod12, lla, l1m, w, l):
    n, d = xs.shape
    row = lambda i: (i, 0)
    return pl.pallas_call(
        functools.partial(_proj_hg_kernel, tm=TM, l=l, d=d),
        grid=(n // TM,),
        in_specs=[pl.BlockSpec((TM, d), row), _const_spec((1, d)), _const_spec((2 * N_MOD, d)),
                  _const_spec((2, d)), _const_spec((2, d)), _const_spec(w.shape)],
        out_specs=[pl.BlockSpec((TM, d), row), pl.BlockSpec((2, TM, d), lambda i: (0, i, 0)),
                   pl.BlockSpec((TM, d), row),
                   pl.BlockSpec((TM // SLAB, d, SLAB), lambda i: (i, 0, 0)),
                   pl.BlockSpec((TM, d), row)],
        out_shape=[jax.ShapeDtypeStruct((n, d), BF16), jax.ShapeDtypeStruct((2, n, d), F32),
                   jax.ShapeDtypeStruct((n, d), BF16), jax.ShapeDtypeStruct((n // SLAB, d, SLAB), BF16),
                   jax.ShapeDtypeStruct((n, d), BF16)],
        compiler_params=_cparams(1),
        name="proj_hgrn",
    )(xs, g, mod12, lla, l1m, w)


def _diff_attn_kernel(lam_ref, q_ref, k_ref, vt_ref, g_ref, o_ref, m_ref, acc_ref,
                      sa_ref, sb_ref, ma_ref, mb_ref, *, tq, tk, l, lc, n, unroll, post_scale):
    lane = lax.broadcasted_iota(jnp.int32, (1, LANES), 1)
    sub = 2 * vt_ref.shape[2]
    n_chunks = l // tk

    def tile_rows(t):
        if isinstance(t, int):
            return slice(t * tq, (t + 1) * tq)
        return pl.ds(pl.multiple_of(t * tq, tq), tq)

    def stacked(t):
        q = q_ref[tile_rows(t), :]
        zero = jnp.zeros_like(q)
        return jnp.concatenate([jnp.where(lane < HEAD_DIM, q, zero),
                                jnp.where(lane >= HEAD_DIM, q, zero)], axis=0)

    def scores(kc, qcat):
        return lax.dot_general(kc, qcat, NT_DIMS, preferred_element_type=F32)

    def pv(pair0, n_pairs, p):
        out = None
        for j in range(n_pairs):
            s0 = 2 * (pair0 + j)
            vt = jnp.concatenate([vt_ref[s0], vt_ref[s0 + 1]], axis=1)
            t = jnp.dot(vt, p[j * sub:(j + 1) * sub, :], preferred_element_type=F32)
            out = t if out is None else out + t
        return out

    def absorb_context(qcat, slot):
        s = scores(k_ref[l:l + lc, :], qcat)
        mx = jnp.max(s, axis=0, keepdims=True)
        m_ref[slot] = mx
        acc_ref[slot] = pv(l // sub, lc // sub, jnp.exp2(s - mx).astype(BF16))

    def qk(c, qcat, s_ref, mx_ref):
        off = pl.multiple_of(c * tk, LANES)
        s = scores(k_ref[pl.ds(off, tk), :], qcat)
        s_ref[...] = s
        mx_ref[...] = jnp.max(s, axis=0, keepdims=True)

    def absorb(c, s_ref, mx_ref, slot):
        m_old = m_ref[slot]
        m_new = jnp.maximum(m_old, mx_ref[...])
        alpha = jnp.exp2(m_old - m_new)
        p = jnp.exp2(s_ref[...] - m_new).astype(BF16)
        m_ref[slot] = m_new
        acc_ref[slot] = alpha * acc_ref[slot] + pv((c * tk) // sub, tk // sub, p)

    def start_sweep(qcat, slot):
        qk(0, qcat, sa_ref, ma_ref)
        absorb_context(qcat, slot)

    def finish(t, slot):
        acc = acc_ref[slot]
        acc = acc[0:LANES, :] * (1.0 / acc[LANES:LANES + 1, :])
        o = acc[:, :tq] - lam_ref[0] * acc[:, tq:]
        o = o * lax.rsqrt(jnp.mean(o * o, axis=0, keepdims=True) + RMS_EPS)
        o = o * (g_ref[...] * post_scale)
        o_ref[tile_rows(t), :] = o.T.astype(BF16)

    bufs = ((sa_ref, ma_ref), (sb_ref, mb_ref))

    def sweep(t, slot):
        qcat = stacked(t)

        def group(c0, count):
            for j in range(count):
                qk(c0 + j + 1, qcat, *bufs[(j + 1) % 2])
                absorb(c0 + j, *bufs[j % 2], slot)

        def body(i, inner):
            group(unroll * i, unroll)
            return inner

        lax.fori_loop(0, n_chunks // unroll - 1, body, 0)
        group(n_chunks - unroll, unroll - 1)
        start_sweep(stacked(t + 1), 1 - slot)
        absorb(n_chunks - 1, *bufs[(unroll - 1) % 2], slot)
        finish(t, slot)

    def tile_pair(i, carry):
        sweep(2 * i, 0)
        sweep(2 * i + 1, 1)
        return carry

    start_sweep(stacked(0), 0)
    lax.fori_loop(0, l // (2 * tq), tile_pair, 0)
    for t in range(l // tq, n // tq):
        absorb_context(stacked(t), 0)
        finish(t, 0)


def _diff_attn_call(lam, q, k, vt3, subln_g, l, lc, post_scale):
    n, d = q.shape
    nh = d // LANES
    nslab, vrows, slab = vt3.shape
    vrows //= nh
    tq = DA_TQ
    tk = min(DA_TK, l // 2)
    assert l % (2 * tq) == 0 and n % tq == 0 and n > l and tk % (2 * slab) == 0
    assert l % (2 * slab) == 0 and lc % (2 * slab) == 0
    assert l % (2 * tk) == 0
    return pl.pallas_call(
        functools.partial(_diff_attn_kernel, tq=tq, tk=tk, l=l, lc=lc, n=n,
                          unroll=next(u for u in DA_UNROLLS if l % (u * tk) == 0),
                          post_scale=post_scale),
        grid=(nh,),
        in_specs=[pl.BlockSpec(memory_space=pltpu.SMEM),
                  pl.BlockSpec((n, LANES), lambda h: (0, h)),
                  pl.BlockSpec((n, LANES), lambda h: (0, h)),
                  pl.BlockSpec((nslab, vrows, slab), lambda h: (0, h, 0)),
                  pl.BlockSpec((LANES, 1), lambda h: (0, 0))],
        out_specs=pl.BlockSpec((n, LANES), lambda h: (0, h)),
        out_shape=jax.ShapeDtypeStruct((n, d), BF16),
        scratch_shapes=[pltpu.VMEM((2, 1, 2 * tq), F32),
                        pltpu.VMEM((2, vrows, 2 * tq), F32),
                        pltpu.VMEM((tk, 2 * tq), F32), pltpu.VMEM((tk, 2 * tq), F32),
                        pltpu.VMEM((1, 2 * tq), F32), pltpu.VMEM((1, 2 * tq), F32)],
        compiler_params=_cparams(1),
        name="diff_attention",
    )(lam, q, k, vt3, subln_g)


def _win_attn_kernel(sink_ref, q_ref, kc_ref, kp_ref, ko_ref, kn_ref,
                     vc_ref, vp_ref, vo_ref, vn_ref, o_ref, *, tq, l, lc, rep, ng):
    j = pl.program_id(0)
    w = WINDOW
    gw = rep * HEAD_DIM
    a = lax.broadcasted_iota(jnp.int32, (w, w), 0)
    b = lax.broadcasted_iota(jnp.int32, (w, w), 1)
    lane = lax.broadcasted_iota(jnp.int32, (1, LANES), 1)
    low = lane < HEAD_DIM
    biases = []
    for t in range(tq // w):
        q0 = j * tq + t * w
        latent = q0 < l
        own_bias = jnp.where(latent, 0.0, NEG_BIG)
        prev_bias = jnp.where(jnp.logical_and(latent, q0 >= w), 0.0, NEG_BIG)
        next_bias = jnp.where(q0 + w < l, 0.0, NEG_BIG)
        bias = jnp.concatenate(
            [jnp.where(a >= b, prev_bias, NEG_BIG),
             jnp.full((w, w), own_bias, F32),
             jnp.where(a <= b, next_bias, NEG_BIG)], axis=0)
        biases.append(jnp.concatenate([bias] * rep, axis=1))
    for g in range(ng):
        gl = slice(g * LANES, (g + 1) * LANES)
        kc = kc_ref[:, gl]
        kwin = jnp.concatenate([kp_ref[:, gl], ko_ref[:, gl], kn_ref[:, gl]], axis=0)
        vtc = jnp.concatenate([vc_ref[s, gl, :] for s in range(lc // w)], axis=1)
        vtwin = jnp.concatenate([vp_ref[0, gl, :]] + [vo_ref[s, gl, :] for s in range(tq // w)]
                                + [vn_ref[0, gl, :]], axis=1)
        for t in range(tq // w):
            qs, sinks = [], []
            for r in range(rep):
                c0 = g * gw + (r // 2) * LANES
                qc = q_ref[t * w:(t + 1) * w, c0:c0 + LANES]
                qs.append(jnp.where(low if r % 2 == 0 else ~low, qc, jnp.zeros_like(qc)))
                sinks.append(jnp.full((1, w), sink_ref[g * rep + r] * LOG2E, F32))
            qst = jnp.concatenate(qs, axis=0)
            sink = jnp.concatenate(sinks, axis=1)
            s_c = lax.dot_general(kc, qst, NT_DIMS, preferred_element_type=F32)
            s_w = (lax.dot_general(kwin[t * w:(t + 3) * w, :], qst, NT_DIMS,
                                   preferred_element_type=F32) + biases[t])
            mx = jnp.maximum(jnp.maximum(jnp.max(s_c, axis=0, keepdims=True),
                                         jnp.max(s_w, axis=0, keepdims=True)), sink)
            e_c = jnp.exp2(s_c - mx)
            e_w = jnp.exp2(s_w - mx)
            den = (jnp.exp2(sink - mx) + jnp.sum(e_c, axis=0, keepdims=True)
                   + jnp.sum(e_w, axis=0, keepdims=True))
            ot = (jnp.dot(vtc, e_c.astype(BF16), preferred_element_type=F32)
                  + jnp.dot(vtwin[:, t * w:(t + 3) * w], e_w.astype(BF16),
                            preferred_element_type=F32)) * (1.0 / den)
            heads = [ot[:, r * w:(r + 1) * w].T for r in range(rep)]
            for c in range(rep // 2):
                c0 = g * gw + c * LANES
                o_ref[t * w:(t + 1) * w, c0:c0 + LANES] = jnp.where(
                    low, heads[2 * c], heads[2 * c + 1]).astype(BF16)


def _win_attn_call(sinks, q, k2, vt2, l, lc):
    n, d = q.shape
    ng = WA_KV_HEADS
    rep = d // HEAD_DIM // ng
    kw = k2.shape[1]
    tq = WA_TQ
    assert n % tq == 0 and tq % WINDOW == 0 and lc % WINDOW == 0 and l % lc == 0 and rep % 2 == 0
    assert kw == ng * LANES
    r = tq // WINDOW
    last = n // WINDOW - 1
    prev = lambda j: jnp.maximum(j * r - 1, 0)
    nxt = lambda j: jnp.minimum((j + 1) * r, last)
    k_specs = [pl.BlockSpec((lc, kw), lambda j: (l // lc, 0)),
               pl.BlockSpec((WINDOW, kw), lambda j: (prev(j), 0)),
               pl.BlockSpec((tq, kw), lambda j: (j, 0)),
               pl.BlockSpec((WINDOW, kw), lambda j: (nxt(j), 0))]
    vt_specs = [pl.BlockSpec((lc // WINDOW, kw, WINDOW), lambda j: (l // lc, 0, 0)),
                pl.BlockSpec((1, kw, WINDOW), lambda j: (prev(j), 0, 0)),
                pl.BlockSpec((r, kw, WINDOW), lambda j: (j, 0, 0)),
                pl.BlockSpec((1, kw, WINDOW), lambda j: (nxt(j), 0, 0))]
    return pl.pallas_call(
        functools.partial(_win_attn_kernel, tq=tq, l=l, lc=lc, rep=rep, ng=ng),
        grid=(n // tq,),
        in_specs=[pl.BlockSpec(memory_space=pltpu.SMEM),
                  pl.BlockSpec((tq, d), lambda j: (j, 0))] + k_specs + vt_specs,
        out_specs=pl.BlockSpec((tq, d), lambda j: (j, 0)),
        out_shape=jax.ShapeDtypeStruct((n, d), BF16),
        compiler_params=_cparams(1),
        name="window_attention",
    )(sinks, q, k2, k2, k2, k2, vt2, vt2, vt2, vt2)


def _hgrn_consts(chunk):
    n_lv = int(math.log2(chunk))
    walls, masks_all = [], []
    for direction in range(2):
        tau = np.arange(chunk) if direction == 0 else chunk - 1 - np.arange(chunk)
        tr, tc = tau[:, None], tau[None, :]
        blocks = [tc <= tr]
        masks = []
        for lv in range(n_lv):
            h = chunk >> (lv + 1)
            mid = (tr // (2 * h)) * 2 * h + h - 1
            second_r = (tr % (2 * h)) >= h
            second_c = (tc % (2 * h)) >= h
            if h > 1:
                blocks.append(np.where(second_r, (tc > mid) & (tc <= tr), (tc > tr) & (tc <= mid)))
            masks.append(second_r & ~second_c & ((tr // (2 * h)) == (tc // (2 * h))))
        masks.append(tr == tc)
        w = np.concatenate(blocks, axis=0).astype(np.float32)
        walls.append(np.concatenate([w, w], axis=1))
        masks_all.append(np.stack(masks).astype(np.float32))
    return np.stack(walls), np.stack(masks_all)


def _hgrn_kernel(wall_ref, mask_ref, q_ref, g_ref, v_ref, vt_ref, o_ref, st_ref,
                 *, chunk, n_lv, n_heads, n_steps, per_step):
    dr = pl.program_id(0)
    step = pl.program_id(1)

    @pl.when(step == 0)
    def _():
        st_ref[...] = jnp.zeros_like(st_ref)

    @pl.when(step >= n_steps)
    def _():
        o_ref[...] = jnp.zeros_like(o_ref)

    @pl.when(step < n_steps)
    def _():
        heads = [slice(hh * LANES, (hh + 1) * LANES) for hh in range(n_heads)]
        nt = lambda x, y: lax.dot_general(x, y, NT_DIMS, preferred_element_type=F32)
        for j in range(per_step):
            idx = jnp.where(dr == 0, j, per_step - 1 - j)
            rows = pl.ds(pl.multiple_of(idx * chunk, chunk), chunk)
            g_all = g_ref[0, rows, :]
            g_hi = g_all.astype(BF16)
            g_lo = (g_all - g_hi.astype(F32)).astype(BF16)
            sums_all = jnp.dot(wall_ref[0], jnp.concatenate([g_hi, g_lo], axis=0),
                               preferred_element_type=F32)
            q16 = q_ref[rows, :]
            q_all = q16.astype(F32)
            f_all = jnp.exp(g_all)
            k_all = 1.0 - f_all
            k16 = k_all.astype(BF16)
            v_all = v_ref[rows, :]
            vt_all = vt_ref[idx]
            b = sums_all[0:chunk]
            b_end = jnp.sum(g_all, axis=0, keepdims=True)

            qb = (q_all * jnp.exp(b)).astype(BF16)
            o = [nt(qb[:, sl], st_ref[hh].astype(BF16)) for hh, sl in enumerate(heads)]
            att = [nt(q16[:, sl], k16[:, sl]) * mask_ref[0, n_lv] for sl in heads]
            for lv in range(n_lv - 1):
                e = jnp.exp(sums_all[(1 + lv) * chunk:(2 + lv) * chunk])
                qe = (q_all * e).astype(BF16)
                ke = (k_all * e).astype(BF16)
                mask = mask_ref[0, lv]
                att = [a + nt(qe[:, sl], ke[:, sl]) * mask for a, sl in zip(att, heads)]
            qe = (q_all * f_all).astype(BF16)
            mask = mask_ref[0, n_lv - 1]
            att = [a + nt(qe[:, sl], k16[:, sl]) * mask for a, sl in zip(att, heads)]
            out = [o[hh] + jnp.dot(att[hh].astype(BF16), v_all[:, sl], preferred_element_type=F32)
                   for hh, sl in enumerate(heads)]
            o_ref[0, rows, :] = jnp.concatenate(out, axis=1)

            ks = (k_all * jnp.exp(b_end - b)).astype(BF16)
            decay = jnp.exp(b_end)
            for hh, sl in enumerate(heads):
                st_ref[hh] = st_ref[hh] * decay[:, sl] + jnp.dot(vt_all[sl, :], ks[:, sl],
                                                                 preferred_element_type=F32)


def _hgrn_call(qh, gates, v, vt3, l, lc):
    n, d = qh.shape
    c = HG_CHUNK
    per = HG_PER_STEP
    rows = per * c
    nh = d // LANES
    nx, ncc, ntot = l // rows, lc // rows, n // rows
    nc = nx + ncc
    assert n % rows == 0 and l % rows == 0 and lc % rows == 0 and vt3.shape[2] == c
    n_lv = int(math.log2(c))
    wall_np, mask_np = _hgrn_consts(c)
    wall = jnp.asarray(wall_np, dtype=BF16)
    masks = jnp.asarray(mask_np, dtype=F32)

    def blk(dr, s):
        fwd = jnp.where(s < ncc, nx + s, s - ncc)
        bwd = jnp.where(s < ncc, nx + ncc - 1 - s, nx - 1 - (s - ncc))
        return jnp.where(s >= nc, s, jnp.where(dr == 0, fwd, bwd))

    return pl.pallas_call(
        functools.partial(_hgrn_kernel, chunk=c, n_lv=n_lv, n_heads=nh, n_steps=nc, per_step=per),
        grid=(2, ntot),
        in_specs=[pl.BlockSpec((1,) + wall_np.shape[1:], lambda dr, s: (dr, 0, 0)),
                  pl.BlockSpec((1,) + mask_np.shape[1:], lambda dr, s: (dr, 0, 0, 0)),
                  pl.BlockSpec((rows, d), lambda dr, s: (blk(dr, s), 0)),
                  pl.BlockSpec((1, rows, d), lambda dr, s: (dr, blk(dr, s), 0)),
                  pl.BlockSpec((rows, d), lambda dr, s: (blk(dr, s), 0)),
                  pl.BlockSpec((per, d, c), lambda dr, s: (blk(dr, s), 0, 0))],
        out_specs=pl.BlockSpec((1, rows, d), lambda dr, s: (dr, blk(dr, s), 0)),
        out_shape=jax.ShapeDtypeStruct((2, n, d), F32),
        scratch_shapes=[pltpu.VMEM((nh, LANES, LANES), F32)],
        compiler_params=_cparams(2),
        name="hgrn_scan",
    )(wall, masks, qh, gates, v, vt3)


def _post_kernel(*refs, mode, final, tm, l, lc, d, f, n_x):
    it = iter(refs)
    x_refs = [next(it) for _ in range(n_x)]
    mod_ref, g2_ref, wo_ref, win_ref, wout_ref = (next(it) for _ in range(5))
    if mode == "plain":
        a_ref = next(it)
    elif mode == "conv":
        b_ref, z_ref, zp_ref, zn_ref, cw_ref = (next(it) for _ in range(5))
    else:
        ofw_ref, obw_ref, og_ref, gn_ref = (next(it) for _ in range(4))
    fin_ref = next(it) if final else None
    o_ref = next(it)

    ti = pl.program_id(0)
    is_ctx = _is_ctx_rows(ti, tm, l)
    if mode == "plain":
        a = a_ref[...]
    elif mode == "conv":
        z = z_ref[...]
        rl = lax.broadcasted_iota(jnp.int32, (tm, 1), 0)
        row = ti * tm + rl
        z_prev = jnp.where(rl == 0, zp_ref[7:8, :], pltpu.roll(z, 1, 0))
        z_next = jnp.where(rl == tm - 1, zn_ref[0:1, :], pltpu.roll(z, tm - 1, 0))
        z_prev = jnp.where((row == 0) | (row == l), 0.0, z_prev)
        z_next = jnp.where((row == l - 1) | (row == l + lc - 1), 0.0, z_next)
        conv = cw_ref[0:1, :] * z_prev + cw_ref[1:2, :] * z + cw_ref[2:3, :] * z_next
        a = (b_ref[...].astype(F32) * conv).astype(BF16)
    else:
        gn = gn_ref[...]
        heads = []
        for hh in range(d // LANES):
            sl = slice(hh * LANES, (hh + 1) * LANES)
            oh = _rms(ofw_ref[0, :, sl] + obw_ref[0, :, sl])
            heads.append((oh * gn * og_ref[:, sl].astype(F32)).astype(BF16))
        a = jnp.concatenate(heads, axis=1)

    y = jnp.dot(a, wo_ref[...], preferred_element_type=F32)
    x = _stream_tile(x_refs, ti, tm, l) + _pick(mod_ref, 2, is_ctx) * y
    h = _norm_mod(x, g2_ref[...], _pick(mod_ref, 3, is_ctx), _pick(mod_ref, 4, is_ctx)).astype(BF16)
    acc = jnp.zeros((tm, d), F32)
    for c0 in range(0, f, FFN_CHUNK):
        u = jnp.dot(h, win_ref[:, c0:c0 + FFN_CHUNK], preferred_element_type=F32)
        w = jnp.dot(h, win_ref[:, f + c0:f + c0 + FFN_CHUNK], preferred_element_type=F32)
        act = (_silu(u) * w).astype(BF16)
        acc = acc + jnp.dot(act, wout_ref[c0:c0 + FFN_CHUNK, :], preferred_element_type=F32)
    x = x + _pick(mod_ref, 5, is_ctx) * acc
    if final:
        x = _rms(x) * fin_ref[...]
    o_ref[...] = x


def _post_call(xs, mod12, g2, wo, win, wout, mixer_inputs, mode, l, lc, final_g=None):
    x_specs, x_args, n, d = _stream_specs(xs, l)
    f = wout.shape[0]
    final = final_g is not None
    assert f % FFN_CHUNK == 0 and l % TM == 0 and n % TM == 0
    n_out = l if final else n
    row = lambda i: (i, 0)
    in_specs = x_specs + [_const_spec((2 * N_MOD, d)), _const_spec((1, d)),
                          _const_spec(wo.shape), _const_spec(win.shape), _const_spec(wout.shape)]
    args = x_args + [mod12, g2, wo, win, wout]
    if mode == "plain":
        in_specs += [pl.BlockSpec((TM, d), row)]
    elif mode == "conv":
        r8 = TM // 8
        last8 = n // 8 - 1
        in_specs += [pl.BlockSpec((TM, d), row), pl.BlockSpec((TM, d), row),
                     pl.BlockSpec((8, d), lambda i: (jnp.maximum(i * r8 - 1, 0), 0)),
                     pl.BlockSpec((8, d), lambda i: (jnp.minimum((i + 1) * r8, last8), 0)),
                     _const_spec((SC_WIDTH, d))]
    else:
        in_specs += [pl.BlockSpec((1, TM, d), lambda i: (0, i, 0)),
                     pl.BlockSpec((1, TM, d), lambda i: (1, i, 0)),
                     pl.BlockSpec((TM, d), row), _const_spec((1, LANES))]
    args += list(mixer_inputs)
    if final:
        in_specs += [_const_spec((1, d))]
        args += [final_g]
    return pl.pallas_call(
        functools.partial(_post_kernel, mode=mode, final=final, tm=TM, l=l, lc=lc, d=d, f=f,
                          n_x=len(x_args)),
        grid=(n_out // TM,),
        in_specs=in_specs,
        out_specs=pl.BlockSpec((TM, d), row),
        out_shape=jax.ShapeDtypeStruct((n_out, d), F32),
        compiler_params=_cparams(1),
        name="post_" + mode + ("_final" if final else ""),
    )(*args)


def _rope_tables(l, n):
    rows = l // GRID_W
    r = jnp.repeat(jnp.arange(rows), GRID_W).astype(F32)
    c = jnp.tile(jnp.arange(GRID_W), rows).astype(F32)
    half = HEAD_DIM // 2
    inv_freq = 1.0 / (ROPE_BASE ** (jnp.arange(0, half, 2, dtype=F32) / half))
    ar, ac = r[:, None] * inv_freq, c[:, None] * inv_freq
    cs = jnp.concatenate([jnp.cos(ar), jnp.cos(ar), jnp.cos(ac), jnp.cos(ac)], axis=-1)
    sn = jnp.concatenate([-jnp.sin(ar), jnp.sin(ar), -jnp.sin(ac), jnp.sin(ac)], axis=-1)
    cs = jnp.concatenate([cs, jnp.ones((n - l, HEAD_DIM), F32)], axis=0)
    sn = jnp.concatenate([sn, jnp.zeros((n - l, HEAD_DIM), F32)], axis=0)
    return jnp.tile(cs, (1, 2)), jnp.tile(sn, (1, 2))


def kernel(x, c, ctx, c_ctx, ada_w, ada_b, norm1_g, norm2_g, ffn_w_in, ffn_w_out, final_g,
           da_wqkv, da_lambda, da_subln_g, da_wo, sc_w_in, sc_conv_w, sc_w_out,
           wa_wqkv, wa_sinks, wa_wo, hg_w_in, hg_lb, hg_gnorm_g, hg_wo):
    b, l, d = x.shape
    lc = ctx.shape[1]
    depth = ada_w.shape[0]
    assert b == 1 and d % LANES == 0 and l % TM == 0
    n = -(-(l + lc) // TM) * TM
    xs = (x[0], jnp.concatenate([ctx[0], jnp.zeros((n - l - lc, d), x.dtype)], axis=0))

    cc = jnp.zeros((8, d), F32).at[0].set(c_ctx).at[1].set(c[0])
    mods = _mod_call(cc, ada_w, ada_b)
    cs, sn = _rope_tables(l, n)

    n_mix = 4
    for i in range(depth):
        mixer, slot = i % n_mix, i // n_mix
        last = i == depth - 1
        mod12 = jnp.concatenate([mods[i, 0].reshape(N_MOD, d), mods[i, 1].reshape(N_MOD, d)], axis=0)
        g1 = norm1_g[i].reshape(1, d)
        g2 = norm2_g[i].reshape(1, d)
        win = ffn_w_in[i].astype(BF16)
        wout = ffn_w_out[i].astype(BF16)
        fin = final_g.reshape(1, d) if last else None
        if isinstance(xs, tuple) and mixer != 0:
            xs = jnp.concatenate(xs, axis=0)

        if mixer == 0:
            w = da_wqkv[slot]
            lam = da_lambda[slot].astype(F32)
            lam_init = 0.8 - 0.6 * math.exp(-0.3 * i)
            lam_full = (jnp.exp(jnp.sum(lam[0] * lam[1])) - jnp.exp(jnp.sum(lam[2] * lam[3]))
                        + lam_init).reshape(1)
            q, k, vt3 = _proj_da_call(xs, g1, mod12, cs, sn, w[:, :2 * d].astype(BF16),
                                      w[:, 2 * d:].T.astype(BF16), l)
            a = _diff_attn_call(lam_full, q, k, vt3, da_subln_g[slot].reshape(LANES, 1), l, lc,
                                1.0 - lam_init)
            xs = _post_call(xs, mod12, g2, da_wo[slot].astype(BF16), win, wout, [a], "plain",
                            l, lc, fin)
        elif mixer == 1:
            bg, z = _proj_sc_call(xs, g1, mod12, sc_w_in[slot].astype(BF16), l)
            xs = _post_call(xs, mod12, g2, sc_w_out[slot].astype(BF16), win, wout,
                            [bg, z, z, z, sc_conv_w[slot]], "conv", l, lc, fin)
        elif mixer == 2:
            w = wa_wqkv[slot]
            kw = WA_KV_HEADS * HEAD_DIM
            dup = lambda m: jnp.tile(m.reshape(d, WA_KV_HEADS, 1, HEAD_DIM), (1, 1, 2, 1)).reshape(d, 2 * kw)
            w2 = jnp.concatenate([w[:, :d], dup(w[:, d:d + kw]), dup(w[:, d + kw:])], axis=1)
            q, k2, vt2 = _proj_wa_call(xs, g1, mod12, cs, sn, w2.astype(BF16), l, 2 * kw)
            a = _win_attn_call(wa_sinks[slot].astype(F32), q, k2, vt2, l, lc)
            xs = _post_call(xs, mod12, g2, wa_wo[slot].astype(BF16), win, wout, [a], "plain",
                            l, lc, fin)
        else:
            w = hg_w_in[slot]
            p = jax.nn.softmax(hg_lb.astype(F32), axis=1)
            lb = (jnp.cumsum(p, axis=1) - p[:, :1])[:, i]
            qh, gates, v, vt3, og = _proj_hg_call(xs, g1, mod12, jnp.log(lb), jnp.log1p(-lb),
                                                  w.astype(BF16), l)
            o2 = _hgrn_call(qh, gates, v, vt3, l, lc)
            xs = _post_call(xs, mod12, g2, hg_wo[slot].astype(BF16), win, wout,
                            [o2, o2, og, hg_gnorm_g[slot].reshape(1, LANES)], "hgrn", l, lc, fin)
    return xs[:l].reshape(1, l, d)
```

```python
import functools
import math

import numpy as np
import jax
import jax.numpy as jnp
from jax import lax
from jax.experimental import pallas as pl
from jax.experimental.pallas import tpu as pltpu

F32 = jnp.float32
BF16 = jnp.bfloat16

HEAD_DIM = 64
GRID_W = 64
ROPE_BASE = 10000.0
RMS_EPS = 1e-6
N_MOD = 6
WINDOW = 128
WA_KV_HEADS = 4
SC_WIDTH = 3
LANES = 128
NEG_BIG = -1e30
LOG2E = math.log2(math.e)

TM = 512
SLAB = 128
HG_CHUNK = 128
HG_PER_STEP = 2
DA_TQ = 256
DA_TK = 1024
DA_ONES_ROWS = 16
DA_UNROLLS = (4, 2)
WA_TQ = 512
FFN_CHUNK = 256
VMEM_LIMIT = 56 * 1024 * 1024

NT_DIMS = (((1,), (1,)), ((), ()))


def _cparams(n_axes):
    return pltpu.CompilerParams(
        dimension_semantics=("arbitrary",) * n_axes, vmem_limit_bytes=VMEM_LIMIT)


def _const_spec(shape):
    nd = len(shape)
    return pl.BlockSpec(shape, lambda *_: (0,) * nd, pipeline_mode=pl.Buffered(1))


def _is_ctx_rows(tile_idx, tm, l):
    row = tile_idx * tm + lax.broadcasted_iota(jnp.int32, (tm, 1), 0)
    return row >= l


def _pick(mod_ref, k, is_ctx):
    return jnp.where(is_ctx, mod_ref[k:k + 1, :], mod_ref[N_MOD + k:N_MOD + k + 1, :])


def _rms(x):
    return x * lax.rsqrt(jnp.mean(x * x, axis=-1, keepdims=True) + RMS_EPS)


def _norm_mod(x, g, shift, scale):
    return (_rms(x) * g) * (1.0 + scale) + shift


def _silu(x):
    h = 0.5 * x
    return h + h * jnp.tanh(h)


def _rope(chunk, cs, sn, first_half):
    partner = jnp.where(first_half, pltpu.roll(chunk, LANES - 16, 1), pltpu.roll(chunk, 16, 1))
    return chunk * cs + partner * sn


def _stream_specs(xs, l):
    if not isinstance(xs, tuple):
        n, d = xs.shape
        return [pl.BlockSpec((TM, d), lambda i: (i, 0))], [xs], n, d
    lat, tail = xs
    d = lat.shape[1]
    nl = l // TM
    assert lat.shape[0] == l and tail.shape[0] % TM == 0
    return ([pl.BlockSpec((TM, d), lambda i: (jnp.minimum(i, nl - 1), 0)),
             pl.BlockSpec((TM, d), lambda i: (jnp.maximum(i - nl, 0), 0))],
            [lat, tail], l + tail.shape[0], d)


def _stream_tile(x_refs, tile_idx, tm, l):
    if len(x_refs) == 1:
        return x_refs[0][...]
    return jnp.where(tile_idx * tm < l, x_refs[0][...], x_refs[1][...])


def _first_half_lanes():
    lane = lax.broadcasted_iota(jnp.int32, (1, LANES), 1)
    return (lane & 31) < 16


def _mod_kernel(cc_ref, w_ref, b_ref, o_ref):
    a = _silu(cc_ref[...])
    o_ref[0] = jnp.dot(a, w_ref[0], precision=lax.Precision.HIGHEST,
                       preferred_element_type=F32) + b_ref[0]


def _mod_call(cc, ada_w, ada_b):
    depth, d, nd = ada_w.shape
    tn = 1536
    assert nd % tn == 0
    return pl.pallas_call(
        _mod_kernel,
        grid=(depth, nd // tn),
        in_specs=[pl.BlockSpec((8, d), lambda i, j: (0, 0)),
                  pl.BlockSpec((1, d, tn), lambda i, j: (i, 0, j)),
                  pl.BlockSpec((1, 1, tn), lambda i, j: (i, 0, j))],
        out_specs=pl.BlockSpec((1, 8, tn), lambda i, j: (i, 0, j)),
        out_shape=jax.ShapeDtypeStruct((depth, 8, nd), F32),
        compiler_params=_cparams(2),
        name="adaln_mod",
    )(cc, ada_w, ada_b.reshape(depth, 1, nd))


def _proj_da_kernel(*refs, tm, l, d, n_x):
    x_refs = refs[:n_x]
    g_ref, mod_ref, cs_ref, sn_ref, wqk_ref, wvt_ref, q_ref, k_ref, vt_ref = refs[n_x:]
    is_ctx = _is_ctx_rows(pl.program_id(0), tm, l)
    x = _stream_tile(x_refs, pl.program_id(0), tm, l)
    h = _norm_mod(x, g_ref[...], _pick(mod_ref, 0, is_ctx), _pick(mod_ref, 1, is_ctx))
    hb = h.astype(BF16)
    qk = jnp.dot(hb, wqk_ref[...], preferred_element_type=F32)
    cs, sn, first = cs_ref[...], sn_ref[...], _first_half_lanes()
    nq = d // LANES
    for j in range(2 * nq):
        r = _rope(qk[:, j * LANES:(j + 1) * LANES], cs, sn, first)
        if j < nq:
            q_ref[:, j * LANES:(j + 1) * LANES] = (r * (LOG2E * HEAD_DIM ** -0.5)).astype(BF16)
        else:
            k_ref[:, (j - nq) * LANES:(j - nq + 1) * LANES] = r.astype(BF16)
    vt = lax.dot_general(wvt_ref[...], hb, NT_DIMS, preferred_element_type=F32)
    ones = jnp.ones((DA_ONES_ROWS, SLAB), BF16)
    vr = LANES + DA_ONES_ROWS
    for s in range(tm // SLAB):
        for hh in range(d // LANES):
            vt_ref[s, hh * vr:hh * vr + LANES, :] = (
                vt[hh * LANES:(hh + 1) * LANES, s * SLAB:(s + 1) * SLAB].astype(BF16))
            vt_ref[s, hh * vr + LANES:(hh + 1) * vr, :] = ones


def _proj_da_call(xs, g, mod12, cs, sn, wqk, wvt, l):
    x_specs, x_args, n, d = _stream_specs(xs, l)
    row = lambda i: (i, 0)
    dv = d // LANES * (LANES + DA_ONES_ROWS)
    return pl.pallas_call(
        functools.partial(_proj_da_kernel, tm=TM, l=l, d=d, n_x=len(x_args)),
        grid=(n // TM,),
        in_specs=x_specs + [_const_spec((1, d)), _const_spec((2 * N_MOD, d)),
                            pl.BlockSpec((TM, LANES), row), pl.BlockSpec((TM, LANES), row),
                            _const_spec(wqk.shape), _const_spec(wvt.shape)],
        out_specs=[pl.BlockSpec((TM, d), row), pl.BlockSpec((TM, d), row),
                   pl.BlockSpec((TM // SLAB, dv, SLAB), lambda i: (i, 0, 0))],
        out_shape=[jax.ShapeDtypeStruct((n, d), BF16), jax.ShapeDtypeStruct((n, d), BF16),
                   jax.ShapeDtypeStruct((n // SLAB, dv, SLAB), BF16)],
        compiler_params=_cparams(1),
        name="proj_diff_attn",
    )(*x_args, g, mod12, cs, sn, wqk, wvt)


def _proj_sc_kernel(x_ref, g_ref, mod_ref, w_ref, b_ref, z_ref, *, tm, l, d):
    is_ctx = _is_ctx_rows(pl.program_id(0), tm, l)
    h = _norm_mod(x_ref[...], g_ref[...], _pick(mod_ref, 0, is_ctx), _pick(mod_ref, 1, is_ctx))
    y = jnp.dot(h.astype(BF16), w_ref[...], preferred_element_type=F32)
    b_ref[...] = y[:, :d].astype(BF16)
    z_ref[...] = y[:, d:2 * d] * y[:, 2 * d:]


def _proj_sc_call(xs, g, mod12, w, l):
    n, d = xs.shape
    row = lambda i: (i, 0)
    return pl.pallas_call(
        functools.partial(_proj_sc_kernel, tm=TM, l=l, d=d),
        grid=(n // TM,),
        in_specs=[pl.BlockSpec((TM, d), row), _const_spec((1, d)), _const_spec((2 * N_MOD, d)),
                  _const_spec(w.shape)],
        out_specs=[pl.BlockSpec((TM, d), row), pl.BlockSpec((TM, d), row)],
        out_shape=[jax.ShapeDtypeStruct((n, d), BF16), jax.ShapeDtypeStruct((n, d), F32)],
        compiler_params=_cparams(1),
        name="proj_short_conv",
    )(xs, g, mod12, w)


def _proj_wa_kernel(x_ref, g_ref, mod_ref, cs_ref, sn_ref, w_ref, q_ref, k_ref, v_ref,
                    *, tm, l, d, kw):
    is_ctx = _is_ctx_rows(pl.program_id(0), tm, l)
    h = _norm_mod(x_ref[...], g_ref[...], _pick(mod_ref, 0, is_ctx), _pick(mod_ref, 1, is_ctx))
    y = jnp.dot(h.astype(BF16), w_ref[...], preferred_element_type=F32)
    cs, sn, first = cs_ref[...], sn_ref[...], _first_half_lanes()
    for j in range(d // LANES):
        r = _rope(y[:, j * LANES:(j + 1) * LANES], cs, sn, first)
        q_ref[:, j * LANES:(j + 1) * LANES] = (r * (LOG2E * HEAD_DIM ** -0.5)).astype(BF16)
    for j in range(kw // LANES):
        r = _rope(y[:, d + j * LANES:d + (j + 1) * LANES], cs, sn, first)
        k_ref[:, j * LANES:(j + 1) * LANES] = r.astype(BF16)
    vt = y[:, d + kw:].T
    for s in range(tm // SLAB):
        v_ref[s] = vt[:, s * SLAB:(s + 1) * SLAB].astype(BF16)


def _proj_wa_call(xs, g, mod12, cs, sn, w, l, kw):
    n, d = xs.shape
    row = lambda i: (i, 0)
    return pl.pallas_call(
        functools.partial(_proj_wa_kernel, tm=TM, l=l, d=d, kw=kw),
        grid=(n // TM,),
        in_specs=[pl.BlockSpec((TM, d), row), _const_spec((1, d)), _const_spec((2 * N_MOD, d)),
                  pl.BlockSpec((TM, LANES), row), pl.BlockSpec((TM, LANES), row),
                  _const_spec(w.shape)],
        out_specs=[pl.BlockSpec((TM, d), row), pl.BlockSpec((TM, kw), row),
                   pl.BlockSpec((TM // SLAB, kw, SLAB), lambda i: (i, 0, 0))],
        out_shape=[jax.ShapeDtypeStruct((n, d), BF16), jax.ShapeDtypeStruct((n, kw), BF16),
                   jax.ShapeDtypeStruct((n // SLAB, kw, SLAB), BF16)],
        compiler_params=_cparams(1),
        name="proj_window_attn",
    )(xs, g, mod12, cs, sn, w)


def _log1p_unit(x):
    return jnp.log(1.0 + x)


def _proj_hg_kernel(x_ref, g_ref, mod_ref, lla_ref, l1m_ref, w_ref,
                    q_ref, gate_ref, v_ref, vt_ref, og_ref, *, tm, l, d):
    is_ctx = _is_ctx_rows(pl.program_id(0), tm, l)
    h = _norm_mod(x_ref[...], g_ref[...], _pick(mod_ref, 0, is_ctx), _pick(mod_ref, 1, is_ctx))
    hb = h.astype(BF16)
    part = lambda j: jnp.dot(hb, w_ref[:, j * d:(j + 1) * d], preferred_element_type=F32)
    q_ref[...] = _silu(part(0)).astype(BF16)
    for dr in range(2):
        f = part(1 + dr)
        log_sig = jnp.minimum(f, 0.0) - _log1p_unit(jnp.exp(-jnp.abs(f)))
        a = lla_ref[dr:dr + 1, :]
        c = l1m_ref[dr:dr + 1, :] + log_sig
        gate_ref[dr] = jnp.maximum(a, c) + _log1p_unit(jnp.exp(-jnp.abs(a - c)))
    v = part(3)
    v_ref[...] = v.astype(BF16)
    vt = v.T
    for s in range(tm // SLAB):
        vt_ref[s] = vt[:, s * SLAB:(s + 1) * SLAB].astype(BF16)
    og_ref[...] = _silu(part(4)).astype(BF16)


def _proj_hg_call(xs, g, mod12, lla, l1m, w, l):
    n, d = xs.shape
    row = lambda i: (i, 0)
    return pl.pallas_call(
        functools.partial(_proj_hg_kernel, tm=TM, l=l, d=d),
        grid=(n // TM,),
        in_specs=[pl.BlockSpec((TM, d), row), _const_spec((1, d)), _const_spec((2 * N_MOD, d)),
                  _const_spec((2, d)), _const_spec((2, d)), _const_spec(w.shape)],
        out_specs=[pl.BlockSpec((TM, d), row), pl.BlockSpec((2, TM, d), lambda i: (0, i, 0)),
                   pl.BlockSpec((TM, d), row),
                   pl.BlockSpec((TM // SLAB, d, SLAB), lambda i: (i, 0, 0)),
                   pl.BlockSpec((TM, d), row)],
        out_shape=[jax.ShapeDtypeStruct((n, d), BF16), jax.ShapeDtypeStruct((2, n, d), F32),
                   jax.ShapeDtypeStruct((n, d), BF16), jax.ShapeDtypeStruct((n // SLAB, d, SLAB), BF16),
                   jax.ShapeDtypeStruct((n, d), BF16)],
        compiler_params=_cparams(1),
        name="proj_hgrn",
    )(xs, g, mod12, lla, l1m, w)


def _diff_attn_kernel(lam_ref, q_ref, k_ref, vt_ref, g_ref, o_ref, m_ref, acc_ref,
                      sa_ref, sb_ref, ma_ref, mb_ref, *, tq, tk, l, lc, n, unroll, post_scale):
    lane = lax.broadcasted_iota(jnp.int32, (1, LANES), 1)
    sub = 2 * vt_ref.shape[2]
    n_chunks = l // tk

    def tile_rows(t):
        if isinstance(t, int):
            return slice(t * tq, (t + 1) * tq)
        return pl.ds(pl.multiple_of(t * tq, tq), tq)

    def stacked(t):
        q = q_ref[tile_rows(t), :]
        zero = jnp.zeros_like(q)
        return jnp.concatenate([jnp.where(lane < HEAD_DIM, q, zero),
                                jnp.where(lane >= HEAD_DIM, q, zero)], axis=0)

    def scores(kc, qcat):
        return lax.dot_general(kc, qcat, NT_DIMS, preferred_element_type=F32)

    def pv(pair0, n_pairs, p):
        out = None
        for j in range(n_pairs):
            s0 = 2 * (pair0 + j)
            vt = jnp.concatenate([vt_ref[s0], vt_ref[s0 + 1]], axis=1)
            t = jnp.dot(vt, p[j * sub:(j + 1) * sub, :], preferred_element_type=F32)
            out = t if out is None else out + t
        return out

    def absorb_context(qcat, slot):
        s = scores(k_ref[l:l + lc, :], qcat)
        mx = jnp.max(s, axis=0, keepdims=True)
        m_ref[slot] = mx
        acc_ref[slot] = pv(l // sub, lc // sub, jnp.exp2(s - mx).astype(BF16))

    def qk(c, qcat, s_ref, mx_ref):
        off = pl.multiple_of(c * tk, LANES)
        s = scores(k_ref[pl.ds(off, tk), :], qcat)
        s_ref[...] = s
        mx_ref[...] = jnp.max(s, axis=0, keepdims=True)

    def absorb(c, s_ref, mx_ref, slot):
        m_old = m_ref[slot]
        m_new = jnp.maximum(m_old, mx_ref[...])
        alpha = jnp.exp2(m_old - m_new)
        p = jnp.exp2(s_ref[...] - m_new).astype(BF16)
        m_ref[slot] = m_new
        acc_ref[slot] = alpha * acc_ref[slot] + pv((c * tk) // sub, tk // sub, p)

    def start_sweep(qcat, slot):
        qk(0, qcat, sa_ref, ma_ref)
        absorb_context(qcat, slot)

    def finish(t, slot):
        acc = acc_ref[slot]
        acc = acc[0:LANES, :] * (1.0 / acc[LANES:LANES + 1, :])
        o = acc[:, :tq] - lam_ref[0] * acc[:, tq:]
        o = o * lax.rsqrt(jnp.mean(o * o, axis=0, keepdims=True) + RMS_EPS)
        o = o * (g_ref[...] * post_scale)
        o_ref[tile_rows(t), :] = o.T.astype(BF16)

    bufs = ((sa_ref, ma_ref), (sb_ref, mb_ref))

    def sweep(t, slot):
        qcat = stacked(t)

        def group(c0, count):
            for j in range(count):
                qk(c0 + j + 1, qcat, *bufs[(j + 1) % 2])
                absorb(c0 + j, *bufs[j % 2], slot)

        def body(i, inner):
            group(unroll * i, unroll)
            return inner

        lax.fori_loop(0, n_chunks // unroll - 1, body, 0)
        group(n_chunks - unroll, unroll - 1)
        start_sweep(stacked(t + 1), 1 - slot)
        absorb(n_chunks - 1, *bufs[(unroll - 1) % 2], slot)
        finish(t, slot)

    def tile_pair(i, carry):
        sweep(2 * i, 0)
        sweep(2 * i + 1, 1)
        return carry

    start_sweep(stacked(0), 0)
    lax.fori_loop(0, l // (2 * tq), tile_pair, 0)
    for t in range(l // tq, n // tq):
        absorb_context(stacked(t), 0)
        finish(t, 0)


def _diff_attn_call(lam, q, k, vt3, subln_g, l, lc, post_scale):
    n, d = q.shape
    nh = d // LANES
    nslab, vrows, slab = vt3.shape
    vrows //= nh
    tq = DA_TQ
    tk = min(DA_TK, l // 2)
    assert l % (2 * tq) == 0 and n % tq == 0 and n > l and tk % (2 * slab) == 0
    assert l % (2 * slab) == 0 and lc % (2 * slab) == 0
    assert l % (2 * tk) == 0
    return pl.pallas_call(
        functools.partial(_diff_attn_kernel, tq=tq, tk=tk, l=l, lc=lc, n=n,
                          unroll=next(u for u in DA_UNROLLS if l % (u * tk) == 0),
                          post_scale=post_scale),
        grid=(nh,),
        in_specs=[pl.BlockSpec(memory_space=pltpu.SMEM),
                  pl.BlockSpec((n, LANES), lambda h: (0, h)),
                  pl.BlockSpec((n, LANES), lambda h: (0, h)),
                  pl.BlockSpec((nslab, vrows, slab), lambda h: (0, h, 0)),
                  pl.BlockSpec((LANES, 1), lambda h: (0, 0))],
        out_specs=pl.BlockSpec((n, LANES), lambda h: (0, h)),
        out_shape=jax.ShapeDtypeStruct((n, d), BF16),
        scratch_shapes=[pltpu.VMEM((2, 1, 2 * tq), F32),
                        pltpu.VMEM((2, vrows, 2 * tq), F32),
                        pltpu.VMEM((tk, 2 * tq), F32), pltpu.VMEM((tk, 2 * tq), F32),
                        pltpu.VMEM((1, 2 * tq), F32), pltpu.VMEM((1, 2 * tq), F32)],
        compiler_params=_cparams(1),
        name="diff_attention",
    )(lam, q, k, vt3, subln_g)


def _win_attn_kernel(sink_ref, q_ref, kc_ref, kp_ref, ko_ref, kn_ref,
                     vc_ref, vp_ref, vo_ref, vn_ref, o_ref, *, tq, l, lc, rep, ng):
    j = pl.program_id(0)
    w = WINDOW
    gw = rep * HEAD_DIM
    a = lax.broadcasted_iota(jnp.int32, (w, w), 0)
    b = lax.broadcasted_iota(jnp.int32, (w, w), 1)
    lane = lax.broadcasted_iota(jnp.int32, (1, LANES), 1)
    low = lane < HEAD_DIM
    biases = []
    for t in range(tq // w):
        q0 = j * tq + t * w
        latent = q0 < l
        own_bias = jnp.where(latent, 0.0, NEG_BIG)
        prev_bias = jnp.where(jnp.logical_and(latent, q0 >= w), 0.0, NEG_BIG)
        next_bias = jnp.where(q0 + w < l, 0.0, NEG_BIG)
        bias = jnp.concatenate(
            [jnp.where(a >= b, prev_bias, NEG_BIG),
             jnp.full((w, w), own_bias, F32),
             jnp.where(a <= b, next_bias, NEG_BIG)], axis=0)
        biases.append(jnp.concatenate([bias] * rep, axis=1))
    for g in range(ng):
        gl = slice(g * LANES, (g + 1) * LANES)
        kc = kc_ref[:, gl]
        kwin = jnp.concatenate([kp_ref[:, gl], ko_ref[:, gl], kn_ref[:, gl]], axis=0)
        vtc = jnp.concatenate([vc_ref[s, gl, :] for s in range(lc // w)], axis=1)
        vtwin = jnp.concatenate([vp_ref[0, gl, :]] + [vo_ref[s, gl, :] for s in range(tq // w)]
                                + [vn_ref[0, gl, :]], axis=1)
        for t in range(tq // w):
            qs, sinks = [], []
            for r in range(rep):
                c0 = g * gw + (r // 2) * LANES
                qc = q_ref[t * w:(t + 1) * w, c0:c0 + LANES]
                qs.append(jnp.where(low if r % 2 == 0 else ~low, qc, jnp.zeros_like(qc)))
                sinks.append(jnp.full((1, w), sink_ref[g * rep + r] * LOG2E, F32))
            qst = jnp.concatenate(qs, axis=0)
            sink = jnp.concatenate(sinks, axis=1)
            s_c = lax.dot_general(kc, qst, NT_DIMS, preferred_element_type=F32)
            s_w = (lax.dot_general(kwin[t * w:(t + 3) * w, :], qst, NT_DIMS,
                                   preferred_element_type=F32) + biases[t])
            mx = jnp.maximum(jnp.maximum(jnp.max(s_c, axis=0, keepdims=True),
                                         jnp.max(s_w, axis=0, keepdims=True)), sink)
            e_c = jnp.exp2(s_c - mx)
            e_w = jnp.exp2(s_w - mx)
            den = (jnp.exp2(sink - mx) + jnp.sum(e_c, axis=0, keepdims=True)
                   + jnp.sum(e_w, axis=0, keepdims=True))
            ot = (jnp.dot(vtc, e_c.astype(BF16), preferred_element_type=F32)
                  + jnp.dot(vtwin[:, t * w:(t + 3) * w], e_w.astype(BF16),
                            preferred_element_type=F32)) * (1.0 / den)
            heads = [ot[:, r * w:(r + 1) * w].T for r in range(rep)]
            for c in range(rep // 2):
                c0 = g * gw + c * LANES
                o_ref[t * w:(t + 1) * w, c0:c0 + LANES] = jnp.where(
                    low, heads[2 * c], heads[2 * c + 1]).astype(BF16)


def _win_attn_call(sinks, q, k2, vt2, l, lc):
    n, d = q.shape
    ng = WA_KV_HEADS
    rep = d // HEAD_DIM // ng
    kw = k2.shape[1]
    tq = WA_TQ
    assert n % tq == 0 and tq % WINDOW == 0 and lc % WINDOW == 0 and l % lc == 0 and rep % 2 == 0
    assert kw == ng * LANES
    r = tq // WINDOW
    last = n // WINDOW - 1
    prev = lambda j: jnp.maximum(j * r - 1, 0)
    nxt = lambda j: jnp.minimum((j + 1) * r, last)
    k_specs = [pl.BlockSpec((lc, kw), lambda j: (l // lc, 0)),
               pl.BlockSpec((WINDOW, kw), lambda j: (prev(j), 0)),
               pl.BlockSpec((tq, kw), lambda j: (j, 0)),
               pl.BlockSpec((WINDOW, kw), lambda j: (nxt(j), 0))]
    vt_specs = [pl.BlockSpec((lc // WINDOW, kw, WINDOW), lambda j: (l // lc, 0, 0)),
                pl.BlockSpec((1, kw, WINDOW), lambda j: (prev(j), 0, 0)),
                pl.BlockSpec((r, kw, WINDOW), lambda j: (j, 0, 0)),
                pl.BlockSpec((1, kw, WINDOW), lambda j: (nxt(j), 0, 0))]
    return pl.pallas_call(
        functools.partial(_win_attn_kernel, tq=tq, l=l, lc=lc, rep=rep, ng=ng),
        grid=(n // tq,),
        in_specs=[pl.BlockSpec(memory_space=pltpu.SMEM),
                  pl.BlockSpec((tq, d), lambda j: (j, 0))] + k_specs + vt_specs,
        out_specs=pl.BlockSpec((tq, d), lambda j: (j, 0)),
        out_shape=jax.ShapeDtypeStruct((n, d), BF16),
        compiler_params=_cparams(1),
        name="window_attention",
    )(sinks, q, k2, k2, k2, k2, vt2, vt2, vt2, vt2)


def _hgrn_consts(chunk):
    n_lv = int(math.log2(chunk))
    walls, masks_all = [], []
    for direction in range(2):
        tau = np.arange(chunk) if direction == 0 else chunk - 1 - np.arange(chunk)
        tr, tc = tau[:, None], tau[None, :]
        blocks = [tc <= tr]
        masks = []
        for lv in range(n_lv):
            h = chunk >> (lv + 1)
            mid = (tr // (2 * h)) * 2 * h + h - 1
            second_r = (tr % (2 * h)) >= h
            second_c = (tc % (2 * h)) >= h
            if 1 < h < chunk // 4:
                blocks.append(np.where(second_r, (tc > mid) & (tc <= tr), (tc > tr) & (tc <= mid)))
            masks.append(second_r & ~second_c & ((tr // (2 * h)) == (tc // (2 * h))))
        masks.append(tr == tc)
        w = np.concatenate(blocks, axis=0).astype(np.float32)
        walls.append(np.concatenate([w, w], axis=1))
        masks_all.append(np.stack(masks).astype(np.float32))
    return np.stack(walls), np.stack(masks_all)


def _hgrn_kernel(wall_ref, mask_ref, q_ref, g_ref, v_ref, vt_ref, o_ref, st_ref,
                 *, chunk, n_lv, n_heads, n_steps, per_step):
    dr = pl.program_id(0)
    step = pl.program_id(1)

    @pl.when(step == 0)
    def _():
        st_ref[...] = jnp.zeros_like(st_ref)

    @pl.when(step >= n_steps)
    def _():
        o_ref[...] = jnp.zeros_like(o_ref)

    @pl.when(step < n_steps)
    def _():
        heads = [slice(hh * LANES, (hh + 1) * LANES) for hh in range(n_heads)]
        nt = lambda x, y: lax.dot_general(x, y, NT_DIMS, preferred_element_type=F32)
        for j in range(per_step):
            idx = jnp.where(dr == 0, j, per_step - 1 - j)
            rows = pl.ds(pl.multiple_of(idx * chunk, chunk), chunk)
            g_all = g_ref[0, rows, :]
            g_hi = g_all.astype(BF16)
            g_lo = (g_all - g_hi.astype(F32)).astype(BF16)
            sums_all = jnp.dot(wall_ref[0], jnp.concatenate([g_hi, g_lo], axis=0),
                               preferred_element_type=F32)
            q16 = q_ref[rows, :]
            q_all = q16.astype(F32)
            f_all = jnp.exp(g_all)
            k_all = 1.0 - f_all
            k16 = k_all.astype(BF16)
            v_all = v_ref[rows, :]
            vt_all = vt_ref[idx]
            b = sums_all[0:chunk]
            b_end = jnp.sum(g_all, axis=0, keepdims=True)

            qb = (q_all * jnp.exp(b)).astype(BF16)
            o = [nt(qb[:, sl], st_ref[hh].astype(BF16)) for hh, sl in enumerate(heads)]
            att = [nt(q16[:, sl], k16[:, sl]) * mask_ref[0, n_lv] for sl in heads]
            half, quarter = chunk // 2, chunk // 4
            mid_row = lambda r: jnp.where(dr == 0, b[r - 1:r, :], b[r:r + 1, :])
            upper = lax.broadcasted_iota(jnp.int32, (chunk, 1), 0) >= half
            coarse = (-jnp.abs(b - mid_row(half)),
                      -jnp.abs(b - jnp.where(upper, mid_row(half + quarter), mid_row(quarter))))
            for lv in range(n_lv - 1):
                e = jnp.exp(coarse[lv] if lv < 2 else sums_all[(lv - 1) * chunk:lv * chunk])
                qe = (q_all * e).astype(BF16)
                ke = (k_all * e).astype(BF16)
                mask = mask_ref[0, lv]
                att = [a + nt(qe[:, sl], ke[:, sl]) * mask for a, sl in zip(att, heads)]
            qe = (q_all * f_all).astype(BF16)
            mask = mask_ref[0, n_lv - 1]
            att = [a + nt(qe[:, sl], k16[:, sl]) * mask for a, sl in zip(att, heads)]
            out = [o[hh] + jnp.dot(att[hh].astype(BF16), v_all[:, sl], preferred_element_type=F32)
                   for hh, sl in enumerate(heads)]
            o_ref[0, rows, :] = jnp.concatenate(out, axis=1)

            ks = (k_all * jnp.exp(b_end - b)).astype(BF16)
            decay = jnp.exp(b_end)
            for hh, sl in enumerate(heads):
                st_ref[hh] = st_ref[hh] * decay[:, sl] + jnp.dot(vt_all[sl, :], ks[:, sl],
                                                                 preferred_element_type=F32)


def _hgrn_call(qh, gates, v, vt3, l, lc):
    n, d = qh.shape
    c = HG_CHUNK
    per = HG_PER_STEP
    rows = per * c
    nh = d // LANES
    nx, ncc, ntot = l // rows, lc // rows, n // rows
    nc = nx + ncc
    assert n % rows == 0 and l % rows == 0 and lc % rows == 0 and vt3.shape[2] == c
    n_lv = int(math.log2(c))
    wall_np, mask_np = _hgrn_consts(c)
    wall = jnp.asarray(wall_np, dtype=BF16)
    masks = jnp.asarray(mask_np, dtype=F32)

    def blk(dr, s):
        fwd = jnp.where(s < ncc, nx + s, s - ncc)
        bwd = jnp.where(s < ncc, nx + ncc - 1 - s, nx - 1 - (s - ncc))
        return jnp.where(s >= nc, s, jnp.where(dr == 0, fwd, bwd))

    return pl.pallas_call(
        functools.partial(_hgrn_kernel, chunk=c, n_lv=n_lv, n_heads=nh, n_steps=nc, per_step=per),
        grid=(2, ntot),
        in_specs=[pl.BlockSpec((1,) + wall_np.shape[1:], lambda dr, s: (dr, 0, 0)),
                  pl.BlockSpec((1,) + mask_np.shape[1:], lambda dr, s: (dr, 0, 0, 0)),
                  pl.BlockSpec((rows, d), lambda dr, s: (blk(dr, s), 0)),
                  pl.BlockSpec((1, rows, d), lambda dr, s: (dr, blk(dr, s), 0)),
                  pl.BlockSpec((rows, d), lambda dr, s: (blk(dr, s), 0)),
                  pl.BlockSpec((per, d, c), lambda dr, s: (blk(dr, s), 0, 0))],
        out_specs=pl.BlockSpec((1, rows, d), lambda dr, s: (dr, blk(dr, s), 0)),
        out_shape=jax.ShapeDtypeStruct((2, n, d), F32),
        scratch_shapes=[pltpu.VMEM((nh, LANES, LANES), F32)],
        compiler_params=_cparams(2),
        name="hgrn_scan",
    )(wall, masks, qh, gates, v, vt3)


def _post_kernel(*refs, mode, final, tm, l, lc, d, f, n_x):
    it = iter(refs)
    x_refs = [next(it) for _ in range(n_x)]
    mod_ref, g2_ref, wo_ref, win_ref, wout_ref = (next(it) for _ in range(5))
    if mode == "plain":
        a_ref = next(it)
    elif mode == "conv":
        b_ref, z_ref, zp_ref, zn_ref, cw_ref = (next(it) for _ in range(5))
    else:
        ofw_ref, obw_ref, og_ref, gn_ref = (next(it) for _ in range(4))
    fin_ref = next(it) if final else None
    o_ref = next(it)

    ti = pl.program_id(0)
    is_ctx = _is_ctx_rows(ti, tm, l)
    if mode == "plain":
        a = a_ref[...]
    elif mode == "conv":
        z = z_ref[...]
        rl = lax.broadcasted_iota(jnp.int32, (tm, 1), 0)
        row = ti * tm + rl
        z_prev = jnp.where(rl == 0, zp_ref[7:8, :], pltpu.roll(z, 1, 0))
        z_next = jnp.where(rl == tm - 1, zn_ref[0:1, :], pltpu.roll(z, tm - 1, 0))
        z_prev = jnp.where((row == 0) | (row == l), 0.0, z_prev)
        z_next = jnp.where((row == l - 1) | (row == l + lc - 1), 0.0, z_next)
        conv = cw_ref[0:1, :] * z_prev + cw_ref[1:2, :] * z + cw_ref[2:3, :] * z_next
        a = (b_ref[...].astype(F32) * conv).astype(BF16)
    else:
        gn = gn_ref[...]
        heads = []
        for hh in range(d // LANES):
            sl = slice(hh * LANES, (hh + 1) * LANES)
            oh = _rms(ofw_ref[0, :, sl] + obw_ref[0, :, sl])
            heads.append((oh * gn * og_ref[:, sl].astype(F32)).astype(BF16))
        a = jnp.concatenate(heads, axis=1)

    y = jnp.dot(a, wo_ref[...], preferred_element_type=F32)
    x = _stream_tile(x_refs, ti, tm, l) + _pick(mod_ref, 2, is_ctx) * y
    h = _norm_mod(x, g2_ref[...], _pick(mod_ref, 3, is_ctx), _pick(mod_ref, 4, is_ctx)).astype(BF16)
    acc = jnp.zeros((tm, d), F32)
    for c0 in range(0, f, FFN_CHUNK):
        u = jnp.dot(h, win_ref[:, c0:c0 + FFN_CHUNK], preferred_element_type=F32)
        w = jnp.dot(h, win_ref[:, f + c0:f + c0 + FFN_CHUNK], preferred_element_type=F32)
        act = (_silu(u) * w).astype(BF16)
        acc = acc + jnp.dot(act, wout_ref[c0:c0 + FFN_CHUNK, :], preferred_element_type=F32)
    x = x + _pick(mod_ref, 5, is_ctx) * acc
    if final:
        x = _rms(x) * fin_ref[...]
    o_ref[...] = x


def _post_call(xs, mod12, g2, wo, win, wout, mixer_inputs, mode, l, lc, final_g=None):
    x_specs, x_args, n, d = _stream_specs(xs, l)
    f = wout.shape[0]
    final = final_g is not None
    assert f % FFN_CHUNK == 0 and l % TM == 0 and n % TM == 0
    n_out = l if final else n
    row = lambda i: (i, 0)
    in_specs = x_specs + [_const_spec((2 * N_MOD, d)), _const_spec((1, d)),
                          _const_spec(wo.shape), _const_spec(win.shape), _const_spec(wout.shape)]
    args = x_args + [mod12, g2, wo, win, wout]
    if mode == "plain":
        in_specs += [pl.BlockSpec((TM, d), row)]
    elif mode == "conv":
        r8 = TM // 8
        last8 = n // 8 - 1
        in_specs += [pl.BlockSpec((TM, d), row), pl.BlockSpec((TM, d), row),
                     pl.BlockSpec((8, d), lambda i: (jnp.maximum(i * r8 - 1, 0), 0)),
                     pl.BlockSpec((8, d), lambda i: (jnp.minimum((i + 1) * r8, last8), 0)),
                     _const_spec((SC_WIDTH, d))]
    else:
        in_specs += [pl.BlockSpec((1, TM, d), lambda i: (0, i, 0)),
                     pl.BlockSpec((1, TM, d), lambda i: (1, i, 0)),
                     pl.BlockSpec((TM, d), row), _const_spec((1, LANES))]
    args += list(mixer_inputs)
    if final:
        in_specs += [_const_spec((1, d))]
        args += [final_g]
    return pl.pallas_call(
        functools.partial(_post_kernel, mode=mode, final=final, tm=TM, l=l, lc=lc, d=d, f=f,
                          n_x=len(x_args)),
        grid=(n_out // TM,),
        in_specs=in_specs,
        out_specs=pl.BlockSpec((TM, d), row),
        out_shape=jax.ShapeDtypeStruct((n_out, d), F32),
        compiler_params=_cparams(1),
        name="post_" + mode + ("_final" if final else ""),
    )(*args)


def _rope_tables(l, n):
    rows = l // GRID_W
    r = jnp.repeat(jnp.arange(rows), GRID_W).astype(F32)
    c = jnp.tile(jnp.arange(GRID_W), rows).astype(F32)
    half = HEAD_DIM // 2
    inv_freq = 1.0 / (ROPE_BASE ** (jnp.arange(0, half, 2, dtype=F32) / half))
    ar, ac = r[:, None] * inv_freq, c[:, None] * inv_freq
    cs = jnp.concatenate([jnp.cos(ar), jnp.cos(ar), jnp.cos(ac), jnp.cos(ac)], axis=-1)
    sn = jnp.concatenate([-jnp.sin(ar), jnp.sin(ar), -jnp.sin(ac), jnp.sin(ac)], axis=-1)
    cs = jnp.concatenate([cs, jnp.ones((n - l, HEAD_DIM), F32)], axis=0)
    sn = jnp.concatenate([sn, jnp.zeros((n - l, HEAD_DIM), F32)], axis=0)
    return jnp.tile(cs, (1, 2)), jnp.tile(sn, (1, 2))


def kernel(x, c, ctx, c_ctx, ada_w, ada_b, norm1_g, norm2_g, ffn_w_in, ffn_w_out, final_g,
           da_wqkv, da_lambda, da_subln_g, da_wo, sc_w_in, sc_conv_w, sc_w_out,
           wa_wqkv, wa_sinks, wa_wo, hg_w_in, hg_lb, hg_gnorm_g, hg_wo):
    b, l, d = x.shape
    lc = ctx.shape[1]
    depth = ada_w.shape[0]
    assert b == 1 and d % LANES == 0 and l % TM == 0
    n = -(-(l + lc) // TM) * TM
    xs = (x[0], jnp.concatenate([ctx[0], jnp.zeros((n - l - lc, d), x.dtype)], axis=0))

    cc = jnp.zeros((8, d), F32).at[0].set(c_ctx).at[1].set(c[0])
    mods = _mod_call(cc, ada_w, ada_b)
    cs, sn = _rope_tables(l, n)

    n_mix = 4
    for i in range(depth):
        mixer, slot = i % n_mix, i // n_mix
        last = i == depth - 1
        mod12 = jnp.concatenate([mods[i, 0].reshape(N_MOD, d), mods[i, 1].reshape(N_MOD, d)], axis=0)
        g1 = norm1_g[i].reshape(1, d)
        g2 = norm2_g[i].reshape(1, d)
        win = ffn_w_in[i].astype(BF16)
        wout = ffn_w_out[i].astype(BF16)
        fin = final_g.reshape(1, d) if last else None
        if isinstance(xs, tuple) and mixer != 0:
            xs = jnp.concatenate(xs, axis=0)

        if mixer == 0:
            w = da_wqkv[slot]
            lam = da_lambda[slot].astype(F32)
            lam_init = 0.8 - 0.6 * math.exp(-0.3 * i)
            lam_full = (jnp.exp(jnp.sum(lam[0] * lam[1])) - jnp.exp(jnp.sum(lam[2] * lam[3]))
                        + lam_init).reshape(1)
            q, k, vt3 = _proj_da_call(xs, g1, mod12, cs, sn, w[:, :2 * d].astype(BF16),
                                      w[:, 2 * d:].T.astype(BF16), l)
            a = _diff_attn_call(lam_full, q, k, vt3, da_subln_g[slot].reshape(LANES, 1), l, lc,
                                1.0 - lam_init)
            xs = _post_call(xs, mod12, g2, da_wo[slot].astype(BF16), win, wout, [a], "plain",
                            l, lc, fin)
        elif mixer == 1:
            bg, z = _proj_sc_call(xs, g1, mod12, sc_w_in[slot].astype(BF16), l)
            xs = _post_call(xs, mod12, g2, sc_w_out[slot].astype(BF16), win, wout,
                            [bg, z, z, z, sc_conv_w[slot]], "conv", l, lc, fin)
        elif mixer == 2:
            w = wa_wqkv[slot]
            kw = WA_KV_HEADS * HEAD_DIM
            dup = lambda m: jnp.tile(m.reshape(d, WA_KV_HEADS, 1, HEAD_DIM), (1, 1, 2, 1)).reshape(d, 2 * kw)
            w2 = jnp.concatenate([w[:, :d], dup(w[:, d:d + kw]), dup(w[:, d + kw:])], axis=1)
            q, k2, vt2 = _proj_wa_call(xs, g1, mod12, cs, sn, w2.astype(BF16), l, 2 * kw)
            a = _win_attn_call(wa_sinks[slot].astype(F32), q, k2, vt2, l, lc)
            xs = _post_call(xs, mod12, g2, wa_wo[slot].astype(BF16), win, wout, [a], "plain",
                            l, lc, fin)
        else:
            w = hg_w_in[slot]
            p = jax.nn.softmax(hg_lb.astype(F32), axis=1)
            lb = (jnp.cumsum(p, axis=1) - p[:, :1])[:, i]
            qh, gates, v, vt3, og = _proj_hg_call(xs, g1, mod12, jnp.log(lb), jnp.log1p(-lb),
                                                  w.astype(BF16), l)
            o2 = _hgrn_call(qh, gates, v, vt3, l, lc)
            xs = _post_call(xs, mod12, g2, hg_wo[slot].astype(BF16), win, wout,
                            [o2, o2, og, hg_gnorm_g[slot].reshape(1, LANES)], "hgrn", l, lc, fin)
    return xs[:l].reshape(1, l, d)
```

```python
import functools
import math

import numpy as np
import jax
import jax.numpy as jnp
from jax import lax
from jax.experimental import pallas as pl
from jax.experimental.pallas import tpu as pltpu

F32 = jnp.float32
BF16 = jnp.bfloat16

HEAD_DIM = 64
GRID_W = 64
ROPE_BASE = 10000.0
RMS_EPS = 1e-6
N_MOD = 6
WINDOW = 128
WA_KV_HEADS = 4
SC_WIDTH = 3
LANES = 128
NEG_BIG = -1e30
LOG2E = math.log2(math.e)

TM = 512
SLAB = 128
HG_CHUNK = 128
HG_PER_STEP = 2
DA_TQ = 256
DA_TK = 1024
DA_ONES_ROWS = 16
DA_UNROLLS = (4, 2)
WA_TQ = 512
FFN_CHUNK = 256
VMEM_LIMIT = 56 * 1024 * 1024

NT_DIMS = (((1,), (1,)), ((), ()))


def _cparams(n_axes):
    return pltpu.CompilerParams(
        dimension_semantics=("arbitrary",) * n_axes, vmem_limit_bytes=VMEM_LIMIT)


def _const_spec(shape):
    nd = len(shape)
    return pl.BlockSpec(shape, lambda *_: (0,) * nd, pipeline_mode=pl.Buffered(1))


def _is_ctx_rows(tile_idx, tm, l):
    row = tile_idx * tm + lax.broadcasted_iota(jnp.int32, (tm, 1), 0)
    return row >= l


def _pick(mod_ref, k, is_ctx):
    return jnp.where(is_ctx, mod_ref[k:k + 1, :], mod_ref[N_MOD + k:N_MOD + k + 1, :])


def _rms(x):
    return x * lax.rsqrt(jnp.mean(x * x, axis=-1, keepdims=True) + RMS_EPS)


def _norm_mod(x, g, shift, scale):
    return (_rms(x) * g) * (1.0 + scale) + shift


def _silu(x):
    h = 0.5 * x
    return h + h * jnp.tanh(h)


def _rope(chunk, cs, sn, first_half):
    partner = jnp.where(first_half, pltpu.roll(chunk, LANES - 16, 1), pltpu.roll(chunk, 16, 1))
    return chunk * cs + partner * sn


def _stream_specs(xs, l):
    if not isinstance(xs, tuple):
        n, d = xs.shape
        return [pl.BlockSpec((TM, d), lambda i: (i, 0))], [xs], n, d
    lat, tail = xs
    d = lat.shape[1]
    nl = l // TM
    assert lat.shape[0] == l and tail.shape[0] % TM == 0
    return ([pl.BlockSpec((TM, d), lambda i: (jnp.minimum(i, nl - 1), 0)),
             pl.BlockSpec((TM, d), lambda i: (jnp.maximum(i - nl, 0), 0))],
            [lat, tail], l + tail.shape[0], d)


def _stream_tile(x_refs, tile_idx, tm, l):
    if len(x_refs) == 1:
        return x_refs[0][...]
    return jnp.where(tile_idx * tm < l, x_refs[0][...], x_refs[1][...])


def _first_half_lanes():
    lane = lax.broadcasted_iota(jnp.int32, (1, LANES), 1)
    return (lane & 31) < 16


def _mod_kernel(cc_ref, w_ref, b_ref, o_ref):
    a = _silu(cc_ref[...])
    o_ref[0] = jnp.dot(a, w_ref[0], precision=lax.Precision.HIGHEST,
                       preferred_element_type=F32) + b_ref[0]


def _mod_call(cc, ada_w, ada_b):
    depth, d, nd = ada_w.shape
    tn = 1536
    assert nd % tn == 0
    return pl.pallas_call(
        _mod_kernel,
        grid=(depth, nd // tn),
        in_specs=[pl.BlockSpec((8, d), lambda i, j: (0, 0)),
                  pl.BlockSpec((1, d, tn), lambda i, j: (i, 0, j)),
                  pl.BlockSpec((1, 1, tn), lambda i, j: (i, 0, j))],
        out_specs=pl.BlockSpec((1, 8, tn), lambda i, j: (i, 0, j)),
        out_shape=jax.ShapeDtypeStruct((depth, 8, nd), F32),
        compiler_params=_cparams(2),
        name="adaln_mod",
    )(cc, ada_w, ada_b.reshape(depth, 1, nd))


def _proj_da_kernel(*refs, tm, l, d, n_x):
    x_refs = refs[:n_x]
    g_ref, mod_ref, cs_ref, sn_ref, wqk_ref, wvt_ref, q_ref, k_ref, vt_ref = refs[n_x:]
    is_ctx = _is_ctx_rows(pl.program_id(0), tm, l)
    x = _stream_tile(x_refs, pl.program_id(0), tm, l)
    h = _norm_mod(x, g_ref[...], _pick(mod_ref, 0, is_ctx), _pick(mod_ref, 1, is_ctx))
    hb = h.astype(BF16)
    qk = jnp.dot(hb, wqk_ref[...], preferred_element_type=F32)
    cs, sn, first = cs_ref[...], sn_ref[...], _first_half_lanes()
    nq = d // LANES
    for j in range(2 * nq):
        r = _rope(qk[:, j * LANES:(j + 1) * LANES], cs, sn, first)
        if j < nq:
            q_ref[:, j * LANES:(j + 1) * LANES] = (r * (LOG2E * HEAD_DIM ** -0.5)).astype(BF16)
        else:
            k_ref[:, (j - nq) * LANES:(j - nq + 1) * LANES] = r.astype(BF16)
    vt = lax.dot_general(wvt_ref[...], hb, NT_DIMS, preferred_element_type=F32)
    ones = jnp.ones((DA_ONES_ROWS, SLAB), BF16)
    vr = LANES + DA_ONES_ROWS
    for s in range(tm // SLAB):
        for hh in range(d // LANES):
            vt_ref[s, hh * vr:hh * vr + LANES, :] = (
                vt[hh * LANES:(hh + 1) * LANES, s * SLAB:(s + 1) * SLAB].astype(BF16))
            vt_ref[s, hh * vr + LANES:(hh + 1) * vr, :] = ones


def _proj_da_call(xs, g, mod12, cs, sn, wqk, wvt, l):
    x_specs, x_args, n, d = _stream_specs(xs, l)
    row = lambda i: (i, 0)
    dv = d // LANES * (LANES + DA_ONES_ROWS)
    return pl.pallas_call(
        functools.partial(_proj_da_kernel, tm=TM, l=l, d=d, n_x=len(x_args)),
        grid=(n // TM,),
        in_specs=x_specs + [_const_spec((1, d)), _const_spec((2 * N_MOD, d)),
                            pl.BlockSpec((TM, LANES), row), pl.BlockSpec((TM, LANES), row),
                            _const_spec(wqk.shape), _const_spec(wvt.shape)],
        out_specs=[pl.BlockSpec((TM, d), row), pl.BlockSpec((TM, d), row),
                   pl.BlockSpec((TM // SLAB, dv, SLAB), lambda i: (i, 0, 0))],
        out_shape=[jax.ShapeDtypeStruct((n, d), BF16), jax.ShapeDtypeStruct((n, d), BF16),
                   jax.ShapeDtypeStruct((n // SLAB, dv, SLAB), BF16)],
        compiler_params=_cparams(1),
        name="proj_diff_attn",
    )(*x_args, g, mod12, cs, sn, wqk, wvt)


def _proj_sc_kernel(x_ref, g_ref, mod_ref, w_ref, b_ref, z_ref, *, tm, l, d):
    is_ctx = _is_ctx_rows(pl.program_id(0), tm, l)
    h = _norm_mod(x_ref[...], g_ref[...], _pick(mod_ref, 0, is_ctx), _pick(mod_ref, 1, is_ctx))
    y = jnp.dot(h.astype(BF16), w_ref[...], preferred_element_type=F32)
    b_ref[...] = y[:, :d].astype(BF16)
    z_ref[...] = y[:, d:2 * d] * y[:, 2 * d:]


def _proj_sc_call(xs, g, mod12, w, l):
    n, d = xs.shape
    row = lambda i: (i, 0)
    return pl.pallas_call(
        functools.partial(_proj_sc_kernel, tm=TM, l=l, d=d),
        grid=(n // TM,),
        in_specs=[pl.BlockSpec((TM, d), row), _const_spec((1, d)), _const_spec((2 * N_MOD, d)),
                  _const_spec(w.shape)],
        out_specs=[pl.BlockSpec((TM, d), row), pl.BlockSpec((TM, d), row)],
        out_shape=[jax.ShapeDtypeStruct((n, d), BF16), jax.ShapeDtypeStruct((n, d), F32)],
        compiler_params=_cparams(1),
        name="proj_short_conv",
    )(xs, g, mod12, w)


def _proj_wa_kernel(x_ref, g_ref, mod_ref, cs_ref, sn_ref, w_ref, q_ref, k_ref, v_ref,
                    *, tm, l, d, kw):
    is_ctx = _is_ctx_rows(pl.program_id(0), tm, l)
    h = _norm_mod(x_ref[...], g_ref[...], _pick(mod_ref, 0, is_ctx), _pick(mod_ref, 1, is_ctx))
    y = jnp.dot(h.astype(BF16), w_ref[...], preferred_element_type=F32)
    cs, sn, first = cs_ref[...], sn_ref[...], _first_half_lanes()
    for j in range(d // LANES):
        r = _rope(y[:, j * LANES:(j + 1) * LANES], cs, sn, first)
        q_ref[:, j * LANES:(j + 1) * LANES] = (r * (LOG2E * HEAD_DIM ** -0.5)).astype(BF16)
    for j in range(kw // LANES):
        r = _rope(y[:, d + j * LANES:d + (j + 1) * LANES], cs, sn, first)
        k_ref[:, j * LANES:(j + 1) * LANES] = r.astype(BF16)
    vt = y[:, d + kw:].T
    for s in range(tm // SLAB):
        v_ref[s] = vt[:, s * SLAB:(s + 1) * SLAB].astype(BF16)


def _proj_wa_call(xs, g, mod12, cs, sn, w, l, kw):
    n, d = xs.shape
    row = lambda i: (i, 0)
    return pl.pallas_call(
        functools.partial(_proj_wa_kernel, tm=TM, l=l, d=d, kw=kw),
        grid=(n // TM,),
        in_specs=[pl.BlockSpec((TM, d), row), _const_spec((1, d)), _const_spec((2 * N_MOD, d)),
                  pl.BlockSpec((TM, LANES), row), pl.BlockSpec((TM, LANES), row),
                  _const_spec(w.shape)],
        out_specs=[pl.BlockSpec((TM, d), row), pl.BlockSpec((TM, kw), row),
                   pl.BlockSpec((TM // SLAB, kw, SLAB), lambda i: (i, 0, 0))],
        out_shape=[jax.ShapeDtypeStruct((n, d), BF16), jax.ShapeDtypeStruct((n, kw), BF16),
                   jax.ShapeDtypeStruct((n // SLAB, kw, SLAB), BF16)],
        compiler_params=_cparams(1),
        name="proj_window_attn",
    )(xs, g, mod12, cs, sn, w)


def _log1p_unit(x):
    return jnp.log(1.0 + x)


def _proj_hg_kernel(x_ref, g_ref, mod_ref, lla_ref, l1m_ref, w_ref,
                    q_ref, gate_ref, v_ref, vt_ref, og_ref, *, tm, l, d):
    is_ctx = _is_ctx_rows(pl.program_id(0), tm, l)
    h = _norm_mod(x_ref[...], g_ref[...], _pick(mod_ref, 0, is_ctx), _pick(mod_ref, 1, is_ctx))
    hb = h.astype(BF16)
    part = lambda j: jnp.dot(hb, w_ref[:, j * d:(j + 1) * d], preferred_element_type=F32)
    q_ref[...] = _silu(part(0)).astype(BF16)
    for dr in range(2):
        f = part(1 + dr)
        log_sig = jnp.minimum(f, 0.0) - _log1p_unit(jnp.exp(-jnp.abs(f)))
        a = lla_ref[dr:dr + 1, :]
        c = l1m_ref[dr:dr + 1, :] + log_sig
        gate_ref[dr] = jnp.maximum(a, c) + _log1p_unit(jnp.exp(-jnp.abs(a - c)))
    v = part(3)
    v_ref[...] = v.astype(BF16)
    vt = v.T
    for s in range(tm // SLAB):
        vt_ref[s] = vt[:, s * SLAB:(s + 1) * SLAB].astype(BF16)
    og_ref[...] = _silu(part(4)).astype(BF16)


def _proj_hg_call(xs, g, mod12, lla, l1m, w, l):
    n, d = xs.shape
    row = lambda i: (i, 0)
    return pl.pallas_call(
        functools.partial(_proj_hg_kernel, tm=TM, l=l, d=d),
        grid=(n // TM,),
        in_specs=[pl.BlockSpec((TM, d), row), _const_spec((1, d)), _const_spec((2 * N_MOD, d)),
                  _const_spec((2, d)), _const_spec((2, d)), _const_spec(w.shape)],
        out_specs=[pl.BlockSpec((TM, d), row), pl.BlockSpec((2, TM, d), lambda i: (0, i, 0)),
                   pl.BlockSpec((TM, d), row),
                   pl.BlockSpec((TM // SLAB, d, SLAB), lambda i: (i, 0, 0)),
                   pl.BlockSpec((TM, d), row)],
        out_shape=[jax.ShapeDtypeStruct((n, d), BF16), jax.ShapeDtypeStruct((2, n, d), F32),
                   jax.ShapeDtypeStruct((n, d), BF16), jax.ShapeDtypeStruct((n // SLAB, d, SLAB), BF16),
                   jax.ShapeDtypeStruct((n, d), BF16)],
        compiler_params=_cparams(1),
        name="proj_hgrn",
    )(xs, g, mod12, lla, l1m, w)


def _diff_attn_kernel(lam_ref, q_ref, k_ref, vt_ref, g_ref, o_ref, m_ref, acc_ref,
                      sa_ref, sb_ref, ma_ref, mb_ref, *, tq, tk, l, lc, n, unroll, post_scale):
    lane = lax.broadcasted_iota(jnp.int32, (1, LANES), 1)
    sub = 2 * vt_ref.shape[2]
    n_chunks = l // tk

    def tile_rows(t):
        if isinstance(t, int):
            return slice(t * tq, (t + 1) * tq)
        return pl.ds(pl.multiple_of(t * tq, tq), tq)

    def stacked(t):
        q = q_ref[tile_rows(t), :]
        zero = jnp.zeros_like(q)
        return jnp.concatenate([jnp.where(lane < HEAD_DIM, q, zero),
                                jnp.where(lane >= HEAD_DIM, q, zero)], axis=0)

    def scores(kc, qcat):
        return lax.dot_general(kc, qcat, NT_DIMS, preferred_element_type=F32)

    def pv(pair0, n_pairs, p):
        out = None
        for j in range(n_pairs):
            s0 = 2 * (pair0 + j)
            vt = jnp.concatenate([vt_ref[s0], vt_ref[s0 + 1]], axis=1)
            t = jnp.dot(vt, p[j * sub:(j + 1) * sub, :], preferred_element_type=F32)
            out = t if out is None else out + t
        return out

    def absorb_context(qcat, slot):
        s = scores(k_ref[l:l + lc, :], qcat)
        mx = jnp.max(s, axis=0, keepdims=True)
        m_ref[slot] = mx
        acc_ref[slot] = pv(l // sub, lc // sub, jnp.exp2(s - mx).astype(BF16))

    def qk(c, qcat, s_ref, mx_ref):
        off = pl.multiple_of(c * tk, LANES)
        s = scores(k_ref[pl.ds(off, tk), :], qcat)
        s_ref[...] = s
        mx_ref[...] = jnp.max(s, axis=0, keepdims=True)

    def absorb(c, s_ref, mx_ref, slot):
        m_old = m_ref[slot]
        m_new = jnp.maximum(m_old, mx_ref[...])
        alpha = jnp.exp2(m_old - m_new)
        p = jnp.exp2(s_ref[...] - m_new).astype(BF16)
        m_ref[slot] = m_new
        acc_ref[slot] = alpha * acc_ref[slot] + pv((c * tk) // sub, tk // sub, p)

    def start_sweep(qcat, slot):
        qk(0, qcat, sa_ref, ma_ref)
        absorb_context(qcat, slot)

    def finish(t, slot):
        acc = acc_ref[slot]
        acc = acc[0:LANES, :] * (1.0 / acc[LANES:LANES + 1, :])
        o = acc[:, :tq] - lam_ref[0] * acc[:, tq:]
        o = o * lax.rsqrt(jnp.mean(o * o, axis=0, keepdims=True) + RMS_EPS)
        o = o * (g_ref[...] * post_scale)
        o_ref[tile_rows(t), :] = o.T.astype(BF16)

    bufs = ((sa_ref, ma_ref), (sb_ref, mb_ref))

    def sweep(t, slot):
        qcat = stacked(t)

        def group(c0, count):
            for j in range(count):
                qk(c0 + j + 1, qcat, *bufs[(j + 1) % 2])
                absorb(c0 + j, *bufs[j % 2], slot)

        def body(i, inner):
            group(unroll * i, unroll)
            return inner

        lax.fori_loop(0, n_chunks // unroll - 1, body, 0)
        group(n_chunks - unroll, unroll - 1)
        start_sweep(stacked(t + 1), 1 - slot)
        absorb(n_chunks - 1, *bufs[(unroll - 1) % 2], slot)
        finish(t, slot)

    def tile_pair(i, carry):
        sweep(2 * i, 0)
        sweep(2 * i + 1, 1)
        return carry

    start_sweep(stacked(0), 0)
    lax.fori_loop(0, l // (2 * tq), tile_pair, 0)
    for t in range(l // tq, n // tq):
        absorb_context(stacked(t), 0)
        finish(t, 0)


def _diff_attn_call(lam, q, k, vt3, subln_g, l, lc, post_scale):
    n, d = q.shape
    nh = d // LANES
    nslab, vrows, slab = vt3.shape
    vrows //= nh
    tq = DA_TQ
    tk = min(DA_TK, l // 2)
    assert l % (2 * tq) == 0 and n % tq == 0 and n > l and tk % (2 * slab) == 0
    assert l % (2 * slab) == 0 and lc % (2 * slab) == 0
    assert l % (2 * tk) == 0
    return pl.pallas_call(
        functools.partial(_diff_attn_kernel, tq=tq, tk=tk, l=l, lc=lc, n=n,
                          unroll=next(u for u in DA_UNROLLS if l % (u * tk) == 0),
                          post_scale=post_scale),
        grid=(nh,),
        in_specs=[pl.BlockSpec(memory_space=pltpu.SMEM),
                  pl.BlockSpec((n, LANES), lambda h: (0, h)),
                  pl.BlockSpec((n, LANES), lambda h: (0, h)),
                  pl.BlockSpec((nslab, vrows, slab), lambda h: (0, h, 0)),
                  pl.BlockSpec((LANES, 1), lambda h: (0, 0))],
        out_specs=pl.BlockSpec((n, LANES), lambda h: (0, h)),
        out_shape=jax.ShapeDtypeStruct((n, d), BF16),
        scratch_shapes=[pltpu.VMEM((2, 1, 2 * tq), F32),
                        pltpu.VMEM((2, vrows, 2 * tq), F32),
                        pltpu.VMEM((tk, 2 * tq), F32), pltpu.VMEM((tk, 2 * tq), F32),
                        pltpu.VMEM((1, 2 * tq), F32), pltpu.VMEM((1, 2 * tq), F32)],
        compiler_params=_cparams(1),
        name="diff_attention",
    )(lam, q, k, vt3, subln_g)


def _win_attn_kernel(sink_ref, q_ref, kc_ref, kp_ref, ko_ref, kn_ref,
                     vc_ref, vp_ref, vo_ref, vn_ref, o_ref, *, tq, l, lc, rep, ng):
    j = pl.program_id(0)
    w = WINDOW
    gw = rep * HEAD_DIM
    a = lax.broadcasted_iota(jnp.int32, (w, w), 0)
    b = lax.broadcasted_iota(jnp.int32, (w, w), 1)
    lane = lax.broadcasted_iota(jnp.int32, (1, LANES), 1)
    low = lane < HEAD_DIM
    biases = []
    for t in range(tq // w):
        q0 = j * tq + t * w
        latent = q0 < l
        own_bias = jnp.where(latent, 0.0, NEG_BIG)
        prev_bias = jnp.where(jnp.logical_and(latent, q0 >= w), 0.0, NEG_BIG)
        next_bias = jnp.where(q0 + w < l, 0.0, NEG_BIG)
        bias = jnp.concatenate(
            [jnp.where(a >= b, prev_bias, NEG_BIG),
             jnp.full((w, w), own_bias, F32),
             jnp.where(a <= b, next_bias, NEG_BIG)], axis=0)
        biases.append(jnp.concatenate([bias] * rep, axis=1))
    for g in range(ng):
        gl = slice(g * LANES, (g + 1) * LANES)
        kc = kc_ref[:, gl]
        kwin = jnp.concatenate([kp_ref[:, gl], ko_ref[:, gl], kn_ref[:, gl]], axis=0)
        vtc = jnp.concatenate([vc_ref[s, gl, :] for s in range(lc // w)], axis=1)
        vtwin = jnp.concatenate([vp_ref[0, gl, :]] + [vo_ref[s, gl, :] for s in range(tq // w)]
                                + [vn_ref[0, gl, :]], axis=1)
        for t in range(tq // w):
            qs, sinks = [], []
            for r in range(rep):
                c0 = g * gw + (r // 2) * LANES
                qc = q_ref[t * w:(t + 1) * w, c0:c0 + LANES]
                qs.append(jnp.where(low if r % 2 == 0 else ~low, qc, jnp.zeros_like(qc)))
                sinks.append(jnp.full((1, w), sink_ref[g * rep + r] * LOG2E, F32))
            qst = jnp.concatenate(qs, axis=0)
            sink = jnp.concatenate(sinks, axis=1)
            s_c = lax.dot_general(kc, qst, NT_DIMS, preferred_element_type=F32)
            s_w = (lax.dot_general(kwin[t * w:(t + 3) * w, :], qst, NT_DIMS,
                                   preferred_element_type=F32) + biases[t])
            mx = jnp.maximum(jnp.maximum(jnp.max(s_c, axis=0, keepdims=True),
                                         jnp.max(s_w, axis=0, keepdims=True)), sink)
            e_c = jnp.exp2(s_c - mx)
            e_w = jnp.exp2(s_w - mx)
            den = (jnp.exp2(sink - mx) + jnp.sum(e_c, axis=0, keepdims=True)
                   + jnp.sum(e_w, axis=0, keepdims=True))
            ot = (jnp.dot(vtc, e_c.astype(BF16), preferred_element_type=F32)
                  + jnp.dot(vtwin[:, t * w:(t + 3) * w], e_w.astype(BF16),
                            preferred_element_type=F32)) * (1.0 / den)
            heads = [ot[:, r * w:(r + 1) * w].T for r in range(rep)]
            for c in range(rep // 2):
                c0 = g * gw + c * LANES
                o_ref[t * w:(t + 1) * w, c0:c0 + LANES] = jnp.where(
                    low, heads[2 * c], heads[2 * c + 1]).astype(BF16)


def _win_attn_call(sinks, q, k2, vt2, l, lc):
    n, d = q.shape
    ng = WA_KV_HEADS
    rep = d // HEAD_DIM // ng
    kw = k2.shape[1]
    tq = WA_TQ
    assert n % tq == 0 and tq % WINDOW == 0 and lc % WINDOW == 0 and l % lc == 0 and rep % 2 == 0
    assert kw == ng * LANES
    r = tq // WINDOW
    last = n // WINDOW - 1
    prev = lambda j: jnp.maximum(j * r - 1, 0)
    nxt = lambda j: jnp.minimum((j + 1) * r, last)
    k_specs = [pl.BlockSpec((lc, kw), lambda j: (l // lc, 0)),
               pl.BlockSpec((WINDOW, kw), lambda j: (prev(j), 0)),
               pl.BlockSpec((tq, kw), lambda j: (j, 0)),
               pl.BlockSpec((WINDOW, kw), lambda j: (nxt(j), 0))]
    vt_specs = [pl.BlockSpec((lc // WINDOW, kw, WINDOW), lambda j: (l // lc, 0, 0)),
                pl.BlockSpec((1, kw, WINDOW), lambda j: (prev(j), 0, 0)),
                pl.BlockSpec((r, kw, WINDOW), lambda j: (j, 0, 0)),
                pl.BlockSpec((1, kw, WINDOW), lambda j: (nxt(j), 0, 0))]
    return pl.pallas_call(
        functools.partial(_win_attn_kernel, tq=tq, l=l, lc=lc, rep=rep, ng=ng),
        grid=(n // tq,),
        in_specs=[pl.BlockSpec(memory_space=pltpu.SMEM),
                  pl.BlockSpec((tq, d), lambda j: (j, 0))] + k_specs + vt_specs,
        out_specs=pl.BlockSpec((tq, d), lambda j: (j, 0)),
        out_shape=jax.ShapeDtypeStruct((n, d), BF16),
        compiler_params=_cparams(1),
        name="window_attention",
    )(sinks, q, k2, k2, k2, k2, vt2, vt2, vt2, vt2)


def _hgrn_consts(chunk):
    n_lv = int(math.log2(chunk))
    walls, masks_all = [], []
    for direction in range(2):
        tau = np.arange(chunk) if direction == 0 else chunk - 1 - np.arange(chunk)
        tr, tc = tau[:, None], tau[None, :]
        blocks = [tc <= tr]
        masks = []
        for lv in range(n_lv):
            h = chunk >> (lv + 1)
            mid = (tr // (2 * h)) * 2 * h + h - 1
            second_r = (tr % (2 * h)) >= h
            second_c = (tc % (2 * h)) >= h
            if h > 1:
                blocks.append(np.where(second_r, (tc > mid) & (tc <= tr), (tc > tr) & (tc <= mid)))
            masks.append(second_r & ~second_c & ((tr // (2 * h)) == (tc // (2 * h))))
        masks.append(tr == tc)
        w = np.concatenate(blocks, axis=0).astype(np.float32)
        walls.append(np.concatenate([w, w], axis=1))
        masks_all.append(np.stack(masks).astype(np.float32))
    return np.stack(walls), np.stack(masks_all)


def _hgrn_kernel(wall_ref, mask_ref, q_ref, g_ref, v_ref, vt_ref, o_ref, st_ref,
                 *, chunk, n_lv, n_heads, n_steps, per_step):
    dr = pl.program_id(0)
    step = pl.program_id(1)

    @pl.when(step == 0)
    def _():
        st_ref[...] = jnp.zeros_like(st_ref)

    @pl.when(step >= n_steps)
    def _():
        o_ref[...] = jnp.zeros_like(o_ref)

    @pl.when(step < n_steps)
    def _():
        heads = [slice(hh * LANES, (hh + 1) * LANES) for hh in range(n_heads)]
        nt = lambda x, y: lax.dot_general(x, y, NT_DIMS, preferred_element_type=F32)
        for j in range(per_step):
            idx = jnp.where(dr == 0, j, per_step - 1 - j)
            rows = pl.ds(pl.multiple_of(idx * chunk, chunk), chunk)
            g_all = g_ref[0, rows, :]
            g_hi = g_all.astype(BF16)
            g_lo = (g_all - g_hi.astype(F32)).astype(BF16)
            sums_all = jnp.dot(wall_ref[0], jnp.concatenate([g_hi, g_lo], axis=0),
                               preferred_element_type=F32)
            q16 = q_ref[rows, :]
            q_all = q16.astype(F32)
            f_all = jnp.exp(g_all)
            k_all = 1.0 - f_all
            k16 = k_all.astype(BF16)
            v_all = v_ref[rows, :]
            vt_all = vt_ref[idx]
            b = sums_all[0:chunk]
            b_end = jnp.sum(g_all, axis=0, keepdims=True)

            qb = (q_all * jnp.exp(b)).astype(BF16)
            o = [nt(qb[:, sl], st_ref[hh].astype(BF16)) for hh, sl in enumerate(heads)]
            att = [nt(q16[:, sl], k16[:, sl]) * mask_ref[0, n_lv] for sl in heads]
            for lv in range(n_lv - 1):
                e = jnp.exp(sums_all[(1 + lv) * chunk:(2 + lv) * chunk])
                qe = (q_all * e).astype(BF16)
                ke = (k_all * e).astype(BF16)
                mask = mask_ref[0, lv]
                att = [a + nt(qe[:, sl], ke[:, sl]) * mask for a, sl in zip(att, heads)]
            qe = (q_all * f_all).astype(BF16)
            mask = mask_ref[0, n_lv - 1]
            att = [a + nt(qe[:, sl], k16[:, sl]) * mask for a, sl in zip(att, heads)]
            out = [o[hh] + jnp.dot(att[hh].astype(BF16), v_all[:, sl], preferred_element_type=F32)
                   for hh, sl in enumerate(heads)]
            o_ref[0, rows, :] = jnp.concatenate(out, axis=1)

            ks = (k_all * jnp.exp(b_end - b)).astype(BF16)
            decay = jnp.exp(b_end)
            for hh, sl in enumerate(heads):
                st_ref[hh] = st_ref[hh] * decay[:, sl] + jnp.dot(vt_all[sl, :], ks[:, sl],
                                                                 preferred_element_type=F32)


def _hgrn_call(qh, gates, v, vt3, l, lc):
    n, d = qh.shape
    c = HG_CHUNK
    per = HG_PER_STEP
    rows = per * c
    nh = d // LANES
    nx, ncc, ntot = l // rows, lc // rows, n // rows
    nc = nx + ncc
    assert n % rows == 0 and l % rows == 0 and lc % rows == 0 and vt3.shape[2] == c
    n_lv = int(math.log2(c))
    wall_np, mask_np = _hgrn_consts(c)
    wall = jnp.asarray(wall_np, dtype=BF16)
    masks = jnp.asarray(mask_np, dtype=F32)

    def blk(dr, s):
        fwd = jnp.where(s < ncc, nx + s, s - ncc)
        bwd = jnp.where(s < ncc, nx + ncc - 1 - s, nx - 1 - (s - ncc))
        return jnp.where(s >= nc, s, jnp.where(dr == 0, fwd, bwd))

    return pl.pallas_call(
        functools.partial(_hgrn_kernel, chunk=c, n_lv=n_lv, n_heads=nh, n_steps=nc, per_step=per),
        grid=(2, ntot),
        in_specs=[pl.BlockSpec((1,) + wall_np.shape[1:], lambda dr, s: (dr, 0, 0)),
                  pl.BlockSpec((1,) + mask_np.shape[1:], lambda dr, s: (dr, 0, 0, 0)),
                  pl.BlockSpec((rows, d), lambda dr, s: (blk(dr, s), 0)),
                  pl.BlockSpec((1, rows, d), lambda dr, s: (dr, blk(dr, s), 0)),
                  pl.BlockSpec((rows, d), lambda dr, s: (blk(dr, s), 0)),
                  pl.BlockSpec((per, d, c), lambda dr, s: (blk(dr, s), 0, 0))],
        out_specs=pl.BlockSpec((1, rows, d), lambda dr, s: (dr, blk(dr, s), 0)),
        out_shape=jax.ShapeDtypeStruct((2, n, d), F32),
        scratch_shapes=[pltpu.VMEM((nh, LANES, LANES), F32)],
        compiler_params=_cparams(2),
        name="hgrn_scan",
    )(wall, masks, qh, gates, v, vt3)


def _post_kernel(*refs, mode, final, tm, l, lc, d, f, n_x, fuse_sc):
    it = iter(refs)
    x_refs = [next(it) for _ in range(n_x)]
    mod_ref, g2_ref, wo_ref, win_ref, wout_ref = (next(it) for _ in range(5))
    if fuse_sc:
        g1n_ref, modn_ref, wsc_ref = (next(it) for _ in range(3))
    if mode == "plain":
        a_ref = next(it)
    elif mode == "conv":
        b_ref, z_ref, zp_ref, zn_ref, cw_ref = (next(it) for _ in range(5))
    else:
        ofw_ref, obw_ref, og_ref, gn_ref = (next(it) for _ in range(4))
    fin_ref = next(it) if final else None
    o_ref = next(it)
    if fuse_sc:
        bn_ref, zn_out_ref = next(it), next(it)

    ti = pl.program_id(0)
    is_ctx = _is_ctx_rows(ti, tm, l)
    if mode == "plain":
        a = a_ref[...]
    elif mode == "conv":
        z = z_ref[...]
        rl = lax.broadcasted_iota(jnp.int32, (tm, 1), 0)
        row = ti * tm + rl
        z_prev = jnp.where(rl == 0, zp_ref[7:8, :], pltpu.roll(z, 1, 0))
        z_next = jnp.where(rl == tm - 1, zn_ref[0:1, :], pltpu.roll(z, tm - 1, 0))
        z_prev = jnp.where((row == 0) | (row == l), 0.0, z_prev)
        z_next = jnp.where((row == l - 1) | (row == l + lc - 1), 0.0, z_next)
        conv = cw_ref[0:1, :] * z_prev + cw_ref[1:2, :] * z + cw_ref[2:3, :] * z_next
        a = (b_ref[...].astype(F32) * conv).astype(BF16)
    else:
        gn = gn_ref[...]
        heads = []
        for hh in range(d // LANES):
            sl = slice(hh * LANES, (hh + 1) * LANES)
            oh = _rms(ofw_ref[0, :, sl] + obw_ref[0, :, sl])
            heads.append((oh * gn * og_ref[:, sl].astype(F32)).astype(BF16))
        a = jnp.concatenate(heads, axis=1)

    y = jnp.dot(a, wo_ref[...], preferred_element_type=F32)
    x = _stream_tile(x_refs, ti, tm, l) + _pick(mod_ref, 2, is_ctx) * y
    h = _norm_mod(x, g2_ref[...], _pick(mod_ref, 3, is_ctx), _pick(mod_ref, 4, is_ctx)).astype(BF16)
    acc = jnp.zeros((tm, d), F32)
    for c0 in range(0, f, FFN_CHUNK):
        u = jnp.dot(h, win_ref[:, c0:c0 + FFN_CHUNK], preferred_element_type=F32)
        w = jnp.dot(h, win_ref[:, f + c0:f + c0 + FFN_CHUNK], preferred_element_type=F32)
        act = (_silu(u) * w).astype(BF16)
        acc = acc + jnp.dot(act, wout_ref[c0:c0 + FFN_CHUNK, :], preferred_element_type=F32)
    x = x + _pick(mod_ref, 5, is_ctx) * acc
    if final:
        x = _rms(x) * fin_ref[...]
    o_ref[...] = x
    if fuse_sc:
        hn = _norm_mod(x, g1n_ref[...], _pick(modn_ref, 0, is_ctx), _pick(modn_ref, 1, is_ctx))
        yn = jnp.dot(hn.astype(BF16), wsc_ref[...], preferred_element_type=F32)
        bn_ref[...] = yn[:, :d].astype(BF16)
        zn_out_ref[...] = yn[:, d:2 * d] * yn[:, 2 * d:]


def _post_call(xs, mod12, g2, wo, win, wout, mixer_inputs, mode, l, lc, final_g=None, next_sc=None):
    x_specs, x_args, n, d = _stream_specs(xs, l)
    f = wout.shape[0]
    final = final_g is not None
    assert f % FFN_CHUNK == 0 and l % TM == 0 and n % TM == 0
    n_out = l if final else n
    row = lambda i: (i, 0)
    in_specs = x_specs + [_const_spec((2 * N_MOD, d)), _const_spec((1, d)),
                          _const_spec(wo.shape), _const_spec(win.shape), _const_spec(wout.shape)]
    args = x_args + [mod12, g2, wo, win, wout]
    out_specs = [pl.BlockSpec((TM, d), row)]
    out_shape = [jax.ShapeDtypeStruct((n_out, d), F32)]
    if next_sc is not None:
        in_specs += [_const_spec((1, d)), _const_spec((2 * N_MOD, d)), _const_spec(next_sc[2].shape)]
        args += list(next_sc)
        out_specs += [pl.BlockSpec((TM, d), row), pl.BlockSpec((TM, d), row)]
        out_shape += [jax.ShapeDtypeStruct((n, d), BF16), jax.ShapeDtypeStruct((n, d), F32)]
    if mode == "plain":
        in_specs += [pl.BlockSpec((TM, d), row)]
    elif mode == "conv":
        r8 = TM // 8
        last8 = n // 8 - 1
        in_specs += [pl.BlockSpec((TM, d), row), pl.BlockSpec((TM, d), row),
                     pl.BlockSpec((8, d), lambda i: (jnp.maximum(i * r8 - 1, 0), 0)),
                     pl.BlockSpec((8, d), lambda i: (jnp.minimum((i + 1) * r8, last8), 0)),
                     _const_spec((SC_WIDTH, d))]
    else:
        in_specs += [pl.BlockSpec((1, TM, d), lambda i: (0, i, 0)),
                     pl.BlockSpec((1, TM, d), lambda i: (1, i, 0)),
                     pl.BlockSpec((TM, d), row), _const_spec((1, LANES))]
    args += list(mixer_inputs)
    if final:
        in_specs += [_const_spec((1, d))]
        args += [final_g]
    outs = pl.pallas_call(
        functools.partial(_post_kernel, mode=mode, final=final, tm=TM, l=l, lc=lc, d=d, f=f,
                          n_x=len(x_args), fuse_sc=next_sc is not None),
        grid=(n_out // TM,),
        in_specs=in_specs,
        out_specs=out_specs,
        out_shape=out_shape,
        compiler_params=_cparams(1),
        name="post_" + mode + ("_final" if final else "") + ("_sc" if next_sc is not None else ""),
    )(*args)
    return outs[0] if next_sc is None else tuple(outs)


def _rope_tables(l, n):
    rows = l // GRID_W
    r = jnp.repeat(jnp.arange(rows), GRID_W).astype(F32)
    c = jnp.tile(jnp.arange(GRID_W), rows).astype(F32)
    half = HEAD_DIM // 2
    inv_freq = 1.0 / (ROPE_BASE ** (jnp.arange(0, half, 2, dtype=F32) / half))
    ar, ac = r[:, None] * inv_freq, c[:, None] * inv_freq
    cs = jnp.concatenate([jnp.cos(ar), jnp.cos(ar), jnp.cos(ac), jnp.cos(ac)], axis=-1)
    sn = jnp.concatenate([-jnp.sin(ar), jnp.sin(ar), -jnp.sin(ac), jnp.sin(ac)], axis=-1)
    cs = jnp.concatenate([cs, jnp.ones((n - l, HEAD_DIM), F32)], axis=0)
    sn = jnp.concatenate([sn, jnp.zeros((n - l, HEAD_DIM), F32)], axis=0)
    return jnp.tile(cs, (1, 2)), jnp.tile(sn, (1, 2))


def kernel(x, c, ctx, c_ctx, ada_w, ada_b, norm1_g, norm2_g, ffn_w_in, ffn_w_out, final_g,
           da_wqkv, da_lambda, da_subln_g, da_wo, sc_w_in, sc_conv_w, sc_w_out,
           wa_wqkv, wa_sinks, wa_wo, hg_w_in, hg_lb, hg_gnorm_g, hg_wo):
    b, l, d = x.shape
    lc = ctx.shape[1]
    depth = ada_w.shape[0]
    assert b == 1 and d % LANES == 0 and l % TM == 0
    n = -(-(l + lc) // TM) * TM
    xs = (x[0], jnp.concatenate([ctx[0], jnp.zeros((n - l - lc, d), x.dtype)], axis=0))

    cc = jnp.zeros((8, d), F32).at[0].set(c_ctx).at[1].set(c[0])
    mods = _mod_call(cc, ada_w, ada_b)
    cs, sn = _rope_tables(l, n)

    n_mix = 4
    sc_ready = None
    for i in range(depth):
        mixer, slot = i % n_mix, i // n_mix
        last = i == depth - 1
        mod12 = jnp.concatenate([mods[i, 0].reshape(N_MOD, d), mods[i, 1].reshape(N_MOD, d)], axis=0)
        g1 = norm1_g[i].reshape(1, d)
        g2 = norm2_g[i].reshape(1, d)
        win = ffn_w_in[i].astype(BF16)
        wout = ffn_w_out[i].astype(BF16)
        fin = final_g.reshape(1, d) if last else None
        if isinstance(xs, tuple) and mixer != 0:
            xs = jnp.concatenate(xs, axis=0)

        if mixer == 0:
            w = da_wqkv[slot]
            lam = da_lambda[slot].astype(F32)
            lam_init = 0.8 - 0.6 * math.exp(-0.3 * i)
            lam_full = (jnp.exp(jnp.sum(lam[0] * lam[1])) - jnp.exp(jnp.sum(lam[2] * lam[3]))
                        + lam_init).reshape(1)
            q, k, vt3 = _proj_da_call(xs, g1, mod12, cs, sn, w[:, :2 * d].astype(BF16),
                                      w[:, 2 * d:].T.astype(BF16), l)
            a = _diff_attn_call(lam_full, q, k, vt3, da_subln_g[slot].reshape(LANES, 1), l, lc,
                                1.0 - lam_init)
            if not last and (i + 1) % n_mix == 1:
                nxt = jnp.concatenate([mods[i + 1, 0].reshape(N_MOD, d),
                                       mods[i + 1, 1].reshape(N_MOD, d)], axis=0)
                xs, *sc_ready = _post_call(
                    xs, mod12, g2, da_wo[slot].astype(BF16), win, wout, [a], "plain", l, lc, fin,
                    next_sc=(norm1_g[i + 1].reshape(1, d), nxt,
                             sc_w_in[(i + 1) // n_mix].astype(BF16)))
            else:
                xs = _post_call(xs, mod12, g2, da_wo[slot].astype(BF16), win, wout, [a], "plain",
                                l, lc, fin)
        elif mixer == 1:
            if sc_ready is not None:
                bg, z = sc_ready
                sc_ready = None
            else:
                bg, z = _proj_sc_call(xs, g1, mod12, sc_w_in[slot].astype(BF16), l)
            xs = _post_call(xs, mod12, g2, sc_w_out[slot].astype(BF16), win, wout,
                            [bg, z, z, z, sc_conv_w[slot]], "conv", l, lc, fin)
        elif mixer == 2:
            w = wa_wqkv[slot]
            kw = WA_KV_HEADS * HEAD_DIM
            dup = lambda m: jnp.tile(m.reshape(d, WA_KV_HEADS, 1, HEAD_DIM), (1, 1, 2, 1)).reshape(d, 2 * kw)
            w2 = jnp.concatenate([w[:, :d], dup(w[:, d:d + kw]), dup(w[:, d + kw:])], axis=1)
            q, k2, vt2 = _proj_wa_call(xs, g1, mod12, cs, sn, w2.astype(BF16), l, 2 * kw)
            a = _win_attn_call(wa_sinks[slot].astype(F32), q, k2, vt2, l, lc)
            xs = _post_call(xs, mod12, g2, wa_wo[slot].astype(BF16), win, wout, [a], "plain",
                            l, lc, fin)
        else:
            w = hg_w_in[slot]
            p = jax.nn.softmax(hg_lb.astype(F32), axis=1)
            lb = (jnp.cumsum(p, axis=1) - p[:, :1])[:, i]
            qh, gates, v, vt3, og = _proj_hg_call(xs, g1, mod12, jnp.log(lb), jnp.log1p(-lb),
                                                  w.astype(BF16), l)
            o2 = _hgrn_call(qh, gates, v, vt3, l, lc)
            xs = _post_call(xs, mod12, g2, hg_wo[slot].astype(BF16), win, wout,
                            [o2, o2, og, hg_gnorm_g[slot].reshape(1, LANES)], "hgrn", l, lc, fin)
    return xs[:l].reshape(1, l, d)
```
